```python
import jax, jax.numpy as jnp
from jax import lax
import numpy as np

D_MODEL = 1024
BATCH = 8
SEQ = 8192
DEPTH = 2

GRID_W = 64
Q_BLOCK = 128
ROPE_THETA = 10000.0
EPS = 1e-6
MLA_HEADS = 6
MLA_NOPE = 64
MLA_ROPE = 32
MLA_V = 64
MLA_Q_RANK = 256
MLA_KV_RANK = 128
GQA_HEADS = 6
GQA_KV_HEADS = 2
GQA_DIM = 64
GMLP_GROUPS = 4
GMLP_DIM = 64
GMLP_CHUNK = 128
W_A = MLA_HEADS * MLA_V
W_B = GQA_HEADS * GQA_DIM
W_C = GMLP_GROUPS * GMLP_DIM
D_MIX = W_A + W_B + W_C
IN_SPLITS = (MLA_Q_RANK, MLA_KV_RANK, MLA_ROPE, W_B, GQA_KV_HEADS * GQA_DIM, GQA_KV_HEADS * GQA_DIM, 2 * W_C)
D_IN = 1568
MEM_TOKENS = 256
MEM_HEADS = 4
MEM_DIM = 128
D_FF = 2816
CONV_W = 3

kernel_name = "hybrid_mla_gqa_gmlp_encoder"


def rms_norm(x, g):
    xf = x.astype(jnp.float32)
    y = xf * lax.rsqrt(jnp.mean(xf * xf, axis=-1, keepdims=True) + EPS)
    return (y * g.astype(jnp.float32)).astype(x.dtype)


def axial_rope_table(S, d_rot):
    rows = S // GRID_W
    row = jnp.repeat(jnp.arange(rows, dtype=jnp.float32), GRID_W)
    col = jnp.tile(jnp.arange(GRID_W, dtype=jnp.float32), rows)
    n = d_rot // 4
    inv = ROPE_THETA ** (-jnp.arange(n, dtype=jnp.float32) / n)
    ang = jnp.concatenate([row[:, None] * inv, col[:, None] * inv], axis=-1)
    return jnp.cos(ang)[:, None, :], jnp.sin(ang)[:, None, :]


def apply_rope(x, cos, sin):
    d = x.shape[-1]
    xp = x.reshape(*x.shape[:-1], d // 2, 2)
    a, b = xp[..., 0], xp[..., 1]
    c, s = cos.astype(x.dtype), sin.astype(x.dtype)
    return jnp.stack([a * c - b * s, a * s + b * c], axis=-1).reshape(x.shape)


def block_attention(q, k, v, scale):
    B, S, H, dk = q.shape
    Hk, dv = k.shape[2], v.shape[-1]
    G = H // Hk
    nb = S // Q_BLOCK
    qb = q.reshape(B, nb, Q_BLOCK, Hk, G, dk).transpose(1, 0, 2, 3, 4, 5)

    def one_block(qi):
        s = jnp.einsum('bqhgd,bkhd->bhgqk', qi, k, preferred_element_type=jnp.float32) * scale
        p = jax.nn.softmax(s, axis=-1).astype(v.dtype)
        return jnp.einsum('bhgqk,bkhe->bqhge', p, v)

    ob = lax.map(one_block, qb)
    return ob.transpose(1, 0, 2, 3, 4, 5).reshape(B, S, H * dv)


def hybrid_mixer(h, rope_a, rope_b, w_in, mla_q_norm, mla_w_uq, mla_kv_norm, mla_w_ukv,
                 gqa_q_norm, gqa_k_norm, gmlp_v_norm, gmlp_w_s, gmlp_b_s, out_norm, w_out):
    B, S, _ = h.shape
    z = h @ w_in
    offs = [int(i) for i in np.cumsum(IN_SPLITS)[:-1]]
    c_q, c_kv, k_rope, g_q, g_k, g_v, g_m = jnp.split(z, offs, axis=-1)

    qa = (rms_norm(c_q, mla_q_norm) @ mla_w_uq).reshape(B, S, MLA_HEADS, MLA_NOPE + MLA_ROPE)
    q_nope, q_pe = qa[..., :MLA_NOPE], apply_rope(qa[..., MLA_NOPE:], *rope_a)
    kva = (rms_norm(c_kv, mla_kv_norm) @ mla_w_ukv).reshape(B, S, MLA_HEADS, MLA_NOPE + MLA_V)
    k_nope, v_a = kva[..., :MLA_NOPE], kva[..., MLA_NOPE:]
    k_pe = apply_rope(k_rope.reshape(B, S, 1, MLA_ROPE), *rope_a)
    q_a = jnp.concatenate([q_nope, q_pe], axis=-1)
    k_a = jnp.concatenate([k_nope, jnp.broadcast_to(k_pe, (B, S, MLA_HEADS, MLA_ROPE))], axis=-1)
    y_a = block_attention(q_a, k_a, v_a, (MLA_NOPE + MLA_ROPE) ** -0.5)

    q_b = apply_rope(rms_norm(g_q.reshape(B, S, GQA_HEADS, GQA_DIM), gqa_q_norm), *rope_b)
    k_b = apply_rope(rms_norm(g_k.reshape(B, S, GQA_KV_HEADS, GQA_DIM), gqa_k_norm), *rope_b)
    v_b = g_v.reshape(B, S, GQA_KV_HEADS, GQA_DIM)
    y_b = block_attention(q_b, k_b, v_b, GQA_DIM ** -0.5)

    g_m = jax.nn.gelu(g_m)
    u, vv = g_m[..., :W_C], g_m[..., W_C:]
    vv = rms_norm(vv, gmlp_v_norm).reshape(B, S // GMLP_CHUNK, GMLP_CHUNK, GMLP_GROUPS, GMLP_DIM)
    mixed = jnp.einsum('gpq,bnqgc->bnpgc', gmlp_w_s, vv) + gmlp_b_s.T[None, None, :, :, None]
    y_c = u * mixed.reshape(B, S, W_C)

    y = jnp.concatenate([rms_norm(y_a, out_norm[:W_A]),
                         rms_norm(y_b, out_norm[W_A:W_A + W_B]),
                         rms_norm(y_c, out_norm[W_A + W_B:])], axis=-1)
    return y @ w_out


def memory_cross_attention(h, mem, mem_kv_norm, mem_w_q, mem_w_kv, mem_w_o):
    B, S, _ = h.shape
    T = mem.shape[1]
    q = (h @ mem_w_q).reshape(B, S, MEM_HEADS, MEM_DIM)
    kv = rms_norm(mem, mem_kv_norm) @ mem_w_kv
    k = kv[..., :MEM_HEADS * MEM_DIM].reshape(B, T, MEM_HEADS, MEM_DIM)
    v = kv[..., MEM_HEADS * MEM_DIM:].reshape(B, T, MEM_HEADS, MEM_DIM)
    return block_attention(q, k, v, MEM_DIM ** -0.5) @ mem_w_o


def conv_gated_ffn(h, w_up, conv_w, conv_b, w_down):
    a = h @ w_up
    ap = jnp.pad(a, ((0, 0), (1, 1), (0, 0)))
    a = ap[:, :-2] * conv_w[0] + ap[:, 1:-1] * conv_w[1] + ap[:, 2:] * conv_w[2] + conv_b
    gate, val = a[..., :D_FF], a[..., D_FF:]
    return (jax.nn.silu(gate) * val) @ w_down


def setup_inputs(seed: int = 0) -> dict:
    key = jax.random.key(seed)
    ks = iter(jax.random.split(key, 48))

    def nrm(shape, scale):
        return jax.random.normal(next(ks), shape, jnp.float32) * scale

    def gain(shape):
        return 1.0 + nrm(shape, 0.05)

    L = DEPTH
    return {
        "x": nrm((BATCH, SEQ, D_MODEL), 1.0),
        "mem": nrm((BATCH, MEM_TOKENS, D_MODEL), 1.0),
        "mix_norm": gain((L, D_MODEL)),
        "w_in": nrm((L, D_MODEL, D_IN), D_MODEL ** -0.5),
        "mla_q_norm": gain((L, MLA_Q_RANK)),
        "mla_w_uq": nrm((L, MLA_Q_RANK, MLA_HEADS * (MLA_NOPE + MLA_ROPE)), MLA_Q_RANK ** -0.5),
        "mla_kv_norm": gain((L, MLA_KV_RANK)),
        "mla_w_ukv": nrm((L, MLA_KV_RANK, MLA_HEADS * (MLA_NOPE + MLA_V)), MLA_KV_RANK ** -0.5),
        "gqa_q_norm": gain((L, GQA_DIM)),
        "gqa_k_norm": gain((L, GQA_DIM)),
        "gmlp_v_norm": gain((L, W_C)),
        "gmlp_w_s": nrm((L, GMLP_GROUPS, GMLP_CHUNK, GMLP_CHUNK), GMLP_CHUNK ** -0.5),
        "gmlp_b_s": gain((L, GMLP_GROUPS, GMLP_CHUNK)),
        "out_norm": gain((L, D_MIX)),
        "w_out": nrm((L, D_MIX, D_MODEL), D_MIX ** -0.5),
        "mem_x_norm": gain((L, D_MODEL)),
        "mem_kv_norm": gain((L, D_MODEL)),
        "mem_w_q": nrm((L, D_MODEL, MEM_HEADS * MEM_DIM), D_MODEL ** -0.5),
        "mem_w_kv": nrm((L, D_MODEL, 2 * MEM_HEADS * MEM_DIM), D_MODEL ** -0.5),
        "mem_w_o": nrm((L, MEM_HEADS * MEM_DIM, D_MODEL), (MEM_HEADS * MEM_DIM) ** -0.5),
        "ffn_norm": gain((L, D_MODEL)),
        "ffn_w_up": nrm((L, D_MODEL, 2 * D_FF), D_MODEL ** -0.5),
        "ffn_conv_w": nrm((L, CONV_W, 2 * D_FF), CONV_W ** -0.5),
        "ffn_conv_b": nrm((L, 2 * D_FF), 0.01),
        "ffn_w_down": nrm((L, D_FF, D_MODEL), D_FF ** -0.5),
        "final_norm": gain((D_MODEL,)),
    }


def reference(x, mem, mix_norm, w_in, mla_q_norm, mla_w_uq, mla_kv_norm, mla_w_ukv,
              gqa_q_norm, gqa_k_norm, gmlp_v_norm, gmlp_w_s, gmlp_b_s, out_norm, w_out,
              mem_x_norm, mem_kv_norm, mem_w_q, mem_w_kv, mem_w_o,
              ffn_norm, ffn_w_up, ffn_conv_w, ffn_conv_b, ffn_w_down, final_norm):
    S = x.shape[1]
    rope_a = axial_rope_table(S, MLA_ROPE)
    rope_b = axial_rope_table(S, GQA_DIM)
    for l in range(DEPTH):
        h = rms_norm(x, mix_norm[l])
        x = x + hybrid_mixer(h, rope_a, rope_b, w_in[l], mla_q_norm[l], mla_w_uq[l], mla_kv_norm[l],
                             mla_w_ukv[l], gqa_q_norm[l], gqa_k_norm[l], gmlp_v_norm[l], gmlp_w_s[l],
                             gmlp_b_s[l], out_norm[l], w_out[l])
        h = rms_norm(x, mem_x_norm[l])
        x = x + memory_cross_attention(h, mem, mem_kv_norm[l], mem_w_q[l], mem_w_kv[l], mem_w_o[l])
        h = rms_norm(x, ffn_norm[l])
        x = x + conv_gated_ffn(h, ffn_w_up[l], ffn_conv_w[l], ffn_conv_b[l], ffn_w_down[l])
    return rms_norm(x, final_norm)
```

```python
import functools

import jax
import jax.numpy as jnp
from jax import lax
from jax.experimental import pallas as pl
from jax.experimental.pallas import tpu as pltpu

F32 = jnp.float32
BF16 = jnp.bfloat16

D_MODEL = 1024
GRID_W = 64
ROPE_THETA = 10000.0
EPS = 1e-6
MLA_HEADS = 6
MLA_NOPE = 64
MLA_ROPE = 32
MLA_V = 64
MLA_Q_RANK = 256
MLA_KV_RANK = 128
GQA_HEADS = 6
GQA_KV_HEADS = 2
GQA_DIM = 64
GMLP_GROUPS = 4
GMLP_DIM = 64
GMLP_CHUNK = 128
W_A = MLA_HEADS * MLA_V
W_B = GQA_HEADS * GQA_DIM
W_C = GMLP_GROUPS * GMLP_DIM
MEM_HEADS = 4
MEM_DIM = 128
D_FF = 2816

LANES = 128
HEAD_SLOT = LANES
TOK_TILE = 512
Q_TILE = 256
KV_TILE = 256
FF_CHUNK = 256
HALO = 16
VMEM_LIMIT = 56 * 1024 * 1024

Z_CQ, Z_CKV, Z_GQ, Z_GQS, Z_GK, Z_GKS, Z_GV, Z_U, Z_VV, Z_KR, Z_END = (
    0, 256, 384, 768, 1152, 1280, 1408, 1536, 1792, 2048, 2176)


def _rms(x, g):
    return x * lax.rsqrt(jnp.mean(x * x, axis=-1, keepdims=True) + EPS) * g


def _dot(a, b):
    return jnp.dot(a, b, preferred_element_type=F32)


def _const_spec(shape):
    zeros = (0,) * len(shape)
    return pl.BlockSpec(shape, lambda *_: zeros, pipeline_mode=pl.Buffered(1))


def _params(n_axes):
    return pltpu.CompilerParams(dimension_semantics=("arbitrary",) * n_axes,
                                vmem_limit_bytes=VMEM_LIMIT)


def _group_ssq(v, bd):
    sq = v * v
    hi = sq.astype(BF16)
    lo = (sq - hi.astype(F32)).astype(BF16)
    return _dot(hi, bd) + _dot(lo, bd)


def _mix_in_kernel(x_ref, g_ref, win_ref, gq_ref, wq_ref, gkv_ref, wk_ref, wv_ref,
                   cqa_ref, sqa_ref, tk_ref, tcq_ref, tsq_ref, tck_ref, tsk_ref,
                   bd_ref, gv_ref, ws_ref, bias_ref, gc_ref,
                   qaT_ref, ka_ref, vaT_ref, qbT_ref, kb_ref, vbT_ref, ync_ref):
    tok = x_ref.shape[1]
    h = _rms(x_ref[0], g_ref[...])
    z = _dot(h.astype(BF16), win_ref[...])

    cq = _rms(z[:, Z_CQ:Z_CKV], gq_ref[...]).astype(BF16)
    qa = _dot(cq, wq_ref[...])
    half = MLA_HEADS * HEAD_SLOT
    q_a = qa[:, :half] * cqa_ref[...] + qa[:, half:] * sqa_ref[...]
    qaT_ref[0] = q_a.T.astype(BF16)

    ckv = _rms(z[:, Z_CKV:Z_GQ], gkv_ref[...])
    kr = z[:, Z_KR:Z_END] * tk_ref[...]
    lhs = jnp.concatenate([ckv, kr], axis=1).astype(BF16)
    ka_ref[0] = _dot(lhs, wk_ref[...]).astype(BF16)
    v_aT = _dot(lhs[:, :MLA_KV_RANK], wv_ref[...]).T
    for n in range(tok // KV_TILE):
        vaT_ref[0, n] = v_aT[:, n * KV_TILE:(n + 1) * KV_TILE].astype(BF16)

    bd = bd_ref[...]
    g_q = z[:, Z_GQ:Z_GQS]
    n_q = lax.rsqrt(_group_ssq(g_q, bd) * (1.0 / GQA_DIM) + EPS)
    q_b = (g_q * tcq_ref[...] + z[:, Z_GQS:Z_GK] * tsq_ref[...]) * n_q
    q_bT = q_b.T
    zero = jnp.zeros((GQA_DIM, tok), BF16)
    for hd in range(GQA_HEADS):
        blk = q_bT[hd * GQA_DIM:(hd + 1) * GQA_DIM].astype(BF16)
        lo = hd * HEAD_SLOT + (hd // (GQA_HEADS // GQA_KV_HEADS)) * GQA_DIM
        other = hd * HEAD_SLOT + (1 - hd // (GQA_HEADS // GQA_KV_HEADS)) * GQA_DIM
        qbT_ref[0, lo:lo + GQA_DIM] = blk
        qbT_ref[0, other:other + GQA_DIM] = zero
    g_k = z[:, Z_GK:Z_GKS]
    n_k = lax.rsqrt(_group_ssq(g_k, bd[:LANES, :LANES]) * (1.0 / GQA_DIM) + EPS)
    kb_ref[0] = ((g_k * tck_ref[...] + z[:, Z_GKS:Z_GV] * tsk_ref[...]) * n_k).astype(BF16)
    v_bT = z[:, Z_GV:Z_U].T
    for n in range(tok // KV_TILE):
        vbT_ref[0, n] = v_bT[:, n * KV_TILE:(n + 1) * KV_TILE].astype(BF16)

    gm = jax.nn.gelu(z[:, Z_U:Z_KR])
    u = gm[:, :W_C]
    vv = _rms(gm[:, W_C:], gv_ref[...])
    lane_grp = lax.broadcasted_iota(jnp.int32, (GMLP_CHUNK, W_C), 1) // GMLP_DIM
    ws = ws_ref[...]
    bias = bias_ref[...]
    ycs = []
    for n in range(tok // GMLP_CHUNK):
        rows = slice(n * GMLP_CHUNK, (n + 1) * GMLP_CHUNK)
        r = _dot(ws, vv[rows].astype(BF16))
        mixed = r[3 * GMLP_CHUNK:]
        for grp in range(GMLP_GROUPS - 2, -1, -1):
            mixed = jnp.where(lane_grp == grp, r[grp * GMLP_CHUNK:(grp + 1) * GMLP_CHUNK], mixed)
        ycs.append(u[rows] * (mixed + bias))
    ync_ref[0] = _rms(jnp.concatenate(ycs, axis=0), gc_ref[...]).astype(BF16)


def _mix_in(x, p):
    B, S, D = x.shape
    T = TOK_TILE
    nkv = T // KV_TILE
    tile = lambda w: pl.BlockSpec((1, T, w), lambda s, b: (b, s, 0))
    tab = lambda w: pl.BlockSpec((T, w), lambda s, b: (s, 0))
    tposed = lambda r: pl.BlockSpec((1, r, T), lambda s, b: (b, 0, s))
    blocked = lambda r: pl.BlockSpec((1, nkv, r, KV_TILE), lambda s, b: (b, s, 0, 0))
    consts = [p["mix_norm"], p["w_in"], p["mla_q_norm"], p["w_uq"], p["mla_kv_norm"], p["w_k"], p["w_v"]]
    tabs = [p["cqa"], p["sqa"], p["tk"], p["tcq"], p["tsq"], p["tck"], p["tsk"]]
    consts2 = [p["bd"], p["gmlp_v_norm"], p["w_s"], p["bias_s"], p["out_norm_c"]]
    in_specs = ([tile(D)] + [_const_spec(c.shape) for c in consts] + [tab(t.shape[1]) for t in tabs]
                + [_const_spec(c.shape) for c in consts2])
    out_shape = [
        jax.ShapeDtypeStruct((B, MLA_HEADS * HEAD_SLOT, S), BF16),
        jax.ShapeDtypeStruct((B, S, MLA_HEADS * HEAD_SLOT), BF16),
        jax.ShapeDtypeStruct((B, S // KV_TILE, W_A, KV_TILE), BF16),
        jax.ShapeDtypeStruct((B, GQA_HEADS * HEAD_SLOT, S), BF16),
        jax.ShapeDtypeStruct((B, S, GQA_KV_HEADS * GQA_DIM), BF16),
        jax.ShapeDtypeStruct((B, S // KV_TILE, GQA_KV_HEADS * GQA_DIM, KV_TILE), BF16),
        jax.ShapeDtypeStruct((B, S, W_C), BF16),
    ]
    out_specs = [tposed(MLA_HEADS * HEAD_SLOT), tile(MLA_HEADS * HEAD_SLOT), blocked(W_A),
                 tposed(GQA_HEADS * HEAD_SLOT), tile(GQA_KV_HEADS * GQA_DIM),
                 blocked(GQA_KV_HEADS * GQA_DIM), tile(W_C)]
    return pl.pallas_call(
        _mix_in_kernel, grid=(S // T, B), in_specs=in_specs, out_specs=out_specs, out_shape=out_shape,
        compiler_params=_params(2), name="mix_in",
    )(x, *consts, *tabs, *consts2)


def _attn_kernel(q_ref, k_ref, v_ref, g_ref, o_ref, oT_ref, *, n_heads, k_lanes, v_rows, dv):
    mq = q_ref.shape[2]
    nkv = v_ref.shape[1]
    for hd in range(n_heads):
        qT = q_ref[0, hd * HEAD_SLOT:(hd + 1) * HEAD_SLOT, :]
        kl = k_lanes[hd]
        vr = v_rows[hd]

        def step(j, carry, qT=qT, kl=kl, vr=vr):
            m, l, acc = carry
            off = pl.multiple_of(j * KV_TILE, KV_TILE)
            k = k_ref[0, pl.ds(off, KV_TILE), kl:kl + HEAD_SLOT]
            sT = _dot(k, qT)
            m_new = jnp.maximum(m, jnp.max(sT, axis=0, keepdims=True))
            alpha = jnp.exp(m - m_new)
            p = jnp.exp(sT - m_new)
            l = alpha * l + jnp.sum(p, axis=0, keepdims=True)
            vT = v_ref[0, j, vr:vr + dv, :]
            acc = alpha * acc + _dot(vT, p.astype(BF16))
            return m_new, l, acc

        init = (jnp.full((1, mq), -1e30, F32), jnp.zeros((1, mq), F32), jnp.zeros((dv, mq), F32))
        _, l, acc = lax.fori_loop(0, nkv, step, init, unroll=2)
        oT_ref[hd * dv:(hd + 1) * dv, :] = acc / l
    o_ref[0] = _rms(oT_ref[...].T, g_ref[...]).astype(BF16)


def _attention(qT, k, vT, gain, *, n_heads, k_lanes, v_rows, dv, name):
    B, _, S = qT.shape
    kernel = functools.partial(_attn_kernel, n_heads=n_heads, k_lanes=k_lanes, v_rows=v_rows, dv=dv)
    whole = lambda shape: pl.BlockSpec((1,) + shape[1:], lambda b, i: (b,) + (0,) * (len(shape) - 1),
                                       pipeline_mode=pl.Buffered(1))
    return pl.pallas_call(
        kernel, grid=(B, S // Q_TILE),
        in_specs=[pl.BlockSpec((1, n_heads * HEAD_SLOT, Q_TILE), lambda b, i: (b, 0, i)),
                  whole(k.shape), whole(vT.shape), _const_spec(gain.shape)],
        out_specs=pl.BlockSpec((1, Q_TILE, n_heads * dv), lambda b, i: (b, i, 0)),
        out_shape=jax.ShapeDtypeStruct((B, S, n_heads * dv), BF16),
        scratch_shapes=[pltpu.VMEM((n_heads * dv, Q_TILE), F32)],
        compiler_params=_params(2), name=name,
    )(qT, k, vT, gain)


def _mem_kv_kernel(mem_ref, g_ref, w_ref, kT_ref, v_ref):
    kv = _dot(_rms(mem_ref[0], g_ref[...]).astype(BF16), w_ref[...])
    width = MEM_HEADS * MEM_DIM
    kT_ref[0] = kv[:, :width].T.astype(BF16)
    ones = jnp.ones((kv.shape[0], MEM_DIM), BF16)
    for hd in range(MEM_HEADS):
        v_ref[0, hd, :, :MEM_DIM] = kv[:, width + hd * MEM_DIM:width + (hd + 1) * MEM_DIM].astype(BF16)
        v_ref[0, hd, :, MEM_DIM:] = ones


def _mem_kv(mem, gain, w_kv):
    B, Tm, D = mem.shape
    width = MEM_HEADS * MEM_DIM
    return pl.pallas_call(
        _mem_kv_kernel, grid=(B,),
        in_specs=[pl.BlockSpec((1, Tm, D), lambda b: (b, 0, 0)), _const_spec(gain.shape), _const_spec(w_kv.shape)],
        out_specs=[pl.BlockSpec((1, width, Tm), lambda b: (b, 0, 0)),
                   pl.BlockSpec((1, MEM_HEADS, Tm, 2 * MEM_DIM), lambda b: (b, 0, 0, 0))],
        out_shape=[jax.ShapeDtypeStruct((B, width, Tm), BF16),
                   jax.ShapeDtypeStruct((B, MEM_HEADS, Tm, 2 * MEM_DIM), BF16)],
        compiler_params=_params(1), name="mem_kv",
    )(mem, gain, w_kv)


def _out_mem_kernel(ya_ref, yb_ref, yc_ref, x_ref, wout_ref, g_ref, wq_ref, kT_ref, v_ref, wo_ref, o_ref):
    y = (_dot(ya_ref[0], wout_ref[0:W_A]) + _dot(yb_ref[0], wout_ref[W_A:W_A + W_B])
         + _dot(yc_ref[0], wout_ref[W_A + W_B:]))
    x1 = x_ref[0] + y
    h = _rms(x1, g_ref[...]).astype(BF16)
    q = (_dot(h, wq_ref[...]) * (MEM_DIM ** -0.5)).astype(BF16)
    heads = []
    for hd in range(MEM_HEADS):
        s = _dot(q[:, hd * MEM_DIM:(hd + 1) * MEM_DIM], kT_ref[0, hd * MEM_DIM:(hd + 1) * MEM_DIM, :])
        p = jnp.exp(s - jnp.max(s, axis=-1, keepdims=True)).astype(BF16)
        pv = _dot(p, v_ref[0, hd])
        heads.append(pv[:, :MEM_DIM] / pv[:, MEM_DIM:])
    o = jnp.concatenate(heads, axis=1).astype(BF16)
    o_ref[0] = x1 + _dot(o, wo_ref[...])


def _out_mem(ya, yb, yc, x, p, mem_kT, mem_v):
    B, S, D = x.shape
    T = TOK_TILE
    tile = lambda w: pl.BlockSpec((1, T, w), lambda b, s: (b, s, 0))
    per_b = lambda a: pl.BlockSpec((1,) + a.shape[1:], lambda b, s: (b,) + (0,) * (a.ndim - 1))
    consts = [p["w_out"], p["mem_x_norm"], p["mem_w_q"]]
    return pl.pallas_call(
        _out_mem_kernel, grid=(B, S // T),
        in_specs=[tile(W_A), tile(W_B), tile(W_C), tile(D)] + [_const_spec(c.shape) for c in consts]
                 + [per_b(mem_kT), per_b(mem_v), _const_spec(p["mem_w_o"].shape)],
        out_specs=tile(D), out_shape=jax.ShapeDtypeStruct((B, S, D), F32),
        compiler_params=_params(2), name="out_mem",
    )(ya, yb, yc, x, *consts, mem_kT, mem_v, p["mem_w_o"])


def _ffn_kernel(x_ref, xp_ref, xn_ref, g_ref, wup_ref, cw_ref, cb_ref, wdn_ref, fg_ref, o_ref,
                h_ref, act_ref, *, final):
    tok = x_ref.shape[1]
    i = pl.program_id(1)
    g = g_ref[...]
    x = x_ref[0]
    keep_prev = (i > 0).astype(F32)
    keep_next = (i < pl.num_programs(1) - 1).astype(F32)
    h_ref[0:HALO] = (_rms(xp_ref[0], g) * keep_prev).astype(BF16)
    h_ref[HALO:HALO + tok] = _rms(x, g).astype(BF16)
    h_ref[HALO + tok:] = (_rms(xn_ref[0], g) * keep_next).astype(BF16)
    hext = h_ref[...]
    rows = tok + 2 * HALO
    for c in range(D_FF // FF_CHUNK):
        cols = slice(2 * FF_CHUNK * c, 2 * FF_CHUNK * (c + 1))
        a = _dot(hext, wup_ref[:, cols])
        w = cw_ref[:, cols]
        ac = (pltpu.roll(a, 1, 0)[HALO:HALO + tok] * w[0:1] + a[HALO:HALO + tok] * w[1:2]
              + pltpu.roll(a, rows - 1, 0)[HALO:HALO + tok] * w[2:3] + cb_ref[:, cols])
        act = jax.nn.silu(ac[:, :FF_CHUNK]) * ac[:, FF_CHUNK:]
        act_ref[:, FF_CHUNK * c:FF_CHUNK * (c + 1)] = act.astype(BF16)
    y = x + _dot(act_ref[...], wdn_ref[...])
    if final:
        y = _rms(y, fg_ref[...])
    o_ref[0] = y


def _ffn(x, p, final_gain, *, final):
    B, S, D = x.shape
    T = TOK_TILE
    per_tile = T // HALO
    n_halo = S // HALO
    tile = pl.BlockSpec((1, T, D), lambda b, s: (b, s, 0))
    prev = pl.BlockSpec((1, HALO, D), lambda b, s: (b, jnp.maximum(s * per_tile - 1, 0), 0))
    nxt = pl.BlockSpec((1, HALO, D), lambda b, s: (b, jnp.minimum((s + 1) * per_tile, n_halo - 1), 0))
    consts = [p["ffn_norm"], p["ffn_w_up"], p["ffn_conv_w"], p["ffn_conv_b"], p["ffn_w_down"], final_gain]
    return pl.pallas_call(
        functools.partial(_ffn_kernel, final=final), grid=(B, S // T),
        in_specs=[tile, prev, nxt] + [_const_spec(c.shape) for c in consts],
        out_specs=tile, out_shape=jax.ShapeDtypeStruct((B, S, D), F32),
        scratch_shapes=[pltpu.VMEM((T + 2 * HALO, D), BF16), pltpu.VMEM((T, D_FF), BF16)],
        compiler_params=_params(2), name="ffn_final" if final else "ffn",
    )(x, x, x, *consts)


def _swap_pairs(w):
    n = w.shape[-1]
    return w.reshape(*w.shape[:-1], n // 2, 2)[..., ::-1].reshape(w.shape)


def _rope_tables(S, d_rot):
    rows = S // GRID_W
    row = jnp.repeat(jnp.arange(rows, dtype=F32), GRID_W)
    col = jnp.tile(jnp.arange(GRID_W, dtype=F32), rows)
    n = d_rot // 4
    inv = ROPE_THETA ** (-jnp.arange(n, dtype=F32) / n)
    ang = jnp.concatenate([row[:, None] * inv, col[:, None] * inv], axis=-1)
    cos, sin = jnp.cos(ang), jnp.sin(ang)
    c = jnp.repeat(cos, 2, axis=-1)
    s = jnp.stack([-sin, sin], axis=-1).reshape(S, d_rot)
    return c, s


def _layer_params(l, S, rope_a, rope_b, P):
    row = lambda v: v.reshape(1, -1)
    ca, sa = rope_a
    cb, sb = rope_b
    p = {}
    w_in = P["w_in"][l]
    c_q, c_kv, k_rope, g_q, g_k, g_v, g_m = jnp.split(
        w_in, [256, 384, 416, 800, 928, 1056], axis=1)
    pad = jnp.zeros((D_MODEL, Z_END - Z_KR - 2 * MLA_ROPE), F32)
    p["w_in"] = jnp.concatenate(
        [c_q, c_kv, g_q, _swap_pairs(g_q), g_k, _swap_pairs(g_k), g_v, g_m, k_rope, _swap_pairs(k_rope), pad],
        axis=1).astype(BF16)
    p["mix_norm"] = row(P["mix_norm"][l])
    p["mla_q_norm"] = row(P["mla_q_norm"][l])
    p["mla_kv_norm"] = row(P["mla_kv_norm"][l])

    w_uq = P["mla_w_uq"][l].reshape(MLA_Q_RANK, MLA_HEADS, MLA_NOPE + MLA_ROPE)
    zpad = jnp.zeros((MLA_Q_RANK, MLA_HEADS, HEAD_SLOT - MLA_NOPE - MLA_ROPE), F32)
    main = jnp.concatenate([w_uq, zpad], axis=-1)
    swapped = jnp.concatenate([jnp.zeros_like(w_uq[..., :MLA_NOPE]), _swap_pairs(w_uq[..., MLA_NOPE:]), zpad], axis=-1)
    p["w_uq"] = jnp.concatenate([main.reshape(MLA_Q_RANK, -1), swapped.reshape(MLA_Q_RANK, -1)], axis=1).astype(BF16)
    scale_a = (MLA_NOPE + MLA_ROPE) ** -0.5
    ones = jnp.ones((S, MLA_NOPE), F32)
    zeros_n = jnp.zeros((S, MLA_NOPE), F32)
    zeros_p = jnp.zeros((S, HEAD_SLOT - MLA_NOPE - MLA_ROPE), F32)
    p["cqa"] = jnp.tile(jnp.concatenate([ones, ca, zeros_p], axis=1) * scale_a, (1, MLA_HEADS))
    p["sqa"] = jnp.tile(jnp.concatenate([zeros_n, sa, zeros_p], axis=1) * scale_a, (1, MLA_HEADS))

    w_ukv = P["mla_w_ukv"][l].reshape(MLA_KV_RANK, MLA_HEADS, MLA_NOPE + MLA_V)
    k_lat = jnp.concatenate(
        [w_ukv[..., :MLA_NOPE], jnp.zeros((MLA_KV_RANK, MLA_HEADS, HEAD_SLOT - MLA_NOPE), F32)], axis=-1)
    place = jnp.concatenate([jnp.zeros((MLA_ROPE, MLA_NOPE), F32), jnp.eye(MLA_ROPE, dtype=F32),
                             jnp.zeros((MLA_ROPE, HEAD_SLOT - MLA_NOPE - MLA_ROPE), F32)], axis=1)
    place = jnp.tile(place, (1, MLA_HEADS))
    zrows = jnp.zeros((Z_END - Z_KR - 2 * MLA_ROPE, MLA_HEADS * HEAD_SLOT), F32)
    p["w_k"] = jnp.concatenate([k_lat.reshape(MLA_KV_RANK, -1), place, place, zrows], axis=0).astype(BF16)
    p["w_v"] = w_ukv[..., MLA_NOPE:].reshape(MLA_KV_RANK, -1).astype(BF16)
    p["tk"] = jnp.concatenate([ca, sa, jnp.zeros((S, Z_END - Z_KR - 2 * MLA_ROPE), F32)], axis=1)

    gq = jnp.tile(P["gqa_q_norm"][l], GQA_HEADS) * (GQA_DIM ** -0.5)
    gk = jnp.tile(P["gqa_k_norm"][l], GQA_KV_HEADS)
    p["tcq"] = jnp.tile(cb, (1, GQA_HEADS)) * gq
    p["tsq"] = jnp.tile(sb, (1, GQA_HEADS)) * _swap_pairs(gq)
    p["tck"] = jnp.tile(cb, (1, GQA_KV_HEADS)) * gk
    p["tsk"] = jnp.tile(sb, (1, GQA_KV_HEADS)) * _swap_pairs(gk)
    grp = jnp.arange(W_B) // GQA_DIM
    p["bd"] = (grp[:, None] == grp[None, :]).astype(BF16)

    p["gmlp_v_norm"] = row(P["gmlp_v_norm"][l])
    p["w_s"] = P["gmlp_w_s"][l].reshape(GMLP_GROUPS * GMLP_CHUNK, GMLP_CHUNK).astype(BF16)
    p["bias_s"] = jnp.repeat(P["gmlp_b_s"][l].T, GMLP_DIM, axis=1)
    out_norm = P["out_norm"][l]
    p["out_norm_a"] = row(out_norm[:W_A])
    p["out_norm_b"] = row(out_norm[W_A:W_A + W_B])
    p["out_norm_c"] = row(out_norm[W_A + W_B:])
    p["w_out"] = P["w_out"][l].astype(BF16)

    p["mem_x_norm"] = row(P["mem_x_norm"][l])
    p["mem_kv_norm"] = row(P["mem_kv_norm"][l])
    p["mem_w_q"] = P["mem_w_q"][l].astype(BF16)
    p["mem_w_kv"] = P["mem_w_kv"][l].astype(BF16)
    p["mem_w_o"] = P["mem_w_o"][l].astype(BF16)

    n_chunks = D_FF // FF_CHUNK
    def interleave(w):
        lead = w.shape[:-1]
        gate = w[..., :D_FF].reshape(*lead, n_chunks, 1, FF_CHUNK)
        val = w[..., D_FF:].reshape(*lead, n_chunks, 1, FF_CHUNK)
        return jnp.concatenate([gate, val], axis=-2).reshape(*lead, 2 * D_FF)
    p["ffn_norm"] = row(P["ffn_norm"][l])
    p["ffn_w_up"] = interleave(P["ffn_w_up"][l]).astype(BF16)
    p["ffn_conv_w"] = interleave(P["ffn_conv_w"][l])
    p["ffn_conv_b"] = interleave(row(P["ffn_conv_b"][l]))
    p["ffn_w_down"] = P["ffn_w_down"][l].astype(BF16)
    return p


def kernel(x, mem, mix_norm, w_in, mla_q_norm, mla_w_uq, mla_kv_norm, mla_w_ukv, gqa_q_norm, gqa_k_norm, gmlp_v_norm, gmlp_w_s, gmlp_b_s, out_norm, w_out, mem_x_norm, mem_kv_norm, mem_w_q, mem_w_kv, mem_w_o, ffn_norm, ffn_w_up, ffn_conv_w, ffn_conv_b, ffn_w_down, final_norm):
    P = dict(mix_norm=mix_norm, w_in=w_in, mla_q_norm=mla_q_norm, mla_w_uq=mla_w_uq, mla_kv_norm=mla_kv_norm,
             mla_w_ukv=mla_w_ukv, gqa_q_norm=gqa_q_norm, gqa_k_norm=gqa_k_norm, gmlp_v_norm=gmlp_v_norm,
             gmlp_w_s=gmlp_w_s, gmlp_b_s=gmlp_b_s, out_norm=out_norm, w_out=w_out, mem_x_norm=mem_x_norm,
             mem_kv_norm=mem_kv_norm, mem_w_q=mem_w_q, mem_w_kv=mem_w_kv, mem_w_o=mem_w_o, ffn_norm=ffn_norm,
             ffn_w_up=ffn_w_up, ffn_conv_w=ffn_conv_w, ffn_conv_b=ffn_conv_b, ffn_w_down=ffn_w_down)
    B, S, D = x.shape
    assert D == D_MODEL and S % TOK_TILE == 0 and S % GRID_W == 0
    depth = w_in.shape[0]
    rope_a = _rope_tables(S, MLA_ROPE)
    rope_b = _rope_tables(S, GQA_DIM)
    final_gain = final_norm.reshape(1, -1)
    group = GQA_HEADS // GQA_KV_HEADS
    for l in range(depth):
        p = _layer_params(l, S, rope_a, rope_b, P)
        qaT, ka, vaT, qbT, kb, vbT, ync = _mix_in(x, p)
        yna = _attention(qaT, ka, vaT, p["out_norm_a"], n_heads=MLA_HEADS,
                         k_lanes=tuple(h * HEAD_SLOT for h in range(MLA_HEADS)),
                         v_rows=tuple(h * MLA_V for h in range(MLA_HEADS)), dv=MLA_V, name="attn_mla")
        ynb = _attention(qbT, kb, vbT, p["out_norm_b"], n_heads=GQA_HEADS,
                         k_lanes=(0,) * GQA_HEADS,
                         v_rows=tuple((h // group) * GQA_DIM for h in range(GQA_HEADS)), dv=GQA_DIM, name="attn_gqa")
        mem_kT, mem_v = _mem_kv(mem, p["mem_kv_norm"], p["mem_w_kv"])
        x = _out_mem(yna, ynb, ync, x, p, mem_kT, mem_v)
        x = _ffn(x, p, final_gain, final=(l == depth - 1))
    return x
```

```python
import functools

import jax
import jax.numpy as jnp
from jax import lax
from jax.experimental import pallas as pl
from jax.experimental.pallas import tpu as pltpu

F32 = jnp.float32
BF16 = jnp.bfloat16

D_MODEL = 1024
GRID_W = 64
ROPE_THETA = 10000.0
EPS = 1e-6
MLA_HEADS = 6
MLA_NOPE = 64
MLA_ROPE = 32
MLA_V = 64
MLA_Q_RANK = 256
MLA_KV_RANK = 128
GQA_HEADS = 6
GQA_KV_HEADS = 2
GQA_DIM = 64
GMLP_GROUPS = 4
GMLP_DIM = 64
GMLP_CHUNK = 128
W_A = MLA_HEADS * MLA_V
W_B = GQA_HEADS * GQA_DIM
W_C = GMLP_GROUPS * GMLP_DIM
MEM_HEADS = 4
MEM_DIM = 128
D_FF = 2816

LANES = 128
HEAD_SLOT = LANES
TOK_TILE = 512
Q_TILE = 512
KV_TILE = 512
V_PAD = 16
LOG2E = 1.4426950408889634
FF_CHUNK = 256
HALO = 16
VMEM_LIMIT = 56 * 1024 * 1024

Z_CQ, Z_CKV, Z_GQ, Z_GQS, Z_GK, Z_GKS, Z_GV, Z_U, Z_VV, Z_KR, Z_END = (
    0, 256, 384, 768, 1152, 1280, 1408, 1536, 1792, 2048, 2176)


def _rms(x, g):
    return x * lax.rsqrt(jnp.mean(x * x, axis=-1, keepdims=True) + EPS) * g


def _dot(a, b):
    return jnp.dot(a, b, preferred_element_type=F32)


def _const_spec(shape):
    zeros = (0,) * len(shape)
    return pl.BlockSpec(shape, lambda *_: zeros, pipeline_mode=pl.Buffered(1))


def _params(n_axes):
    return pltpu.CompilerParams(dimension_semantics=("arbitrary",) * n_axes,
                                vmem_limit_bytes=VMEM_LIMIT)


def _group_ssq(v, bd):
    sq = v * v
    hi = sq.astype(BF16)
    lo = (sq - hi.astype(F32)).astype(BF16)
    return _dot(hi, bd) + _dot(lo, bd)


def _store_vT(ref, vT, n_heads, dv):
    ext = (lax.broadcasted_iota(jnp.int32, (V_PAD, KV_TILE), 0) == 0).astype(BF16)
    for n in range(vT.shape[1] // KV_TILE):
        for hd in range(n_heads):
            base = hd * (dv + V_PAD)
            ref[0, n, base:base + dv] = vT[hd * dv:(hd + 1) * dv, n * KV_TILE:(n + 1) * KV_TILE].astype(BF16)
            ref[0, n, base + dv:base + dv + V_PAD] = ext


def _mix_in_kernel(x_ref, g_ref, win_ref, gq_ref, wq_ref, gkv_ref, wk_ref, wv_ref,
                   cqa_ref, sqa_ref, tk_ref, tcq_ref, tsq_ref, tck_ref, tsk_ref,
                   bd_ref, gv_ref, ws_ref, bias_ref, gc_ref,
                   qaT_ref, ka_ref, vaT_ref, qbT_ref, kb_ref, vbT_ref, ync_ref):
    tok = x_ref.shape[1]
    h = _rms(x_ref[0], g_ref[...])
    z = _dot(h.astype(BF16), win_ref[...])

    cq = _rms(z[:, Z_CQ:Z_CKV], gq_ref[...]).astype(BF16)
    qa = _dot(cq, wq_ref[...])
    half = MLA_HEADS * HEAD_SLOT
    q_a = qa[:, :half] * cqa_ref[...] + qa[:, half:] * sqa_ref[...]
    qaT_ref[0] = q_a.T.astype(BF16)

    ckv = _rms(z[:, Z_CKV:Z_GQ], gkv_ref[...])
    kr = z[:, Z_KR:Z_END] * tk_ref[...]
    lhs = jnp.concatenate([ckv, kr], axis=1).astype(BF16)
    ka_ref[0] = _dot(lhs, wk_ref[...]).astype(BF16)
    _store_vT(vaT_ref, _dot(lhs[:, :MLA_KV_RANK], wv_ref[...]).T, MLA_HEADS, MLA_V)

    bd = bd_ref[...]
    g_q = z[:, Z_GQ:Z_GQS]
    n_q = lax.rsqrt(_group_ssq(g_q, bd) * (1.0 / GQA_DIM) + EPS)
    q_b = (g_q * tcq_ref[...] + z[:, Z_GQS:Z_GK] * tsq_ref[...]) * n_q
    q_bT = q_b.T
    zero = jnp.zeros((GQA_DIM, tok), BF16)
    for hd in range(GQA_HEADS):
        blk = q_bT[hd * GQA_DIM:(hd + 1) * GQA_DIM].astype(BF16)
        lo = hd * HEAD_SLOT + (hd // (GQA_HEADS // GQA_KV_HEADS)) * GQA_DIM
        other = hd * HEAD_SLOT + (1 - hd // (GQA_HEADS // GQA_KV_HEADS)) * GQA_DIM
        qbT_ref[0, lo:lo + GQA_DIM] = blk
        qbT_ref[0, other:other + GQA_DIM] = zero
    g_k = z[:, Z_GK:Z_GKS]
    n_k = lax.rsqrt(_group_ssq(g_k, bd[:LANES, :LANES]) * (1.0 / GQA_DIM) + EPS)
    kb_ref[0] = ((g_k * tck_ref[...] + z[:, Z_GKS:Z_GV] * tsk_ref[...]) * n_k).astype(BF16)
    _store_vT(vbT_ref, z[:, Z_GV:Z_U].T, GQA_KV_HEADS, GQA_DIM)

    gm = jax.nn.gelu(z[:, Z_U:Z_KR])
    u = gm[:, :W_C]
    vv = _rms(gm[:, W_C:], gv_ref[...])
    lane_grp = lax.broadcasted_iota(jnp.int32, (GMLP_CHUNK, W_C), 1) // GMLP_DIM
    ws = ws_ref[...]
    bias = bias_ref[...]
    ycs = []
    for n in range(tok // GMLP_CHUNK):
        rows = slice(n * GMLP_CHUNK, (n + 1) * GMLP_CHUNK)
        r = _dot(ws, vv[rows].astype(BF16))
        mixed = r[3 * GMLP_CHUNK:]
        for grp in range(GMLP_GROUPS - 2, -1, -1):
            mixed = jnp.where(lane_grp == grp, r[grp * GMLP_CHUNK:(grp + 1) * GMLP_CHUNK], mixed)
        ycs.append(u[rows] * (mixed + bias))
    ync_ref[0] = _rms(jnp.concatenate(ycs, axis=0), gc_ref[...]).astype(BF16)


def _mix_in(x, p):
    B, S, D = x.shape
    T = TOK_TILE
    nkv = T // KV_TILE
    tile = lambda w: pl.BlockSpec((1, T, w), lambda s, b: (b, s, 0))
    tab = lambda w: pl.BlockSpec((T, w), lambda s, b: (s, 0))
    tposed = lambda r: pl.BlockSpec((1, r, T), lambda s, b: (b, 0, s))
    blocked = lambda r: pl.BlockSpec((1, nkv, r, KV_TILE), lambda s, b: (b, s, 0, 0))
    consts = [p["mix_norm"], p["w_in"], p["mla_q_norm"], p["w_uq"], p["mla_kv_norm"], p["w_k"], p["w_v"]]
    tabs = [p["cqa"], p["sqa"], p["tk"], p["tcq"], p["tsq"], p["tck"], p["tsk"]]
    consts2 = [p["bd"], p["gmlp_v_norm"], p["w_s"], p["bias_s"], p["out_norm_c"]]
    in_specs = ([tile(D)] + [_const_spec(c.shape) for c in consts] + [tab(t.shape[1]) for t in tabs]
                + [_const_spec(c.shape) for c in consts2])
    out_shape = [
        jax.ShapeDtypeStruct((B, MLA_HEADS * HEAD_SLOT, S), BF16),
        jax.ShapeDtypeStruct((B, S, MLA_HEADS * HEAD_SLOT), BF16),
        jax.ShapeDtypeStruct((B, S // KV_TILE, MLA_HEADS * (MLA_V + V_PAD), KV_TILE), BF16),
        jax.ShapeDtypeStruct((B, GQA_HEADS * HEAD_SLOT, S), BF16),
        jax.ShapeDtypeStruct((B, S, GQA_KV_HEADS * GQA_DIM), BF16),
        jax.ShapeDtypeStruct((B, S // KV_TILE, GQA_KV_HEADS * (GQA_DIM + V_PAD), KV_TILE), BF16),
        jax.ShapeDtypeStruct((B, S, W_C), BF16),
    ]
    out_specs = [tposed(MLA_HEADS * HEAD_SLOT), tile(MLA_HEADS * HEAD_SLOT), blocked(MLA_HEADS * (MLA_V + V_PAD)),
                 tposed(GQA_HEADS * HEAD_SLOT), tile(GQA_KV_HEADS * GQA_DIM),
                 blocked(GQA_KV_HEADS * (GQA_DIM + V_PAD)), tile(W_C)]
    return pl.pallas_call(
        _mix_in_kernel, grid=(S // T, B), in_specs=in_specs, out_specs=out_specs, out_shape=out_shape,
        compiler_params=_params(2), name="mix_in",
    )(x, *consts, *tabs, *consts2)


def _attn_kernel(q_ref, k_ref, v_ref, g_ref, o_ref, s_ref, p_ref, m_ref, acc_ref, oT_ref, *,
                 n_heads, k_lanes, v_rows, dv):
    nkv = v_ref.shape[1]
    m_ref[...] = jnp.full(m_ref.shape, -1e30, F32)
    acc_ref[...] = jnp.zeros(acc_ref.shape, F32)

    def step(j, carry):
        off = pl.multiple_of(j * KV_TILE, KV_TILE)
        for hd in range(n_heads):
            qT = q_ref[0, hd * HEAD_SLOT:(hd + 1) * HEAD_SLOT, :]
            k = k_ref[0, pl.ds(off, KV_TILE), k_lanes[hd]:k_lanes[hd] + HEAD_SLOT]
            s_ref[hd] = _dot(k, qT)
            m_old = m_ref[hd]
            m_new = jnp.maximum(m_old, jnp.max(s_ref[hd], axis=0, keepdims=True))
            m_ref[hd] = m_new
            p_ref[hd] = jnp.exp2(s_ref[hd] - m_new).astype(BF16)
            vT = v_ref[0, j, v_rows[hd]:v_rows[hd] + dv + V_PAD, :]
            acc_ref[hd] = jnp.exp2(m_old - m_new) * acc_ref[hd] + _dot(vT, p_ref[hd])
        return carry

    lax.fori_loop(0, nkv, step, 0)
    for hd in range(n_heads):
        oT_ref[hd * dv:(hd + 1) * dv, :] = acc_ref[hd, :dv] / acc_ref[hd, dv:dv + 1]
    o_ref[0] = _rms(oT_ref[...].T, g_ref[...]).astype(BF16)


def _attention(qT, k, vT, gain, *, n_heads, k_lanes, v_rows, dv, name):
    B, _, S = qT.shape
    kernel = functools.partial(_attn_kernel, n_heads=n_heads, k_lanes=k_lanes, v_rows=v_rows, dv=dv)
    whole = lambda shape: pl.BlockSpec((1,) + shape[1:], lambda b, i: (b,) + (0,) * (len(shape) - 1),
                                       pipeline_mode=pl.Buffered(1))
    return pl.pallas_call(
        kernel, grid=(B, S // Q_TILE),
        in_specs=[pl.BlockSpec((1, n_heads * HEAD_SLOT, Q_TILE), lambda b, i: (b, 0, i)),
                  whole(k.shape), whole(vT.shape), _const_spec(gain.shape)],
        out_specs=pl.BlockSpec((1, Q_TILE, n_heads * dv), lambda b, i: (b, i, 0)),
        out_shape=jax.ShapeDtypeStruct((B, S, n_heads * dv), BF16),
        scratch_shapes=[pltpu.VMEM((n_heads, KV_TILE, Q_TILE), F32), pltpu.VMEM((n_heads, KV_TILE, Q_TILE), BF16),
                        pltpu.VMEM((n_heads, 1, Q_TILE), F32), pltpu.VMEM((n_heads, dv + V_PAD, Q_TILE), F32),
                        pltpu.VMEM((n_heads * dv, Q_TILE), F32)],
        compiler_params=_params(2), name=name,
    )(qT, k, vT, gain)


def _mem_kv_kernel(mem_ref, g_ref, w_ref, kT_ref, v_ref):
    kv = _dot(_rms(mem_ref[0], g_ref[...]).astype(BF16), w_ref[...])
    width = MEM_HEADS * MEM_DIM
    kT_ref[0] = kv[:, :width].T.astype(BF16)
    ones = jnp.ones((kv.shape[0], MEM_DIM), BF16)
    for hd in range(MEM_HEADS):
        v_ref[0, hd, :, :MEM_DIM] = kv[:, width + hd * MEM_DIM:width + (hd + 1) * MEM_DIM].astype(BF16)
        v_ref[0, hd, :, MEM_DIM:] = ones


def _mem_kv(mem, gain, w_kv):
    B, Tm, D = mem.shape
    width = MEM_HEADS * MEM_DIM
    return pl.pallas_call(
        _mem_kv_kernel, grid=(B,),
        in_specs=[pl.BlockSpec((1, Tm, D), lambda b: (b, 0, 0)), _const_spec(gain.shape), _const_spec(w_kv.shape)],
        out_specs=[pl.BlockSpec((1, width, Tm), lambda b: (b, 0, 0)),
                   pl.BlockSpec((1, MEM_HEADS, Tm, 2 * MEM_DIM), lambda b: (b, 0, 0, 0))],
        out_shape=[jax.ShapeDtypeStruct((B, width, Tm), BF16),
                   jax.ShapeDtypeStruct((B, MEM_HEADS, Tm, 2 * MEM_DIM), BF16)],
        compiler_params=_params(1), name="mem_kv",
    )(mem, gain, w_kv)


def _out_mem_kernel(ya_ref, yb_ref, yc_ref, x_ref, wout_ref, g_ref, wq_ref, kT_ref, v_ref, wo_ref, o_ref):
    y = (_dot(ya_ref[0], wout_ref[0:W_A]) + _dot(yb_ref[0], wout_ref[W_A:W_A + W_B])
         + _dot(yc_ref[0], wout_ref[W_A + W_B:]))
    x1 = x_ref[0] + y
    h = _rms(x1, g_ref[...]).astype(BF16)
    q = (_dot(h, wq_ref[...]) * (MEM_DIM ** -0.5)).astype(BF16)
    heads = []
    for hd in range(MEM_HEADS):
        s = _dot(q[:, hd * MEM_DIM:(hd + 1) * MEM_DIM], kT_ref[0, hd * MEM_DIM:(hd + 1) * MEM_DIM, :])
        p = jnp.exp(s - jnp.max(s, axis=-1, keepdims=True)).astype(BF16)
        pv = _dot(p, v_ref[0, hd])
        heads.append(pv[:, :MEM_DIM] / pv[:, MEM_DIM:])
    o = jnp.concatenate(heads, axis=1).astype(BF16)
    o_ref[0] = x1 + _dot(o, wo_ref[...])


def _out_mem(ya, yb, yc, x, p, mem_kT, mem_v):
    B, S, D = x.shape
    T = TOK_TILE
    tile = lambda w: pl.BlockSpec((1, T, w), lambda b, s: (b, s, 0))
    per_b = lambda a: pl.BlockSpec((1,) + a.shape[1:], lambda b, s: (b,) + (0,) * (a.ndim - 1))
    consts = [p["w_out"], p["mem_x_norm"], p["mem_w_q"]]
    return pl.pallas_call(
        _out_mem_kernel, grid=(B, S // T),
        in_specs=[tile(W_A), tile(W_B), tile(W_C), tile(D)] + [_const_spec(c.shape) for c in consts]
                 + [per_b(mem_kT), per_b(mem_v), _const_spec(p["mem_w_o"].shape)],
        out_specs=tile(D), out_shape=jax.ShapeDtypeStruct((B, S, D), F32),
        compiler_params=_params(2), name="out_mem",
    )(ya, yb, yc, x, *consts, mem_kT, mem_v, p["mem_w_o"])


def _ffn_kernel(x_ref, xp_ref, xn_ref, g_ref, wup_ref, cw_ref, cb_ref, wdn_ref, fg_ref, o_ref,
                h_ref, act_ref, *, final):
    tok = x_ref.shape[1]
    i = pl.program_id(1)
    g = g_ref[...]
    x = x_ref[0]
    keep_prev = (i > 0).astype(F32)
    keep_next = (i < pl.num_programs(1) - 1).astype(F32)
    h_ref[0:HALO] = (_rms(xp_ref[0], g) * keep_prev).astype(BF16)
    h_ref[HALO:HALO + tok] = _rms(x, g).astype(BF16)
    h_ref[HALO + tok:] = (_rms(xn_ref[0], g) * keep_next).astype(BF16)
    hext = h_ref[...]
    rows = tok + 2 * HALO
    for c in range(D_FF // FF_CHUNK):
        cols = slice(2 * FF_CHUNK * c, 2 * FF_CHUNK * (c + 1))
        a = _dot(hext, wup_ref[:, cols])
        w = cw_ref[:, cols]
        ac = (pltpu.roll(a, 1, 0)[HALO:HALO + tok] * w[0:1] + a[HALO:HALO + tok] * w[1:2]
              + pltpu.roll(a, rows - 1, 0)[HALO:HALO + tok] * w[2:3] + cb_ref[:, cols])
        act = jax.nn.silu(ac[:, :FF_CHUNK]) * ac[:, FF_CHUNK:]
        act_ref[:, FF_CHUNK * c:FF_CHUNK * (c + 1)] = act.astype(BF16)
    y = x + _dot(act_ref[...], wdn_ref[...])
    if final:
        y = _rms(y, fg_ref[...])
    o_ref[0] = y


def _ffn(x, p, final_gain, *, final):
    B, S, D = x.shape
    T = TOK_TILE
    per_tile = T // HALO
    n_halo = S // HALO
    tile = pl.BlockSpec((1, T, D), lambda b, s: (b, s, 0))
    prev = pl.BlockSpec((1, HALO, D), lambda b, s: (b, jnp.maximum(s * per_tile - 1, 0), 0))
    nxt = pl.BlockSpec((1, HALO, D), lambda b, s: (b, jnp.minimum((s + 1) * per_tile, n_halo - 1), 0))
    consts = [p["ffn_norm"], p["ffn_w_up"], p["ffn_conv_w"], p["ffn_conv_b"], p["ffn_w_down"], final_gain]
    return pl.pallas_call(
        functools.partial(_ffn_kernel, final=final), grid=(B, S // T),
        in_specs=[tile, prev, nxt] + [_const_spec(c.shape) for c in consts],
        out_specs=tile, out_shape=jax.ShapeDtypeStruct((B, S, D), F32),
        scratch_shapes=[pltpu.VMEM((T + 2 * HALO, D), BF16), pltpu.VMEM((T, D_FF), BF16)],
        compiler_params=_params(2), name="ffn_final" if final else "ffn",
    )(x, x, x, *consts)


def _swap_pairs(w):
    n = w.shape[-1]
    return w.reshape(*w.shape[:-1], n // 2, 2)[..., ::-1].reshape(w.shape)


def _rope_tables(S, d_rot):
    rows = S // GRID_W
    row = jnp.repeat(jnp.arange(rows, dtype=F32), GRID_W)
    col = jnp.tile(jnp.arange(GRID_W, dtype=F32), rows)
    n = d_rot // 4
    inv = ROPE_THETA ** (-jnp.arange(n, dtype=F32) / n)
    ang = jnp.concatenate([row[:, None] * inv, col[:, None] * inv], axis=-1)
    cos, sin = jnp.cos(ang), jnp.sin(ang)
    c = jnp.repeat(cos, 2, axis=-1)
    s = jnp.stack([-sin, sin], axis=-1).reshape(S, d_rot)
    return c, s


def _layer_params(l, S, rope_a, rope_b, P):
    row = lambda v: v.reshape(1, -1)
    ca, sa = rope_a
    cb, sb = rope_b
    p = {}
    w_in = P["w_in"][l]
    c_q, c_kv, k_rope, g_q, g_k, g_v, g_m = jnp.split(
        w_in, [256, 384, 416, 800, 928, 1056], axis=1)
    pad = jnp.zeros((D_MODEL, Z_END - Z_KR - 2 * MLA_ROPE), F32)
    p["w_in"] = jnp.concatenate(
        [c_q, c_kv, g_q, _swap_pairs(g_q), g_k, _swap_pairs(g_k), g_v, g_m, k_rope, _swap_pairs(k_rope), pad],
        axis=1).astype(BF16)
    p["mix_norm"] = row(P["mix_norm"][l])
    p["mla_q_norm"] = row(P["mla_q_norm"][l])
    p["mla_kv_norm"] = row(P["mla_kv_norm"][l])

    w_uq = P["mla_w_uq"][l].reshape(MLA_Q_RANK, MLA_HEADS, MLA_NOPE + MLA_ROPE)
    zpad = jnp.zeros((MLA_Q_RANK, MLA_HEADS, HEAD_SLOT - MLA_NOPE - MLA_ROPE), F32)
    main = jnp.concatenate([w_uq, zpad], axis=-1)
    swapped = jnp.concatenate([jnp.zeros_like(w_uq[..., :MLA_NOPE]), _swap_pairs(w_uq[..., MLA_NOPE:]), zpad], axis=-1)
    p["w_uq"] = jnp.concatenate([main.reshape(MLA_Q_RANK, -1), swapped.reshape(MLA_Q_RANK, -1)], axis=1).astype(BF16)
    scale_a = (MLA_NOPE + MLA_ROPE) ** -0.5 * LOG2E
    ones = jnp.ones((S, MLA_NOPE), F32)
    zeros_n = jnp.zeros((S, MLA_NOPE), F32)
    zeros_p = jnp.zeros((S, HEAD_SLOT - MLA_NOPE - MLA_ROPE), F32)
    p["cqa"] = jnp.tile(jnp.concatenate([ones, ca, zeros_p], axis=1) * scale_a, (1, MLA_HEADS))
    p["sqa"] = jnp.tile(jnp.concatenate([zeros_n, sa, zeros_p], axis=1) * scale_a, (1, MLA_HEADS))

    w_ukv = P["mla_w_ukv"][l].reshape(MLA_KV_RANK, MLA_HEADS, MLA_NOPE + MLA_V)
    k_lat = jnp.concatenate(
        [w_ukv[..., :MLA_NOPE], jnp.zeros((MLA_KV_RANK, MLA_HEADS, HEAD_SLOT - MLA_NOPE), F32)], axis=-1)
    place = jnp.concatenate([jnp.zeros((MLA_ROPE, MLA_NOPE), F32), jnp.eye(MLA_ROPE, dtype=F32),
                             jnp.zeros((MLA_ROPE, HEAD_SLOT - MLA_NOPE - MLA_ROPE), F32)], axis=1)
    place = jnp.tile(place, (1, MLA_HEADS))
    zrows = jnp.zeros((Z_END - Z_KR - 2 * MLA_ROPE, MLA_HEADS * HEAD_SLOT), F32)
    p["w_k"] = jnp.concatenate([k_lat.reshape(MLA_KV_RANK, -1), place, place, zrows], axis=0).astype(BF16)
    p["w_v"] = w_ukv[..., MLA_NOPE:].reshape(MLA_KV_RANK, -1).astype(BF16)
    p["tk"] = jnp.concatenate([ca, sa, jnp.zeros((S, Z_END - Z_KR - 2 * MLA_ROPE), F32)], axis=1)

    gq = jnp.tile(P["gqa_q_norm"][l], GQA_HEADS) * (GQA_DIM ** -0.5 * LOG2E)
    gk = jnp.tile(P["gqa_k_norm"][l], GQA_KV_HEADS)
    p["tcq"] = jnp.tile(cb, (1, GQA_HEADS)) * gq
    p["tsq"] = jnp.tile(sb, (1, GQA_HEADS)) * _swap_pairs(gq)
    p["tck"] = jnp.tile(cb, (1, GQA_KV_HEADS)) * gk
    p["tsk"] = jnp.tile(sb, (1, GQA_KV_HEADS)) * _swap_pairs(gk)
    grp = jnp.arange(W_B) // GQA_DIM
    p["bd"] = (grp[:, None] == grp[None, :]).astype(BF16)

    p["gmlp_v_norm"] = row(P["gmlp_v_norm"][l])
    p["w_s"] = P["gmlp_w_s"][l].reshape(GMLP_GROUPS * GMLP_CHUNK, GMLP_CHUNK).astype(BF16)
    p["bias_s"] = jnp.repeat(P["gmlp_b_s"][l].T, GMLP_DIM, axis=1)
    out_norm = P["out_norm"][l]
    p["out_norm_a"] = row(out_norm[:W_A])
    p["out_norm_b"] = row(out_norm[W_A:W_A + W_B])
    p["out_norm_c"] = row(out_norm[W_A + W_B:])
    p["w_out"] = P["w_out"][l].astype(BF16)

    p["mem_x_norm"] = row(P["mem_x_norm"][l])
    p["mem_kv_norm"] = row(P["mem_kv_norm"][l])
    p["mem_w_q"] = P["mem_w_q"][l].astype(BF16)
    p["mem_w_kv"] = P["mem_w_kv"][l].astype(BF16)
    p["mem_w_o"] = P["mem_w_o"][l].astype(BF16)

    n_chunks = D_FF // FF_CHUNK
    def interleave(w):
        lead = w.shape[:-1]
        gate = w[..., :D_FF].reshape(*lead, n_chunks, 1, FF_CHUNK)
        val = w[..., D_FF:].reshape(*lead, n_chunks, 1, FF_CHUNK)
        return jnp.concatenate([gate, val], axis=-2).reshape(*lead, 2 * D_FF)
    p["ffn_norm"] = row(P["ffn_norm"][l])
    p["ffn_w_up"] = interleave(P["ffn_w_up"][l]).astype(BF16)
    p["ffn_conv_w"] = interleave(P["ffn_conv_w"][l])
    p["ffn_conv_b"] = interleave(row(P["ffn_conv_b"][l]))
    p["ffn_w_down"] = P["ffn_w_down"][l].astype(BF16)
    return p


def kernel(x, mem, mix_norm, w_in, mla_q_norm, mla_w_uq, mla_kv_norm, mla_w_ukv, gqa_q_norm, gqa_k_norm, gmlp_v_norm, gmlp_w_s, gmlp_b_s, out_norm, w_out, mem_x_norm, mem_kv_norm, mem_w_q, mem_w_kv, mem_w_o, ffn_norm, ffn_w_up, ffn_conv_w, ffn_conv_b, ffn_w_down, final_norm):
    P = dict(mix_norm=mix_norm, w_in=w_in, mla_q_norm=mla_q_norm, mla_w_uq=mla_w_uq, mla_kv_norm=mla_kv_norm,
             mla_w_ukv=mla_w_ukv, gqa_q_norm=gqa_q_norm, gqa_k_norm=gqa_k_norm, gmlp_v_norm=gmlp_v_norm,
             gmlp_w_s=gmlp_w_s, gmlp_b_s=gmlp_b_s, out_norm=out_norm, w_out=w_out, mem_x_norm=mem_x_norm,
             mem_kv_norm=mem_kv_norm, mem_w_q=mem_w_q, mem_w_kv=mem_w_kv, mem_w_o=mem_w_o, ffn_norm=ffn_norm,
             ffn_w_up=ffn_w_up, ffn_conv_w=ffn_conv_w, ffn_conv_b=ffn_conv_b, ffn_w_down=ffn_w_down)
    B, S, D = x.shape
    assert D == D_MODEL and S % TOK_TILE == 0 and S % GRID_W == 0
    depth = w_in.shape[0]
    rope_a = _rope_tables(S, MLA_ROPE)
    rope_b = _rope_tables(S, GQA_DIM)
    final_gain = final_norm.reshape(1, -1)
    group = GQA_HEADS // GQA_KV_HEADS
    for l in range(depth):
        p = _layer_params(l, S, rope_a, rope_b, P)
        qaT, ka, vaT, qbT, kb, vbT, ync = _mix_in(x, p)
        yna = _attention(qaT, ka, vaT, p["out_norm_a"], n_heads=MLA_HEADS,
                         k_lanes=tuple(h * HEAD_SLOT for h in range(MLA_HEADS)),
                         v_rows=tuple(h * (MLA_V + V_PAD) for h in range(MLA_HEADS)), dv=MLA_V, name="attn_mla")
        ynb = _attention(qbT, kb, vbT, p["out_norm_b"], n_heads=GQA_HEADS,
                         k_lanes=(0,) * GQA_HEADS,
                         v_rows=tuple((h // group) * (GQA_DIM + V_PAD) for h in range(GQA_HEADS)), dv=GQA_DIM, name="attn_gqa")
        mem_kT, mem_v = _mem_kv(mem, p["mem_kv_norm"], p["mem_w_kv"])
        x = _out_mem(yna, ynb, ync, x, p, mem_kT, mem_v)
        x = _ffn(x, p, final_gain, final=(l == depth - 1))
    return x
```

```python
import functools

import jax
import jax.numpy as jnp
from jax import lax
from jax.experimental import pallas as pl
from jax.experimental.pallas import tpu as pltpu

F32 = jnp.float32
BF16 = jnp.bfloat16

D_MODEL = 1024
GRID_W = 64
ROPE_THETA = 10000.0
EPS = 1e-6
MLA_HEADS = 6
MLA_NOPE = 64
MLA_ROPE = 32
MLA_V = 64
MLA_Q_RANK = 256
MLA_KV_RANK = 128
GQA_HEADS = 6
GQA_KV_HEADS = 2
GQA_DIM = 64
GMLP_GROUPS = 4
GMLP_DIM = 64
GMLP_CHUNK = 128
W_A = MLA_HEADS * MLA_V
W_B = GQA_HEADS * GQA_DIM
W_C = GMLP_GROUPS * GMLP_DIM
MEM_HEADS = 4
MEM_DIM = 128
D_FF = 2816

LANES = 128
HEAD_SLOT = LANES
TOK_TILE = 512
Q_TILE = 512
KV_TILE = 512
V_PAD = 16
LOG2E = 1.4426950408889634
FF_CHUNK = 256
HALO = 16
VMEM_LIMIT = 56 * 1024 * 1024

Z_CQ, Z_CKV, Z_GQ, Z_GQS, Z_GK, Z_GKS, Z_GV, Z_U, Z_VV, Z_KR, Z_END = (
    0, 256, 384, 768, 1152, 1280, 1408, 1536, 1792, 2048, 2176)


def _rms(x, g):
    return x * lax.rsqrt(jnp.mean(x * x, axis=-1, keepdims=True) + EPS) * g


def _dot(a, b):
    return jnp.dot(a, b, preferred_element_type=F32)


def _const_spec(shape):
    zeros = (0,) * len(shape)
    return pl.BlockSpec(shape, lambda *_: zeros, pipeline_mode=pl.Buffered(1))


def _params(n_axes):
    return pltpu.CompilerParams(dimension_semantics=("arbitrary",) * n_axes,
                                vmem_limit_bytes=VMEM_LIMIT)


def _group_ssq(v, bd):
    sq = v * v
    hi = sq.astype(BF16)
    lo = (sq - hi.astype(F32)).astype(BF16)
    return _dot(hi, bd) + _dot(lo, bd)


def _store_vT(ref, vT, n_heads, dv):
    ext = (lax.broadcasted_iota(jnp.int32, (V_PAD, KV_TILE), 0) == 0).astype(BF16)
    for n in range(vT.shape[1] // KV_TILE):
        for hd in range(n_heads):
            base = hd * (dv + V_PAD)
            ref[0, n, base:base + dv] = vT[hd * dv:(hd + 1) * dv, n * KV_TILE:(n + 1) * KV_TILE].astype(BF16)
            ref[0, n, base + dv:base + dv + V_PAD] = ext


def _mix_in_kernel(x_ref, g_ref, win_ref, gq_ref, wq_ref, gkv_ref, wk_ref, wv_ref,
                   cqa_ref, sqa_ref, tk_ref, tcq_ref, tsq_ref, tck_ref, tsk_ref,
                   bd_ref, gv_ref, ws_ref, bias_ref, gc_ref,
                   qaT_ref, ka_ref, vaT_ref, qbT_ref, kb_ref, vbT_ref, ync_ref):
    tok = x_ref.shape[1]
    h = _rms(x_ref[0], g_ref[...])
    z = _dot(h.astype(BF16), win_ref[...])

    cq = _rms(z[:, Z_CQ:Z_CKV], gq_ref[...]).astype(BF16)
    qa = _dot(cq, wq_ref[...])
    half = MLA_HEADS * HEAD_SLOT
    q_a = qa[:, :half] * cqa_ref[...] + qa[:, half:] * sqa_ref[...]
    qaT_ref[0] = q_a.T.astype(BF16)

    ckv = _rms(z[:, Z_CKV:Z_GQ], gkv_ref[...])
    kr = z[:, Z_KR:Z_END] * tk_ref[...]
    lhs = jnp.concatenate([ckv, kr], axis=1).astype(BF16)
    ka_ref[0] = _dot(lhs, wk_ref[...]).astype(BF16)
    _store_vT(vaT_ref, _dot(lhs[:, :MLA_KV_RANK], wv_ref[...]).T, MLA_HEADS, MLA_V)

    bd = bd_ref[...]
    g_q = z[:, Z_GQ:Z_GQS]
    n_q = lax.rsqrt(_group_ssq(g_q, bd) * (1.0 / GQA_DIM) + EPS)
    q_b = (g_q * tcq_ref[...] + z[:, Z_GQS:Z_GK] * tsq_ref[...]) * n_q
    q_bT = q_b.T
    zero = jnp.zeros((GQA_DIM, tok), BF16)
    for hd in range(GQA_HEADS):
        blk = q_bT[hd * GQA_DIM:(hd + 1) * GQA_DIM].astype(BF16)
        lo = hd * HEAD_SLOT + (hd // (GQA_HEADS // GQA_KV_HEADS)) * GQA_DIM
        other = hd * HEAD_SLOT + (1 - hd // (GQA_HEADS // GQA_KV_HEADS)) * GQA_DIM
        qbT_ref[0, lo:lo + GQA_DIM] = blk
        qbT_ref[0, other:other + GQA_DIM] = zero
    g_k = z[:, Z_GK:Z_GKS]
    n_k = lax.rsqrt(_group_ssq(g_k, bd[:LANES, :LANES]) * (1.0 / GQA_DIM) + EPS)
    kb_ref[0] = ((g_k * tck_ref[...] + z[:, Z_GKS:Z_GV] * tsk_ref[...]) * n_k).astype(BF16)
    _store_vT(vbT_ref, z[:, Z_GV:Z_U].T, GQA_KV_HEADS, GQA_DIM)

    gm = jax.nn.gelu(z[:, Z_U:Z_KR])
    u = gm[:, :W_C]
    vv = _rms(gm[:, W_C:], gv_ref[...])
    lane_grp = lax.broadcasted_iota(jnp.int32, (GMLP_CHUNK, W_C), 1) // GMLP_DIM
    ws = ws_ref[...]
    bias = bias_ref[...]
    ycs = []
    for n in range(tok // GMLP_CHUNK):
        rows = slice(n * GMLP_CHUNK, (n + 1) * GMLP_CHUNK)
        r = _dot(ws, vv[rows].astype(BF16))
        mixed = r[3 * GMLP_CHUNK:]
        for grp in range(GMLP_GROUPS - 2, -1, -1):
            mixed = jnp.where(lane_grp == grp, r[grp * GMLP_CHUNK:(grp + 1) * GMLP_CHUNK], mixed)
        ycs.append(u[rows] * (mixed + bias))
    ync_ref[0] = _rms(jnp.concatenate(ycs, axis=0), gc_ref[...]).astype(BF16)


def _mix_in(x, p):
    B, S, D = x.shape
    T = TOK_TILE
    nkv = T // KV_TILE
    tile = lambda w: pl.BlockSpec((1, T, w), lambda s, b: (b, s, 0))
    tab = lambda w: pl.BlockSpec((T, w), lambda s, b: (s, 0))
    tposed = lambda r: pl.BlockSpec((1, r, T), lambda s, b: (b, 0, s))
    blocked = lambda r: pl.BlockSpec((1, nkv, r, KV_TILE), lambda s, b: (b, s, 0, 0))
    consts = [p["mix_norm"], p["w_in"], p["mla_q_norm"], p["w_uq"], p["mla_kv_norm"], p["w_k"], p["w_v"]]
    tabs = [p["cqa"], p["sqa"], p["tk"], p["tcq"], p["tsq"], p["tck"], p["tsk"]]
    consts2 = [p["bd"], p["gmlp_v_norm"], p["w_s"], p["bias_s"], p["out_norm_c"]]
    in_specs = ([tile(D)] + [_const_spec(c.shape) for c in consts] + [tab(t.shape[1]) for t in tabs]
                + [_const_spec(c.shape) for c in consts2])
    out_shape = [
        jax.ShapeDtypeStruct((B, MLA_HEADS * HEAD_SLOT, S), BF16),
        jax.ShapeDtypeStruct((B, S, MLA_HEADS * HEAD_SLOT), BF16),
        jax.ShapeDtypeStruct((B, S // KV_TILE, MLA_HEADS * (MLA_V + V_PAD), KV_TILE), BF16),
        jax.ShapeDtypeStruct((B, GQA_HEADS * HEAD_SLOT, S), BF16),
        jax.ShapeDtypeStruct((B, S, GQA_KV_HEADS * GQA_DIM), BF16),
        jax.ShapeDtypeStruct((B, S // KV_TILE, GQA_KV_HEADS * (GQA_DIM + V_PAD), KV_TILE), BF16),
        jax.ShapeDtypeStruct((B, S, W_C), BF16),
    ]
    out_specs = [tposed(MLA_HEADS * HEAD_SLOT), tile(MLA_HEADS * HEAD_SLOT), blocked(MLA_HEADS * (MLA_V + V_PAD)),
                 tposed(GQA_HEADS * HEAD_SLOT), tile(GQA_KV_HEADS * GQA_DIM),
                 blocked(GQA_KV_HEADS * (GQA_DIM + V_PAD)), tile(W_C)]
    return pl.pallas_call(
        _mix_in_kernel, grid=(S // T, B), in_specs=in_specs, out_specs=out_specs, out_shape=out_shape,
        compiler_params=_params(2), name="mix_in",
    )(x, *consts, *tabs, *consts2)


def _attn_kernel(q_ref, k_ref, v_ref, g_ref, o_ref, s_ref, p_ref, m_ref, acc_ref, oT_ref, *,
                 n_heads, k_lanes, v_rows, dv):
    nkv = v_ref.shape[1]
    assert n_heads % 2 == 0
    m_ref[...] = jnp.full(m_ref.shape, -1e30, F32)
    acc_ref[...] = jnp.zeros(acc_ref.shape, F32)

    def scores(hd, off):
        qT = q_ref[0, hd * HEAD_SLOT:(hd + 1) * HEAD_SLOT, :]
        k = k_ref[0, pl.ds(off, KV_TILE), k_lanes[hd]:k_lanes[hd] + HEAD_SLOT]
        s_ref[hd % 2] = _dot(k, qT)

    scores(0, 0)

    def step(j, carry):
        off = pl.multiple_of(j * KV_TILE, KV_TILE)
        off_next = pl.multiple_of(jnp.minimum(j + 1, nkv - 1) * KV_TILE, KV_TILE)
        for hd in range(n_heads):
            if hd + 1 < n_heads:
                scores(hd + 1, off)
            else:
                scores(0, off_next)
            s = s_ref[hd % 2]
            m_old = m_ref[hd]
            m_new = jnp.maximum(m_old, jnp.max(s, axis=0, keepdims=True))
            m_ref[hd] = m_new
            p_ref[hd] = jnp.exp2(s_ref[hd % 2] - m_new).astype(BF16)
            vT = v_ref[0, j, v_rows[hd]:v_rows[hd] + dv + V_PAD, :]
            acc_ref[hd] = jnp.exp2(m_old - m_new) * acc_ref[hd] + _dot(vT, p_ref[hd])
        return carry

    lax.fori_loop(0, nkv, step, 0)
    for hd in range(n_heads):
        oT_ref[hd * dv:(hd + 1) * dv, :] = acc_ref[hd, :dv] / acc_ref[hd, dv:dv + 1]
    o_ref[0] = _rms(oT_ref[...].T, g_ref[...]).astype(BF16)


def _attention(qT, k, vT, gain, *, n_heads, k_lanes, v_rows, dv, name):
    B, _, S = qT.shape
    kernel = functools.partial(_attn_kernel, n_heads=n_heads, k_lanes=k_lanes, v_rows=v_rows, dv=dv)
    whole = lambda shape: pl.BlockSpec((1,) + shape[1:], lambda b, i: (b,) + (0,) * (len(shape) - 1),
                                       pipeline_mode=pl.Buffered(1))
    return pl.pallas_call(
        kernel, grid=(B, S // Q_TILE),
        in_specs=[pl.BlockSpec((1, n_heads * HEAD_SLOT, Q_TILE), lambda b, i: (b, 0, i)),
                  whole(k.shape), whole(vT.shape), _const_spec(gain.shape)],
        out_specs=pl.BlockSpec((1, Q_TILE, n_heads * dv), lambda b, i: (b, i, 0)),
        out_shape=jax.ShapeDtypeStruct((B, S, n_heads * dv), BF16),
        scratch_shapes=[pltpu.VMEM((2, KV_TILE, Q_TILE), F32), pltpu.VMEM((n_heads, KV_TILE, Q_TILE), BF16),
                        pltpu.VMEM((n_heads, 1, Q_TILE), F32), pltpu.VMEM((n_heads, dv + V_PAD, Q_TILE), F32),
                        pltpu.VMEM((n_heads * dv, Q_TILE), F32)],
        compiler_params=_params(2), name=name,
    )(qT, k, vT, gain)


def _mem_kv_kernel(mem_ref, g_ref, w_ref, kT_ref, v_ref):
    kv = _dot(_rms(mem_ref[0], g_ref[...]).astype(BF16), w_ref[...])
    width = MEM_HEADS * MEM_DIM
    kT_ref[0] = kv[:, :width].T.astype(BF16)
    ones = jnp.ones((kv.shape[0], MEM_DIM), BF16)
    for hd in range(MEM_HEADS):
        v_ref[0, hd, :, :MEM_DIM] = kv[:, width + hd * MEM_DIM:width + (hd + 1) * MEM_DIM].astype(BF16)
        v_ref[0, hd, :, MEM_DIM:] = ones


def _mem_kv(mem, gain, w_kv):
    B, Tm, D = mem.shape
    width = MEM_HEADS * MEM_DIM
    return pl.pallas_call(
        _mem_kv_kernel, grid=(B,),
        in_specs=[pl.BlockSpec((1, Tm, D), lambda b: (b, 0, 0)), _const_spec(gain.shape), _const_spec(w_kv.shape)],
        out_specs=[pl.BlockSpec((1, width, Tm), lambda b: (b, 0, 0)),
                   pl.BlockSpec((1, MEM_HEADS, Tm, 2 * MEM_DIM), lambda b: (b, 0, 0, 0))],
        out_shape=[jax.ShapeDtypeStruct((B, width, Tm), BF16),
                   jax.ShapeDtypeStruct((B, MEM_HEADS, Tm, 2 * MEM_DIM), BF16)],
        compiler_params=_params(1), name="mem_kv",
    )(mem, gain, w_kv)


def _out_mem_kernel(ya_ref, yb_ref, yc_ref, x_ref, wout_ref, g_ref, wq_ref, kT_ref, v_ref, wo_ref, o_ref):
    y = (_dot(ya_ref[0], wout_ref[0:W_A]) + _dot(yb_ref[0], wout_ref[W_A:W_A + W_B])
         + _dot(yc_ref[0], wout_ref[W_A + W_B:]))
    x1 = x_ref[0] + y
    h = _rms(x1, g_ref[...]).astype(BF16)
    q = (_dot(h, wq_ref[...]) * (MEM_DIM ** -0.5)).astype(BF16)
    heads = []
    for hd in range(MEM_HEADS):
        s = _dot(q[:, hd * MEM_DIM:(hd + 1) * MEM_DIM], kT_ref[0, hd * MEM_DIM:(hd + 1) * MEM_DIM, :])
        p = jnp.exp(s - jnp.max(s, axis=-1, keepdims=True)).astype(BF16)
        pv = _dot(p, v_ref[0, hd])
        heads.append(pv[:, :MEM_DIM] / pv[:, MEM_DIM:])
    o = jnp.concatenate(heads, axis=1).astype(BF16)
    o_ref[0] = x1 + _dot(o, wo_ref[...])


def _out_mem(ya, yb, yc, x, p, mem_kT, mem_v):
    B, S, D = x.shape
    T = TOK_TILE
    tile = lambda w: pl.BlockSpec((1, T, w), lambda b, s: (b, s, 0))
    per_b = lambda a: pl.BlockSpec((1,) + a.shape[1:], lambda b, s: (b,) + (0,) * (a.ndim - 1))
    consts = [p["w_out"], p["mem_x_norm"], p["mem_w_q"]]
    return pl.pallas_call(
        _out_mem_kernel, grid=(B, S // T),
        in_specs=[tile(W_A), tile(W_B), tile(W_C), tile(D)] + [_const_spec(c.shape) for c in consts]
                 + [per_b(mem_kT), per_b(mem_v), _const_spec(p["mem_w_o"].shape)],
        out_specs=tile(D), out_shape=jax.ShapeDtypeStruct((B, S, D), F32),
        compiler_params=_params(2), name="out_mem",
    )(ya, yb, yc, x, *consts, mem_kT, mem_v, p["mem_w_o"])


def _ffn_kernel(x_ref, xp_ref, xn_ref, g_ref, wup_ref, cw_ref, cb_ref, wdn_ref, fg_ref, o_ref,
                h_ref, act_ref, *, final):
    tok = x_ref.shape[1]
    i = pl.program_id(1)
    g = g_ref[...]
    x = x_ref[0]
    keep_prev = (i > 0).astype(F32)
    keep_next = (i < pl.num_programs(1) - 1).astype(F32)
    h_ref[0:HALO] = (_rms(xp_ref[0], g) * keep_prev).astype(BF16)
    h_ref[HALO:HALO + tok] = _rms(x, g).astype(BF16)
    h_ref[HALO + tok:] = (_rms(xn_ref[0], g) * keep_next).astype(BF16)
    hext = h_ref[...]
    rows = tok + 2 * HALO
    for c in range(D_FF // FF_CHUNK):
        cols = slice(2 * FF_CHUNK * c, 2 * FF_CHUNK * (c + 1))
        a = _dot(hext, wup_ref[:, cols])
        w = cw_ref[:, cols]
        ac = (pltpu.roll(a, 1, 0)[HALO:HALO + tok] * w[0:1] + a[HALO:HALO + tok] * w[1:2]
              + pltpu.roll(a, rows - 1, 0)[HALO:HALO + tok] * w[2:3] + cb_ref[:, cols])
        act = jax.nn.silu(ac[:, :FF_CHUNK]) * ac[:, FF_CHUNK:]
        act_ref[:, FF_CHUNK * c:FF_CHUNK * (c + 1)] = act.astype(BF16)
    y = x + _dot(act_ref[...], wdn_ref[...])
    if final:
        y = _rms(y, fg_ref[...])
    o_ref[0] = y


def _ffn(x, p, final_gain, *, final):
    B, S, D = x.shape
    T = TOK_TILE
    per_tile = T // HALO
    n_halo = S // HALO
    tile = pl.BlockSpec((1, T, D), lambda b, s: (b, s, 0))
    prev = pl.BlockSpec((1, HALO, D), lambda b, s: (b, jnp.maximum(s * per_tile - 1, 0), 0))
    nxt = pl.BlockSpec((1, HALO, D), lambda b, s: (b, jnp.minimum((s + 1) * per_tile, n_halo - 1), 0))
    consts = [p["ffn_norm"], p["ffn_w_up"], p["ffn_conv_w"], p["ffn_conv_b"], p["ffn_w_down"], final_gain]
    return pl.pallas_call(
        functools.partial(_ffn_kernel, final=final), grid=(B, S // T),
        in_specs=[tile, prev, nxt] + [_const_spec(c.shape) for c in consts],
        out_specs=tile, out_shape=jax.ShapeDtypeStruct((B, S, D), F32),
        scratch_shapes=[pltpu.VMEM((T + 2 * HALO, D), BF16), pltpu.VMEM((T, D_FF), BF16)],
        compiler_params=_params(2), name="ffn_final" if final else "ffn",
    )(x, x, x, *consts)


def _swap_pairs(w):
    n = w.shape[-1]
    return w.reshape(*w.shape[:-1], n // 2, 2)[..., ::-1].reshape(w.shape)


def _rope_tables(S, d_rot):
    rows = S // GRID_W
    row = jnp.repeat(jnp.arange(rows, dtype=F32), GRID_W)
    col = jnp.tile(jnp.arange(GRID_W, dtype=F32), rows)
    n = d_rot // 4
    inv = ROPE_THETA ** (-jnp.arange(n, dtype=F32) / n)
    ang = jnp.concatenate([row[:, None] * inv, col[:, None] * inv], axis=-1)
    cos, sin = jnp.cos(ang), jnp.sin(ang)
    c = jnp.repeat(cos, 2, axis=-1)
    s = jnp.stack([-sin, sin], axis=-1).reshape(S, d_rot)
    return c, s


def _layer_params(l, S, rope_a, rope_b, P):
    row = lambda v: v.reshape(1, -1)
    ca, sa = rope_a
    cb, sb = rope_b
    p = {}
    w_in = P["w_in"][l]
    c_q, c_kv, k_rope, g_q, g_k, g_v, g_m = jnp.split(
        w_in, [256, 384, 416, 800, 928, 1056], axis=1)
    pad = jnp.zeros((D_MODEL, Z_END - Z_KR - 2 * MLA_ROPE), F32)
    p["w_in"] = jnp.concatenate(
        [c_q, c_kv, g_q, _swap_pairs(g_q), g_k, _swap_pairs(g_k), g_v, g_m, k_rope, _swap_pairs(k_rope), pad],
        axis=1).astype(BF16)
    p["mix_norm"] = row(P["mix_norm"][l])
    p["mla_q_norm"] = row(P["mla_q_norm"][l])
    p["mla_kv_norm"] = row(P["mla_kv_norm"][l])

    w_uq = P["mla_w_uq"][l].reshape(MLA_Q_RANK, MLA_HEADS, MLA_NOPE + MLA_ROPE)
    zpad = jnp.zeros((MLA_Q_RANK, MLA_HEADS, HEAD_SLOT - MLA_NOPE - MLA_ROPE), F32)
    main = jnp.concatenate([w_uq, zpad], axis=-1)
    swapped = jnp.concatenate([jnp.zeros_like(w_uq[..., :MLA_NOPE]), _swap_pairs(w_uq[..., MLA_NOPE:]), zpad], axis=-1)
    p["w_uq"] = jnp.concatenate([main.reshape(MLA_Q_RANK, -1), swapped.reshape(MLA_Q_RANK, -1)], axis=1).astype(BF16)
    scale_a = (MLA_NOPE + MLA_ROPE) ** -0.5 * LOG2E
    ones = jnp.ones((S, MLA_NOPE), F32)
    zeros_n = jnp.zeros((S, MLA_NOPE), F32)
    zeros_p = jnp.zeros((S, HEAD_SLOT - MLA_NOPE - MLA_ROPE), F32)
    p["cqa"] = jnp.tile(jnp.concatenate([ones, ca, zeros_p], axis=1) * scale_a, (1, MLA_HEADS))
    p["sqa"] = jnp.tile(jnp.concatenate([zeros_n, sa, zeros_p], axis=1) * scale_a, (1, MLA_HEADS))

    w_ukv = P["mla_w_ukv"][l].reshape(MLA_KV_RANK, MLA_HEADS, MLA_NOPE + MLA_V)
    k_lat = jnp.concatenate(
        [w_ukv[..., :MLA_NOPE], jnp.zeros((MLA_KV_RANK, MLA_HEADS, HEAD_SLOT - MLA_NOPE), F32)], axis=-1)
    place = jnp.concatenate([jnp.zeros((MLA_ROPE, MLA_NOPE), F32), jnp.eye(MLA_ROPE, dtype=F32),
                             jnp.zeros((MLA_ROPE, HEAD_SLOT - MLA_NOPE - MLA_ROPE), F32)], axis=1)
    place = jnp.tile(place, (1, MLA_HEADS))
    zrows = jnp.zeros((Z_END - Z_KR - 2 * MLA_ROPE, MLA_HEADS * HEAD_SLOT), F32)
    p["w_k"] = jnp.concatenate([k_lat.reshape(MLA_KV_RANK, -1), place, place, zrows], axis=0).astype(BF16)
    p["w_v"] = w_ukv[..., MLA_NOPE:].reshape(MLA_KV_RANK, -1).astype(BF16)
    p["tk"] = jnp.concatenate([ca, sa, jnp.zeros((S, Z_END - Z_KR - 2 * MLA_ROPE), F32)], axis=1)

    gq = jnp.tile(P["gqa_q_norm"][l], GQA_HEADS) * (GQA_DIM ** -0.5 * LOG2E)
    gk = jnp.tile(P["gqa_k_norm"][l], GQA_KV_HEADS)
    p["tcq"] = jnp.tile(cb, (1, GQA_HEADS)) * gq
    p["tsq"] = jnp.tile(sb, (1, GQA_HEADS)) * _swap_pairs(gq)
    p["tck"] = jnp.tile(cb, (1, GQA_KV_HEADS)) * gk
    p["tsk"] = jnp.tile(sb, (1, GQA_KV_HEADS)) * _swap_pairs(gk)
    grp = jnp.arange(W_B) // GQA_DIM
    p["bd"] = (grp[:, None] == grp[None, :]).astype(BF16)

    p["gmlp_v_norm"] = row(P["gmlp_v_norm"][l])
    p["w_s"] = P["gmlp_w_s"][l].reshape(GMLP_GROUPS * GMLP_CHUNK, GMLP_CHUNK).astype(BF16)
    p["bias_s"] = jnp.repeat(P["gmlp_b_s"][l].T, GMLP_DIM, axis=1)
    out_norm = P["out_norm"][l]
    p["out_norm_a"] = row(out_norm[:W_A])
    p["out_norm_b"] = row(out_norm[W_A:W_A + W_B])
    p["out_norm_c"] = row(out_norm[W_A + W_B:])
    p["w_out"] = P["w_out"][l].astype(BF16)

    p["mem_x_norm"] = row(P["mem_x_norm"][l])
    p["mem_kv_norm"] = row(P["mem_kv_norm"][l])
    p["mem_w_q"] = P["mem_w_q"][l].astype(BF16)
    p["mem_w_kv"] = P["mem_w_kv"][l].astype(BF16)
    p["mem_w_o"] = P["mem_w_o"][l].astype(BF16)

    n_chunks = D_FF // FF_CHUNK
    def interleave(w):
        lead = w.shape[:-1]
        gate = w[..., :D_FF].reshape(*lead, n_chunks, 1, FF_CHUNK)
        val = w[..., D_FF:].reshape(*lead, n_chunks, 1, FF_CHUNK)
        return jnp.concatenate([gate, val], axis=-2).reshape(*lead, 2 * D_FF)
    p["ffn_norm"] = row(P["ffn_norm"][l])
    p["ffn_w_up"] = interleave(P["ffn_w_up"][l]).astype(BF16)
    p["ffn_conv_w"] = interleave(P["ffn_conv_w"][l])
    p["ffn_conv_b"] = interleave(row(P["ffn_conv_b"][l]))
    p["ffn_w_down"] = P["ffn_w_down"][l].astype(BF16)
    return p


def kernel(x, mem, mix_norm, w_in, mla_q_norm, mla_w_uq, mla_kv_norm, mla_w_ukv, gqa_q_norm, gqa_k_norm, gmlp_v_norm, gmlp_w_s, gmlp_b_s, out_norm, w_out, mem_x_norm, mem_kv_norm, mem_w_q, mem_w_kv, mem_w_o, ffn_norm, ffn_w_up, ffn_conv_w, ffn_conv_b, ffn_w_down, final_norm):
    P = dict(mix_norm=mix_norm, w_in=w_in, mla_q_norm=mla_q_norm, mla_w_uq=mla_w_uq, mla_kv_norm=mla_kv_norm,
             mla_w_ukv=mla_w_ukv, gqa_q_norm=gqa_q_norm, gqa_k_norm=gqa_k_norm, gmlp_v_norm=gmlp_v_norm,
             gmlp_w_s=gmlp_w_s, gmlp_b_s=gmlp_b_s, out_norm=out_norm, w_out=w_out, mem_x_norm=mem_x_norm,
             mem_kv_norm=mem_kv_norm, mem_w_q=mem_w_q, mem_w_kv=mem_w_kv, mem_w_o=mem_w_o, ffn_norm=ffn_norm,
             ffn_w_up=ffn_w_up, ffn_conv_w=ffn_conv_w, ffn_conv_b=ffn_conv_b, ffn_w_down=ffn_w_down)
    B, S, D = x.shape
    assert D == D_MODEL and S % TOK_TILE == 0 and S % GRID_W == 0
    depth = w_in.shape[0]
    rope_a = _rope_tables(S, MLA_ROPE)
    rope_b = _rope_tables(S, GQA_DIM)
    final_gain = final_norm.reshape(1, -1)
    group = GQA_HEADS // GQA_KV_HEADS
    for l in range(depth):
        p = _layer_params(l, S, rope_a, rope_b, P)
        qaT, ka, vaT, qbT, kb, vbT, ync = _mix_in(x, p)
        yna = _attention(qaT, ka, vaT, p["out_norm_a"], n_heads=MLA_HEADS,
                         k_lanes=tuple(h * HEAD_SLOT for h in range(MLA_HEADS)),
                         v_rows=tuple(h * (MLA_V + V_PAD) for h in range(MLA_HEADS)), dv=MLA_V, name="attn_mla")
        ynb = _attention(qbT, kb, vbT, p["out_norm_b"], n_heads=GQA_HEADS,
                         k_lanes=(0,) * GQA_HEADS,
                         v_rows=tuple((h // group) * (GQA_DIM + V_PAD) for h in range(GQA_HEADS)), dv=GQA_DIM, name="attn_gqa")
        mem_kT, mem_v = _mem_kv(mem, p["mem_kv_norm"], p["mem_w_kv"])
        x = _out_mem(yna, ynb, ync, x, p, mem_kT, mem_v)
        x = _ffn(x, p, final_gain, final=(l == depth - 1))
    return x
```

```python
import functools

import jax
import jax.numpy as jnp
from jax import lax
from jax.experimental import pallas as pl
from jax.experimental.pallas import tpu as pltpu

F32 = jnp.float32
BF16 = jnp.bfloat16

D_MODEL = 1024
GRID_W = 64
ROPE_THETA = 10000.0
EPS = 1e-6
MLA_HEADS = 6
MLA_NOPE = 64
MLA_ROPE = 32
MLA_V = 64
MLA_Q_RANK = 256
MLA_KV_RANK = 128
GQA_HEADS = 6
GQA_KV_HEADS = 2
GQA_DIM = 64
GMLP_GROUPS = 4
GMLP_DIM = 64
GMLP_CHUNK = 128
W_A = MLA_HEADS * MLA_V
W_B = GQA_HEADS * GQA_DIM
W_C = GMLP_GROUPS * GMLP_DIM
MEM_HEADS = 4
MEM_DIM = 128
D_FF = 2816

LANES = 128
HEAD_SLOT = LANES
TOK_TILE = 512
Q_TILE = 512
KV_TILE = 512
BF16_ROWS = 16
V_PAD = BF16_ROWS
LOG2E = 1.4426950408889634
MLA_BIAS_LANE = MLA_NOPE + MLA_ROPE
GQA_BIAS_LANE = GQA_DIM
STABILISER_MAX = 50.0
FF_CHUNK = 256
HALO = BF16_ROWS
VMEM_LIMIT = 56 * 1024 * 1024

Z_CQ, Z_CKV, Z_GQ, Z_GQS, Z_GK, Z_GKS, Z_GV, Z_U, Z_VV, Z_KR, Z_END = (
    0, 256, 384, 768, 1152, 1280, 1408, 1536, 1792, 2048, 2176)


def _rms(x, g):
    return x * lax.rsqrt(jnp.mean(x * x, axis=-1, keepdims=True) + EPS) * g


def _dot(a, b):
    return jnp.dot(a, b, preferred_element_type=F32)


def _const_spec(shape):
    zeros = (0,) * len(shape)
    return pl.BlockSpec(shape, lambda *_: zeros, pipeline_mode=pl.Buffered(1))


def _params(n_axes):
    return pltpu.CompilerParams(dimension_semantics=("arbitrary",) * n_axes,
                                vmem_limit_bytes=VMEM_LIMIT)


def _group_ssq(v, bd):
    sq = v * v
    hi = sq.astype(BF16)
    lo = (sq - hi.astype(F32)).astype(BF16)
    return _dot(hi, bd) + _dot(lo, bd)


def _store_vT(ref, vT, n_heads, dv):
    ext = (lax.broadcasted_iota(jnp.int32, (V_PAD, KV_TILE), 0) == 0).astype(BF16)
    for n in range(vT.shape[1] // KV_TILE):
        for hd in range(n_heads):
            base = hd * (dv + V_PAD)
            ref[0, n, base:base + dv] = vT[hd * dv:(hd + 1) * dv, n * KV_TILE:(n + 1) * KV_TILE].astype(BF16)
            ref[0, n, base + dv:base + dv + V_PAD] = ext


def _mix_in_kernel(x_ref, g_ref, win_ref, gq_ref, wq_ref, gkv_ref, wk_ref, wv_ref,
                   cqa_ref, sqa_ref, tk_ref, tcq_ref, tsq_ref, tck_ref, tsk_ref,
                   bd_ref, gv_ref, ws_ref, bias_ref, gc_ref,
                   qaT_ref, ka_ref, vaT_ref, qbT_ref, kb_ref, vbT_ref, ync_ref):
    tok = x_ref.shape[1]
    h = _rms(x_ref[0], g_ref[...])
    z = _dot(h.astype(BF16), win_ref[...])

    cq = _rms(z[:, Z_CQ:Z_CKV], gq_ref[...]).astype(BF16)
    qa = _dot(cq, wq_ref[...])
    half = MLA_HEADS * HEAD_SLOT
    q_a = qa[:, :half] * cqa_ref[...] + qa[:, half:] * sqa_ref[...]
    qaT_ref[0] = q_a.T.astype(BF16)

    ckv = _rms(z[:, Z_CKV:Z_GQ], gkv_ref[...])
    kr = z[:, Z_KR:Z_END] * tk_ref[...]
    lhs = jnp.concatenate([ckv, kr], axis=1).astype(BF16)
    slot_lane = lax.broadcasted_iota(jnp.int32, (1, MLA_HEADS * HEAD_SLOT), 1) % HEAD_SLOT
    ka_ref[0] = (_dot(lhs, wk_ref[...]) + (slot_lane == MLA_BIAS_LANE).astype(F32)).astype(BF16)
    _store_vT(vaT_ref, _dot(lhs[:, :MLA_KV_RANK], wv_ref[...]).T, MLA_HEADS, MLA_V)

    bd = bd_ref[...]
    g_q = z[:, Z_GQ:Z_GQS]
    n_q = lax.rsqrt(_group_ssq(g_q, bd) * (1.0 / GQA_DIM) + EPS)
    q_b = (g_q * tcq_ref[...] + z[:, Z_GQS:Z_GK] * tsq_ref[...]) * n_q
    q_bT = q_b.T
    zero = jnp.zeros((HEAD_SLOT - GQA_DIM, tok), BF16)
    for hd in range(GQA_HEADS):
        qbT_ref[0, hd * HEAD_SLOT:hd * HEAD_SLOT + GQA_DIM] = q_bT[hd * GQA_DIM:(hd + 1) * GQA_DIM].astype(BF16)
        qbT_ref[0, hd * HEAD_SLOT + GQA_DIM:(hd + 1) * HEAD_SLOT] = zero
    g_k = z[:, Z_GK:Z_GKS]
    n_k = lax.rsqrt(_group_ssq(g_k, bd[:LANES, :LANES]) * (1.0 / GQA_DIM) + EPS)
    k_b = (g_k * tck_ref[...] + z[:, Z_GKS:Z_GV] * tsk_ref[...]) * n_k
    lane = lax.broadcasted_iota(jnp.int32, (tok, LANES), 1)
    bias_lane = (lane == GQA_BIAS_LANE).astype(F32)
    k_slots = [jnp.where(lane < GQA_DIM, k_b, bias_lane),
               jnp.where(lane < GQA_DIM, pltpu.roll(k_b, GQA_DIM, 1), bias_lane)]
    kb_ref[0] = jnp.concatenate(k_slots, axis=1).astype(BF16)
    _store_vT(vbT_ref, z[:, Z_GV:Z_U].T, GQA_KV_HEADS, GQA_DIM)

    gm = jax.nn.gelu(z[:, Z_U:Z_KR])
    u = gm[:, :W_C]
    vv = _rms(gm[:, W_C:], gv_ref[...])
    lane_grp = lax.broadcasted_iota(jnp.int32, (GMLP_CHUNK, W_C), 1) // GMLP_DIM
    ws = ws_ref[...]
    bias = bias_ref[...]
    ycs = []
    for n in range(tok // GMLP_CHUNK):
        rows = slice(n * GMLP_CHUNK, (n + 1) * GMLP_CHUNK)
        r = _dot(ws, vv[rows].astype(BF16))
        mixed = r[3 * GMLP_CHUNK:]
        for grp in range(GMLP_GROUPS - 2, -1, -1):
            mixed = jnp.where(lane_grp == grp, r[grp * GMLP_CHUNK:(grp + 1) * GMLP_CHUNK], mixed)
        ycs.append(u[rows] * (mixed + bias))
    ync_ref[0] = _rms(jnp.concatenate(ycs, axis=0), gc_ref[...]).astype(BF16)


def _mix_in(x, p):
    B, S, D = x.shape
    T = TOK_TILE
    nkv = T // KV_TILE
    tile = lambda w: pl.BlockSpec((1, T, w), lambda s, b: (b, s, 0))
    tab = lambda w: pl.BlockSpec((T, w), lambda s, b: (s, 0))
    tposed = lambda r: pl.BlockSpec((1, r, T), lambda s, b: (b, 0, s))
    blocked = lambda r: pl.BlockSpec((1, nkv, r, KV_TILE), lambda s, b: (b, s, 0, 0))
    consts = [p["mix_norm"], p["w_in"], p["mla_q_norm"], p["w_uq"], p["mla_kv_norm"], p["w_k"], p["w_v"]]
    tabs = [p["cqa"], p["sqa"], p["tk"], p["tcq"], p["tsq"], p["tck"], p["tsk"]]
    consts2 = [p["bd"], p["gmlp_v_norm"], p["w_s"], p["bias_s"], p["out_norm_c"]]
    in_specs = ([tile(D)] + [_const_spec(c.shape) for c in consts] + [tab(t.shape[1]) for t in tabs]
                + [_const_spec(c.shape) for c in consts2])
    out_shape = [
        jax.ShapeDtypeStruct((B, MLA_HEADS * HEAD_SLOT, S), BF16),
        jax.ShapeDtypeStruct((B, S, MLA_HEADS * HEAD_SLOT), BF16),
        jax.ShapeDtypeStruct((B, S // KV_TILE, MLA_HEADS * (MLA_V + V_PAD), KV_TILE), BF16),
        jax.ShapeDtypeStruct((B, GQA_HEADS * HEAD_SLOT, S), BF16),
        jax.ShapeDtypeStruct((B, S, GQA_KV_HEADS * HEAD_SLOT), BF16),
        jax.ShapeDtypeStruct((B, S // KV_TILE, GQA_KV_HEADS * (GQA_DIM + V_PAD), KV_TILE), BF16),
        jax.ShapeDtypeStruct((B, S, W_C), BF16),
    ]
    out_specs = [tposed(MLA_HEADS * HEAD_SLOT), tile(MLA_HEADS * HEAD_SLOT), blocked(MLA_HEADS * (MLA_V + V_PAD)),
                 tposed(GQA_HEADS * HEAD_SLOT), tile(GQA_KV_HEADS * HEAD_SLOT),
                 blocked(GQA_KV_HEADS * (GQA_DIM + V_PAD)), tile(W_C)]
    return pl.pallas_call(
        _mix_in_kernel, grid=(S // T, B), in_specs=in_specs, out_specs=out_specs, out_shape=out_shape,
        compiler_params=_params(2), name="mix_in",
    )(x, *consts, *tabs, *consts2)


def _attn_kernel(q_ref, k_ref, v_ref, g_ref, o_ref, s_ref, p_ref, m_ref, acc_ref, oT_ref, qx_ref, kmax_ref, *,
                 n_heads, k_lanes, bias_lane, v_rows, dv):
    nkv = v_ref.shape[1]
    mq = q_ref.shape[2]
    assert n_heads % 2 == 0
    k_slots = sorted(set(k_lanes))

    @pl.when(pl.program_id(1) == 0)
    def _key_norms():
        lane = lax.broadcasted_iota(jnp.int32, (KV_TILE, HEAD_SLOT), 1)
        for n, kl in enumerate(k_slots):
            def body(c, mx, kl=kl):
                off = pl.multiple_of(c * KV_TILE, KV_TILE)
                x = jnp.where(lane < bias_lane, k_ref[0, pl.ds(off, KV_TILE), kl:kl + HEAD_SLOT].astype(F32), 0.0)
                n2 = jnp.sum(x * x, axis=-1, keepdims=True)
                return jnp.maximum(mx, jnp.max(n2, axis=0, keepdims=True))
            kmax_ref[n] = jnp.broadcast_to(lax.fori_loop(0, nkv, body, jnp.zeros((1, 1), F32)), (1, LANES))

    first_row = lax.broadcasted_iota(jnp.int32, (BF16_ROWS, mq), 0) == 0
    bound = jnp.zeros((1, 1), F32)
    for hd in range(n_heads):
        rows = slice(hd * HEAD_SLOT, (hd + 1) * HEAD_SLOT)
        q = q_ref[0, rows, :]
        qf = q.astype(F32)
        u = jnp.sqrt(jnp.sum(qf * qf, axis=0, keepdims=True) * kmax_ref[k_slots.index(k_lanes[hd])][:, 0:1])
        bound = jnp.maximum(bound, jnp.max(u, axis=1, keepdims=True))
        qx_ref[rows, :] = q
        bias_rows = slice(hd * HEAD_SLOT + bias_lane, hd * HEAD_SLOT + bias_lane + BF16_ROWS)
        qx_ref[bias_rows, :] = jnp.where(first_row, -u, 0.0).astype(BF16)
    acc_ref[...] = jnp.zeros(acc_ref.shape, F32)

    def run(stabilised):
        def scores(hd, off):
            rows = slice(hd * HEAD_SLOT, (hd + 1) * HEAD_SLOT)
            qT = qx_ref[rows, :] if stabilised else q_ref[0, rows, :]
            k = k_ref[0, pl.ds(off, KV_TILE), k_lanes[hd]:k_lanes[hd] + HEAD_SLOT]
            s_ref[hd % 2] = _dot(k, qT)

        scores(0, 0)

        def step(j, carry):
            off = pl.multiple_of(j * KV_TILE, KV_TILE)
            off_next = pl.multiple_of(jnp.minimum(j + 1, nkv - 1) * KV_TILE, KV_TILE)
            for hd in range(n_heads):
                if hd + 1 < n_heads:
                    scores(hd + 1, off)
                else:
                    scores(0, off_next)
                vT = v_ref[0, j, v_rows[hd]:v_rows[hd] + dv + V_PAD, :]
                if stabilised:
                    p_ref[hd] = jnp.exp2(s_ref[hd % 2]).astype(BF16)
                    acc_ref[hd] += _dot(vT, p_ref[hd])
                else:
                    m_old = m_ref[hd]
                    m_new = jnp.maximum(m_old, jnp.max(s_ref[hd % 2], axis=0, keepdims=True))
                    m_ref[hd] = m_new
                    p_ref[hd] = jnp.exp2(s_ref[hd % 2] - m_new).astype(BF16)
                    acc_ref[hd] = jnp.exp2(m_old - m_new) * acc_ref[hd] + _dot(vT, p_ref[hd])
            return carry

        lax.fori_loop(0, nkv, step, 0, unroll=2 if stabilised else 1)

    def running_max():
        m_ref[...] = jnp.full(m_ref.shape, -1e30, F32)
        run(False)

    lax.cond(jnp.max(bound) <= STABILISER_MAX, lambda: run(True), running_max)
    for hd in range(n_heads):
        oT_ref[hd * dv:(hd + 1) * dv, :] = acc_ref[hd, :dv] / acc_ref[hd, dv:dv + 1]
    o_ref[0] = _rms(oT_ref[...].T, g_ref[...]).astype(BF16)


def _attention(qT, k, vT, gain, *, n_heads, k_lanes, bias_lane, v_rows, dv, name):
    B, _, S = qT.shape
    kernel = functools.partial(_attn_kernel, n_heads=n_heads, k_lanes=k_lanes, bias_lane=bias_lane,
                               v_rows=v_rows, dv=dv)
    whole = lambda shape: pl.BlockSpec((1,) + shape[1:], lambda b, i: (b,) + (0,) * (len(shape) - 1),
                                       pipeline_mode=pl.Buffered(1))
    return pl.pallas_call(
        kernel, grid=(B, S // Q_TILE),
        in_specs=[pl.BlockSpec((1, n_heads * HEAD_SLOT, Q_TILE), lambda b, i: (b, 0, i)),
                  whole(k.shape), whole(vT.shape), _const_spec(gain.shape)],
        out_specs=pl.BlockSpec((1, Q_TILE, n_heads * dv), lambda b, i: (b, i, 0)),
        out_shape=jax.ShapeDtypeStruct((B, S, n_heads * dv), BF16),
        scratch_shapes=[pltpu.VMEM((2, KV_TILE, Q_TILE), F32), pltpu.VMEM((n_heads, KV_TILE, Q_TILE), BF16),
                        pltpu.VMEM((n_heads, 1, Q_TILE), F32), pltpu.VMEM((n_heads, dv + V_PAD, Q_TILE), F32),
                        pltpu.VMEM((n_heads * dv, Q_TILE), F32), pltpu.VMEM((n_heads * HEAD_SLOT, Q_TILE), BF16),
                        pltpu.VMEM((len(set(k_lanes)), 1, LANES), F32)],
        compiler_params=_params(2), name=name,
    )(qT, k, vT, gain)


def _mem_kv_kernel(mem_ref, g_ref, w_ref, kT_ref, v_ref):
    kv = _dot(_rms(mem_ref[0], g_ref[...]).astype(BF16), w_ref[...])
    width = MEM_HEADS * MEM_DIM
    kT_ref[0] = kv[:, :width].T.astype(BF16)
    ones = jnp.ones((kv.shape[0], MEM_DIM), BF16)
    for hd in range(MEM_HEADS):
        v_ref[0, hd, :, :MEM_DIM] = kv[:, width + hd * MEM_DIM:width + (hd + 1) * MEM_DIM].astype(BF16)
        v_ref[0, hd, :, MEM_DIM:] = ones


def _mem_kv(mem, gain, w_kv):
    B, Tm, D = mem.shape
    width = MEM_HEADS * MEM_DIM
    return pl.pallas_call(
        _mem_kv_kernel, grid=(B,),
        in_specs=[pl.BlockSpec((1, Tm, D), lambda b: (b, 0, 0)), _const_spec(gain.shape), _const_spec(w_kv.shape)],
        out_specs=[pl.BlockSpec((1, width, Tm), lambda b: (b, 0, 0)),
                   pl.BlockSpec((1, MEM_HEADS, Tm, 2 * MEM_DIM), lambda b: (b, 0, 0, 0))],
        out_shape=[jax.ShapeDtypeStruct((B, width, Tm), BF16),
                   jax.ShapeDtypeStruct((B, MEM_HEADS, Tm, 2 * MEM_DIM), BF16)],
        compiler_params=_params(1), name="mem_kv",
    )(mem, gain, w_kv)


def _out_mem_kernel(ya_ref, yb_ref, yc_ref, x_ref, wout_ref, g_ref, wq_ref, kT_ref, v_ref, wo_ref, o_ref):
    y = (_dot(ya_ref[0], wout_ref[0:W_A]) + _dot(yb_ref[0], wout_ref[W_A:W_A + W_B])
         + _dot(yc_ref[0], wout_ref[W_A + W_B:]))
    x1 = x_ref[0] + y
    h = _rms(x1, g_ref[...]).astype(BF16)
    q = (_dot(h, wq_ref[...]) * (MEM_DIM ** -0.5)).astype(BF16)
    heads = []
    for hd in range(MEM_HEADS):
        s = _dot(q[:, hd * MEM_DIM:(hd + 1) * MEM_DIM], kT_ref[0, hd * MEM_DIM:(hd + 1) * MEM_DIM, :])
        p = jnp.exp(s - jnp.max(s, axis=-1, keepdims=True)).astype(BF16)
        pv = _dot(p, v_ref[0, hd])
        heads.append(pv[:, :MEM_DIM] / pv[:, MEM_DIM:])
    o = jnp.concatenate(heads, axis=1).astype(BF16)
    o_ref[0] = x1 + _dot(o, wo_ref[...])


def _out_mem(ya, yb, yc, x, p, mem_kT, mem_v):
    B, S, D = x.shape
    T = TOK_TILE
    tile = lambda w: pl.BlockSpec((1, T, w), lambda b, s: (b, s, 0))
    per_b = lambda a: pl.BlockSpec((1,) + a.shape[1:], lambda b, s: (b,) + (0,) * (a.ndim - 1))
    consts = [p["w_out"], p["mem_x_norm"], p["mem_w_q"]]
    return pl.pallas_call(
        _out_mem_kernel, grid=(B, S // T),
        in_specs=[tile(W_A), tile(W_B), tile(W_C), tile(D)] + [_const_spec(c.shape) for c in consts]
                 + [per_b(mem_kT), per_b(mem_v), _const_spec(p["mem_w_o"].shape)],
        out_specs=tile(D), out_shape=jax.ShapeDtypeStruct((B, S, D), F32),
        compiler_params=_params(2), name="out_mem",
    )(ya, yb, yc, x, *consts, mem_kT, mem_v, p["mem_w_o"])


def _ffn_kernel(x_ref, xp_ref, xn_ref, g_ref, wup_ref, cw_ref, cb_ref, wdn_ref, fg_ref, o_ref,
                h_ref, act_ref, *, final):
    tok = x_ref.shape[1]
    i = pl.program_id(1)
    g = g_ref[...]
    x = x_ref[0]
    keep_prev = (i > 0).astype(F32)
    keep_next = (i < pl.num_programs(1) - 1).astype(F32)
    h_ref[0:HALO] = (_rms(xp_ref[0], g) * keep_prev).astype(BF16)
    h_ref[HALO:HALO + tok] = _rms(x, g).astype(BF16)
    h_ref[HALO + tok:] = (_rms(xn_ref[0], g) * keep_next).astype(BF16)
    hext = h_ref[...]
    rows = tok + 2 * HALO
    for c in range(D_FF // FF_CHUNK):
        cols = slice(2 * FF_CHUNK * c, 2 * FF_CHUNK * (c + 1))
        a = _dot(hext, wup_ref[:, cols])
        w = cw_ref[:, cols]
        ac = (pltpu.roll(a, 1, 0)[HALO:HALO + tok] * w[0:1] + a[HALO:HALO + tok] * w[1:2]
              + pltpu.roll(a, rows - 1, 0)[HALO:HALO + tok] * w[2:3] + cb_ref[:, cols])
        act = jax.nn.silu(ac[:, :FF_CHUNK]) * ac[:, FF_CHUNK:]
        act_ref[:, FF_CHUNK * c:FF_CHUNK * (c + 1)] = act.astype(BF16)
    y = x + _dot(act_ref[...], wdn_ref[...])
    if final:
        y = _rms(y, fg_ref[...])
    o_ref[0] = y


def _ffn(x, p, final_gain, *, final):
    B, S, D = x.shape
    T = TOK_TILE
    per_tile = T // HALO
    n_halo = S // HALO
    tile = pl.BlockSpec((1, T, D), lambda b, s: (b, s, 0))
    prev = pl.BlockSpec((1, HALO, D), lambda b, s: (b, jnp.maximum(s * per_tile - 1, 0), 0))
    nxt = pl.BlockSpec((1, HALO, D), lambda b, s: (b, jnp.minimum((s + 1) * per_tile, n_halo - 1), 0))
    consts = [p["ffn_norm"], p["ffn_w_up"], p["ffn_conv_w"], p["ffn_conv_b"], p["ffn_w_down"], final_gain]
    return pl.pallas_call(
        functools.partial(_ffn_kernel, final=final), grid=(B, S // T),
        in_specs=[tile, prev, nxt] + [_const_spec(c.shape) for c in consts],
        out_specs=tile, out_shape=jax.ShapeDtypeStruct((B, S, D), F32),
        scratch_shapes=[pltpu.VMEM((T + 2 * HALO, D), BF16), pltpu.VMEM((T, D_FF), BF16)],
        compiler_params=_params(2), name="ffn_final" if final else "ffn",
    )(x, x, x, *consts)


def _swap_pairs(w):
    n = w.shape[-1]
    return w.reshape(*w.shape[:-1], n // 2, 2)[..., ::-1].reshape(w.shape)


def _rope_tables(S, d_rot):
    rows = S // GRID_W
    row = jnp.repeat(jnp.arange(rows, dtype=F32), GRID_W)
    col = jnp.tile(jnp.arange(GRID_W, dtype=F32), rows)
    n = d_rot // 4
    inv = ROPE_THETA ** (-jnp.arange(n, dtype=F32) / n)
    ang = jnp.concatenate([row[:, None] * inv, col[:, None] * inv], axis=-1)
    cos, sin = jnp.cos(ang), jnp.sin(ang)
    c = jnp.repeat(cos, 2, axis=-1)
    s = jnp.stack([-sin, sin], axis=-1).reshape(S, d_rot)
    return c, s


def _layer_params(l, S, rope_a, rope_b, P):
    row = lambda v: v.reshape(1, -1)
    ca, sa = rope_a
    cb, sb = rope_b
    p = {}
    w_in = P["w_in"][l]
    c_q, c_kv, k_rope, g_q, g_k, g_v, g_m = jnp.split(
        w_in, [256, 384, 416, 800, 928, 1056], axis=1)
    pad = jnp.zeros((D_MODEL, Z_END - Z_KR - 2 * MLA_ROPE), F32)
    p["w_in"] = jnp.concatenate(
        [c_q, c_kv, g_q, _swap_pairs(g_q), g_k, _swap_pairs(g_k), g_v, g_m, k_rope, _swap_pairs(k_rope), pad],
        axis=1).astype(BF16)
    p["mix_norm"] = row(P["mix_norm"][l])
    p["mla_q_norm"] = row(P["mla_q_norm"][l])
    p["mla_kv_norm"] = row(P["mla_kv_norm"][l])

    w_uq = P["mla_w_uq"][l].reshape(MLA_Q_RANK, MLA_HEADS, MLA_NOPE + MLA_ROPE)
    zpad = jnp.zeros((MLA_Q_RANK, MLA_HEADS, HEAD_SLOT - MLA_NOPE - MLA_ROPE), F32)
    main = jnp.concatenate([w_uq, zpad], axis=-1)
    swapped = jnp.concatenate([jnp.zeros_like(w_uq[..., :MLA_NOPE]), _swap_pairs(w_uq[..., MLA_NOPE:]), zpad], axis=-1)
    p["w_uq"] = jnp.concatenate([main.reshape(MLA_Q_RANK, -1), swapped.reshape(MLA_Q_RANK, -1)], axis=1).astype(BF16)
    scale_a = (MLA_NOPE + MLA_ROPE) ** -0.5 * LOG2E
    ones = jnp.ones((S, MLA_NOPE), F32)
    zeros_n = jnp.zeros((S, MLA_NOPE), F32)
    zeros_p = jnp.zeros((S, HEAD_SLOT - MLA_NOPE - MLA_ROPE), F32)
    p["cqa"] = jnp.tile(jnp.concatenate([ones, ca, zeros_p], axis=1) * scale_a, (1, MLA_HEADS))
    p["sqa"] = jnp.tile(jnp.concatenate([zeros_n, sa, zeros_p], axis=1) * scale_a, (1, MLA_HEADS))

    w_ukv = P["mla_w_ukv"][l].reshape(MLA_KV_RANK, MLA_HEADS, MLA_NOPE + MLA_V)
    k_lat = jnp.concatenate(
        [w_ukv[..., :MLA_NOPE], jnp.zeros((MLA_KV_RANK, MLA_HEADS, HEAD_SLOT - MLA_NOPE), F32)], axis=-1)
    place = jnp.concatenate([jnp.zeros((MLA_ROPE, MLA_NOPE), F32), jnp.eye(MLA_ROPE, dtype=F32),
                             jnp.zeros((MLA_ROPE, HEAD_SLOT - MLA_NOPE - MLA_ROPE), F32)], axis=1)
    place = jnp.tile(place, (1, MLA_HEADS))
    zrows = jnp.zeros((Z_END - Z_KR - 2 * MLA_ROPE, MLA_HEADS * HEAD_SLOT), F32)
    p["w_k"] = jnp.concatenate([k_lat.reshape(MLA_KV_RANK, -1), place, place, zrows], axis=0).astype(BF16)
    p["w_v"] = w_ukv[..., MLA_NOPE:].reshape(MLA_KV_RANK, -1).astype(BF16)
    p["tk"] = jnp.concatenate([ca, sa, jnp.zeros((S, Z_END - Z_KR - 2 * MLA_ROPE), F32)], axis=1)

    gq = jnp.tile(P["gqa_q_norm"][l], GQA_HEADS) * (GQA_DIM ** -0.5 * LOG2E)
    gk = jnp.tile(P["gqa_k_norm"][l], GQA_KV_HEADS)
    p["tcq"] = jnp.tile(cb, (1, GQA_HEADS)) * gq
    p["tsq"] = jnp.tile(sb, (1, GQA_HEADS)) * _swap_pairs(gq)
    p["tck"] = jnp.tile(cb, (1, GQA_KV_HEADS)) * gk
    p["tsk"] = jnp.tile(sb, (1, GQA_KV_HEADS)) * _swap_pairs(gk)
    grp = jnp.arange(W_B) // GQA_DIM
    p["bd"] = (grp[:, None] == grp[None, :]).astype(BF16)

    p["gmlp_v_norm"] = row(P["gmlp_v_norm"][l])
    p["w_s"] = P["gmlp_w_s"][l].reshape(GMLP_GROUPS * GMLP_CHUNK, GMLP_CHUNK).astype(BF16)
    p["bias_s"] = jnp.repeat(P["gmlp_b_s"][l].T, GMLP_DIM, axis=1)
    out_norm = P["out_norm"][l]
    p["out_norm_a"] = row(out_norm[:W_A])
    p["out_norm_b"] = row(out_norm[W_A:W_A + W_B])
    p["out_norm_c"] = row(out_norm[W_A + W_B:])
    p["w_out"] = P["w_out"][l].astype(BF16)

    p["mem_x_norm"] = row(P["mem_x_norm"][l])
    p["mem_kv_norm"] = row(P["mem_kv_norm"][l])
    p["mem_w_q"] = P["mem_w_q"][l].astype(BF16)
    p["mem_w_kv"] = P["mem_w_kv"][l].astype(BF16)
    p["mem_w_o"] = P["mem_w_o"][l].astype(BF16)

    n_chunks = D_FF // FF_CHUNK
    def interleave(w):
        lead = w.shape[:-1]
        gate = w[..., :D_FF].reshape(*lead, n_chunks, 1, FF_CHUNK)
        val = w[..., D_FF:].reshape(*lead, n_chunks, 1, FF_CHUNK)
        return jnp.concatenate([gate, val], axis=-2).reshape(*lead, 2 * D_FF)
    p["ffn_norm"] = row(P["ffn_norm"][l])
    p["ffn_w_up"] = interleave(P["ffn_w_up"][l]).astype(BF16)
    p["ffn_conv_w"] = interleave(P["ffn_conv_w"][l])
    p["ffn_conv_b"] = interleave(row(P["ffn_conv_b"][l]))
    p["ffn_w_down"] = P["ffn_w_down"][l].astype(BF16)
    return p


def kernel(x, mem, mix_norm, w_in, mla_q_norm, mla_w_uq, mla_kv_norm, mla_w_ukv, gqa_q_norm, gqa_k_norm, gmlp_v_norm, gmlp_w_s, gmlp_b_s, out_norm, w_out, mem_x_norm, mem_kv_norm, mem_w_q, mem_w_kv, mem_w_o, ffn_norm, ffn_w_up, ffn_conv_w, ffn_conv_b, ffn_w_down, final_norm):
    P = dict(mix_norm=mix_norm, w_in=w_in, mla_q_norm=mla_q_norm, mla_w_uq=mla_w_uq, mla_kv_norm=mla_kv_norm,
             mla_w_ukv=mla_w_ukv, gqa_q_norm=gqa_q_norm, gqa_k_norm=gqa_k_norm, gmlp_v_norm=gmlp_v_norm,
             gmlp_w_s=gmlp_w_s, gmlp_b_s=gmlp_b_s, out_norm=out_norm, w_out=w_out, mem_x_norm=mem_x_norm,
             mem_kv_norm=mem_kv_norm, mem_w_q=mem_w_q, mem_w_kv=mem_w_kv, mem_w_o=mem_w_o, ffn_norm=ffn_norm,
             ffn_w_up=ffn_w_up, ffn_conv_w=ffn_conv_w, ffn_conv_b=ffn_conv_b, ffn_w_down=ffn_w_down)
    B, S, D = x.shape
    assert D == D_MODEL and S % TOK_TILE == 0 and S % GRID_W == 0
    depth = w_in.shape[0]
    rope_a = _rope_tables(S, MLA_ROPE)
    rope_b = _rope_tables(S, GQA_DIM)
    final_gain = final_norm.reshape(1, -1)
    group = GQA_HEADS // GQA_KV_HEADS
    for l in range(depth):
        p = _layer_params(l, S, rope_a, rope_b, P)
        qaT, ka, vaT, qbT, kb, vbT, ync = _mix_in(x, p)
        yna = _attention(qaT, ka, vaT, p["out_norm_a"], n_heads=MLA_HEADS,
                         k_lanes=tuple(h * HEAD_SLOT for h in range(MLA_HEADS)), bias_lane=MLA_BIAS_LANE,
                         v_rows=tuple(h * (MLA_V + V_PAD) for h in range(MLA_HEADS)), dv=MLA_V, name="attn_mla")
        ynb = _attention(qbT, kb, vbT, p["out_norm_b"], n_heads=GQA_HEADS,
                         k_lanes=tuple((h // group) * HEAD_SLOT for h in range(GQA_HEADS)), bias_lane=GQA_BIAS_LANE,
                         v_rows=tuple((h // group) * (GQA_DIM + V_PAD) for h in range(GQA_HEADS)), dv=GQA_DIM, name="attn_gqa")
        mem_kT, mem_v = _mem_kv(mem, p["mem_kv_norm"], p["mem_w_kv"])
        x = _out_mem(yna, ynb, ync, x, p, mem_kT, mem_v)
        x = _ffn(x, p, final_gain, final=(l == depth - 1))
    return x
```

```python
import functools

import jax
import jax.numpy as jnp
from jax import lax
from jax.experimental import pallas as pl
from jax.experimental.pallas import tpu as pltpu

F32 = jnp.float32
BF16 = jnp.bfloat16

D_MODEL = 1024
GRID_W = 64
ROPE_THETA = 10000.0
EPS = 1e-6
MLA_HEADS = 6
MLA_NOPE = 64
MLA_ROPE = 32
MLA_V = 64
MLA_Q_RANK = 256
MLA_KV_RANK = 128
GQA_HEADS = 6
GQA_KV_HEADS = 2
GQA_DIM = 64
GMLP_GROUPS = 4
GMLP_DIM = 64
GMLP_CHUNK = 128
W_A = MLA_HEADS * MLA_V
W_B = GQA_HEADS * GQA_DIM
W_C = GMLP_GROUPS * GMLP_DIM
MEM_HEADS = 4
MEM_DIM = 128
D_FF = 2816

LANES = 128
HEAD_SLOT = LANES
TOK_TILE = 512
Q_TILE = 512
KV_TILE = 512
F32_ROWS = 8
BF16_ROWS = 16
V_PAD = BF16_ROWS
LOG2E = 1.4426950408889634
MLA_BIAS_LANE = MLA_NOPE + MLA_ROPE
GQA_BIAS_LANE = GQA_DIM
STABILISER_MAX = 50.0
FF_CHUNK = 256
HALO = BF16_ROWS
VMEM_LIMIT = 56 * 1024 * 1024

Z_CQ, Z_CKV, Z_GQ, Z_GQS, Z_GK, Z_GKS, Z_GV, Z_U, Z_VV, Z_KR, Z_END = (
    0, 256, 384, 768, 1152, 1280, 1408, 1536, 1792, 2048, 2176)


def _rms(x, g):
    return x * lax.rsqrt(jnp.mean(x * x, axis=-1, keepdims=True) + EPS) * g


def _dot(a, b):
    return jnp.dot(a, b, preferred_element_type=F32)


def _const_spec(shape):
    zeros = (0,) * len(shape)
    return pl.BlockSpec(shape, lambda *_: zeros, pipeline_mode=pl.Buffered(1))


def _params(n_axes):
    return pltpu.CompilerParams(dimension_semantics=("arbitrary",) * n_axes,
                                vmem_limit_bytes=VMEM_LIMIT)


def _group_ssq(v, bd):
    sq = v * v
    hi = sq.astype(BF16)
    lo = (sq - hi.astype(F32)).astype(BF16)
    return _dot(hi, bd) + _dot(lo, bd)


def _store_vT(ref, vT, n_heads, dv):
    ext = (lax.broadcasted_iota(jnp.int32, (V_PAD, KV_TILE), 0) == 0).astype(BF16)
    for n in range(vT.shape[1] // KV_TILE):
        for hd in range(n_heads):
            base = hd * (dv + V_PAD)
            ref[0, n, base:base + dv] = vT[hd * dv:(hd + 1) * dv, n * KV_TILE:(n + 1) * KV_TILE].astype(BF16)
            ref[0, n, base + dv:base + dv + V_PAD] = ext


def _store_key_norms(ref, k, n_slots, n_feat):
    lane = lax.broadcasted_iota(jnp.int32, (k.shape[0], HEAD_SLOT), 1)
    rows = []
    for n in range(n_slots):
        x = jnp.where(lane < n_feat, k[:, n * HEAD_SLOT:(n + 1) * HEAD_SLOT].astype(F32), 0.0)
        n2 = jnp.max(jnp.sum(x * x, axis=-1, keepdims=True), axis=0, keepdims=True)
        rows.append(jnp.broadcast_to(n2, (1, LANES)))
    rows.append(jnp.zeros((F32_ROWS - n_slots, LANES), F32))
    ref[0, 0] = jnp.concatenate(rows, axis=0)


def _mix_in_kernel(x_ref, g_ref, win_ref, gq_ref, wq_ref, gkv_ref, wk_ref, wv_ref,
                   cqa_ref, sqa_ref, tk_ref, tcq_ref, tsq_ref, tck_ref, tsk_ref,
                   bd_ref, gv_ref, ws_ref, bias_ref, gc_ref,
                   qaT_ref, ka_ref, vaT_ref, kna_ref, qbT_ref, kb_ref, vbT_ref, knb_ref, ync_ref):
    tok = x_ref.shape[1]
    h = _rms(x_ref[0], g_ref[...])
    z = _dot(h.astype(BF16), win_ref[...])

    cq = _rms(z[:, Z_CQ:Z_CKV], gq_ref[...]).astype(BF16)
    qa = _dot(cq, wq_ref[...])
    half = MLA_HEADS * HEAD_SLOT
    q_a = qa[:, :half] * cqa_ref[...] + qa[:, half:] * sqa_ref[...]
    qaT_ref[0] = q_a.T.astype(BF16)

    ckv = _rms(z[:, Z_CKV:Z_GQ], gkv_ref[...])
    kr = z[:, Z_KR:Z_END] * tk_ref[...]
    lhs = jnp.concatenate([ckv, kr], axis=1).astype(BF16)
    slot_lane = lax.broadcasted_iota(jnp.int32, (1, MLA_HEADS * HEAD_SLOT), 1) % HEAD_SLOT
    k_a = (_dot(lhs, wk_ref[...]) + (slot_lane == MLA_BIAS_LANE).astype(F32)).astype(BF16)
    ka_ref[0] = k_a
    _store_key_norms(kna_ref, k_a, MLA_HEADS, MLA_BIAS_LANE)
    _store_vT(vaT_ref, _dot(lhs[:, :MLA_KV_RANK], wv_ref[...]).T, MLA_HEADS, MLA_V)

    bd = bd_ref[...]
    g_q = z[:, Z_GQ:Z_GQS]
    n_q = lax.rsqrt(_group_ssq(g_q, bd) * (1.0 / GQA_DIM) + EPS)
    q_b = (g_q * tcq_ref[...] + z[:, Z_GQS:Z_GK] * tsq_ref[...]) * n_q
    q_bT = q_b.T
    zero = jnp.zeros((HEAD_SLOT - GQA_DIM, tok), BF16)
    for hd in range(GQA_HEADS):
        qbT_ref[0, hd * HEAD_SLOT:hd * HEAD_SLOT + GQA_DIM] = q_bT[hd * GQA_DIM:(hd + 1) * GQA_DIM].astype(BF16)
        qbT_ref[0, hd * HEAD_SLOT + GQA_DIM:(hd + 1) * HEAD_SLOT] = zero
    g_k = z[:, Z_GK:Z_GKS]
    n_k = lax.rsqrt(_group_ssq(g_k, bd[:LANES, :LANES]) * (1.0 / GQA_DIM) + EPS)
    k_b = (g_k * tck_ref[...] + z[:, Z_GKS:Z_GV] * tsk_ref[...]) * n_k
    lane = lax.broadcasted_iota(jnp.int32, (tok, LANES), 1)
    bias_lane = (lane == GQA_BIAS_LANE).astype(F32)
    k_slots = [jnp.where(lane < GQA_DIM, k_b, bias_lane),
               jnp.where(lane < GQA_DIM, pltpu.roll(k_b, GQA_DIM, 1), bias_lane)]
    k_b = jnp.concatenate(k_slots, axis=1).astype(BF16)
    kb_ref[0] = k_b
    _store_key_norms(knb_ref, k_b, GQA_KV_HEADS, GQA_BIAS_LANE)
    _store_vT(vbT_ref, z[:, Z_GV:Z_U].T, GQA_KV_HEADS, GQA_DIM)

    gm = jax.nn.gelu(z[:, Z_U:Z_KR])
    u = gm[:, :W_C]
    vv = _rms(gm[:, W_C:], gv_ref[...])
    lane_grp = lax.broadcasted_iota(jnp.int32, (GMLP_CHUNK, W_C), 1) // GMLP_DIM
    ws = ws_ref[...]
    bias = bias_ref[...]
    ycs = []
    for n in range(tok // GMLP_CHUNK):
        rows = slice(n * GMLP_CHUNK, (n + 1) * GMLP_CHUNK)
        r = _dot(ws, vv[rows].astype(BF16))
        mixed = r[3 * GMLP_CHUNK:]
        for grp in range(GMLP_GROUPS - 2, -1, -1):
            mixed = jnp.where(lane_grp == grp, r[grp * GMLP_CHUNK:(grp + 1) * GMLP_CHUNK], mixed)
        ycs.append(u[rows] * (mixed + bias))
    ync_ref[0] = _rms(jnp.concatenate(ycs, axis=0), gc_ref[...]).astype(BF16)


def _mix_in(x, p):
    B, S, D = x.shape
    T = TOK_TILE
    nkv = T // KV_TILE
    tile = lambda w: pl.BlockSpec((1, T, w), lambda s, b: (b, s, 0))
    tab = lambda w: pl.BlockSpec((T, w), lambda s, b: (s, 0))
    tposed = lambda r: pl.BlockSpec((1, r, T), lambda s, b: (b, 0, s))
    blocked = lambda r: pl.BlockSpec((1, nkv, r, KV_TILE), lambda s, b: (b, s, 0, 0))
    consts = [p["mix_norm"], p["w_in"], p["mla_q_norm"], p["w_uq"], p["mla_kv_norm"], p["w_k"], p["w_v"]]
    tabs = [p["cqa"], p["sqa"], p["tk"], p["tcq"], p["tsq"], p["tck"], p["tsk"]]
    consts2 = [p["bd"], p["gmlp_v_norm"], p["w_s"], p["bias_s"], p["out_norm_c"]]
    in_specs = ([tile(D)] + [_const_spec(c.shape) for c in consts] + [tab(t.shape[1]) for t in tabs]
                + [_const_spec(c.shape) for c in consts2])
    out_shape = [
        jax.ShapeDtypeStruct((B, MLA_HEADS * HEAD_SLOT, S), BF16),
        jax.ShapeDtypeStruct((B, S, MLA_HEADS * HEAD_SLOT), BF16),
        jax.ShapeDtypeStruct((B, S // KV_TILE, MLA_HEADS * (MLA_V + V_PAD), KV_TILE), BF16),
        jax.ShapeDtypeStruct((B, S // T, F32_ROWS, LANES), F32),
        jax.ShapeDtypeStruct((B, GQA_HEADS * HEAD_SLOT, S), BF16),
        jax.ShapeDtypeStruct((B, S, GQA_KV_HEADS * HEAD_SLOT), BF16),
        jax.ShapeDtypeStruct((B, S // KV_TILE, GQA_KV_HEADS * (GQA_DIM + V_PAD), KV_TILE), BF16),
        jax.ShapeDtypeStruct((B, S // T, F32_ROWS, LANES), F32),
        jax.ShapeDtypeStruct((B, S, W_C), BF16),
    ]
    norms = pl.BlockSpec((1, 1, F32_ROWS, LANES), lambda s, b: (b, s, 0, 0))
    out_specs = [tposed(MLA_HEADS * HEAD_SLOT), tile(MLA_HEADS * HEAD_SLOT), blocked(MLA_HEADS * (MLA_V + V_PAD)),
                 norms, tposed(GQA_HEADS * HEAD_SLOT), tile(GQA_KV_HEADS * HEAD_SLOT),
                 blocked(GQA_KV_HEADS * (GQA_DIM + V_PAD)), norms, tile(W_C)]
    return pl.pallas_call(
        _mix_in_kernel, grid=(S // T, B), in_specs=in_specs, out_specs=out_specs, out_shape=out_shape,
        compiler_params=_params(2), name="mix_in",
    )(x, *consts, *tabs, *consts2)


def _attn_kernel(q_ref, k_ref, v_ref, kn_ref, g_ref, o_ref, s_ref, p_ref, m_ref, acc_ref, oT_ref, qx_ref, *,
                 n_heads, k_lanes, bias_lane, v_rows, dv):
    nkv = v_ref.shape[1]
    mq = q_ref.shape[2]
    assert n_heads % 2 == 0

    kmax = jnp.max(kn_ref[0], axis=0)
    first_row = lax.broadcasted_iota(jnp.int32, (BF16_ROWS, mq), 0) == 0
    bound = jnp.zeros((1, 1), F32)
    for hd in range(n_heads):
        rows = slice(hd * HEAD_SLOT, (hd + 1) * HEAD_SLOT)
        q = q_ref[0, rows, :]
        qf = q.astype(F32)
        u = jnp.sqrt(jnp.sum(qf * qf, axis=0, keepdims=True) * kmax[k_lanes[hd] // HEAD_SLOT:k_lanes[hd] // HEAD_SLOT + 1, 0:1])
        bound = jnp.maximum(bound, jnp.max(u, axis=1, keepdims=True))
        qx_ref[rows, :] = q
        bias_rows = slice(hd * HEAD_SLOT + bias_lane, hd * HEAD_SLOT + bias_lane + BF16_ROWS)
        qx_ref[bias_rows, :] = jnp.where(first_row, -u, 0.0).astype(BF16)
    acc_ref[...] = jnp.zeros(acc_ref.shape, F32)

    def run(stabilised):
        def scores(hd, off):
            rows = slice(hd * HEAD_SLOT, (hd + 1) * HEAD_SLOT)
            qT = qx_ref[rows, :] if stabilised else q_ref[0, rows, :]
            k = k_ref[0, pl.ds(off, KV_TILE), k_lanes[hd]:k_lanes[hd] + HEAD_SLOT]
            s_ref[hd % 2] = _dot(k, qT)

        scores(0, 0)

        def step(j, carry):
            off = pl.multiple_of(j * KV_TILE, KV_TILE)
            off_next = pl.multiple_of(jnp.minimum(j + 1, nkv - 1) * KV_TILE, KV_TILE)
            for hd in range(n_heads):
                if hd + 1 < n_heads:
                    scores(hd + 1, off)
                else:
                    scores(0, off_next)
                vT = v_ref[0, j, v_rows[hd]:v_rows[hd] + dv + V_PAD, :]
                if stabilised:
                    p_ref[hd] = jnp.exp2(s_ref[hd % 2].astype(BF16))
                    acc_ref[hd] += _dot(vT, p_ref[hd])
                else:
                    m_old = m_ref[hd]
                    m_new = jnp.maximum(m_old, jnp.max(s_ref[hd % 2], axis=0, keepdims=True))
                    m_ref[hd] = m_new
                    p_ref[hd] = jnp.exp2(s_ref[hd % 2] - m_new).astype(BF16)
                    acc_ref[hd] = jnp.exp2(m_old - m_new) * acc_ref[hd] + _dot(vT, p_ref[hd])
            return carry

        lax.fori_loop(0, nkv, step, 0, unroll=2 if stabilised else 1)

    def running_max():
        m_ref[...] = jnp.full(m_ref.shape, -1e30, F32)
        run(False)

    lax.cond(jnp.max(bound) <= STABILISER_MAX, lambda: run(True), running_max)
    for hd in range(n_heads):
        oT_ref[hd * dv:(hd + 1) * dv, :] = acc_ref[hd, :dv] / acc_ref[hd, dv:dv + 1]
    o_ref[0] = _rms(oT_ref[...].T, g_ref[...]).astype(BF16)


def _attention(qT, k, vT, kn, gain, *, n_heads, k_lanes, bias_lane, v_rows, dv, name):
    B, _, S = qT.shape
    kernel = functools.partial(_attn_kernel, n_heads=n_heads, k_lanes=k_lanes, bias_lane=bias_lane,
                               v_rows=v_rows, dv=dv)
    whole = lambda shape: pl.BlockSpec((1,) + shape[1:], lambda b, i: (b,) + (0,) * (len(shape) - 1),
                                       pipeline_mode=pl.Buffered(1))
    return pl.pallas_call(
        kernel, grid=(B, S // Q_TILE),
        in_specs=[pl.BlockSpec((1, n_heads * HEAD_SLOT, Q_TILE), lambda b, i: (b, 0, i)),
                  whole(k.shape), whole(vT.shape), whole(kn.shape), _const_spec(gain.shape)],
        out_specs=pl.BlockSpec((1, Q_TILE, n_heads * dv), lambda b, i: (b, i, 0)),
        out_shape=jax.ShapeDtypeStruct((B, S, n_heads * dv), BF16),
        scratch_shapes=[pltpu.VMEM((2, KV_TILE, Q_TILE), F32), pltpu.VMEM((n_heads, KV_TILE, Q_TILE), BF16),
                        pltpu.VMEM((n_heads, 1, Q_TILE), F32), pltpu.VMEM((n_heads, dv + V_PAD, Q_TILE), F32),
                        pltpu.VMEM((n_heads * dv, Q_TILE), F32), pltpu.VMEM((n_heads * HEAD_SLOT, Q_TILE), BF16)],
        compiler_params=_params(2), name=name,
    )(qT, k, vT, kn, gain)


def _mem_kv_kernel(mem_ref, g_ref, w_ref, kT_ref, v_ref):
    kv = _dot(_rms(mem_ref[0], g_ref[...]).astype(BF16), w_ref[...])
    width = MEM_HEADS * MEM_DIM
    kT_ref[0] = kv[:, :width].T.astype(BF16)
    ones = jnp.ones((kv.shape[0], MEM_DIM), BF16)
    for hd in range(MEM_HEADS):
        v_ref[0, hd, :, :MEM_DIM] = kv[:, width + hd * MEM_DIM:width + (hd + 1) * MEM_DIM].astype(BF16)
        v_ref[0, hd, :, MEM_DIM:] = ones


def _mem_kv(mem, gain, w_kv):
    B, Tm, D = mem.shape
    width = MEM_HEADS * MEM_DIM
    return pl.pallas_call(
        _mem_kv_kernel, grid=(B,),
        in_specs=[pl.BlockSpec((1, Tm, D), lambda b: (b, 0, 0)), _const_spec(gain.shape), _const_spec(w_kv.shape)],
        out_specs=[pl.BlockSpec((1, width, Tm), lambda b: (b, 0, 0)),
                   pl.BlockSpec((1, MEM_HEADS, Tm, 2 * MEM_DIM), lambda b: (b, 0, 0, 0))],
        out_shape=[jax.ShapeDtypeStruct((B, width, Tm), BF16),
                   jax.ShapeDtypeStruct((B, MEM_HEADS, Tm, 2 * MEM_DIM), BF16)],
        compiler_params=_params(1), name="mem_kv",
    )(mem, gain, w_kv)


def _out_mem_kernel(ya_ref, yb_ref, yc_ref, x_ref, wout_ref, g_ref, wq_ref, kT_ref, v_ref, wo_ref, o_ref):
    y = (_dot(ya_ref[0], wout_ref[0:W_A]) + _dot(yb_ref[0], wout_ref[W_A:W_A + W_B])
         + _dot(yc_ref[0], wout_ref[W_A + W_B:]))
    x1 = x_ref[0] + y
    h = _rms(x1, g_ref[...]).astype(BF16)
    q = (_dot(h, wq_ref[...]) * (MEM_DIM ** -0.5)).astype(BF16)
    heads = []
    for hd in range(MEM_HEADS):
        s = _dot(q[:, hd * MEM_DIM:(hd + 1) * MEM_DIM], kT_ref[0, hd * MEM_DIM:(hd + 1) * MEM_DIM, :])
        p = jnp.exp(s - jnp.max(s, axis=-1, keepdims=True)).astype(BF16)
        pv = _dot(p, v_ref[0, hd])
        heads.append(pv[:, :MEM_DIM] / pv[:, MEM_DIM:])
    o = jnp.concatenate(heads, axis=1).astype(BF16)
    o_ref[0] = x1 + _dot(o, wo_ref[...])


def _out_mem(ya, yb, yc, x, p, mem_kT, mem_v):
    B, S, D = x.shape
    T = TOK_TILE
    tile = lambda w: pl.BlockSpec((1, T, w), lambda b, s: (b, s, 0))
    per_b = lambda a: pl.BlockSpec((1,) + a.shape[1:], lambda b, s: (b,) + (0,) * (a.ndim - 1))
    consts = [p["w_out"], p["mem_x_norm"], p["mem_w_q"]]
    return pl.pallas_call(
        _out_mem_kernel, grid=(B, S // T),
        in_specs=[tile(W_A), tile(W_B), tile(W_C), tile(D)] + [_const_spec(c.shape) for c in consts]
                 + [per_b(mem_kT), per_b(mem_v), _const_spec(p["mem_w_o"].shape)],
        out_specs=tile(D), out_shape=jax.ShapeDtypeStruct((B, S, D), F32),
        compiler_params=_params(2), name="out_mem",
    )(ya, yb, yc, x, *consts, mem_kT, mem_v, p["mem_w_o"])


def _ffn_kernel(x_ref, xp_ref, xn_ref, g_ref, wup_ref, cw_ref, cb_ref, wdn_ref, fg_ref, o_ref,
                h_ref, act_ref, *, final):
    tok = x_ref.shape[1]
    i = pl.program_id(1)
    g = g_ref[...]
    x = x_ref[0]
    keep_prev = (i > 0).astype(F32)
    keep_next = (i < pl.num_programs(1) - 1).astype(F32)
    h_ref[0:HALO] = (_rms(xp_ref[0], g) * keep_prev).astype(BF16)
    h_ref[HALO:HALO + tok] = _rms(x, g).astype(BF16)
    h_ref[HALO + tok:] = (_rms(xn_ref[0], g) * keep_next).astype(BF16)
    hext = h_ref[...]
    rows = tok + 2 * HALO
    for c in range(D_FF // FF_CHUNK):
        cols = slice(2 * FF_CHUNK * c, 2 * FF_CHUNK * (c + 1))
        a = _dot(hext, wup_ref[:, cols])
        w = cw_ref[:, cols]
        ac = (pltpu.roll(a, 1, 0)[HALO:HALO + tok] * w[0:1] + a[HALO:HALO + tok] * w[1:2]
              + pltpu.roll(a, rows - 1, 0)[HALO:HALO + tok] * w[2:3] + cb_ref[:, cols])
        act = jax.nn.silu(ac[:, :FF_CHUNK]) * ac[:, FF_CHUNK:]
        act_ref[:, FF_CHUNK * c:FF_CHUNK * (c + 1)] = act.astype(BF16)
    y = x + _dot(act_ref[...], wdn_ref[...])
    if final:
        y = _rms(y, fg_ref[...])
    o_ref[0] = y


def _ffn(x, p, final_gain, *, final):
    B, S, D = x.shape
    T = TOK_TILE
    per_tile = T // HALO
    n_halo = S // HALO
    tile = pl.BlockSpec((1, T, D), lambda b, s: (b, s, 0))
    prev = pl.BlockSpec((1, HALO, D), lambda b, s: (b, jnp.maximum(s * per_tile - 1, 0), 0))
    nxt = pl.BlockSpec((1, HALO, D), lambda b, s: (b, jnp.minimum((s + 1) * per_tile, n_halo - 1), 0))
    consts = [p["ffn_norm"], p["ffn_w_up"], p["ffn_conv_w"], p["ffn_conv_b"], p["ffn_w_down"], final_gain]
    return pl.pallas_call(
        functools.partial(_ffn_kernel, final=final), grid=(B, S // T),
        in_specs=[tile, prev, nxt] + [_const_spec(c.shape) for c in consts],
        out_specs=tile, out_shape=jax.ShapeDtypeStruct((B, S, D), F32),
        scratch_shapes=[pltpu.VMEM((T + 2 * HALO, D), BF16), pltpu.VMEM((T, D_FF), BF16)],
        compiler_params=_params(2), name="ffn_final" if final else "ffn",
    )(x, x, x, *consts)


def _swap_pairs(w):
    n = w.shape[-1]
    return w.reshape(*w.shape[:-1], n // 2, 2)[..., ::-1].reshape(w.shape)


def _rope_tables(S, d_rot):
    rows = S // GRID_W
    row = jnp.repeat(jnp.arange(rows, dtype=F32), GRID_W)
    col = jnp.tile(jnp.arange(GRID_W, dtype=F32), rows)
    n = d_rot // 4
    inv = ROPE_THETA ** (-jnp.arange(n, dtype=F32) / n)
    ang = jnp.concatenate([row[:, None] * inv, col[:, None] * inv], axis=-1)
    cos, sin = jnp.cos(ang), jnp.sin(ang)
    c = jnp.repeat(cos, 2, axis=-1)
    s = jnp.stack([-sin, sin], axis=-1).reshape(S, d_rot)
    return c, s


def _layer_params(l, S, rope_a, rope_b, P):
    row = lambda v: v.reshape(1, -1)
    ca, sa = rope_a
    cb, sb = rope_b
    p = {}
    w_in = P["w_in"][l]
    c_q, c_kv, k_rope, g_q, g_k, g_v, g_m = jnp.split(
        w_in, [256, 384, 416, 800, 928, 1056], axis=1)
    pad = jnp.zeros((D_MODEL, Z_END - Z_KR - 2 * MLA_ROPE), F32)
    p["w_in"] = jnp.concatenate(
        [c_q, c_kv, g_q, _swap_pairs(g_q), g_k, _swap_pairs(g_k), g_v, g_m, k_rope, _swap_pairs(k_rope), pad],
        axis=1).astype(BF16)
    p["mix_norm"] = row(P["mix_norm"][l])
    p["mla_q_norm"] = row(P["mla_q_norm"][l])
    p["mla_kv_norm"] = row(P["mla_kv_norm"][l])

    w_uq = P["mla_w_uq"][l].reshape(MLA_Q_RANK, MLA_HEADS, MLA_NOPE + MLA_ROPE)
    zpad = jnp.zeros((MLA_Q_RANK, MLA_HEADS, HEAD_SLOT - MLA_NOPE - MLA_ROPE), F32)
    main = jnp.concatenate([w_uq, zpad], axis=-1)
    swapped = jnp.concatenate([jnp.zeros_like(w_uq[..., :MLA_NOPE]), _swap_pairs(w_uq[..., MLA_NOPE:]), zpad], axis=-1)
    p["w_uq"] = jnp.concatenate([main.reshape(MLA_Q_RANK, -1), swapped.reshape(MLA_Q_RANK, -1)], axis=1).astype(BF16)
    scale_a = (MLA_NOPE + MLA_ROPE) ** -0.5 * LOG2E
    ones = jnp.ones((S, MLA_NOPE), F32)
    zeros_n = jnp.zeros((S, MLA_NOPE), F32)
    zeros_p = jnp.zeros((S, HEAD_SLOT - MLA_NOPE - MLA_ROPE), F32)
    p["cqa"] = jnp.tile(jnp.concatenate([ones, ca, zeros_p], axis=1) * scale_a, (1, MLA_HEADS))
    p["sqa"] = jnp.tile(jnp.concatenate([zeros_n, sa, zeros_p], axis=1) * scale_a, (1, MLA_HEADS))

    w_ukv = P["mla_w_ukv"][l].reshape(MLA_KV_RANK, MLA_HEADS, MLA_NOPE + MLA_V)
    k_lat = jnp.concatenate(
        [w_ukv[..., :MLA_NOPE], jnp.zeros((MLA_KV_RANK, MLA_HEADS, HEAD_SLOT - MLA_NOPE), F32)], axis=-1)
    place = jnp.concatenate([jnp.zeros((MLA_ROPE, MLA_NOPE), F32), jnp.eye(MLA_ROPE, dtype=F32),
                             jnp.zeros((MLA_ROPE, HEAD_SLOT - MLA_NOPE - MLA_ROPE), F32)], axis=1)
    place = jnp.tile(place, (1, MLA_HEADS))
    zrows = jnp.zeros((Z_END - Z_KR - 2 * MLA_ROPE, MLA_HEADS * HEAD_SLOT), F32)
    p["w_k"] = jnp.concatenate([k_lat.reshape(MLA_KV_RANK, -1), place, place, zrows], axis=0).astype(BF16)
    p["w_v"] = w_ukv[..., MLA_NOPE:].reshape(MLA_KV_RANK, -1).astype(BF16)
    p["tk"] = jnp.concatenate([ca, sa, jnp.zeros((S, Z_END - Z_KR - 2 * MLA_ROPE), F32)], axis=1)

    gq = jnp.tile(P["gqa_q_norm"][l], GQA_HEADS) * (GQA_DIM ** -0.5 * LOG2E)
    gk = jnp.tile(P["gqa_k_norm"][l], GQA_KV_HEADS)
    p["tcq"] = jnp.tile(cb, (1, GQA_HEADS)) * gq
    p["tsq"] = jnp.tile(sb, (1, GQA_HEADS)) * _swap_pairs(gq)
    p["tck"] = jnp.tile(cb, (1, GQA_KV_HEADS)) * gk
    p["tsk"] = jnp.tile(sb, (1, GQA_KV_HEADS)) * _swap_pairs(gk)
    grp = jnp.arange(W_B) // GQA_DIM
    p["bd"] = (grp[:, None] == grp[None, :]).astype(BF16)

    p["gmlp_v_norm"] = row(P["gmlp_v_norm"][l])
    p["w_s"] = P["gmlp_w_s"][l].reshape(GMLP_GROUPS * GMLP_CHUNK, GMLP_CHUNK).astype(BF16)
    p["bias_s"] = jnp.repeat(P["gmlp_b_s"][l].T, GMLP_DIM, axis=1)
    out_norm = P["out_norm"][l]
    p["out_norm_a"] = row(out_norm[:W_A])
    p["out_norm_b"] = row(out_norm[W_A:W_A + W_B])
    p["out_norm_c"] = row(out_norm[W_A + W_B:])
    p["w_out"] = P["w_out"][l].astype(BF16)

    p["mem_x_norm"] = row(P["mem_x_norm"][l])
    p["mem_kv_norm"] = row(P["mem_kv_norm"][l])
    p["mem_w_q"] = P["mem_w_q"][l].astype(BF16)
    p["mem_w_kv"] = P["mem_w_kv"][l].astype(BF16)
    p["mem_w_o"] = P["mem_w_o"][l].astype(BF16)

    n_chunks = D_FF // FF_CHUNK
    def interleave(w):
        lead = w.shape[:-1]
        gate = w[..., :D_FF].reshape(*lead, n_chunks, 1, FF_CHUNK)
        val = w[..., D_FF:].reshape(*lead, n_chunks, 1, FF_CHUNK)
        return jnp.concatenate([gate, val], axis=-2).reshape(*lead, 2 * D_FF)
    p["ffn_norm"] = row(P["ffn_norm"][l])
    p["ffn_w_up"] = interleave(P["ffn_w_up"][l]).astype(BF16)
    p["ffn_conv_w"] = interleave(P["ffn_conv_w"][l])
    p["ffn_conv_b"] = interleave(row(P["ffn_conv_b"][l]))
    p["ffn_w_down"] = P["ffn_w_down"][l].astype(BF16)
    return p


def kernel(x, mem, mix_norm, w_in, mla_q_norm, mla_w_uq, mla_kv_norm, mla_w_ukv, gqa_q_norm, gqa_k_norm, gmlp_v_norm, gmlp_w_s, gmlp_b_s, out_norm, w_out, mem_x_norm, mem_kv_norm, mem_w_q, mem_w_kv, mem_w_o, ffn_norm, ffn_w_up, ffn_conv_w, ffn_conv_b, ffn_w_down, final_norm):
    P = dict(mix_norm=mix_norm, w_in=w_in, mla_q_norm=mla_q_norm, mla_w_uq=mla_w_uq, mla_kv_norm=mla_kv_norm,
             mla_w_ukv=mla_w_ukv, gqa_q_norm=gqa_q_norm, gqa_k_norm=gqa_k_norm, gmlp_v_norm=gmlp_v_norm,
             gmlp_w_s=gmlp_w_s, gmlp_b_s=gmlp_b_s, out_norm=out_norm, w_out=w_out, mem_x_norm=mem_x_norm,
             mem_kv_norm=mem_kv_norm, mem_w_q=mem_w_q, mem_w_kv=mem_w_kv, mem_w_o=mem_w_o, ffn_norm=ffn_norm,
             ffn_w_up=ffn_w_up, ffn_conv_w=ffn_conv_w, ffn_conv_b=ffn_conv_b, ffn_w_down=ffn_w_down)
    B, S, D = x.shape
    assert D == D_MODEL and S % TOK_TILE == 0 and S % GRID_W == 0
    depth = w_in.shape[0]
    rope_a = _rope_tables(S, MLA_ROPE)
    rope_b = _rope_tables(S, GQA_DIM)
    final_gain = final_norm.reshape(1, -1)
    group = GQA_HEADS // GQA_KV_HEADS
    for l in range(depth):
        p = _layer_params(l, S, rope_a, rope_b, P)
        qaT, ka, vaT, kna, qbT, kb, vbT, knb, ync = _mix_in(x, p)
        yna = _attention(qaT, ka, vaT, kna, p["out_norm_a"], n_heads=MLA_HEADS,
                         k_lanes=tuple(h * HEAD_SLOT for h in range(MLA_HEADS)), bias_lane=MLA_BIAS_LANE,
                         v_rows=tuple(h * (MLA_V + V_PAD) for h in range(MLA_HEADS)), dv=MLA_V, name="attn_mla")
        ynb = _attention(qbT, kb, vbT, knb, p["out_norm_b"], n_heads=GQA_HEADS,
                         k_lanes=tuple((h // group) * HEAD_SLOT for h in range(GQA_HEADS)), bias_lane=GQA_BIAS_LANE,
                         v_rows=tuple((h // group) * (GQA_DIM + V_PAD) for h in range(GQA_HEADS)), dv=GQA_DIM, name="attn_gqa")
        mem_kT, mem_v = _mem_kv(mem, p["mem_kv_norm"], p["mem_w_kv"])
        x = _out_mem(yna, ynb, ync, x, p, mem_kT, mem_v)
        x = _ffn(x, p, final_gain, final=(l == depth - 1))
    return x
```

```python
import functools

import jax
import jax.numpy as jnp
from jax import lax
from jax.experimental import pallas as pl
from jax.experimental.pallas import tpu as pltpu

F32 = jnp.float32
BF16 = jnp.bfloat16

D_MODEL = 1024
GRID_W = 64
ROPE_THETA = 10000.0
EPS = 1e-6
MLA_HEADS = 6
MLA_NOPE = 64
MLA_ROPE = 32
MLA_V = 64
MLA_Q_RANK = 256
MLA_KV_RANK = 128
GQA_HEADS = 6
GQA_KV_HEADS = 2
GQA_DIM = 64
GMLP_GROUPS = 4
GMLP_DIM = 64
GMLP_CHUNK = 128
W_A = MLA_HEADS * MLA_V
W_B = GQA_HEADS * GQA_DIM
W_C = GMLP_GROUPS * GMLP_DIM
MEM_HEADS = 4
MEM_DIM = 128
D_FF = 2816

LANES = 128
HEAD_SLOT = LANES
TOK_TILE = 512
Q_TILE = 512
KV_TILE = 512
F32_ROWS = 8
BF16_ROWS = 16
V_PAD = BF16_ROWS
LOG2E = 1.4426950408889634
MLA_BIAS_LANE = MLA_NOPE + MLA_ROPE
GQA_BIAS_LANE = GQA_DIM
STABILISER_MAX = 50.0
FF_CHUNK = 256
HALO = BF16_ROWS
VMEM_LIMIT = 56 * 1024 * 1024

Z_CQ, Z_CKV, Z_GQ, Z_GQS, Z_GK, Z_GKS, Z_GV, Z_U, Z_VV, Z_KR, Z_END = (
    0, 256, 384, 768, 1152, 1280, 1408, 1536, 1792, 2048, 2176)


def _rms(x, g):
    return x * lax.rsqrt(jnp.mean(x * x, axis=-1, keepdims=True) + EPS) * g


def _dot(a, b):
    return jnp.dot(a, b, preferred_element_type=F32)


def _const_spec(shape):
    zeros = (0,) * len(shape)
    return pl.BlockSpec(shape, lambda *_: zeros, pipeline_mode=pl.Buffered(1))


def _params(n_axes):
    return pltpu.CompilerParams(dimension_semantics=("arbitrary",) * n_axes,
                                vmem_limit_bytes=VMEM_LIMIT)


def _group_ssq(v, bd):
    sq = v * v
    hi = sq.astype(BF16)
    lo = (sq - hi.astype(F32)).astype(BF16)
    return _dot(hi, bd) + _dot(lo, bd)


def _store_vT(ref, vT, n_heads, dv):
    ext = (lax.broadcasted_iota(jnp.int32, (V_PAD, KV_TILE), 0) == 0).astype(BF16)
    for n in range(vT.shape[1] // KV_TILE):
        for hd in range(n_heads):
            base = hd * (dv + V_PAD)
            ref[0, n, base:base + dv] = vT[hd * dv:(hd + 1) * dv, n * KV_TILE:(n + 1) * KV_TILE].astype(BF16)
            ref[0, n, base + dv:base + dv + V_PAD] = ext


def _store_key_norms(ref, k, n_slots, n_feat):
    lane = lax.broadcasted_iota(jnp.int32, (k.shape[0], HEAD_SLOT), 1)
    rows = []
    for n in range(n_slots):
        x = jnp.where(lane < n_feat, k[:, n * HEAD_SLOT:(n + 1) * HEAD_SLOT].astype(F32), 0.0)
        n2 = jnp.max(jnp.sum(x * x, axis=-1, keepdims=True), axis=0, keepdims=True)
        rows.append(jnp.broadcast_to(n2, (1, LANES)))
    rows.append(jnp.zeros((F32_ROWS - n_slots, LANES), F32))
    ref[0, 0] = jnp.concatenate(rows, axis=0)


def _mix_in_kernel(x_ref, g_ref, win_ref, gq_ref, wq_ref, gkv_ref, wk_ref, wv_ref,
                   cqa_ref, sqa_ref, tk_ref, tcq_ref, tsq_ref, tck_ref, tsk_ref,
                   bd_ref, gv_ref, ws_ref, bias_ref, gc_ref,
                   qaT_ref, ka_ref, vaT_ref, kna_ref, qbT_ref, kb_ref, vbT_ref, knb_ref, ync_ref):
    tok = x_ref.shape[1]
    h = _rms(x_ref[0], g_ref[...])
    z = _dot(h.astype(BF16), win_ref[...])

    cq = _rms(z[:, Z_CQ:Z_CKV], gq_ref[...]).astype(BF16)
    qa = _dot(cq, wq_ref[...])
    half = MLA_HEADS * HEAD_SLOT
    q_a = qa[:, :half] * cqa_ref[...] + qa[:, half:] * sqa_ref[...]
    qaT_ref[0] = q_a.T.astype(BF16)

    ckv = _rms(z[:, Z_CKV:Z_GQ], gkv_ref[...])
    kr = z[:, Z_KR:Z_END] * tk_ref[...]
    lhs = jnp.concatenate([ckv, kr], axis=1).astype(BF16)
    slot_lane = lax.broadcasted_iota(jnp.int32, (1, MLA_HEADS * HEAD_SLOT), 1) % HEAD_SLOT
    k_a = (_dot(lhs, wk_ref[...]) + (slot_lane == MLA_BIAS_LANE).astype(F32)).astype(BF16)
    ka_ref[0] = k_a
    _store_key_norms(kna_ref, k_a, MLA_HEADS, MLA_BIAS_LANE)
    _store_vT(vaT_ref, _dot(lhs[:, :MLA_KV_RANK], wv_ref[...]).T, MLA_HEADS, MLA_V)

    bd = bd_ref[...]
    g_q = z[:, Z_GQ:Z_GQS]
    n_q = lax.rsqrt(_group_ssq(g_q, bd) * (1.0 / GQA_DIM) + EPS)
    q_b = (g_q * tcq_ref[...] + z[:, Z_GQS:Z_GK] * tsq_ref[...]) * n_q
    q_bT = q_b.T
    zero = jnp.zeros((HEAD_SLOT - GQA_DIM, tok), BF16)
    for hd in range(GQA_HEADS):
        qbT_ref[0, hd * HEAD_SLOT:hd * HEAD_SLOT + GQA_DIM] = q_bT[hd * GQA_DIM:(hd + 1) * GQA_DIM].astype(BF16)
        qbT_ref[0, hd * HEAD_SLOT + GQA_DIM:(hd + 1) * HEAD_SLOT] = zero
    g_k = z[:, Z_GK:Z_GKS]
    n_k = lax.rsqrt(_group_ssq(g_k, bd[:LANES, :LANES]) * (1.0 / GQA_DIM) + EPS)
    k_b = (g_k * tck_ref[...] + z[:, Z_GKS:Z_GV] * tsk_ref[...]) * n_k
    lane = lax.broadcasted_iota(jnp.int32, (tok, LANES), 1)
    bias_lane = (lane == GQA_BIAS_LANE).astype(F32)
    k_slots = [jnp.where(lane < GQA_DIM, k_b, bias_lane),
               jnp.where(lane < GQA_DIM, pltpu.roll(k_b, GQA_DIM, 1), bias_lane)]
    k_b = jnp.concatenate(k_slots, axis=1).astype(BF16)
    kb_ref[0] = k_b
    _store_key_norms(knb_ref, k_b, GQA_KV_HEADS, GQA_BIAS_LANE)
    _store_vT(vbT_ref, z[:, Z_GV:Z_U].T, GQA_KV_HEADS, GQA_DIM)

    gm = jax.nn.gelu(z[:, Z_U:Z_KR])
    u = gm[:, :W_C]
    vv = _rms(gm[:, W_C:], gv_ref[...])
    lane_grp = lax.broadcasted_iota(jnp.int32, (GMLP_CHUNK, W_C), 1) // GMLP_DIM
    ws = ws_ref[...]
    bias = bias_ref[...]
    ycs = []
    for n in range(tok // GMLP_CHUNK):
        rows = slice(n * GMLP_CHUNK, (n + 1) * GMLP_CHUNK)
        r = _dot(ws, vv[rows].astype(BF16))
        mixed = r[3 * GMLP_CHUNK:]
        for grp in range(GMLP_GROUPS - 2, -1, -1):
            mixed = jnp.where(lane_grp == grp, r[grp * GMLP_CHUNK:(grp + 1) * GMLP_CHUNK], mixed)
        ycs.append(u[rows] * (mixed + bias))
    ync_ref[0] = _rms(jnp.concatenate(ycs, axis=0), gc_ref[...]).astype(BF16)


def _mix_in(x, p):
    B, S, D = x.shape
    T = TOK_TILE
    nkv = T // KV_TILE
    tile = lambda w: pl.BlockSpec((1, T, w), lambda s, b: (b, s, 0))
    tab = lambda w: pl.BlockSpec((T, w), lambda s, b: (s, 0))
    tposed = lambda r: pl.BlockSpec((1, r, T), lambda s, b: (b, 0, s))
    blocked = lambda r: pl.BlockSpec((1, nkv, r, KV_TILE), lambda s, b: (b, s, 0, 0))
    consts = [p["mix_norm"], p["w_in"], p["mla_q_norm"], p["w_uq"], p["mla_kv_norm"], p["w_k"], p["w_v"]]
    tabs = [p["cqa"], p["sqa"], p["tk"], p["tcq"], p["tsq"], p["tck"], p["tsk"]]
    consts2 = [p["bd"], p["gmlp_v_norm"], p["w_s"], p["bias_s"], p["out_norm_c"]]
    in_specs = ([tile(D)] + [_const_spec(c.shape) for c in consts] + [tab(t.shape[1]) for t in tabs]
                + [_const_spec(c.shape) for c in consts2])
    out_shape = [
        jax.ShapeDtypeStruct((B, MLA_HEADS * HEAD_SLOT, S), BF16),
        jax.ShapeDtypeStruct((B, S, MLA_HEADS * HEAD_SLOT), BF16),
        jax.ShapeDtypeStruct((B, S // KV_TILE, MLA_HEADS * (MLA_V + V_PAD), KV_TILE), BF16),
        jax.ShapeDtypeStruct((B, S // T, F32_ROWS, LANES), F32),
        jax.ShapeDtypeStruct((B, GQA_HEADS * HEAD_SLOT, S), BF16),
        jax.ShapeDtypeStruct((B, S, GQA_KV_HEADS * HEAD_SLOT), BF16),
        jax.ShapeDtypeStruct((B, S // KV_TILE, GQA_KV_HEADS * (GQA_DIM + V_PAD), KV_TILE), BF16),
        jax.ShapeDtypeStruct((B, S // T, F32_ROWS, LANES), F32),
        jax.ShapeDtypeStruct((B, S, W_C), BF16),
    ]
    norms = pl.BlockSpec((1, 1, F32_ROWS, LANES), lambda s, b: (b, s, 0, 0))
    out_specs = [tposed(MLA_HEADS * HEAD_SLOT), tile(MLA_HEADS * HEAD_SLOT), blocked(MLA_HEADS * (MLA_V + V_PAD)),
                 norms, tposed(GQA_HEADS * HEAD_SLOT), tile(GQA_KV_HEADS * HEAD_SLOT),
                 blocked(GQA_KV_HEADS * (GQA_DIM + V_PAD)), norms, tile(W_C)]
    return pl.pallas_call(
        _mix_in_kernel, grid=(S // T, B), in_specs=in_specs, out_specs=out_specs, out_shape=out_shape,
        compiler_params=_params(2), name="mix_in",
    )(x, *consts, *tabs, *consts2)


def _attn_kernel(q_ref, k_ref, v_ref, kn_ref, g_ref, o_ref, s_ref, p_ref, m_ref, acc_ref, oT_ref, qx_ref, *,
                 n_heads, k_lanes, bias_lane, v_rows, dv):
    nkv = v_ref.shape[1]
    mq = q_ref.shape[2]
    assert n_heads % 2 == 0

    kmax = jnp.max(kn_ref[0], axis=0)
    first_row = lax.broadcasted_iota(jnp.int32, (BF16_ROWS, mq), 0) == 0
    bound = jnp.zeros((1, 1), F32)
    for hd in range(n_heads):
        rows = slice(hd * HEAD_SLOT, (hd + 1) * HEAD_SLOT)
        q = q_ref[0, rows, :]
        qf = q.astype(F32)
        u = jnp.sqrt(jnp.sum(qf * qf, axis=0, keepdims=True) * kmax[k_lanes[hd] // HEAD_SLOT:k_lanes[hd] // HEAD_SLOT + 1, 0:1])
        bound = jnp.maximum(bound, jnp.max(u, axis=1, keepdims=True))
        qx_ref[rows, :] = q
        bias_rows = slice(hd * HEAD_SLOT + bias_lane, hd * HEAD_SLOT + bias_lane + BF16_ROWS)
        qx_ref[bias_rows, :] = jnp.where(first_row, -u, 0.0).astype(BF16)
    acc_ref[...] = jnp.zeros(acc_ref.shape, F32)

    def run(stabilised):
        def scores(hd, off):
            rows = slice(hd * HEAD_SLOT, (hd + 1) * HEAD_SLOT)
            qT = qx_ref[rows, :] if stabilised else q_ref[0, rows, :]
            k = k_ref[0, pl.ds(off, KV_TILE), k_lanes[hd]:k_lanes[hd] + HEAD_SLOT]
            s_ref[hd % 2] = _dot(k, qT)

        scores(0, 0)

        def step(j, carry):
            off = pl.multiple_of(j * KV_TILE, KV_TILE)
            off_next = pl.multiple_of(jnp.minimum(j + 1, nkv - 1) * KV_TILE, KV_TILE)
            for hd in range(n_heads):
                if hd + 1 < n_heads:
                    scores(hd + 1, off)
                else:
                    scores(0, off_next)
                vT = v_ref[0, j, v_rows[hd]:v_rows[hd] + dv + V_PAD, :]
                if stabilised:
                    p_ref[hd] = jnp.exp2(s_ref[hd % 2].astype(BF16))
                    acc_ref[hd] += _dot(vT, p_ref[hd])
                else:
                    m_old = m_ref[hd]
                    m_new = jnp.maximum(m_old, jnp.max(s_ref[hd % 2], axis=0, keepdims=True))
                    m_ref[hd] = m_new
                    p_ref[hd] = jnp.exp2(s_ref[hd % 2] - m_new).astype(BF16)
                    acc_ref[hd] = jnp.exp2(m_old - m_new) * acc_ref[hd] + _dot(vT, p_ref[hd])
            return carry

        lax.fori_loop(0, nkv, step, 0, unroll=8 if stabilised else 1)

    def running_max():
        m_ref[...] = jnp.full(m_ref.shape, -1e30, F32)
        run(False)

    lax.cond(jnp.max(bound) <= STABILISER_MAX, lambda: run(True), running_max)
    for hd in range(n_heads):
        oT_ref[hd * dv:(hd + 1) * dv, :] = acc_ref[hd, :dv] / acc_ref[hd, dv:dv + 1]
    o_ref[0] = _rms(oT_ref[...].T, g_ref[...]).astype(BF16)


def _attention(qT, k, vT, kn, gain, *, n_heads, k_lanes, bias_lane, v_rows, dv, name):
    B, _, S = qT.shape
    kernel = functools.partial(_attn_kernel, n_heads=n_heads, k_lanes=k_lanes, bias_lane=bias_lane,
                               v_rows=v_rows, dv=dv)
    whole = lambda shape: pl.BlockSpec((1,) + shape[1:], lambda b, i: (b,) + (0,) * (len(shape) - 1),
                                       pipeline_mode=pl.Buffered(1))
    return pl.pallas_call(
        kernel, grid=(B, S // Q_TILE),
        in_specs=[pl.BlockSpec((1, n_heads * HEAD_SLOT, Q_TILE), lambda b, i: (b, 0, i)),
                  whole(k.shape), whole(vT.shape), whole(kn.shape), _const_spec(gain.shape)],
        out_specs=pl.BlockSpec((1, Q_TILE, n_heads * dv), lambda b, i: (b, i, 0)),
        out_shape=jax.ShapeDtypeStruct((B, S, n_heads * dv), BF16),
        scratch_shapes=[pltpu.VMEM((2, KV_TILE, Q_TILE), F32), pltpu.VMEM((n_heads, KV_TILE, Q_TILE), BF16),
                        pltpu.VMEM((n_heads, 1, Q_TILE), F32), pltpu.VMEM((n_heads, dv + V_PAD, Q_TILE), F32),
                        pltpu.VMEM((n_heads * dv, Q_TILE), F32), pltpu.VMEM((n_heads * HEAD_SLOT, Q_TILE), BF16)],
        compiler_params=_params(2), name=name,
    )(qT, k, vT, kn, gain)


def _mem_kv_kernel(mem_ref, g_ref, w_ref, kT_ref, v_ref):
    kv = _dot(_rms(mem_ref[0], g_ref[...]).astype(BF16), w_ref[...])
    width = MEM_HEADS * MEM_DIM
    kT_ref[0] = kv[:, :width].T.astype(BF16)
    ones = jnp.ones((kv.shape[0], MEM_DIM), BF16)
    for hd in range(MEM_HEADS):
        v_ref[0, hd, :, :MEM_DIM] = kv[:, width + hd * MEM_DIM:width + (hd + 1) * MEM_DIM].astype(BF16)
        v_ref[0, hd, :, MEM_DIM:] = ones


def _mem_kv(mem, gain, w_kv):
    B, Tm, D = mem.shape
    width = MEM_HEADS * MEM_DIM
    return pl.pallas_call(
        _mem_kv_kernel, grid=(B,),
        in_specs=[pl.BlockSpec((1, Tm, D), lambda b: (b, 0, 0)), _const_spec(gain.shape), _const_spec(w_kv.shape)],
        out_specs=[pl.BlockSpec((1, width, Tm), lambda b: (b, 0, 0)),
                   pl.BlockSpec((1, MEM_HEADS, Tm, 2 * MEM_DIM), lambda b: (b, 0, 0, 0))],
        out_shape=[jax.ShapeDtypeStruct((B, width, Tm), BF16),
                   jax.ShapeDtypeStruct((B, MEM_HEADS, Tm, 2 * MEM_DIM), BF16)],
        compiler_params=_params(1), name="mem_kv",
    )(mem, gain, w_kv)


def _out_mem_kernel(ya_ref, yb_ref, yc_ref, x_ref, wout_ref, g_ref, wq_ref, kT_ref, v_ref, wo_ref, o_ref):
    y = jnp.concatenate([ya_ref[0], yb_ref[0], yc_ref[0]], axis=1)
    x1 = x_ref[0] + _dot(y, wout_ref[...])
    h = _rms(x1, g_ref[...]).astype(BF16)
    q = (_dot(h, wq_ref[...]) * (MEM_DIM ** -0.5)).astype(BF16)
    heads = []
    for hd in range(MEM_HEADS):
        s = _dot(q[:, hd * MEM_DIM:(hd + 1) * MEM_DIM], kT_ref[0, hd * MEM_DIM:(hd + 1) * MEM_DIM, :])
        p = jnp.exp(s - jnp.max(s, axis=-1, keepdims=True)).astype(BF16)
        pv = _dot(p, v_ref[0, hd])
        heads.append(pv[:, :MEM_DIM] / pv[:, MEM_DIM:])
    o = jnp.concatenate(heads, axis=1).astype(BF16)
    o_ref[0] = x1 + _dot(o, wo_ref[...])


def _out_mem(ya, yb, yc, x, p, mem_kT, mem_v):
    B, S, D = x.shape
    T = TOK_TILE
    tile = lambda w: pl.BlockSpec((1, T, w), lambda b, s: (b, s, 0))
    per_b = lambda a: pl.BlockSpec((1,) + a.shape[1:], lambda b, s: (b,) + (0,) * (a.ndim - 1))
    consts = [p["w_out"], p["mem_x_norm"], p["mem_w_q"]]
    return pl.pallas_call(
        _out_mem_kernel, grid=(B, S // T),
        in_specs=[tile(W_A), tile(W_B), tile(W_C), tile(D)] + [_const_spec(c.shape) for c in consts]
                 + [per_b(mem_kT), per_b(mem_v), _const_spec(p["mem_w_o"].shape)],
        out_specs=tile(D), out_shape=jax.ShapeDtypeStruct((B, S, D), F32),
        compiler_params=_params(2), name="out_mem",
    )(ya, yb, yc, x, *consts, mem_kT, mem_v, p["mem_w_o"])


def _ffn_kernel(x_ref, xp_ref, xn_ref, g_ref, wup_ref, cw_ref, cb_ref, wdn_ref, fg_ref, o_ref,
                h_ref, act_ref, *, final):
    tok = x_ref.shape[1]
    i = pl.program_id(1)
    g = g_ref[...]
    x = x_ref[0]
    keep_prev = (i > 0).astype(F32)
    keep_next = (i < pl.num_programs(1) - 1).astype(F32)
    h_ref[0:HALO] = (_rms(xp_ref[0], g) * keep_prev).astype(BF16)
    h_ref[HALO:HALO + tok] = _rms(x, g).astype(BF16)
    h_ref[HALO + tok:] = (_rms(xn_ref[0], g) * keep_next).astype(BF16)
    hext = h_ref[...]
    rows = tok + 2 * HALO

    def conv_up(cols):
        a = _dot(hext, wup_ref[:, cols])
        w = cw_ref[:, cols]
        return (pltpu.roll(a, 1, 0)[HALO:HALO + tok] * w[0:1] + a[HALO:HALO + tok] * w[1:2]
                + pltpu.roll(a, rows - 1, 0)[HALO:HALO + tok] * w[2:3] + cb_ref[:, cols])

    for c in range(D_FF // FF_CHUNK):
        gate = conv_up(slice(FF_CHUNK * c, FF_CHUNK * (c + 1)))
        val = conv_up(slice(D_FF + FF_CHUNK * c, D_FF + FF_CHUNK * (c + 1)))
        act_ref[:, FF_CHUNK * c:FF_CHUNK * (c + 1)] = (jax.nn.silu(gate) * val).astype(BF16)
    y = x + _dot(act_ref[...], wdn_ref[...])
    if final:
        y = _rms(y, fg_ref[...])
    o_ref[0] = y


def _ffn(x, p, final_gain, *, final):
    B, S, D = x.shape
    T = TOK_TILE
    per_tile = T // HALO
    n_halo = S // HALO
    tile = pl.BlockSpec((1, T, D), lambda b, s: (b, s, 0))
    prev = pl.BlockSpec((1, HALO, D), lambda b, s: (b, jnp.maximum(s * per_tile - 1, 0), 0))
    nxt = pl.BlockSpec((1, HALO, D), lambda b, s: (b, jnp.minimum((s + 1) * per_tile, n_halo - 1), 0))
    consts = [p["ffn_norm"], p["ffn_w_up"], p["ffn_conv_w"], p["ffn_conv_b"], p["ffn_w_down"], final_gain]
    return pl.pallas_call(
        functools.partial(_ffn_kernel, final=final), grid=(B, S // T),
        in_specs=[tile, prev, nxt] + [_const_spec(c.shape) for c in consts],
        out_specs=tile, out_shape=jax.ShapeDtypeStruct((B, S, D), F32),
        scratch_shapes=[pltpu.VMEM((T + 2 * HALO, D), BF16), pltpu.VMEM((T, D_FF), BF16)],
        compiler_params=_params(2), name="ffn_final" if final else "ffn",
    )(x, x, x, *consts)


def _swap_pairs(w):
    n = w.shape[-1]
    return w.reshape(*w.shape[:-1], n // 2, 2)[..., ::-1].reshape(w.shape)


def _rope_tables(S, d_rot):
    rows = S // GRID_W
    row = jnp.repeat(jnp.arange(rows, dtype=F32), GRID_W)
    col = jnp.tile(jnp.arange(GRID_W, dtype=F32), rows)
    n = d_rot // 4
    inv = ROPE_THETA ** (-jnp.arange(n, dtype=F32) / n)
    ang = jnp.concatenate([row[:, None] * inv, col[:, None] * inv], axis=-1)
    cos, sin = jnp.cos(ang), jnp.sin(ang)
    c = jnp.repeat(cos, 2, axis=-1)
    s = jnp.stack([-sin, sin], axis=-1).reshape(S, d_rot)
    return c, s


def _prep_params(S, P):
    L = P["w_in"].shape[0]
    row = lambda v: v[:, None, :]
    ca, sa = _rope_tables(S, MLA_ROPE)
    cb, sb = _rope_tables(S, GQA_DIM)
    p, shared = {}, {}
    c_q, c_kv, k_rope, g_q, g_k, g_v, g_m = jnp.split(P["w_in"], [256, 384, 416, 800, 928, 1056], axis=2)
    pad = jnp.zeros((L, D_MODEL, Z_END - Z_KR - 2 * MLA_ROPE), F32)
    p["w_in"] = jnp.concatenate(
        [c_q, c_kv, g_q, _swap_pairs(g_q), g_k, _swap_pairs(g_k), g_v, g_m, k_rope, _swap_pairs(k_rope), pad],
        axis=2).astype(BF16)
    for name in ("mix_norm", "mla_q_norm", "mla_kv_norm", "gmlp_v_norm", "mem_x_norm", "mem_kv_norm", "ffn_norm",
                 "ffn_conv_b"):
        p[name] = row(P[name])

    w_uq = P["mla_w_uq"].reshape(L, MLA_Q_RANK, MLA_HEADS, MLA_NOPE + MLA_ROPE)
    zpad = jnp.zeros((L, MLA_Q_RANK, MLA_HEADS, HEAD_SLOT - MLA_NOPE - MLA_ROPE), F32)
    main = jnp.concatenate([w_uq, zpad], axis=-1)
    swapped = jnp.concatenate([jnp.zeros_like(w_uq[..., :MLA_NOPE]), _swap_pairs(w_uq[..., MLA_NOPE:]), zpad], axis=-1)
    p["w_uq"] = jnp.concatenate([main.reshape(L, MLA_Q_RANK, -1), swapped.reshape(L, MLA_Q_RANK, -1)],
                                axis=2).astype(BF16)
    scale_a = (MLA_NOPE + MLA_ROPE) ** -0.5 * LOG2E
    ones = jnp.ones((S, MLA_NOPE), F32)
    zeros_n = jnp.zeros((S, MLA_NOPE), F32)
    zeros_p = jnp.zeros((S, HEAD_SLOT - MLA_NOPE - MLA_ROPE), F32)
    shared["cqa"] = jnp.tile(jnp.concatenate([ones, ca, zeros_p], axis=1) * scale_a, (1, MLA_HEADS))
    shared["sqa"] = jnp.tile(jnp.concatenate([zeros_n, sa, zeros_p], axis=1) * scale_a, (1, MLA_HEADS))

    w_ukv = P["mla_w_ukv"].reshape(L, MLA_KV_RANK, MLA_HEADS, MLA_NOPE + MLA_V)
    k_lat = jnp.concatenate(
        [w_ukv[..., :MLA_NOPE], jnp.zeros((L, MLA_KV_RANK, MLA_HEADS, HEAD_SLOT - MLA_NOPE), F32)], axis=-1)
    place = jnp.concatenate([jnp.zeros((MLA_ROPE, MLA_NOPE), F32), jnp.eye(MLA_ROPE, dtype=F32),
                             jnp.zeros((MLA_ROPE, HEAD_SLOT - MLA_NOPE - MLA_ROPE), F32)], axis=1)
    place = jnp.broadcast_to(jnp.tile(place, (1, MLA_HEADS)), (L, MLA_ROPE, MLA_HEADS * HEAD_SLOT))
    zrows = jnp.zeros((L, Z_END - Z_KR - 2 * MLA_ROPE, MLA_HEADS * HEAD_SLOT), F32)
    p["w_k"] = jnp.concatenate([k_lat.reshape(L, MLA_KV_RANK, -1), place, place, zrows], axis=1).astype(BF16)
    p["w_v"] = w_ukv[..., MLA_NOPE:].reshape(L, MLA_KV_RANK, -1).astype(BF16)
    shared["tk"] = jnp.concatenate([ca, sa, jnp.zeros((S, Z_END - Z_KR - 2 * MLA_ROPE), F32)], axis=1)

    gq = row(jnp.tile(P["gqa_q_norm"], (1, GQA_HEADS)) * (GQA_DIM ** -0.5 * LOG2E))
    gk = row(jnp.tile(P["gqa_k_norm"], (1, GQA_KV_HEADS)))
    p["tcq"] = jnp.tile(cb, (1, GQA_HEADS)) * gq
    p["tsq"] = jnp.tile(sb, (1, GQA_HEADS)) * _swap_pairs(gq)
    p["tck"] = jnp.tile(cb, (1, GQA_KV_HEADS)) * gk
    p["tsk"] = jnp.tile(sb, (1, GQA_KV_HEADS)) * _swap_pairs(gk)
    grp = jnp.arange(W_B) // GQA_DIM
    shared["bd"] = (grp[:, None] == grp[None, :]).astype(BF16)

    p["w_s"] = P["gmlp_w_s"].reshape(L, GMLP_GROUPS * GMLP_CHUNK, GMLP_CHUNK).astype(BF16)
    p["bias_s"] = jnp.repeat(jnp.swapaxes(P["gmlp_b_s"], 1, 2), GMLP_DIM, axis=2)
    p["out_norm_a"] = row(P["out_norm"][:, :W_A])
    p["out_norm_b"] = row(P["out_norm"][:, W_A:W_A + W_B])
    p["out_norm_c"] = row(P["out_norm"][:, W_A + W_B:])
    for name in ("w_out", "mem_w_q", "mem_w_kv", "mem_w_o", "ffn_w_up", "ffn_w_down"):
        p[name] = P[name].astype(BF16)
    p["ffn_conv_w"] = P["ffn_conv_w"]
    return p, shared


def kernel(x, mem, mix_norm, w_in, mla_q_norm, mla_w_uq, mla_kv_norm, mla_w_ukv, gqa_q_norm, gqa_k_norm, gmlp_v_norm, gmlp_w_s, gmlp_b_s, out_norm, w_out, mem_x_norm, mem_kv_norm, mem_w_q, mem_w_kv, mem_w_o, ffn_norm, ffn_w_up, ffn_conv_w, ffn_conv_b, ffn_w_down, final_norm):
    P = dict(mix_norm=mix_norm, w_in=w_in, mla_q_norm=mla_q_norm, mla_w_uq=mla_w_uq, mla_kv_norm=mla_kv_norm,
             mla_w_ukv=mla_w_ukv, gqa_q_norm=gqa_q_norm, gqa_k_norm=gqa_k_norm, gmlp_v_norm=gmlp_v_norm,
             gmlp_w_s=gmlp_w_s, gmlp_b_s=gmlp_b_s, out_norm=out_norm, w_out=w_out, mem_x_norm=mem_x_norm,
             mem_kv_norm=mem_kv_norm, mem_w_q=mem_w_q, mem_w_kv=mem_w_kv, mem_w_o=mem_w_o, ffn_norm=ffn_norm,
             ffn_w_up=ffn_w_up, ffn_conv_w=ffn_conv_w, ffn_conv_b=ffn_conv_b, ffn_w_down=ffn_w_down)
    B, S, D = x.shape
    assert D == D_MODEL and S % TOK_TILE == 0 and S % GRID_W == 0
    depth = w_in.shape[0]
    stacked, shared = _prep_params(S, P)
    final_gain = final_norm.reshape(1, -1)
    group = GQA_HEADS // GQA_KV_HEADS
    for l in range(depth):
        p = dict(shared, **{name: v[l] for name, v in stacked.items()})
        qaT, ka, vaT, kna, qbT, kb, vbT, knb, ync = _mix_in(x, p)
        yna = _attention(qaT, ka, vaT, kna, p["out_norm_a"], n_heads=MLA_HEADS,
                         k_lanes=tuple(h * HEAD_SLOT for h in range(MLA_HEADS)), bias_lane=MLA_BIAS_LANE,
                         v_rows=tuple(h * (MLA_V + V_PAD) for h in range(MLA_HEADS)), dv=MLA_V, name="attn_mla")
        ynb = _attention(qbT, kb, vbT, knb, p["out_norm_b"], n_heads=GQA_HEADS,
                         k_lanes=tuple((h // group) * HEAD_SLOT for h in range(GQA_HEADS)), bias_lane=GQA_BIAS_LANE,
                         v_rows=tuple((h // group) * (GQA_DIM + V_PAD) for h in range(GQA_HEADS)), dv=GQA_DIM, name="attn_gqa")
        mem_kT, mem_v = _mem_kv(mem, p["mem_kv_norm"], p["mem_w_kv"])
        x = _out_mem(yna, ynb, ync, x, p, mem_kT, mem_v)
        x = _ffn(x, p, final_gain, final=(l == depth - 1))
    return x
```

```python
import functools

import jax
import jax.numpy as jnp
from jax import lax
from jax.experimental import pallas as pl
from jax.experimental.pallas import tpu as pltpu

F32 = jnp.float32
BF16 = jnp.bfloat16
F8 = jnp.float8_e4m3fn

D_MODEL = 1024
GRID_W = 64
ROPE_THETA = 10000.0
EPS = 1e-6
MLA_HEADS = 6
MLA_NOPE = 64
MLA_ROPE = 32
MLA_V = 64
MLA_Q_RANK = 256
MLA_KV_RANK = 128
GQA_HEADS = 6
GQA_KV_HEADS = 2
GQA_DIM = 64
GMLP_GROUPS = 4
GMLP_DIM = 64
GMLP_CHUNK = 128
W_A = MLA_HEADS * MLA_V
W_B = GQA_HEADS * GQA_DIM
W_C = GMLP_GROUPS * GMLP_DIM
MEM_HEADS = 4
MEM_DIM = 128
D_FF = 2816

LANES = 128
HEAD_SLOT = LANES
TOK_TILE = 512
Q_TILE = 512
KV_TILE = 512
F32_ROWS = 8
BF16_ROWS = 16
V_PAD = BF16_ROWS
LOG2E = 1.4426950408889634
MLA_BIAS_LANE = MLA_NOPE + MLA_ROPE
GQA_BIAS_LANE = GQA_DIM
STABILISER_MAX = 50.0
FP8_ROWS = 32
Q8_SLOT = 256
Q8_SCALE = 32.0
K8_SCALE = 8.0
K8_BIAS = 32.0
FP8_MAX_SCALED = 400.0
FF_CHUNK = 256
HALO = BF16_ROWS
VMEM_LIMIT = 56 * 1024 * 1024

Z_CQ, Z_CKV, Z_GQ, Z_GQS, Z_GK, Z_GKS, Z_GV, Z_U, Z_VV, Z_KR, Z_END = (
    0, 256, 384, 768, 1152, 1280, 1408, 1536, 1792, 2048, 2176)


def _rms(x, g):
    return x * lax.rsqrt(jnp.mean(x * x, axis=-1, keepdims=True) + EPS) * g


def _dot(a, b):
    return jnp.dot(a, b, preferred_element_type=F32)


def _const_spec(shape):
    zeros = (0,) * len(shape)
    return pl.BlockSpec(shape, lambda *_: zeros, pipeline_mode=pl.Buffered(1))


def _params(n_axes):
    return pltpu.CompilerParams(dimension_semantics=("arbitrary",) * n_axes,
                                vmem_limit_bytes=VMEM_LIMIT)


def _group_ssq(v, bd):
    sq = v * v
    hi = sq.astype(BF16)
    lo = (sq - hi.astype(F32)).astype(BF16)
    return _dot(hi, bd) + _dot(lo, bd)


def _round_e4m3(x):
    c = x * (2.0 ** (24 - 4) + 1.0)
    return c - (c - x)


def _store_vT(ref, vT, n_heads, dv):
    ext = (lax.broadcasted_iota(jnp.int32, (V_PAD, KV_TILE), 0) == 0).astype(BF16)
    for n in range(vT.shape[1] // KV_TILE):
        for hd in range(n_heads):
            base = hd * (dv + V_PAD)
            ref[0, n, base:base + dv] = vT[hd * dv:(hd + 1) * dv, n * KV_TILE:(n + 1) * KV_TILE].astype(BF16)
            ref[0, n, base + dv:base + dv + V_PAD] = ext


def _store_key_norms(ref, k, n_slots, n_feat):
    lane = lax.broadcasted_iota(jnp.int32, (k.shape[0], HEAD_SLOT), 1)
    rows = []
    for n in range(n_slots):
        x = jnp.where(lane < n_feat, k[:, n * HEAD_SLOT:(n + 1) * HEAD_SLOT].astype(F32), 0.0)
        n2 = jnp.max(jnp.sum(x * x, axis=-1, keepdims=True), axis=0, keepdims=True)
        rows.append(jnp.broadcast_to(n2, (1, LANES)))
    rows.append(jnp.zeros((F32_ROWS - n_slots, LANES), F32))
    ref[0, 0] = jnp.concatenate(rows, axis=0)


def _mix_in_kernel(x_ref, g_ref, win_ref, gq_ref, wq_ref, gkv_ref, wk_ref, wv_ref,
                   cqa_ref, sqa_ref, tk_ref, tcq_ref, tsq_ref, tck_ref, tsk_ref,
                   bd_ref, gv_ref, ws_ref, bias_ref, gc_ref,
                   qaT_ref, ka_ref, vaT_ref, kna_ref, qbT_ref, kb_ref, vbT_ref, knb_ref, qb8T_ref, kb8_ref, ync_ref):
    tok = x_ref.shape[1]
    h = _rms(x_ref[0], g_ref[...])
    z = _dot(h.astype(BF16), win_ref[...])

    cq = _rms(z[:, Z_CQ:Z_CKV], gq_ref[...]).astype(BF16)
    qa = _dot(cq, wq_ref[...])
    half = MLA_HEADS * HEAD_SLOT
    q_a = qa[:, :half] * cqa_ref[...] + qa[:, half:] * sqa_ref[...]
    qaT_ref[0] = q_a.T.astype(BF16)

    ckv = _rms(z[:, Z_CKV:Z_GQ], gkv_ref[...])
    kr = z[:, Z_KR:Z_END] * tk_ref[...]
    lhs = jnp.concatenate([ckv, kr], axis=1).astype(BF16)
    slot_lane = lax.broadcasted_iota(jnp.int32, (1, MLA_HEADS * HEAD_SLOT), 1) % HEAD_SLOT
    k_a = (_dot(lhs, wk_ref[...]) + (slot_lane == MLA_BIAS_LANE).astype(F32)).astype(BF16)
    ka_ref[0] = k_a
    _store_key_norms(kna_ref, k_a, MLA_HEADS, MLA_BIAS_LANE)
    _store_vT(vaT_ref, _dot(lhs[:, :MLA_KV_RANK], wv_ref[...]).T, MLA_HEADS, MLA_V)

    bd = bd_ref[...]
    g_q = z[:, Z_GQ:Z_GQS]
    n_q = lax.rsqrt(_group_ssq(g_q, bd) * (1.0 / GQA_DIM) + EPS)
    q_b = (g_q * tcq_ref[...] + z[:, Z_GQS:Z_GK] * tsq_ref[...]) * n_q
    q_bT = q_b.T
    zero = jnp.zeros((HEAD_SLOT - GQA_DIM, tok), BF16)
    for hd in range(GQA_HEADS):
        qbT_ref[0, hd * HEAD_SLOT:hd * HEAD_SLOT + GQA_DIM] = q_bT[hd * GQA_DIM:(hd + 1) * GQA_DIM].astype(BF16)
        qbT_ref[0, hd * HEAD_SLOT + GQA_DIM:(hd + 1) * HEAD_SLOT] = zero
    q8 = q_bT * Q8_SCALE
    q_hi32 = _round_e4m3(q8)
    q_hi = q_hi32.astype(F8)
    q_lo = _round_e4m3(q8 - q_hi32).astype(F8)
    zero8 = jnp.zeros((Q8_SLOT - 3 * GQA_DIM, tok), F8)
    for hd in range(GQA_HEADS):
        rows = slice(hd * GQA_DIM, (hd + 1) * GQA_DIM)
        base = hd * Q8_SLOT
        qb8T_ref[0, base:base + GQA_DIM] = q_hi[rows]
        qb8T_ref[0, base + GQA_DIM:base + 2 * GQA_DIM] = q_hi[rows]
        qb8T_ref[0, base + 2 * GQA_DIM:base + 3 * GQA_DIM] = q_lo[rows]
        qb8T_ref[0, base + 3 * GQA_DIM:base + Q8_SLOT] = zero8
    g_k = z[:, Z_GK:Z_GKS]
    n_k = lax.rsqrt(_group_ssq(g_k, bd[:LANES, :LANES]) * (1.0 / GQA_DIM) + EPS)
    k_b = (g_k * tck_ref[...] + z[:, Z_GKS:Z_GV] * tsk_ref[...]) * n_k
    lane = lax.broadcasted_iota(jnp.int32, (tok, LANES), 1)
    bias_lane = (lane == GQA_BIAS_LANE).astype(F32)
    low = lane < GQA_DIM
    k8 = k_b * K8_SCALE
    k_hi = _round_e4m3(k8)
    k_lo = _round_e4m3(k8 - k_hi)
    hi_r = pltpu.roll(k_hi, GQA_DIM, 1)
    lo_r = pltpu.roll(k_lo, GQA_DIM, 1)
    bias8 = jnp.where(lane == GQA_DIM, K8_BIAS, 0.0)
    kb8_ref[0] = jnp.concatenate([jnp.where(low, k_hi, lo_r), jnp.where(low, k_hi, bias8),
                                  jnp.where(low, hi_r, k_lo), jnp.where(low, hi_r, bias8)], axis=1).astype(F8)
    k_slots = [jnp.where(low, k_b, bias_lane), jnp.where(low, pltpu.roll(k_b, GQA_DIM, 1), bias_lane)]
    k_b = jnp.concatenate(k_slots, axis=1).astype(BF16)
    kb_ref[0] = k_b
    _store_key_norms(knb_ref, k_b, GQA_KV_HEADS, GQA_BIAS_LANE)
    _store_vT(vbT_ref, z[:, Z_GV:Z_U].T, GQA_KV_HEADS, GQA_DIM)

    gm = jax.nn.gelu(z[:, Z_U:Z_KR])
    u = gm[:, :W_C]
    vv = _rms(gm[:, W_C:], gv_ref[...])
    lane_grp = lax.broadcasted_iota(jnp.int32, (GMLP_CHUNK, W_C), 1) // GMLP_DIM
    ws = ws_ref[...]
    bias = bias_ref[...]
    ycs = []
    for n in range(tok // GMLP_CHUNK):
        rows = slice(n * GMLP_CHUNK, (n + 1) * GMLP_CHUNK)
        r = _dot(ws, vv[rows].astype(BF16))
        mixed = r[3 * GMLP_CHUNK:]
        for grp in range(GMLP_GROUPS - 2, -1, -1):
            mixed = jnp.where(lane_grp == grp, r[grp * GMLP_CHUNK:(grp + 1) * GMLP_CHUNK], mixed)
        ycs.append(u[rows] * (mixed + bias))
    ync_ref[0] = _rms(jnp.concatenate(ycs, axis=0), gc_ref[...]).astype(BF16)


def _mix_in(x, p):
    B, S, D = x.shape
    T = TOK_TILE
    nkv = T // KV_TILE
    tile = lambda w: pl.BlockSpec((1, T, w), lambda s, b: (b, s, 0))
    tab = lambda w: pl.BlockSpec((T, w), lambda s, b: (s, 0))
    tposed = lambda r: pl.BlockSpec((1, r, T), lambda s, b: (b, 0, s))
    blocked = lambda r: pl.BlockSpec((1, nkv, r, KV_TILE), lambda s, b: (b, s, 0, 0))
    consts = [p["mix_norm"], p["w_in"], p["mla_q_norm"], p["w_uq"], p["mla_kv_norm"], p["w_k"], p["w_v"]]
    tabs = [p["cqa"], p["sqa"], p["tk"], p["tcq"], p["tsq"], p["tck"], p["tsk"]]
    consts2 = [p["bd"], p["gmlp_v_norm"], p["w_s"], p["bias_s"], p["out_norm_c"]]
    in_specs = ([tile(D)] + [_const_spec(c.shape) for c in consts] + [tab(t.shape[1]) for t in tabs]
                + [_const_spec(c.shape) for c in consts2])
    out_shape = [
        jax.ShapeDtypeStruct((B, MLA_HEADS * HEAD_SLOT, S), BF16),
        jax.ShapeDtypeStruct((B, S, MLA_HEADS * HEAD_SLOT), BF16),
        jax.ShapeDtypeStruct((B, S // KV_TILE, MLA_HEADS * (MLA_V + V_PAD), KV_TILE), BF16),
        jax.ShapeDtypeStruct((B, S // T, F32_ROWS, LANES), F32),
        jax.ShapeDtypeStruct((B, GQA_HEADS * HEAD_SLOT, S), BF16),
        jax.ShapeDtypeStruct((B, S, GQA_KV_HEADS * HEAD_SLOT), BF16),
        jax.ShapeDtypeStruct((B, S // KV_TILE, GQA_KV_HEADS * (GQA_DIM + V_PAD), KV_TILE), BF16),
        jax.ShapeDtypeStruct((B, S // T, F32_ROWS, LANES), F32),
        jax.ShapeDtypeStruct((B, GQA_HEADS * Q8_SLOT, S), F8),
        jax.ShapeDtypeStruct((B, S, GQA_KV_HEADS * Q8_SLOT), F8),
        jax.ShapeDtypeStruct((B, S, W_C), BF16),
    ]
    norms = pl.BlockSpec((1, 1, F32_ROWS, LANES), lambda s, b: (b, s, 0, 0))
    out_specs = [tposed(MLA_HEADS * HEAD_SLOT), tile(MLA_HEADS * HEAD_SLOT), blocked(MLA_HEADS * (MLA_V + V_PAD)),
                 norms, tposed(GQA_HEADS * HEAD_SLOT), tile(GQA_KV_HEADS * HEAD_SLOT),
                 blocked(GQA_KV_HEADS * (GQA_DIM + V_PAD)), norms, tposed(GQA_HEADS * Q8_SLOT),
                 tile(GQA_KV_HEADS * Q8_SLOT), tile(W_C)]
    return pl.pallas_call(
        _mix_in_kernel, grid=(S // T, B), in_specs=in_specs, out_specs=out_specs, out_shape=out_shape,
        compiler_params=_params(2), name="mix_in",
    )(x, *consts, *tabs, *consts2)


def _attn_kernel(q_ref, k_ref, v_ref, kn_ref, g_ref, *rest, n_heads, k_lanes, bias_lane, v_rows, dv, fp8):
    if fp8:
        q8_ref, k8_ref, o_ref, s_ref, p_ref, m_ref, acc_ref, oT_ref, qx_ref = rest
    else:
        o_ref, s_ref, p_ref, m_ref, acc_ref, oT_ref, qx_ref = rest
    nkv = v_ref.shape[1]
    mq = q_ref.shape[2]
    assert n_heads % 2 == 0

    kmax = jnp.max(kn_ref[0], axis=0)
    first_row = lax.broadcasted_iota(jnp.int32, (FP8_ROWS if fp8 else BF16_ROWS, mq), 0) == 0
    bound = jnp.zeros((1, 1), F32)
    qn2_max = jnp.zeros((1, 1), F32)
    kn2_max = jnp.zeros((1, 1), F32)
    for hd in range(n_heads):
        rows = slice(hd * HEAD_SLOT, (hd + 1) * HEAD_SLOT)
        q = q_ref[0, rows, :]
        qf = q.astype(F32)
        qn2 = jnp.sum(qf * qf, axis=0, keepdims=True)
        kn2 = kmax[k_lanes[hd] // HEAD_SLOT:k_lanes[hd] // HEAD_SLOT + 1, 0:1]
        u = jnp.sqrt(qn2 * kn2)
        bound = jnp.maximum(bound, jnp.max(u, axis=1, keepdims=True))
        if fp8:
            qn2_max = jnp.maximum(qn2_max, jnp.max(qn2, axis=1, keepdims=True))
            kn2_max = jnp.maximum(kn2_max, kn2)
            rows8 = slice(hd * Q8_SLOT, (hd + 1) * Q8_SLOT)
            qx_ref[rows8, :] = q8_ref[0, rows8, :]
            m_ref[hd] = u
        else:
            qx_ref[rows, :] = q
            bias_rows = slice(hd * HEAD_SLOT + bias_lane, hd * HEAD_SLOT + bias_lane + BF16_ROWS)
            qx_ref[bias_rows, :] = jnp.where(first_row, -u, 0.0).astype(BF16)
    stabilise = jnp.max(bound) <= STABILISER_MAX
    if fp8:
        stabilise = jnp.logical_and(stabilise, jnp.logical_and(
            jnp.max(qn2_max) <= (FP8_MAX_SCALED / Q8_SCALE) ** 2, jnp.max(kn2_max) <= (FP8_MAX_SCALED / K8_SCALE) ** 2))
    acc_ref[...] = jnp.zeros(acc_ref.shape, F32)

    def run(stabilised):
        def scores(hd, off):
            rows = slice(hd * HEAD_SLOT, (hd + 1) * HEAD_SLOT)
            if stabilised and fp8:
                slot = k_lanes[hd] // HEAD_SLOT
                k = k8_ref[0, pl.ds(off, KV_TILE), slot * Q8_SLOT:(slot + 1) * Q8_SLOT]
                qT = qx_ref[hd * Q8_SLOT:(hd + 1) * Q8_SLOT, :]
            else:
                k = k_ref[0, pl.ds(off, KV_TILE), k_lanes[hd]:k_lanes[hd] + HEAD_SLOT]
                qT = qx_ref[rows, :] if stabilised else q_ref[0, rows, :]
            s_ref[hd % 2] = _dot(k, qT)

        scores(0, 0)

        def step(j, carry):
            off = pl.multiple_of(j * KV_TILE, KV_TILE)
            off_next = pl.multiple_of(jnp.minimum(j + 1, nkv - 1) * KV_TILE, KV_TILE)
            for hd in range(n_heads):
                if hd + 1 < n_heads:
                    scores(hd + 1, off)
                else:
                    scores(0, off_next)
                vT = v_ref[0, j, v_rows[hd]:v_rows[hd] + dv + V_PAD, :]
                if stabilised:
                    s = s_ref[hd % 2]
                    if fp8:
                        s = s * (1.0 / (Q8_SCALE * K8_SCALE)) - m_ref[hd]
                    p_ref[hd] = jnp.exp2(s.astype(BF16))
                    acc_ref[hd] += _dot(vT, p_ref[hd])
                else:
                    m_old = m_ref[hd]
                    m_new = jnp.maximum(m_old, jnp.max(s_ref[hd % 2], axis=0, keepdims=True))
                    m_ref[hd] = m_new
                    p_ref[hd] = jnp.exp2(s_ref[hd % 2] - m_new).astype(BF16)
                    acc_ref[hd] = jnp.exp2(m_old - m_new) * acc_ref[hd] + _dot(vT, p_ref[hd])
            return carry

        lax.fori_loop(0, nkv, step, 0, unroll=8 if stabilised else 1)

    def running_max():
        m_ref[...] = jnp.full(m_ref.shape, -1e30, F32)
        run(False)

    lax.cond(stabilise, lambda: run(True), running_max)
    for hd in range(n_heads):
        oT_ref[hd * dv:(hd + 1) * dv, :] = acc_ref[hd, :dv] / acc_ref[hd, dv:dv + 1]
    o_ref[0] = _rms(oT_ref[...].T, g_ref[...]).astype(BF16)


def _attention(qT, k, vT, kn, gain, fp8_operands=(), *, n_heads, k_lanes, bias_lane, v_rows, dv, name):
    B, _, S = qT.shape
    fp8 = bool(fp8_operands)
    kernel = functools.partial(_attn_kernel, n_heads=n_heads, k_lanes=k_lanes, bias_lane=bias_lane,
                               v_rows=v_rows, dv=dv, fp8=fp8)
    q_slot, q_dtype = (Q8_SLOT, F8) if fp8 else (HEAD_SLOT, BF16)
    q_tile = lambda slot: pl.BlockSpec((1, n_heads * slot, Q_TILE), lambda b, i: (b, 0, i))
    whole = lambda shape: pl.BlockSpec((1,) + shape[1:], lambda b, i: (b,) + (0,) * (len(shape) - 1),
                                       pipeline_mode=pl.Buffered(1))
    in_specs = [q_tile(HEAD_SLOT), whole(k.shape), whole(vT.shape), whole(kn.shape), _const_spec(gain.shape)]
    if fp8:
        in_specs += [q_tile(Q8_SLOT), whole(fp8_operands[1].shape)]
    return pl.pallas_call(
        kernel, grid=(B, S // Q_TILE), in_specs=in_specs,
        out_specs=pl.BlockSpec((1, Q_TILE, n_heads * dv), lambda b, i: (b, i, 0)),
        out_shape=jax.ShapeDtypeStruct((B, S, n_heads * dv), BF16),
        scratch_shapes=[pltpu.VMEM((2, KV_TILE, Q_TILE), F32), pltpu.VMEM((n_heads, KV_TILE, Q_TILE), BF16),
                        pltpu.VMEM((n_heads, 1, Q_TILE), F32), pltpu.VMEM((n_heads, dv + V_PAD, Q_TILE), F32),
                        pltpu.VMEM((n_heads * dv, Q_TILE), F32), pltpu.VMEM((n_heads * q_slot, Q_TILE), q_dtype)],
        compiler_params=_params(2), name=name,
    )(qT, k, vT, kn, gain, *fp8_operands)


def _mem_kv_kernel(mem_ref, g_ref, w_ref, kT_ref, v_ref):
    kv = _dot(_rms(mem_ref[0], g_ref[...]).astype(BF16), w_ref[...])
    width = MEM_HEADS * MEM_DIM
    kT_ref[0] = kv[:, :width].T.astype(BF16)
    ones = jnp.ones((kv.shape[0], MEM_DIM), BF16)
    for hd in range(MEM_HEADS):
        v_ref[0, hd, :, :MEM_DIM] = kv[:, width + hd * MEM_DIM:width + (hd + 1) * MEM_DIM].astype(BF16)
        v_ref[0, hd, :, MEM_DIM:] = ones


def _mem_kv(mem, gain, w_kv):
    B, Tm, D = mem.shape
    width = MEM_HEADS * MEM_DIM
    return pl.pallas_call(
        _mem_kv_kernel, grid=(B,),
        in_specs=[pl.BlockSpec((1, Tm, D), lambda b: (b, 0, 0)), _const_spec(gain.shape), _const_spec(w_kv.shape)],
        out_specs=[pl.BlockSpec((1, width, Tm), lambda b: (b, 0, 0)),
                   pl.BlockSpec((1, MEM_HEADS, Tm, 2 * MEM_DIM), lambda b: (b, 0, 0, 0))],
        out_shape=[jax.ShapeDtypeStruct((B, width, Tm), BF16),
                   jax.ShapeDtypeStruct((B, MEM_HEADS, Tm, 2 * MEM_DIM), BF16)],
        compiler_params=_params(1), name="mem_kv",
    )(mem, gain, w_kv)


def _out_mem_kernel(ya_ref, yb_ref, yc_ref, x_ref, wout_ref, g_ref, wq_ref, kT_ref, v_ref, wo_ref, o_ref):
    y = jnp.concatenate([ya_ref[0], yb_ref[0], yc_ref[0]], axis=1)
    x1 = x_ref[0] + _dot(y, wout_ref[...])
    h = _rms(x1, g_ref[...]).astype(BF16)
    q = (_dot(h, wq_ref[...]) * (MEM_DIM ** -0.5)).astype(BF16)
    heads = []
    for hd in range(MEM_HEADS):
        s = _dot(q[:, hd * MEM_DIM:(hd + 1) * MEM_DIM], kT_ref[0, hd * MEM_DIM:(hd + 1) * MEM_DIM, :])
        p = jnp.exp(s - jnp.max(s, axis=-1, keepdims=True)).astype(BF16)
        pv = _dot(p, v_ref[0, hd])
        heads.append(pv[:, :MEM_DIM] / pv[:, MEM_DIM:])
    o = jnp.concatenate(heads, axis=1).astype(BF16)
    o_ref[0] = x1 + _dot(o, wo_ref[...])


def _out_mem(ya, yb, yc, x, p, mem_kT, mem_v):
    B, S, D = x.shape
    T = TOK_TILE
    tile = lambda w: pl.BlockSpec((1, T, w), lambda b, s: (b, s, 0))
    per_b = lambda a: pl.BlockSpec((1,) + a.shape[1:], lambda b, s: (b,) + (0,) * (a.ndim - 1))
    consts = [p["w_out"], p["mem_x_norm"], p["mem_w_q"]]
    return pl.pallas_call(
        _out_mem_kernel, grid=(B, S // T),
        in_specs=[tile(W_A), tile(W_B), tile(W_C), tile(D)] + [_const_spec(c.shape) for c in consts]
                 + [per_b(mem_kT), per_b(mem_v), _const_spec(p["mem_w_o"].shape)],
        out_specs=tile(D), out_shape=jax.ShapeDtypeStruct((B, S, D), F32),
        compiler_params=_params(2), name="out_mem",
    )(ya, yb, yc, x, *consts, mem_kT, mem_v, p["mem_w_o"])


def _ffn_kernel(x_ref, xp_ref, xn_ref, g_ref, wup_ref, cw_ref, cb_ref, wdn_ref, fg_ref, o_ref,
                h_ref, act_ref, *, final):
    tok = x_ref.shape[1]
    i = pl.program_id(1)
    g = g_ref[...]
    x = x_ref[0]
    keep_prev = (i > 0).astype(F32)
    keep_next = (i < pl.num_programs(1) - 1).astype(F32)
    h_ref[0:HALO] = (_rms(xp_ref[0], g) * keep_prev).astype(BF16)
    h_ref[HALO:HALO + tok] = _rms(x, g).astype(BF16)
    h_ref[HALO + tok:] = (_rms(xn_ref[0], g) * keep_next).astype(BF16)
    hext = h_ref[...]
    rows = tok + 2 * HALO

    def conv_up(cols):
        a = _dot(hext, wup_ref[:, cols])
        w = cw_ref[:, cols]
        return (pltpu.roll(a, 1, 0)[HALO:HALO + tok] * w[0:1] + a[HALO:HALO + tok] * w[1:2]
                + pltpu.roll(a, rows - 1, 0)[HALO:HALO + tok] * w[2:3] + cb_ref[:, cols])

    for c in range(D_FF // FF_CHUNK):
        gate = conv_up(slice(FF_CHUNK * c, FF_CHUNK * (c + 1)))
        val = conv_up(slice(D_FF + FF_CHUNK * c, D_FF + FF_CHUNK * (c + 1)))
        act_ref[:, FF_CHUNK * c:FF_CHUNK * (c + 1)] = (jax.nn.silu(gate) * val).astype(BF16)
    y = x + _dot(act_ref[...], wdn_ref[...])
    if final:
        y = _rms(y, fg_ref[...])
    o_ref[0] = y


def _ffn(x, p, final_gain, *, final):
    B, S, D = x.shape
    T = TOK_TILE
    per_tile = T // HALO
    n_halo = S // HALO
    tile = pl.BlockSpec((1, T, D), lambda b, s: (b, s, 0))
    prev = pl.BlockSpec((1, HALO, D), lambda b, s: (b, jnp.maximum(s * per_tile - 1, 0), 0))
    nxt = pl.BlockSpec((1, HALO, D), lambda b, s: (b, jnp.minimum((s + 1) * per_tile, n_halo - 1), 0))
    consts = [p["ffn_norm"], p["ffn_w_up"], p["ffn_conv_w"], p["ffn_conv_b"], p["ffn_w_down"], final_gain]
    return pl.pallas_call(
        functools.partial(_ffn_kernel, final=final), grid=(B, S // T),
        in_specs=[tile, prev, nxt] + [_const_spec(c.shape) for c in consts],
        out_specs=tile, out_shape=jax.ShapeDtypeStruct((B, S, D), F32),
        scratch_shapes=[pltpu.VMEM((T + 2 * HALO, D), BF16), pltpu.VMEM((T, D_FF), BF16)],
        compiler_params=_params(2), name="ffn_final" if final else "ffn",
    )(x, x, x, *consts)


def _swap_pairs(w):
    n = w.shape[-1]
    return w.reshape(*w.shape[:-1], n // 2, 2)[..., ::-1].reshape(w.shape)


def _rope_tables(S, d_rot):
    rows = S // GRID_W
    row = jnp.repeat(jnp.arange(rows, dtype=F32), GRID_W)
    col = jnp.tile(jnp.arange(GRID_W, dtype=F32), rows)
    n = d_rot // 4
    inv = ROPE_THETA ** (-jnp.arange(n, dtype=F32) / n)
    ang = jnp.concatenate([row[:, None] * inv, col[:, None] * inv], axis=-1)
    cos, sin = jnp.cos(ang), jnp.sin(ang)
    c = jnp.repeat(cos, 2, axis=-1)
    s = jnp.stack([-sin, sin], axis=-1).reshape(S, d_rot)
    return c, s


def _prep_params(S, P):
    L = P["w_in"].shape[0]
    row = lambda v: v[:, None, :]
    ca, sa = _rope_tables(S, MLA_ROPE)
    cb, sb = _rope_tables(S, GQA_DIM)
    p, shared = {}, {}
    c_q, c_kv, k_rope, g_q, g_k, g_v, g_m = jnp.split(P["w_in"], [256, 384, 416, 800, 928, 1056], axis=2)
    pad = jnp.zeros((L, D_MODEL, Z_END - Z_KR - 2 * MLA_ROPE), F32)
    p["w_in"] = jnp.concatenate(
        [c_q, c_kv, g_q, _swap_pairs(g_q), g_k, _swap_pairs(g_k), g_v, g_m, k_rope, _swap_pairs(k_rope), pad],
        axis=2).astype(BF16)
    for name in ("mix_norm", "mla_q_norm", "mla_kv_norm", "gmlp_v_norm", "mem_x_norm", "mem_kv_norm", "ffn_norm",
                 "ffn_conv_b"):
        p[name] = row(P[name])

    w_uq = P["mla_w_uq"].reshape(L, MLA_Q_RANK, MLA_HEADS, MLA_NOPE + MLA_ROPE)
    zpad = jnp.zeros((L, MLA_Q_RANK, MLA_HEADS, HEAD_SLOT - MLA_NOPE - MLA_ROPE), F32)
    main = jnp.concatenate([w_uq, zpad], axis=-1)
    swapped = jnp.concatenate([jnp.zeros_like(w_uq[..., :MLA_NOPE]), _swap_pairs(w_uq[..., MLA_NOPE:]), zpad], axis=-1)
    p["w_uq"] = jnp.concatenate([main.reshape(L, MLA_Q_RANK, -1), swapped.reshape(L, MLA_Q_RANK, -1)],
                                axis=2).astype(BF16)
    scale_a = (MLA_NOPE + MLA_ROPE) ** -0.5 * LOG2E
    ones = jnp.ones((S, MLA_NOPE), F32)
    zeros_n = jnp.zeros((S, MLA_NOPE), F32)
    zeros_p = jnp.zeros((S, HEAD_SLOT - MLA_NOPE - MLA_ROPE), F32)
    shared["cqa"] = jnp.tile(jnp.concatenate([ones, ca, zeros_p], axis=1) * scale_a, (1, MLA_HEADS))
    shared["sqa"] = jnp.tile(jnp.concatenate([zeros_n, sa, zeros_p], axis=1) * scale_a, (1, MLA_HEADS))

    w_ukv = P["mla_w_ukv"].reshape(L, MLA_KV_RANK, MLA_HEADS, MLA_NOPE + MLA_V)
    k_lat = jnp.concatenate(
        [w_ukv[..., :MLA_NOPE], jnp.zeros((L, MLA_KV_RANK, MLA_HEADS, HEAD_SLOT - MLA_NOPE), F32)], axis=-1)
    place = jnp.concatenate([jnp.zeros((MLA_ROPE, MLA_NOPE), F32), jnp.eye(MLA_ROPE, dtype=F32),
                             jnp.zeros((MLA_ROPE, HEAD_SLOT - MLA_NOPE - MLA_ROPE), F32)], axis=1)
    place = jnp.broadcast_to(jnp.tile(place, (1, MLA_HEADS)), (L, MLA_ROPE, MLA_HEADS * HEAD_SLOT))
    zrows = jnp.zeros((L, Z_END - Z_KR - 2 * MLA_ROPE, MLA_HEADS * HEAD_SLOT), F32)
    p["w_k"] = jnp.concatenate([k_lat.reshape(L, MLA_KV_RANK, -1), place, place, zrows], axis=1).astype(BF16)
    p["w_v"] = w_ukv[..., MLA_NOPE:].reshape(L, MLA_KV_RANK, -1).astype(BF16)
    shared["tk"] = jnp.concatenate([ca, sa, jnp.zeros((S, Z_END - Z_KR - 2 * MLA_ROPE), F32)], axis=1)

    gq = row(jnp.tile(P["gqa_q_norm"], (1, GQA_HEADS)) * (GQA_DIM ** -0.5 * LOG2E))
    gk = row(jnp.tile(P["gqa_k_norm"], (1, GQA_KV_HEADS)))
    p["tcq"] = jnp.tile(cb, (1, GQA_HEADS)) * gq
    p["tsq"] = jnp.tile(sb, (1, GQA_HEADS)) * _swap_pairs(gq)
    p["tck"] = jnp.tile(cb, (1, GQA_KV_HEADS)) * gk
    p["tsk"] = jnp.tile(sb, (1, GQA_KV_HEADS)) * _swap_pairs(gk)
    grp = jnp.arange(W_B) // GQA_DIM
    shared["bd"] = (grp[:, None] == grp[None, :]).astype(BF16)

    p["w_s"] = P["gmlp_w_s"].reshape(L, GMLP_GROUPS * GMLP_CHUNK, GMLP_CHUNK).astype(BF16)
    p["bias_s"] = jnp.repeat(jnp.swapaxes(P["gmlp_b_s"], 1, 2), GMLP_DIM, axis=2)
    p["out_norm_a"] = row(P["out_norm"][:, :W_A])
    p["out_norm_b"] = row(P["out_norm"][:, W_A:W_A + W_B])
    p["out_norm_c"] = row(P["out_norm"][:, W_A + W_B:])
    for name in ("w_out", "mem_w_q", "mem_w_kv", "mem_w_o", "ffn_w_up", "ffn_w_down"):
        p[name] = P[name].astype(BF16)
    p["ffn_conv_w"] = P["ffn_conv_w"]
    return p, shared


def kernel(x, mem, mix_norm, w_in, mla_q_norm, mla_w_uq, mla_kv_norm, mla_w_ukv, gqa_q_norm, gqa_k_norm, gmlp_v_norm, gmlp_w_s, gmlp_b_s, out_norm, w_out, mem_x_norm, mem_kv_norm, mem_w_q, mem_w_kv, mem_w_o, ffn_norm, ffn_w_up, ffn_conv_w, ffn_conv_b, ffn_w_down, final_norm):
    P = dict(mix_norm=mix_norm, w_in=w_in, mla_q_norm=mla_q_norm, mla_w_uq=mla_w_uq, mla_kv_norm=mla_kv_norm,
             mla_w_ukv=mla_w_ukv, gqa_q_norm=gqa_q_norm, gqa_k_norm=gqa_k_norm, gmlp_v_norm=gmlp_v_norm,
             gmlp_w_s=gmlp_w_s, gmlp_b_s=gmlp_b_s, out_norm=out_norm, w_out=w_out, mem_x_norm=mem_x_norm,
             mem_kv_norm=mem_kv_norm, mem_w_q=mem_w_q, mem_w_kv=mem_w_kv, mem_w_o=mem_w_o, ffn_norm=ffn_norm,
             ffn_w_up=ffn_w_up, ffn_conv_w=ffn_conv_w, ffn_conv_b=ffn_conv_b, ffn_w_down=ffn_w_down)
    B, S, D = x.shape
    assert D == D_MODEL and S % TOK_TILE == 0 and S % GRID_W == 0
    depth = w_in.shape[0]
    stacked, shared = _prep_params(S, P)
    final_gain = final_norm.reshape(1, -1)
    group = GQA_HEADS // GQA_KV_HEADS
    for l in range(depth):
        p = dict(shared, **{name: v[l] for name, v in stacked.items()})
        qaT, ka, vaT, kna, qbT, kb, vbT, knb, qb8T, kb8, ync = _mix_in(x, p)
        yna = _attention(qaT, ka, vaT, kna, p["out_norm_a"], n_heads=MLA_HEADS,
                         k_lanes=tuple(h * HEAD_SLOT for h in range(MLA_HEADS)), bias_lane=MLA_BIAS_LANE,
                         v_rows=tuple(h * (MLA_V + V_PAD) for h in range(MLA_HEADS)), dv=MLA_V, name="attn_mla")
        ynb = _attention(qbT, kb, vbT, knb, p["out_norm_b"], (qb8T, kb8), n_heads=GQA_HEADS,
                         k_lanes=tuple((h // group) * HEAD_SLOT for h in range(GQA_HEADS)), bias_lane=GQA_BIAS_LANE,
                         v_rows=tuple((h // group) * (GQA_DIM + V_PAD) for h in range(GQA_HEADS)), dv=GQA_DIM, name="attn_gqa")
        mem_kT, mem_v = _mem_kv(mem, p["mem_kv_norm"], p["mem_w_kv"])
        x = _out_mem(yna, ynb, ync, x, p, mem_kT, mem_v)
        x = _ffn(x, p, final_gain, final=(l == depth - 1))
    return x
```

```python
import functools

import jax
import jax.numpy as jnp
from jax import lax
from jax.experimental import pallas as pl
from jax.experimental.pallas import tpu as pltpu

F32 = jnp.float32
BF16 = jnp.bfloat16
F8 = jnp.float8_e4m3fn

D_MODEL = 1024
GRID_W = 64
ROPE_THETA = 10000.0
EPS = 1e-6
MLA_HEADS = 6
MLA_NOPE = 64
MLA_ROPE = 32
MLA_V = 64
MLA_Q_RANK = 256
MLA_KV_RANK = 128
GQA_HEADS = 6
GQA_KV_HEADS = 2
GQA_DIM = 64
GMLP_GROUPS = 4
GMLP_DIM = 64
GMLP_CHUNK = 128
W_A = MLA_HEADS * MLA_V
W_B = GQA_HEADS * GQA_DIM
W_C = GMLP_GROUPS * GMLP_DIM
MEM_HEADS = 4
MEM_DIM = 128
D_FF = 2816

LANES = 128
HEAD_SLOT = LANES
TOK_TILE = 512
Q_TILE = 512
KV_TILE = 512
F32_ROWS = 8
BF16_ROWS = 16
V_PAD = BF16_ROWS
LOG2E = 1.4426950408889634
MLA_BIAS_LANE = MLA_NOPE + MLA_ROPE
GQA_BIAS_LANE = GQA_DIM
STABILISER_MAX = 50.0
FP8_ROWS = 32
Q8_SLOT = 256
Q8_SCALE = 32.0
K8_SCALE = 8.0
K8_BIAS = 32.0
FP8_MAX_SCALED = 400.0
FF_CHUNK = 256
HALO = BF16_ROWS
VMEM_LIMIT = 56 * 1024 * 1024

Z_CQ, Z_CKV, Z_GQ, Z_GQS, Z_GK, Z_GKS, Z_GV, Z_U, Z_VV, Z_KR, Z_END = (
    0, 256, 384, 768, 1152, 1280, 1408, 1536, 1792, 2048, 2176)


def _rms(x, g):
    return x * lax.rsqrt(jnp.mean(x * x, axis=-1, keepdims=True) + EPS) * g


def _dot(a, b):
    return jnp.dot(a, b, preferred_element_type=F32)


def _const_spec(shape):
    zeros = (0,) * len(shape)
    return pl.BlockSpec(shape, lambda *_: zeros, pipeline_mode=pl.Buffered(1))


def _params(n_axes):
    return pltpu.CompilerParams(dimension_semantics=("arbitrary",) * n_axes,
                                vmem_limit_bytes=VMEM_LIMIT)


def _group_ssq(v, bd):
    sq = v * v
    hi = sq.astype(BF16)
    lo = (sq - hi.astype(F32)).astype(BF16)
    return _dot(hi, bd) + _dot(lo, bd)


def _round_e4m3(x):
    c = x * (2.0 ** (24 - 4) + 1.0)
    return c - (c - x)


def _store_vT(ref, vT, n_heads, dv):
    ext = (lax.broadcasted_iota(jnp.int32, (V_PAD, KV_TILE), 0) == 0).astype(BF16)
    for n in range(vT.shape[1] // KV_TILE):
        for hd in range(n_heads):
            base = hd * (dv + V_PAD)
            ref[0, n, base:base + dv] = vT[hd * dv:(hd + 1) * dv, n * KV_TILE:(n + 1) * KV_TILE].astype(BF16)
            ref[0, n, base + dv:base + dv + V_PAD] = ext


def _store_key_norms(ref, k, n_slots, n_feat):
    lane = lax.broadcasted_iota(jnp.int32, (k.shape[0], HEAD_SLOT), 1)
    rows = []
    for n in range(n_slots):
        x = jnp.where(lane < n_feat, k[:, n * HEAD_SLOT:(n + 1) * HEAD_SLOT].astype(F32), 0.0)
        n2 = jnp.max(jnp.sum(x * x, axis=-1, keepdims=True), axis=0, keepdims=True)
        rows.append(jnp.broadcast_to(n2, (1, LANES)))
    rows.append(jnp.zeros((F32_ROWS - n_slots, LANES), F32))
    ref[0, 0] = jnp.concatenate(rows, axis=0)


def _mix_in_kernel(x_ref, g_ref, win_ref, gq_ref, wq_ref, gkv_ref, wk_ref, wk8_ref, wv_ref,
                   cqa_ref, sqa_ref, tk_ref, tcq_ref, tsq_ref, tck_ref, tsk_ref,
                   bd_ref, gv_ref, ws_ref, bias_ref, gc_ref,
                   qaT_ref, ka_ref, vaT_ref, kna_ref, qa8T_ref, ka8_ref,
                   qbT_ref, kb_ref, vbT_ref, knb_ref, qb8T_ref, kb8_ref, ync_ref):
    tok = x_ref.shape[1]
    h = _rms(x_ref[0], g_ref[...])
    z = _dot(h.astype(BF16), win_ref[...])

    cq = _rms(z[:, Z_CQ:Z_CKV], gq_ref[...]).astype(BF16)
    qa = _dot(cq, wq_ref[...])
    half = MLA_HEADS * HEAD_SLOT
    q_a = qa[:, :half] * cqa_ref[...] + qa[:, half:] * sqa_ref[...]
    q_aT = q_a.T
    qaT_ref[0] = q_aT.astype(BF16)
    q8 = q_aT * Q8_SCALE
    q_hi32 = _round_e4m3(q8)
    q_hi = q_hi32.astype(F8)
    q_lo = _round_e4m3(q8 - q_hi32).astype(F8)
    for hd in range(MLA_HEADS):
        nope = slice(hd * HEAD_SLOT, hd * HEAD_SLOT + MLA_NOPE)
        rope = slice(hd * HEAD_SLOT + MLA_NOPE, hd * HEAD_SLOT + MLA_NOPE + MLA_ROPE)
        base = hd * Q8_SLOT
        qa8T_ref[0, base:base + MLA_NOPE] = q_hi[nope]
        qa8T_ref[0, base + MLA_NOPE:base + 2 * MLA_NOPE] = q_hi[nope]
        qa8T_ref[0, base + 2 * MLA_NOPE:base + 3 * MLA_NOPE] = q_lo[nope]
        qa8T_ref[0, base + 3 * MLA_NOPE:base + 3 * MLA_NOPE + MLA_ROPE] = q_hi[rope]
        qa8T_ref[0, base + 3 * MLA_NOPE + MLA_ROPE:base + Q8_SLOT] = q_hi[rope]

    ckv = _rms(z[:, Z_CKV:Z_GQ], gkv_ref[...])
    kr = z[:, Z_KR:Z_END] * tk_ref[...]
    lhs = jnp.concatenate([ckv, kr], axis=1).astype(BF16)
    slot_lane = lax.broadcasted_iota(jnp.int32, (1, MLA_HEADS * HEAD_SLOT), 1) % HEAD_SLOT
    k_a = (_dot(lhs, wk_ref[...]) + (slot_lane == MLA_BIAS_LANE).astype(F32)).astype(BF16)
    ka_ref[0] = k_a
    _store_key_norms(kna_ref, k_a, MLA_HEADS, MLA_BIAS_LANE)
    k8 = _dot(lhs, wk8_ref[...]) * K8_SCALE
    k_hi = _round_e4m3(k8)
    k_lo = _round_e4m3(k8 - k_hi)
    depth = lax.broadcasted_iota(jnp.int32, (1, MLA_HEADS * Q8_SLOT), 1) % Q8_SLOT
    lo_block = jnp.logical_or(jnp.logical_and(depth >= MLA_NOPE, depth < 2 * MLA_NOPE),
                              depth >= 3 * MLA_NOPE + MLA_ROPE)
    ka8_ref[0] = jnp.where(lo_block, k_lo, k_hi).astype(F8)
    _store_vT(vaT_ref, _dot(lhs[:, :MLA_KV_RANK], wv_ref[...]).T, MLA_HEADS, MLA_V)

    bd = bd_ref[...]
    g_q = z[:, Z_GQ:Z_GQS]
    n_q = lax.rsqrt(_group_ssq(g_q, bd) * (1.0 / GQA_DIM) + EPS)
    q_b = (g_q * tcq_ref[...] + z[:, Z_GQS:Z_GK] * tsq_ref[...]) * n_q
    q_bT = q_b.T
    zero = jnp.zeros((HEAD_SLOT - GQA_DIM, tok), BF16)
    for hd in range(GQA_HEADS):
        qbT_ref[0, hd * HEAD_SLOT:hd * HEAD_SLOT + GQA_DIM] = q_bT[hd * GQA_DIM:(hd + 1) * GQA_DIM].astype(BF16)
        qbT_ref[0, hd * HEAD_SLOT + GQA_DIM:(hd + 1) * HEAD_SLOT] = zero
    q8 = q_bT * Q8_SCALE
    q_hi32 = _round_e4m3(q8)
    q_hi = q_hi32.astype(F8)
    q_lo = _round_e4m3(q8 - q_hi32).astype(F8)
    zero8 = jnp.zeros((Q8_SLOT - 3 * GQA_DIM, tok), F8)
    for hd in range(GQA_HEADS):
        rows = slice(hd * GQA_DIM, (hd + 1) * GQA_DIM)
        base = hd * Q8_SLOT
        qb8T_ref[0, base:base + GQA_DIM] = q_hi[rows]
        qb8T_ref[0, base + GQA_DIM:base + 2 * GQA_DIM] = q_hi[rows]
        qb8T_ref[0, base + 2 * GQA_DIM:base + 3 * GQA_DIM] = q_lo[rows]
        qb8T_ref[0, base + 3 * GQA_DIM:base + Q8_SLOT] = zero8
    g_k = z[:, Z_GK:Z_GKS]
    n_k = lax.rsqrt(_group_ssq(g_k, bd[:LANES, :LANES]) * (1.0 / GQA_DIM) + EPS)
    k_b = (g_k * tck_ref[...] + z[:, Z_GKS:Z_GV] * tsk_ref[...]) * n_k
    lane = lax.broadcasted_iota(jnp.int32, (tok, LANES), 1)
    bias_lane = (lane == GQA_BIAS_LANE).astype(F32)
    low = lane < GQA_DIM
    k8 = k_b * K8_SCALE
    k_hi = _round_e4m3(k8)
    k_lo = _round_e4m3(k8 - k_hi)
    hi_r = pltpu.roll(k_hi, GQA_DIM, 1)
    lo_r = pltpu.roll(k_lo, GQA_DIM, 1)
    bias8 = jnp.where(lane == GQA_DIM, K8_BIAS, 0.0)
    kb8_ref[0] = jnp.concatenate([jnp.where(low, k_hi, lo_r), jnp.where(low, k_hi, bias8),
                                  jnp.where(low, hi_r, k_lo), jnp.where(low, hi_r, bias8)], axis=1).astype(F8)
    k_slots = [jnp.where(low, k_b, bias_lane), jnp.where(low, pltpu.roll(k_b, GQA_DIM, 1), bias_lane)]
    k_b = jnp.concatenate(k_slots, axis=1).astype(BF16)
    kb_ref[0] = k_b
    _store_key_norms(knb_ref, k_b, GQA_KV_HEADS, GQA_BIAS_LANE)
    _store_vT(vbT_ref, z[:, Z_GV:Z_U].T, GQA_KV_HEADS, GQA_DIM)

    gm = jax.nn.gelu(z[:, Z_U:Z_KR])
    u = gm[:, :W_C]
    vv = _rms(gm[:, W_C:], gv_ref[...])
    lane_grp = lax.broadcasted_iota(jnp.int32, (GMLP_CHUNK, W_C), 1) // GMLP_DIM
    ws = ws_ref[...]
    bias = bias_ref[...]
    ycs = []
    for n in range(tok // GMLP_CHUNK):
        rows = slice(n * GMLP_CHUNK, (n + 1) * GMLP_CHUNK)
        r = _dot(ws, vv[rows].astype(BF16))
        mixed = r[3 * GMLP_CHUNK:]
        for grp in range(GMLP_GROUPS - 2, -1, -1):
            mixed = jnp.where(lane_grp == grp, r[grp * GMLP_CHUNK:(grp + 1) * GMLP_CHUNK], mixed)
        ycs.append(u[rows] * (mixed + bias))
    ync_ref[0] = _rms(jnp.concatenate(ycs, axis=0), gc_ref[...]).astype(BF16)


def _mix_in(x, p):
    B, S, D = x.shape
    T = TOK_TILE
    nkv = T // KV_TILE
    tile = lambda w: pl.BlockSpec((1, T, w), lambda s, b: (b, s, 0))
    tab = lambda w: pl.BlockSpec((T, w), lambda s, b: (s, 0))
    tposed = lambda r: pl.BlockSpec((1, r, T), lambda s, b: (b, 0, s))
    blocked = lambda r: pl.BlockSpec((1, nkv, r, KV_TILE), lambda s, b: (b, s, 0, 0))
    consts = [p["mix_norm"], p["w_in"], p["mla_q_norm"], p["w_uq"], p["mla_kv_norm"], p["w_k"], p["w_k8"], p["w_v"]]
    tabs = [p["cqa"], p["sqa"], p["tk"], p["tcq"], p["tsq"], p["tck"], p["tsk"]]
    consts2 = [p["bd"], p["gmlp_v_norm"], p["w_s"], p["bias_s"], p["out_norm_c"]]
    in_specs = ([tile(D)] + [_const_spec(c.shape) for c in consts] + [tab(t.shape[1]) for t in tabs]
                + [_const_spec(c.shape) for c in consts2])
    out_shape = [
        jax.ShapeDtypeStruct((B, MLA_HEADS * HEAD_SLOT, S), BF16),
        jax.ShapeDtypeStruct((B, S, MLA_HEADS * HEAD_SLOT), BF16),
        jax.ShapeDtypeStruct((B, S // KV_TILE, MLA_HEADS * (MLA_V + V_PAD), KV_TILE), BF16),
        jax.ShapeDtypeStruct((B, S // T, F32_ROWS, LANES), F32),
        jax.ShapeDtypeStruct((B, MLA_HEADS * Q8_SLOT, S), F8),
        jax.ShapeDtypeStruct((B, S, MLA_HEADS * Q8_SLOT), F8),
        jax.ShapeDtypeStruct((B, GQA_HEADS * HEAD_SLOT, S), BF16),
        jax.ShapeDtypeStruct((B, S, GQA_KV_HEADS * HEAD_SLOT), BF16),
        jax.ShapeDtypeStruct((B, S // KV_TILE, GQA_KV_HEADS * (GQA_DIM + V_PAD), KV_TILE), BF16),
        jax.ShapeDtypeStruct((B, S // T, F32_ROWS, LANES), F32),
        jax.ShapeDtypeStruct((B, GQA_HEADS * Q8_SLOT, S), F8),
        jax.ShapeDtypeStruct((B, S, GQA_KV_HEADS * Q8_SLOT), F8),
        jax.ShapeDtypeStruct((B, S, W_C), BF16),
    ]
    norms = pl.BlockSpec((1, 1, F32_ROWS, LANES), lambda s, b: (b, s, 0, 0))
    out_specs = [tposed(MLA_HEADS * HEAD_SLOT), tile(MLA_HEADS * HEAD_SLOT), blocked(MLA_HEADS * (MLA_V + V_PAD)),
                 norms, tposed(MLA_HEADS * Q8_SLOT), tile(MLA_HEADS * Q8_SLOT),
                 tposed(GQA_HEADS * HEAD_SLOT), tile(GQA_KV_HEADS * HEAD_SLOT),
                 blocked(GQA_KV_HEADS * (GQA_DIM + V_PAD)), norms, tposed(GQA_HEADS * Q8_SLOT),
                 tile(GQA_KV_HEADS * Q8_SLOT), tile(W_C)]
    return pl.pallas_call(
        _mix_in_kernel, grid=(S // T, B), in_specs=in_specs, out_specs=out_specs, out_shape=out_shape,
        compiler_params=_params(2), name="mix_in",
    )(x, *consts, *tabs, *consts2)


def _attn_kernel(q_ref, k_ref, v_ref, kn_ref, g_ref, *rest, n_heads, k_lanes, bias_lane, v_rows, dv, fp8):
    if fp8:
        q8_ref, k8_ref, o_ref, s_ref, p_ref, m_ref, acc_ref, oT_ref, qx_ref = rest
    else:
        o_ref, s_ref, p_ref, m_ref, acc_ref, oT_ref, qx_ref = rest
    nkv = v_ref.shape[1]
    mq = q_ref.shape[2]
    assert n_heads % 2 == 0

    kmax = jnp.max(kn_ref[0], axis=0)
    first_row = lax.broadcasted_iota(jnp.int32, (FP8_ROWS if fp8 else BF16_ROWS, mq), 0) == 0
    bound = jnp.zeros((1, 1), F32)
    qn2_max = jnp.zeros((1, 1), F32)
    kn2_max = jnp.zeros((1, 1), F32)
    for hd in range(n_heads):
        rows = slice(hd * HEAD_SLOT, (hd + 1) * HEAD_SLOT)
        q = q_ref[0, rows, :]
        qf = q.astype(F32)
        qn2 = jnp.sum(qf * qf, axis=0, keepdims=True)
        kn2 = kmax[k_lanes[hd] // HEAD_SLOT:k_lanes[hd] // HEAD_SLOT + 1, 0:1]
        u = jnp.sqrt(qn2 * kn2)
        bound = jnp.maximum(bound, jnp.max(u, axis=1, keepdims=True))
        if fp8:
            qn2_max = jnp.maximum(qn2_max, jnp.max(qn2, axis=1, keepdims=True))
            kn2_max = jnp.maximum(kn2_max, kn2)
            rows8 = slice(hd * Q8_SLOT, (hd + 1) * Q8_SLOT)
            qx_ref[rows8, :] = q8_ref[0, rows8, :]
            m_ref[hd] = u
        else:
            qx_ref[rows, :] = q
            bias_rows = slice(hd * HEAD_SLOT + bias_lane, hd * HEAD_SLOT + bias_lane + BF16_ROWS)
            qx_ref[bias_rows, :] = jnp.where(first_row, -u, 0.0).astype(BF16)
    stabilise = jnp.max(bound) <= STABILISER_MAX
    if fp8:
        stabilise = jnp.logical_and(stabilise, jnp.logical_and(
            jnp.max(qn2_max) <= (FP8_MAX_SCALED / Q8_SCALE) ** 2, jnp.max(kn2_max) <= (FP8_MAX_SCALED / K8_SCALE) ** 2))
    acc_ref[...] = jnp.zeros(acc_ref.shape, F32)

    def run(stabilised):
        def scores(hd, off):
            rows = slice(hd * HEAD_SLOT, (hd + 1) * HEAD_SLOT)
            if stabilised and fp8:
                slot = k_lanes[hd] // HEAD_SLOT
                k = k8_ref[0, pl.ds(off, KV_TILE), slot * Q8_SLOT:(slot + 1) * Q8_SLOT]
                qT = qx_ref[hd * Q8_SLOT:(hd + 1) * Q8_SLOT, :]
            else:
                k = k_ref[0, pl.ds(off, KV_TILE), k_lanes[hd]:k_lanes[hd] + HEAD_SLOT]
                qT = qx_ref[rows, :] if stabilised else q_ref[0, rows, :]
            s_ref[hd % 2] = _dot(k, qT)

        scores(0, 0)

        def step(j, carry):
            off = pl.multiple_of(j * KV_TILE, KV_TILE)
            off_next = pl.multiple_of(jnp.minimum(j + 1, nkv - 1) * KV_TILE, KV_TILE)
            for hd in range(n_heads):
                if hd + 1 < n_heads:
                    scores(hd + 1, off)
                else:
                    scores(0, off_next)
                vT = v_ref[0, j, v_rows[hd]:v_rows[hd] + dv + V_PAD, :]
                if stabilised:
                    s = s_ref[hd % 2]
                    if fp8:
                        s = s * (1.0 / (Q8_SCALE * K8_SCALE)) - m_ref[hd]
                    p_ref[hd] = jnp.exp2(s.astype(BF16))
                    acc_ref[hd] += _dot(vT, p_ref[hd])
                else:
                    m_old = m_ref[hd]
                    m_new = jnp.maximum(m_old, jnp.max(s_ref[hd % 2], axis=0, keepdims=True))
                    m_ref[hd] = m_new
                    p_ref[hd] = jnp.exp2(s_ref[hd % 2] - m_new).astype(BF16)
                    acc_ref[hd] = jnp.exp2(m_old - m_new) * acc_ref[hd] + _dot(vT, p_ref[hd])
            return carry

        lax.fori_loop(0, nkv, step, 0, unroll=8 if stabilised else 1)

    def running_max():
        m_ref[...] = jnp.full(m_ref.shape, -1e30, F32)
        run(False)

    lax.cond(stabilise, lambda: run(True), running_max)
    for hd in range(n_heads):
        oT_ref[hd * dv:(hd + 1) * dv, :] = acc_ref[hd, :dv] / acc_ref[hd, dv:dv + 1]
    o_ref[0] = _rms(oT_ref[...].T, g_ref[...]).astype(BF16)


def _attention(qT, k, vT, kn, gain, fp8_operands=(), *, n_heads, k_lanes, bias_lane, v_rows, dv, name):
    B, _, S = qT.shape
    fp8 = bool(fp8_operands)
    kernel = functools.partial(_attn_kernel, n_heads=n_heads, k_lanes=k_lanes, bias_lane=bias_lane,
                               v_rows=v_rows, dv=dv, fp8=fp8)
    q_slot, q_dtype = (Q8_SLOT, F8) if fp8 else (HEAD_SLOT, BF16)
    q_tile = lambda slot: pl.BlockSpec((1, n_heads * slot, Q_TILE), lambda b, i: (b, 0, i))
    whole = lambda shape: pl.BlockSpec((1,) + shape[1:], lambda b, i: (b,) + (0,) * (len(shape) - 1),
                                       pipeline_mode=pl.Buffered(1))
    in_specs = [q_tile(HEAD_SLOT), whole(k.shape), whole(vT.shape), whole(kn.shape), _const_spec(gain.shape)]
    if fp8:
        in_specs += [q_tile(Q8_SLOT), whole(fp8_operands[1].shape)]
    return pl.pallas_call(
        kernel, grid=(B, S // Q_TILE), in_specs=in_specs,
        out_specs=pl.BlockSpec((1, Q_TILE, n_heads * dv), lambda b, i: (b, i, 0)),
        out_shape=jax.ShapeDtypeStruct((B, S, n_heads * dv), BF16),
        scratch_shapes=[pltpu.VMEM((2, KV_TILE, Q_TILE), F32), pltpu.VMEM((n_heads, KV_TILE, Q_TILE), BF16),
                        pltpu.VMEM((n_heads, 1, Q_TILE), F32), pltpu.VMEM((n_heads, dv + V_PAD, Q_TILE), F32),
                        pltpu.VMEM((n_heads * dv, Q_TILE), F32), pltpu.VMEM((n_heads * q_slot, Q_TILE), q_dtype)],
        compiler_params=_params(2), name=name,
    )(qT, k, vT, kn, gain, *fp8_operands)


def _mem_kv_kernel(mem_ref, g_ref, w_ref, kT_ref, v_ref):
    kv = _dot(_rms(mem_ref[0], g_ref[...]).astype(BF16), w_ref[...])
    width = MEM_HEADS * MEM_DIM
    kT_ref[0] = kv[:, :width].T.astype(BF16)
    ones = jnp.ones((kv.shape[0], MEM_DIM), BF16)
    for hd in range(MEM_HEADS):
        v_ref[0, hd, :, :MEM_DIM] = kv[:, width + hd * MEM_DIM:width + (hd + 1) * MEM_DIM].astype(BF16)
        v_ref[0, hd, :, MEM_DIM:] = ones


def _mem_kv(mem, gain, w_kv):
    B, Tm, D = mem.shape
    width = MEM_HEADS * MEM_DIM
    return pl.pallas_call(
        _mem_kv_kernel, grid=(B,),
        in_specs=[pl.BlockSpec((1, Tm, D), lambda b: (b, 0, 0)), _const_spec(gain.shape), _const_spec(w_kv.shape)],
        out_specs=[pl.BlockSpec((1, width, Tm), lambda b: (b, 0, 0)),
                   pl.BlockSpec((1, MEM_HEADS, Tm, 2 * MEM_DIM), lambda b: (b, 0, 0, 0))],
        out_shape=[jax.ShapeDtypeStruct((B, width, Tm), BF16),
                   jax.ShapeDtypeStruct((B, MEM_HEADS, Tm, 2 * MEM_DIM), BF16)],
        compiler_params=_params(1), name="mem_kv",
    )(mem, gain, w_kv)


def _out_mem_kernel(ya_ref, yb_ref, yc_ref, x_ref, wout_ref, g_ref, wq_ref, kT_ref, v_ref, wo_ref, o_ref):
    y = jnp.concatenate([ya_ref[0], yb_ref[0], yc_ref[0]], axis=1)
    x1 = x_ref[0] + _dot(y, wout_ref[...])
    h = _rms(x1, g_ref[...]).astype(BF16)
    q = (_dot(h, wq_ref[...]) * (MEM_DIM ** -0.5)).astype(BF16)
    heads = []
    for hd in range(MEM_HEADS):
        s = _dot(q[:, hd * MEM_DIM:(hd + 1) * MEM_DIM], kT_ref[0, hd * MEM_DIM:(hd + 1) * MEM_DIM, :])
        p = jnp.exp(s - jnp.max(s, axis=-1, keepdims=True)).astype(BF16)
        pv = _dot(p, v_ref[0, hd])
        heads.append(pv[:, :MEM_DIM] / pv[:, MEM_DIM:])
    o = jnp.concatenate(heads, axis=1).astype(BF16)
    o_ref[0] = x1 + _dot(o, wo_ref[...])


def _out_mem(ya, yb, yc, x, p, mem_kT, mem_v):
    B, S, D = x.shape
    T = TOK_TILE
    tile = lambda w: pl.BlockSpec((1, T, w), lambda b, s: (b, s, 0))
    per_b = lambda a: pl.BlockSpec((1,) + a.shape[1:], lambda b, s: (b,) + (0,) * (a.ndim - 1))
    consts = [p["w_out"], p["mem_x_norm"], p["mem_w_q"]]
    return pl.pallas_call(
        _out_mem_kernel, grid=(B, S // T),
        in_specs=[tile(W_A), tile(W_B), tile(W_C), tile(D)] + [_const_spec(c.shape) for c in consts]
                 + [per_b(mem_kT), per_b(mem_v), _const_spec(p["mem_w_o"].shape)],
        out_specs=tile(D), out_shape=jax.ShapeDtypeStruct((B, S, D), F32),
        compiler_params=_params(2), name="out_mem",
    )(ya, yb, yc, x, *consts, mem_kT, mem_v, p["mem_w_o"])


def _ffn_kernel(x_ref, xp_ref, xn_ref, g_ref, wup_ref, cw_ref, cb_ref, wdn_ref, fg_ref, o_ref,
                h_ref, act_ref, *, final):
    tok = x_ref.shape[1]
    i = pl.program_id(1)
    g = g_ref[...]
    x = x_ref[0]
    keep_prev = (i > 0).astype(F32)
    keep_next = (i < pl.num_programs(1) - 1).astype(F32)
    h_ref[0:HALO] = (_rms(xp_ref[0], g) * keep_prev).astype(BF16)
    h_ref[HALO:HALO + tok] = _rms(x, g).astype(BF16)
    h_ref[HALO + tok:] = (_rms(xn_ref[0], g) * keep_next).astype(BF16)
    hext = h_ref[...]
    rows = tok + 2 * HALO

    def conv_up(cols):
        a = _dot(hext, wup_ref[:, cols])
        w = cw_ref[:, cols]
        return (pltpu.roll(a, 1, 0)[HALO:HALO + tok] * w[0:1] + a[HALO:HALO + tok] * w[1:2]
                + pltpu.roll(a, rows - 1, 0)[HALO:HALO + tok] * w[2:3] + cb_ref[:, cols])

    for c in range(D_FF // FF_CHUNK):
        gate = conv_up(slice(FF_CHUNK * c, FF_CHUNK * (c + 1)))
        val = conv_up(slice(D_FF + FF_CHUNK * c, D_FF + FF_CHUNK * (c + 1)))
        act_ref[:, FF_CHUNK * c:FF_CHUNK * (c + 1)] = (jax.nn.silu(gate) * val).astype(BF16)
    y = x + _dot(act_ref[...], wdn_ref[...])
    if final:
        y = _rms(y, fg_ref[...])
    o_ref[0] = y


def _ffn(x, p, final_gain, *, final):
    B, S, D = x.shape
    T = TOK_TILE
    per_tile = T // HALO
    n_halo = S // HALO
    tile = pl.BlockSpec((1, T, D), lambda b, s: (b, s, 0))
    prev = pl.BlockSpec((1, HALO, D), lambda b, s: (b, jnp.maximum(s * per_tile - 1, 0), 0))
    nxt = pl.BlockSpec((1, HALO, D), lambda b, s: (b, jnp.minimum((s + 1) * per_tile, n_halo - 1), 0))
    consts = [p["ffn_norm"], p["ffn_w_up"], p["ffn_conv_w"], p["ffn_conv_b"], p["ffn_w_down"], final_gain]
    return pl.pallas_call(
        functools.partial(_ffn_kernel, final=final), grid=(B, S // T),
        in_specs=[tile, prev, nxt] + [_const_spec(c.shape) for c in consts],
        out_specs=tile, out_shape=jax.ShapeDtypeStruct((B, S, D), F32),
        scratch_shapes=[pltpu.VMEM((T + 2 * HALO, D), BF16), pltpu.VMEM((T, D_FF), BF16)],
        compiler_params=_params(2), name="ffn_final" if final else "ffn",
    )(x, x, x, *consts)


def _swap_pairs(w):
    n = w.shape[-1]
    return w.reshape(*w.shape[:-1], n // 2, 2)[..., ::-1].reshape(w.shape)


def _rope_tables(S, d_rot):
    rows = S // GRID_W
    row = jnp.repeat(jnp.arange(rows, dtype=F32), GRID_W)
    col = jnp.tile(jnp.arange(GRID_W, dtype=F32), rows)
    n = d_rot // 4
    inv = ROPE_THETA ** (-jnp.arange(n, dtype=F32) / n)
    ang = jnp.concatenate([row[:, None] * inv, col[:, None] * inv], axis=-1)
    cos, sin = jnp.cos(ang), jnp.sin(ang)
    c = jnp.repeat(cos, 2, axis=-1)
    s = jnp.stack([-sin, sin], axis=-1).reshape(S, d_rot)
    return c, s


def _prep_params(S, P):
    L = P["w_in"].shape[0]
    row = lambda v: v[:, None, :]
    ca, sa = _rope_tables(S, MLA_ROPE)
    cb, sb = _rope_tables(S, GQA_DIM)
    p, shared = {}, {}
    c_q, c_kv, k_rope, g_q, g_k, g_v, g_m = jnp.split(P["w_in"], [256, 384, 416, 800, 928, 1056], axis=2)
    pad = jnp.zeros((L, D_MODEL, Z_END - Z_KR - 2 * MLA_ROPE), F32)
    p["w_in"] = jnp.concatenate(
        [c_q, c_kv, g_q, _swap_pairs(g_q), g_k, _swap_pairs(g_k), g_v, g_m, k_rope, _swap_pairs(k_rope), pad],
        axis=2).astype(BF16)
    for name in ("mix_norm", "mla_q_norm", "mla_kv_norm", "gmlp_v_norm", "mem_x_norm", "mem_kv_norm", "ffn_norm",
                 "ffn_conv_b"):
        p[name] = row(P[name])

    w_uq = P["mla_w_uq"].reshape(L, MLA_Q_RANK, MLA_HEADS, MLA_NOPE + MLA_ROPE)
    zpad = jnp.zeros((L, MLA_Q_RANK, MLA_HEADS, HEAD_SLOT - MLA_NOPE - MLA_ROPE), F32)
    main = jnp.concatenate([w_uq, zpad], axis=-1)
    swapped = jnp.concatenate([jnp.zeros_like(w_uq[..., :MLA_NOPE]), _swap_pairs(w_uq[..., MLA_NOPE:]), zpad], axis=-1)
    p["w_uq"] = jnp.concatenate([main.reshape(L, MLA_Q_RANK, -1), swapped.reshape(L, MLA_Q_RANK, -1)],
                                axis=2).astype(BF16)
    scale_a = (MLA_NOPE + MLA_ROPE) ** -0.5 * LOG2E
    ones = jnp.ones((S, MLA_NOPE), F32)
    zeros_n = jnp.zeros((S, MLA_NOPE), F32)
    zeros_p = jnp.zeros((S, HEAD_SLOT - MLA_NOPE - MLA_ROPE), F32)
    shared["cqa"] = jnp.tile(jnp.concatenate([ones, ca, zeros_p], axis=1) * scale_a, (1, MLA_HEADS))
    shared["sqa"] = jnp.tile(jnp.concatenate([zeros_n, sa, zeros_p], axis=1) * scale_a, (1, MLA_HEADS))

    w_ukv = P["mla_w_ukv"].reshape(L, MLA_KV_RANK, MLA_HEADS, MLA_NOPE + MLA_V)
    k_lat = jnp.concatenate(
        [w_ukv[..., :MLA_NOPE], jnp.zeros((L, MLA_KV_RANK, MLA_HEADS, HEAD_SLOT - MLA_NOPE), F32)], axis=-1)
    place = jnp.concatenate([jnp.zeros((MLA_ROPE, MLA_NOPE), F32), jnp.eye(MLA_ROPE, dtype=F32),
                             jnp.zeros((MLA_ROPE, HEAD_SLOT - MLA_NOPE - MLA_ROPE), F32)], axis=1)
    place = jnp.broadcast_to(jnp.tile(place, (1, MLA_HEADS)), (L, MLA_ROPE, MLA_HEADS * HEAD_SLOT))
    zrows = jnp.zeros((L, Z_END - Z_KR - 2 * MLA_ROPE, MLA_HEADS * HEAD_SLOT), F32)
    p["w_k"] = jnp.concatenate([k_lat.reshape(L, MLA_KV_RANK, -1), place, place, zrows], axis=1).astype(BF16)
    p["w_v"] = w_ukv[..., MLA_NOPE:].reshape(L, MLA_KV_RANK, -1).astype(BF16)
    nope_w = w_ukv[..., :MLA_NOPE]
    lat8 = jnp.concatenate([nope_w, nope_w, nope_w, jnp.zeros((L, MLA_KV_RANK, MLA_HEADS, 2 * MLA_ROPE), F32)], axis=-1)
    place8 = jnp.concatenate([jnp.zeros((MLA_ROPE, 3 * MLA_NOPE), F32), jnp.eye(MLA_ROPE, dtype=F32),
                              jnp.eye(MLA_ROPE, dtype=F32)], axis=1)
    place8 = jnp.broadcast_to(jnp.tile(place8, (1, MLA_HEADS)), (L, MLA_ROPE, MLA_HEADS * Q8_SLOT))
    zrows8 = jnp.zeros((L, Z_END - Z_KR - 2 * MLA_ROPE, MLA_HEADS * Q8_SLOT), F32)
    p["w_k8"] = jnp.concatenate([lat8.reshape(L, MLA_KV_RANK, -1), place8, place8, zrows8], axis=1).astype(BF16)
    shared["tk"] = jnp.concatenate([ca, sa, jnp.zeros((S, Z_END - Z_KR - 2 * MLA_ROPE), F32)], axis=1)

    gq = row(jnp.tile(P["gqa_q_norm"], (1, GQA_HEADS)) * (GQA_DIM ** -0.5 * LOG2E))
    gk = row(jnp.tile(P["gqa_k_norm"], (1, GQA_KV_HEADS)))
    p["tcq"] = jnp.tile(cb, (1, GQA_HEADS)) * gq
    p["tsq"] = jnp.tile(sb, (1, GQA_HEADS)) * _swap_pairs(gq)
    p["tck"] = jnp.tile(cb, (1, GQA_KV_HEADS)) * gk
    p["tsk"] = jnp.tile(sb, (1, GQA_KV_HEADS)) * _swap_pairs(gk)
    grp = jnp.arange(W_B) // GQA_DIM
    shared["bd"] = (grp[:, None] == grp[None, :]).astype(BF16)

    p["w_s"] = P["gmlp_w_s"].reshape(L, GMLP_GROUPS * GMLP_CHUNK, GMLP_CHUNK).astype(BF16)
    p["bias_s"] = jnp.repeat(jnp.swapaxes(P["gmlp_b_s"], 1, 2), GMLP_DIM, axis=2)
    p["out_norm_a"] = row(P["out_norm"][:, :W_A])
    p["out_norm_b"] = row(P["out_norm"][:, W_A:W_A + W_B])
    p["out_norm_c"] = row(P["out_norm"][:, W_A + W_B:])
    for name in ("w_out", "mem_w_q", "mem_w_kv", "mem_w_o", "ffn_w_up", "ffn_w_down"):
        p[name] = P[name].astype(BF16)
    p["ffn_conv_w"] = P["ffn_conv_w"]
    return p, shared


def kernel(x, mem, mix_norm, w_in, mla_q_norm, mla_w_uq, mla_kv_norm, mla_w_ukv, gqa_q_norm, gqa_k_norm, gmlp_v_norm, gmlp_w_s, gmlp_b_s, out_norm, w_out, mem_x_norm, mem_kv_norm, mem_w_q, mem_w_kv, mem_w_o, ffn_norm, ffn_w_up, ffn_conv_w, ffn_conv_b, ffn_w_down, final_norm):
    P = dict(mix_norm=mix_norm, w_in=w_in, mla_q_norm=mla_q_norm, mla_w_uq=mla_w_uq, mla_kv_norm=mla_kv_norm,
             mla_w_ukv=mla_w_ukv, gqa_q_norm=gqa_q_norm, gqa_k_norm=gqa_k_norm, gmlp_v_norm=gmlp_v_norm,
             gmlp_w_s=gmlp_w_s, gmlp_b_s=gmlp_b_s, out_norm=out_norm, w_out=w_out, mem_x_norm=mem_x_norm,
             mem_kv_norm=mem_kv_norm, mem_w_q=mem_w_q, mem_w_kv=mem_w_kv, mem_w_o=mem_w_o, ffn_norm=ffn_norm,
             ffn_w_up=ffn_w_up, ffn_conv_w=ffn_conv_w, ffn_conv_b=ffn_conv_b, ffn_w_down=ffn_w_down)
    B, S, D = x.shape
    assert D == D_MODEL and S % TOK_TILE == 0 and S % GRID_W == 0
    depth = w_in.shape[0]
    stacked, shared = _prep_params(S, P)
    final_gain = final_norm.reshape(1, -1)
    group = GQA_HEADS // GQA_KV_HEADS
    for l in range(depth):
        p = dict(shared, **{name: v[l] for name, v in stacked.items()})
        qaT, ka, vaT, kna, qa8T, ka8, qbT, kb, vbT, knb, qb8T, kb8, ync = _mix_in(x, p)
        yna = _attention(qaT, ka, vaT, kna, p["out_norm_a"], (qa8T, ka8), n_heads=MLA_HEADS,
                         k_lanes=tuple(h * HEAD_SLOT for h in range(MLA_HEADS)), bias_lane=MLA_BIAS_LANE,
                         v_rows=tuple(h * (MLA_V + V_PAD) for h in range(MLA_HEADS)), dv=MLA_V, name="attn_mla")
        ynb = _attention(qbT, kb, vbT, knb, p["out_norm_b"], (qb8T, kb8), n_heads=GQA_HEADS,
                         k_lanes=tuple((h // group) * HEAD_SLOT for h in range(GQA_HEADS)), bias_lane=GQA_BIAS_LANE,
                         v_rows=tuple((h // group) * (GQA_DIM + V_PAD) for h in range(GQA_HEADS)), dv=GQA_DIM, name="attn_gqa")
        mem_kT, mem_v = _mem_kv(mem, p["mem_kv_norm"], p["mem_w_kv"])
        x = _out_mem(yna, ynb, ync, x, p, mem_kT, mem_v)
        x = _ffn(x, p, final_gain, final=(l == depth - 1))
    return x
```

```python
import functools

import jax
import jax.numpy as jnp
from jax import lax
from jax.experimental import pallas as pl
from jax.experimental.pallas import tpu as pltpu

F32 = jnp.float32
BF16 = jnp.bfloat16
F8 = jnp.float8_e4m3fn

D_MODEL = 1024
GRID_W = 64
ROPE_THETA = 10000.0
EPS = 1e-6
MLA_HEADS = 6
MLA_NOPE = 64
MLA_ROPE = 32
MLA_V = 64
MLA_Q_RANK = 256
MLA_KV_RANK = 128
GQA_HEADS = 6
GQA_KV_HEADS = 2
GQA_DIM = 64
GMLP_GROUPS = 4
GMLP_DIM = 64
GMLP_CHUNK = 128
W_A = MLA_HEADS * MLA_V
W_B = GQA_HEADS * GQA_DIM
W_C = GMLP_GROUPS * GMLP_DIM
MEM_HEADS = 4
MEM_DIM = 128
D_FF = 2816

LANES = 128
HEAD_SLOT = LANES
TOK_TILE = 512
Q_TILE = 512
KV_TILE = 512
F32_ROWS = 8
BF16_ROWS = 16
V_PAD = BF16_ROWS
LOG2E = 1.4426950408889634
MLA_BIAS_LANE = MLA_NOPE + MLA_ROPE
GQA_BIAS_LANE = GQA_DIM
STABILISER_MAX = 50.0
Q8_SLOT = 256
Q8_SCALE = 32.0
K8_SCALE = 8.0
FP8_MAX_SCALED = 400.0
FF_CHUNK = 256
HALO = BF16_ROWS
VMEM_LIMIT = 56 * 1024 * 1024

Z_CQ, Z_CKV, Z_GQ, Z_GQS, Z_GK, Z_GKS, Z_GV, Z_U, Z_VV, Z_KR, Z_END = (
    0, 256, 384, 768, 1152, 1280, 1408, 1536, 1792, 2048, 2176)


def _rms(x, g):
    return x * lax.rsqrt(jnp.mean(x * x, axis=-1, keepdims=True) + EPS) * g


def _dot(a, b):
    return jnp.dot(a, b, preferred_element_type=F32)


def _const_spec(shape):
    zeros = (0,) * len(shape)
    return pl.BlockSpec(shape, lambda *_: zeros, pipeline_mode=pl.Buffered(1))


def _params(n_axes):
    return pltpu.CompilerParams(dimension_semantics=("arbitrary",) * n_axes,
                                vmem_limit_bytes=VMEM_LIMIT)


def _group_ssq(v, bd):
    sq = v * v
    hi = sq.astype(BF16)
    lo = (sq - hi.astype(F32)).astype(BF16)
    return _dot(hi, bd) + _dot(lo, bd)


def _split_e4m3(x):
    hi = x.astype(F8)
    return hi, (x - hi.astype(F32)).astype(F8)


def _store_vT(ref, vT, n_heads, dv):
    ext = (lax.broadcasted_iota(jnp.int32, (V_PAD, KV_TILE), 0) == 0).astype(BF16)
    for n in range(vT.shape[1] // KV_TILE):
        for hd in range(n_heads):
            base = hd * (dv + V_PAD)
            ref[0, n, base:base + dv] = vT[hd * dv:(hd + 1) * dv, n * KV_TILE:(n + 1) * KV_TILE].astype(BF16)
            ref[0, n, base + dv:base + dv + V_PAD] = ext


def _store_key_norms(ref, k, n_slots, n_feat):
    lane = lax.broadcasted_iota(jnp.int32, (k.shape[0], HEAD_SLOT), 1)
    rows = []
    for n in range(n_slots):
        x = jnp.where(lane < n_feat, k[:, n * HEAD_SLOT:(n + 1) * HEAD_SLOT].astype(F32), 0.0)
        n2 = jnp.max(jnp.sum(x * x, axis=-1, keepdims=True), axis=0, keepdims=True)
        rows.append(jnp.broadcast_to(n2, (1, LANES)))
    rows.append(jnp.zeros((F32_ROWS - n_slots, LANES), F32))
    ref[0, 0] = jnp.concatenate(rows, axis=0)


def _mix_in_kernel(x_ref, g_ref, win_ref, gq_ref, wq_ref, gkv_ref, wk_ref, wk8_ref, wv_ref,
                   cqa_ref, sqa_ref, tk_ref, tcq_ref, tsq_ref, tck_ref, tsk_ref,
                   bd_ref, gv_ref, ws_ref, bias_ref, gc_ref,
                   qaT_ref, ka_ref, vaT_ref, kna_ref, qa8T_ref, ka8_ref,
                   qbT_ref, kb_ref, vbT_ref, knb_ref, qb8T_ref, kb8_ref, ync_ref):
    tok = x_ref.shape[1]
    h = _rms(x_ref[0], g_ref[...])
    z = _dot(h.astype(BF16), win_ref[...])

    cq = _rms(z[:, Z_CQ:Z_CKV], gq_ref[...]).astype(BF16)
    qa = _dot(cq, wq_ref[...])
    half = MLA_HEADS * HEAD_SLOT
    q_a = qa[:, :half] * cqa_ref[...] + qa[:, half:] * sqa_ref[...]
    q_aT = q_a.T
    qaT_ref[0] = q_aT.astype(BF16)
    q_hi, q_lo = _split_e4m3(q_aT * Q8_SCALE)
    for hd in range(MLA_HEADS):
        nope = slice(hd * HEAD_SLOT, hd * HEAD_SLOT + MLA_NOPE)
        rope = slice(hd * HEAD_SLOT + MLA_NOPE, hd * HEAD_SLOT + MLA_NOPE + MLA_ROPE)
        base = hd * Q8_SLOT
        qa8T_ref[0, base:base + MLA_NOPE] = q_hi[nope]
        qa8T_ref[0, base + MLA_NOPE:base + 2 * MLA_NOPE] = q_hi[nope]
        qa8T_ref[0, base + 2 * MLA_NOPE:base + 3 * MLA_NOPE] = q_lo[nope]
        qa8T_ref[0, base + 3 * MLA_NOPE:base + 3 * MLA_NOPE + MLA_ROPE] = q_hi[rope]
        qa8T_ref[0, base + 3 * MLA_NOPE + MLA_ROPE:base + Q8_SLOT] = q_hi[rope]

    ckv = _rms(z[:, Z_CKV:Z_GQ], gkv_ref[...])
    kr = z[:, Z_KR:Z_END] * tk_ref[...]
    lhs = jnp.concatenate([ckv, kr], axis=1).astype(BF16)
    slot_lane = lax.broadcasted_iota(jnp.int32, (1, MLA_HEADS * HEAD_SLOT), 1) % HEAD_SLOT
    k_a = (_dot(lhs, wk_ref[...]) + (slot_lane == MLA_BIAS_LANE).astype(F32)).astype(BF16)
    ka_ref[0] = k_a
    _store_key_norms(kna_ref, k_a, MLA_HEADS, MLA_BIAS_LANE)
    k8 = _dot(lhs, wk8_ref[...])
    depth = lax.broadcasted_iota(jnp.int32, (1, MLA_HEADS * Q8_SLOT), 1) % Q8_SLOT
    lo_block = jnp.logical_or(jnp.logical_and(depth >= MLA_NOPE, depth < 2 * MLA_NOPE),
                              depth >= 3 * MLA_NOPE + MLA_ROPE)
    ka8_ref[0] = jnp.where(lo_block, k8 - k8.astype(F8).astype(F32), k8).astype(F8)
    _store_vT(vaT_ref, _dot(lhs[:, :MLA_KV_RANK], wv_ref[...]).T, MLA_HEADS, MLA_V)

    bd = bd_ref[...]
    g_q = z[:, Z_GQ:Z_GQS]
    n_q = lax.rsqrt(_group_ssq(g_q, bd) * (1.0 / GQA_DIM) + EPS)
    q_b = (g_q * tcq_ref[...] + z[:, Z_GQS:Z_GK] * tsq_ref[...]) * n_q
    q_bT = q_b.T
    zero = jnp.zeros((HEAD_SLOT - GQA_DIM, tok), BF16)
    for hd in range(GQA_HEADS):
        qbT_ref[0, hd * HEAD_SLOT:hd * HEAD_SLOT + GQA_DIM] = q_bT[hd * GQA_DIM:(hd + 1) * GQA_DIM].astype(BF16)
        qbT_ref[0, hd * HEAD_SLOT + GQA_DIM:(hd + 1) * HEAD_SLOT] = zero
    q_hi, q_lo = _split_e4m3(q_bT * Q8_SCALE)
    zero8 = jnp.zeros((Q8_SLOT - 3 * GQA_DIM, tok), F8)
    for hd in range(GQA_HEADS):
        rows = slice(hd * GQA_DIM, (hd + 1) * GQA_DIM)
        base = hd * Q8_SLOT
        qb8T_ref[0, base:base + GQA_DIM] = q_hi[rows]
        qb8T_ref[0, base + GQA_DIM:base + 2 * GQA_DIM] = q_hi[rows]
        qb8T_ref[0, base + 2 * GQA_DIM:base + 3 * GQA_DIM] = q_lo[rows]
        qb8T_ref[0, base + 3 * GQA_DIM:base + Q8_SLOT] = zero8
    g_k = z[:, Z_GK:Z_GKS]
    n_k = lax.rsqrt(_group_ssq(g_k, bd[:LANES, :LANES]) * (1.0 / GQA_DIM) + EPS)
    k_b = (g_k * tck_ref[...] + z[:, Z_GKS:Z_GV] * tsk_ref[...]) * n_k
    lane = lax.broadcasted_iota(jnp.int32, (tok, LANES), 1)
    bias_lane = (lane == GQA_BIAS_LANE).astype(F32)
    low = lane < GQA_DIM
    k8 = k_b * K8_SCALE
    k_lo = k8 - k8.astype(F8).astype(F32)
    hi_r = pltpu.roll(k8, GQA_DIM, 1)
    lo_r = pltpu.roll(k_lo, GQA_DIM, 1)
    kb8_ref[0] = jnp.concatenate([jnp.where(low, k8, lo_r), jnp.where(low, k8, 0.0),
                                  jnp.where(low, hi_r, k_lo), jnp.where(low, hi_r, 0.0)], axis=1).astype(F8)
    k_slots = [jnp.where(low, k_b, bias_lane), jnp.where(low, pltpu.roll(k_b, GQA_DIM, 1), bias_lane)]
    k_b = jnp.concatenate(k_slots, axis=1).astype(BF16)
    kb_ref[0] = k_b
    _store_key_norms(knb_ref, k_b, GQA_KV_HEADS, GQA_BIAS_LANE)
    _store_vT(vbT_ref, z[:, Z_GV:Z_U].T, GQA_KV_HEADS, GQA_DIM)

    gm = jax.nn.gelu(z[:, Z_U:Z_KR])
    u = gm[:, :W_C]
    vv = _rms(gm[:, W_C:], gv_ref[...])
    lane_grp = lax.broadcasted_iota(jnp.int32, (GMLP_CHUNK, W_C), 1) // GMLP_DIM
    ws = ws_ref[...]
    bias = bias_ref[...]
    ycs = []
    for n in range(tok // GMLP_CHUNK):
        rows = slice(n * GMLP_CHUNK, (n + 1) * GMLP_CHUNK)
        r = _dot(ws, vv[rows].astype(BF16))
        mixed = r[3 * GMLP_CHUNK:]
        for grp in range(GMLP_GROUPS - 2, -1, -1):
            mixed = jnp.where(lane_grp == grp, r[grp * GMLP_CHUNK:(grp + 1) * GMLP_CHUNK], mixed)
        ycs.append(u[rows] * (mixed + bias))
    ync_ref[0] = _rms(jnp.concatenate(ycs, axis=0), gc_ref[...]).astype(BF16)


def _mix_in(x, p):
    B, S, D = x.shape
    T = TOK_TILE
    nkv = T // KV_TILE
    tile = lambda w: pl.BlockSpec((1, T, w), lambda s, b: (b, s, 0))
    tab = lambda w: pl.BlockSpec((T, w), lambda s, b: (s, 0))
    tposed = lambda r: pl.BlockSpec((1, r, T), lambda s, b: (b, 0, s))
    blocked = lambda r: pl.BlockSpec((1, nkv, r, KV_TILE), lambda s, b: (b, s, 0, 0))
    consts = [p["mix_norm"], p["w_in"], p["mla_q_norm"], p["w_uq"], p["mla_kv_norm"], p["w_k"], p["w_k8"], p["w_v"]]
    tabs = [p["cqa"], p["sqa"], p["tk"], p["tcq"], p["tsq"], p["tck"], p["tsk"]]
    consts2 = [p["bd"], p["gmlp_v_norm"], p["w_s"], p["bias_s"], p["out_norm_c"]]
    in_specs = ([tile(D)] + [_const_spec(c.shape) for c in consts] + [tab(t.shape[1]) for t in tabs]
                + [_const_spec(c.shape) for c in consts2])
    out_shape = [
        jax.ShapeDtypeStruct((B, MLA_HEADS * HEAD_SLOT, S), BF16),
        jax.ShapeDtypeStruct((B, S, MLA_HEADS * HEAD_SLOT), BF16),
        jax.ShapeDtypeStruct((B, S // KV_TILE, MLA_HEADS * (MLA_V + V_PAD), KV_TILE), BF16),
        jax.ShapeDtypeStruct((B, S // T, F32_ROWS, LANES), F32),
        jax.ShapeDtypeStruct((B, MLA_HEADS * Q8_SLOT, S), F8),
        jax.ShapeDtypeStruct((B, S, MLA_HEADS * Q8_SLOT), F8),
        jax.ShapeDtypeStruct((B, GQA_HEADS * HEAD_SLOT, S), BF16),
        jax.ShapeDtypeStruct((B, S, GQA_KV_HEADS * HEAD_SLOT), BF16),
        jax.ShapeDtypeStruct((B, S // KV_TILE, GQA_KV_HEADS * (GQA_DIM + V_PAD), KV_TILE), BF16),
        jax.ShapeDtypeStruct((B, S // T, F32_ROWS, LANES), F32),
        jax.ShapeDtypeStruct((B, GQA_HEADS * Q8_SLOT, S), F8),
        jax.ShapeDtypeStruct((B, S, GQA_KV_HEADS * Q8_SLOT), F8),
        jax.ShapeDtypeStruct((B, S, W_C), BF16),
    ]
    norms = pl.BlockSpec((1, 1, F32_ROWS, LANES), lambda s, b: (b, s, 0, 0))
    out_specs = [tposed(MLA_HEADS * HEAD_SLOT), tile(MLA_HEADS * HEAD_SLOT), blocked(MLA_HEADS * (MLA_V + V_PAD)),
                 norms, tposed(MLA_HEADS * Q8_SLOT), tile(MLA_HEADS * Q8_SLOT),
                 tposed(GQA_HEADS * HEAD_SLOT), tile(GQA_KV_HEADS * HEAD_SLOT),
                 blocked(GQA_KV_HEADS * (GQA_DIM + V_PAD)), norms, tposed(GQA_HEADS * Q8_SLOT),
                 tile(GQA_KV_HEADS * Q8_SLOT), tile(W_C)]
    return pl.pallas_call(
        _mix_in_kernel, grid=(S // T, B), in_specs=in_specs, out_specs=out_specs, out_shape=out_shape,
        compiler_params=_params(2), name="mix_in",
    )(x, *consts, *tabs, *consts2)


def _attn_kernel(q_ref, k_ref, v_ref, kn_ref, g_ref, *rest, n_heads, k_lanes, bias_lane, v_rows, dv, fp8):
    if fp8:
        q8_ref, k8_ref, o_ref, s_ref, p_ref, m_ref, acc_ref, oT_ref, qx_ref = rest
    else:
        o_ref, s_ref, p_ref, m_ref, acc_ref, oT_ref, qx_ref = rest
    nkv = v_ref.shape[1]
    mq = q_ref.shape[2]
    assert n_heads % 2 == 0

    kmax = jnp.max(kn_ref[0], axis=0)
    first_row = lax.broadcasted_iota(jnp.int32, (BF16_ROWS, mq), 0) == 0
    bound = jnp.zeros((1, 1), F32)
    qn2_max = jnp.zeros((1, 1), F32)
    kn2_max = jnp.zeros((1, 1), F32)
    for hd in range(n_heads):
        rows = slice(hd * HEAD_SLOT, (hd + 1) * HEAD_SLOT)
        q = q_ref[0, rows, :]
        qf = q.astype(F32)
        qn2 = jnp.sum(qf * qf, axis=0, keepdims=True)
        kn2 = kmax[k_lanes[hd] // HEAD_SLOT:k_lanes[hd] // HEAD_SLOT + 1, 0:1]
        u = jnp.sqrt(qn2 * kn2)
        bound = jnp.maximum(bound, jnp.max(u, axis=1, keepdims=True))
        if fp8:
            qn2_max = jnp.maximum(qn2_max, jnp.max(qn2, axis=1, keepdims=True))
            kn2_max = jnp.maximum(kn2_max, kn2)
            rows8 = slice(hd * Q8_SLOT, (hd + 1) * Q8_SLOT)
            qx_ref[rows8, :] = q8_ref[0, rows8, :]
            m_ref[hd] = u
        else:
            qx_ref[rows, :] = q
            bias_rows = slice(hd * HEAD_SLOT + bias_lane, hd * HEAD_SLOT + bias_lane + BF16_ROWS)
            qx_ref[bias_rows, :] = jnp.where(first_row, -u, 0.0).astype(BF16)
    stabilise = jnp.max(bound) <= STABILISER_MAX
    if fp8:
        stabilise = jnp.logical_and(stabilise, jnp.logical_and(
            jnp.max(qn2_max) <= (FP8_MAX_SCALED / Q8_SCALE) ** 2, jnp.max(kn2_max) <= (FP8_MAX_SCALED / K8_SCALE) ** 2))
    acc_ref[...] = jnp.zeros(acc_ref.shape, F32)

    def run(stabilised):
        def scores(hd, off):
            rows = slice(hd * HEAD_SLOT, (hd + 1) * HEAD_SLOT)
            if stabilised and fp8:
                slot = k_lanes[hd] // HEAD_SLOT
                k = k8_ref[0, pl.ds(off, KV_TILE), slot * Q8_SLOT:(slot + 1) * Q8_SLOT]
                qT = qx_ref[hd * Q8_SLOT:(hd + 1) * Q8_SLOT, :]
            else:
                k = k_ref[0, pl.ds(off, KV_TILE), k_lanes[hd]:k_lanes[hd] + HEAD_SLOT]
                qT = qx_ref[rows, :] if stabilised else q_ref[0, rows, :]
            s_ref[hd % 2] = _dot(k, qT)

        scores(0, 0)

        def step(j, carry):
            off = pl.multiple_of(j * KV_TILE, KV_TILE)
            off_next = pl.multiple_of(jnp.minimum(j + 1, nkv - 1) * KV_TILE, KV_TILE)
            for hd in range(n_heads):
                if hd + 1 < n_heads:
                    scores(hd + 1, off)
                else:
                    scores(0, off_next)
                vT = v_ref[0, j, v_rows[hd]:v_rows[hd] + dv + V_PAD, :]
                if stabilised:
                    s = s_ref[hd % 2]
                    if fp8:
                        s = s * (1.0 / (Q8_SCALE * K8_SCALE)) - m_ref[hd]
                    p_ref[hd] = jnp.exp2(s.astype(BF16))
                    acc_ref[hd] += _dot(vT, p_ref[hd])
                else:
                    m_old = m_ref[hd]
                    m_new = jnp.maximum(m_old, jnp.max(s_ref[hd % 2], axis=0, keepdims=True))
                    m_ref[hd] = m_new
                    p_ref[hd] = jnp.exp2(s_ref[hd % 2] - m_new).astype(BF16)
                    acc_ref[hd] = jnp.exp2(m_old - m_new) * acc_ref[hd] + _dot(vT, p_ref[hd])
            return carry

        lax.fori_loop(0, nkv, step, 0, unroll=8 if stabilised else 1)

    def running_max():
        m_ref[...] = jnp.full(m_ref.shape, -1e30, F32)
        run(False)

    lax.cond(stabilise, lambda: run(True), running_max)
    for hd in range(n_heads):
        oT_ref[hd * dv:(hd + 1) * dv, :] = acc_ref[hd, :dv] / acc_ref[hd, dv:dv + 1]
    o_ref[0] = _rms(oT_ref[...].T, g_ref[...]).astype(BF16)


def _attention(qT, k, vT, kn, gain, fp8_operands=(), *, n_heads, k_lanes, bias_lane, v_rows, dv, name):
    B, _, S = qT.shape
    fp8 = bool(fp8_operands)
    kernel = functools.partial(_attn_kernel, n_heads=n_heads, k_lanes=k_lanes, bias_lane=bias_lane,
                               v_rows=v_rows, dv=dv, fp8=fp8)
    q_slot, q_dtype = (Q8_SLOT, F8) if fp8 else (HEAD_SLOT, BF16)
    q_tile = lambda slot: pl.BlockSpec((1, n_heads * slot, Q_TILE), lambda b, i: (b, 0, i))
    whole = lambda shape: pl.BlockSpec((1,) + shape[1:], lambda b, i: (b,) + (0,) * (len(shape) - 1),
                                       pipeline_mode=pl.Buffered(1))
    in_specs = [q_tile(HEAD_SLOT), whole(k.shape), whole(vT.shape), whole(kn.shape), _const_spec(gain.shape)]
    if fp8:
        in_specs += [q_tile(Q8_SLOT), whole(fp8_operands[1].shape)]
    return pl.pallas_call(
        kernel, grid=(B, S // Q_TILE), in_specs=in_specs,
        out_specs=pl.BlockSpec((1, Q_TILE, n_heads * dv), lambda b, i: (b, i, 0)),
        out_shape=jax.ShapeDtypeStruct((B, S, n_heads * dv), BF16),
        scratch_shapes=[pltpu.VMEM((2, KV_TILE, Q_TILE), F32), pltpu.VMEM((n_heads, KV_TILE, Q_TILE), BF16),
                        pltpu.VMEM((n_heads, 1, Q_TILE), F32), pltpu.VMEM((n_heads, dv + V_PAD, Q_TILE), F32),
                        pltpu.VMEM((n_heads * dv, Q_TILE), F32), pltpu.VMEM((n_heads * q_slot, Q_TILE), q_dtype)],
        compiler_params=_params(2), name=name,
    )(qT, k, vT, kn, gain, *fp8_operands)


def _mem_kv_kernel(mem_ref, g_ref, w_ref, kT_ref, v_ref):
    kv = _dot(_rms(mem_ref[0], g_ref[...]).astype(BF16), w_ref[...])
    width = MEM_HEADS * MEM_DIM
    kT_ref[0] = kv[:, :width].T.astype(BF16)
    ones = jnp.ones((kv.shape[0], MEM_DIM), BF16)
    for hd in range(MEM_HEADS):
        v_ref[0, hd, :, :MEM_DIM] = kv[:, width + hd * MEM_DIM:width + (hd + 1) * MEM_DIM].astype(BF16)
        v_ref[0, hd, :, MEM_DIM:] = ones


def _mem_kv(mem, gain, w_kv):
    B, Tm, D = mem.shape
    width = MEM_HEADS * MEM_DIM
    return pl.pallas_call(
        _mem_kv_kernel, grid=(B,),
        in_specs=[pl.BlockSpec((1, Tm, D), lambda b: (b, 0, 0)), _const_spec(gain.shape), _const_spec(w_kv.shape)],
        out_specs=[pl.BlockSpec((1, width, Tm), lambda b: (b, 0, 0)),
                   pl.BlockSpec((1, MEM_HEADS, Tm, 2 * MEM_DIM), lambda b: (b, 0, 0, 0))],
        out_shape=[jax.ShapeDtypeStruct((B, width, Tm), BF16),
                   jax.ShapeDtypeStruct((B, MEM_HEADS, Tm, 2 * MEM_DIM), BF16)],
        compiler_params=_params(1), name="mem_kv",
    )(mem, gain, w_kv)


def _out_mem_kernel(ya_ref, yb_ref, yc_ref, x_ref, wout_ref, g_ref, wq_ref, kT_ref, v_ref, wo_ref, o_ref):
    y = jnp.concatenate([ya_ref[0], yb_ref[0], yc_ref[0]], axis=1)
    x1 = x_ref[0] + _dot(y, wout_ref[...])
    h = _rms(x1, g_ref[...]).astype(BF16)
    q = (_dot(h, wq_ref[...]) * (MEM_DIM ** -0.5)).astype(BF16)
    heads = []
    for hd in range(MEM_HEADS):
        s = _dot(q[:, hd * MEM_DIM:(hd + 1) * MEM_DIM], kT_ref[0, hd * MEM_DIM:(hd + 1) * MEM_DIM, :])
        p = jnp.exp(s - jnp.max(s, axis=-1, keepdims=True)).astype(BF16)
        pv = _dot(p, v_ref[0, hd])
        heads.append(pv[:, :MEM_DIM] / pv[:, MEM_DIM:])
    o = jnp.concatenate(heads, axis=1).astype(BF16)
    o_ref[0] = x1 + _dot(o, wo_ref[...])


def _out_mem(ya, yb, yc, x, p, mem_kT, mem_v):
    B, S, D = x.shape
    T = TOK_TILE
    tile = lambda w: pl.BlockSpec((1, T, w), lambda b, s: (b, s, 0))
    per_b = lambda a: pl.BlockSpec((1,) + a.shape[1:], lambda b, s: (b,) + (0,) * (a.ndim - 1))
    consts = [p["w_out"], p["mem_x_norm"], p["mem_w_q"]]
    return pl.pallas_call(
        _out_mem_kernel, grid=(B, S // T),
        in_specs=[tile(W_A), tile(W_B), tile(W_C), tile(D)] + [_const_spec(c.shape) for c in consts]
                 + [per_b(mem_kT), per_b(mem_v), _const_spec(p["mem_w_o"].shape)],
        out_specs=tile(D), out_shape=jax.ShapeDtypeStruct((B, S, D), F32),
        compiler_params=_params(2), name="out_mem",
    )(ya, yb, yc, x, *consts, mem_kT, mem_v, p["mem_w_o"])


def _ffn_kernel(x_ref, xp_ref, xn_ref, g_ref, wup_ref, cw_ref, cb_ref, wdn_ref, fg_ref, o_ref,
                h_ref, act_ref, *, final):
    tok = x_ref.shape[1]
    i = pl.program_id(1)
    g = g_ref[...]
    x = x_ref[0]
    keep_prev = (i > 0).astype(F32)
    keep_next = (i < pl.num_programs(1) - 1).astype(F32)
    h_ref[0:HALO] = (_rms(xp_ref[0], g) * keep_prev).astype(BF16)
    h_ref[HALO:HALO + tok] = _rms(x, g).astype(BF16)
    h_ref[HALO + tok:] = (_rms(xn_ref[0], g) * keep_next).astype(BF16)
    hext = h_ref[...]
    rows = tok + 2 * HALO

    def conv_up(cols):
        a = _dot(hext, wup_ref[:, cols])
        w = cw_ref[:, cols]
        return (pltpu.roll(a, 1, 0)[HALO:HALO + tok] * w[0:1] + a[HALO:HALO + tok] * w[1:2]
                + pltpu.roll(a, rows - 1, 0)[HALO:HALO + tok] * w[2:3] + cb_ref[:, cols])

    for c in range(D_FF // FF_CHUNK):
        gate = conv_up(slice(FF_CHUNK * c, FF_CHUNK * (c + 1)))
        val = conv_up(slice(D_FF + FF_CHUNK * c, D_FF + FF_CHUNK * (c + 1)))
        act_ref[:, FF_CHUNK * c:FF_CHUNK * (c + 1)] = (jax.nn.silu(gate) * val).astype(BF16)
    y = x + _dot(act_ref[...], wdn_ref[...])
    if final:
        y = _rms(y, fg_ref[...])
    o_ref[0] = y


def _ffn(x, p, final_gain, *, final):
    B, S, D = x.shape
    T = TOK_TILE
    per_tile = T // HALO
    n_halo = S // HALO
    tile = pl.BlockSpec((1, T, D), lambda b, s: (b, s, 0))
    prev = pl.BlockSpec((1, HALO, D), lambda b, s: (b, jnp.maximum(s * per_tile - 1, 0), 0))
    nxt = pl.BlockSpec((1, HALO, D), lambda b, s: (b, jnp.minimum((s + 1) * per_tile, n_halo - 1), 0))
    consts = [p["ffn_norm"], p["ffn_w_up"], p["ffn_conv_w"], p["ffn_conv_b"], p["ffn_w_down"], final_gain]
    return pl.pallas_call(
        functools.partial(_ffn_kernel, final=final), grid=(B, S // T),
        in_specs=[tile, prev, nxt] + [_const_spec(c.shape) for c in consts],
        out_specs=tile, out_shape=jax.ShapeDtypeStruct((B, S, D), F32),
        scratch_shapes=[pltpu.VMEM((T + 2 * HALO, D), BF16), pltpu.VMEM((T, D_FF), BF16)],
        compiler_params=_params(2), name="ffn_final" if final else "ffn",
    )(x, x, x, *consts)


def _swap_pairs(w):
    n = w.shape[-1]
    return w.reshape(*w.shape[:-1], n // 2, 2)[..., ::-1].reshape(w.shape)


def _rope_tables(S, d_rot):
    rows = S // GRID_W
    row = jnp.repeat(jnp.arange(rows, dtype=F32), GRID_W)
    col = jnp.tile(jnp.arange(GRID_W, dtype=F32), rows)
    n = d_rot // 4
    inv = ROPE_THETA ** (-jnp.arange(n, dtype=F32) / n)
    ang = jnp.concatenate([row[:, None] * inv, col[:, None] * inv], axis=-1)
    cos, sin = jnp.cos(ang), jnp.sin(ang)
    c = jnp.repeat(cos, 2, axis=-1)
    s = jnp.stack([-sin, sin], axis=-1).reshape(S, d_rot)
    return c, s


def _prep_params(S, P):
    L = P["w_in"].shape[0]
    row = lambda v: v[:, None, :]
    ca, sa = _rope_tables(S, MLA_ROPE)
    cb, sb = _rope_tables(S, GQA_DIM)
    p, shared = {}, {}
    c_q, c_kv, k_rope, g_q, g_k, g_v, g_m = jnp.split(P["w_in"], [256, 384, 416, 800, 928, 1056], axis=2)
    pad = jnp.zeros((L, D_MODEL, Z_END - Z_KR - 2 * MLA_ROPE), F32)
    p["w_in"] = jnp.concatenate(
        [c_q, c_kv, g_q, _swap_pairs(g_q), g_k, _swap_pairs(g_k), g_v, g_m, k_rope, _swap_pairs(k_rope), pad],
        axis=2).astype(BF16)
    for name in ("mix_norm", "mla_q_norm", "mla_kv_norm", "gmlp_v_norm", "mem_x_norm", "mem_kv_norm", "ffn_norm",
                 "ffn_conv_b"):
        p[name] = row(P[name])

    w_uq = P["mla_w_uq"].reshape(L, MLA_Q_RANK, MLA_HEADS, MLA_NOPE + MLA_ROPE)
    zpad = jnp.zeros((L, MLA_Q_RANK, MLA_HEADS, HEAD_SLOT - MLA_NOPE - MLA_ROPE), F32)
    main = jnp.concatenate([w_uq, zpad], axis=-1)
    swapped = jnp.concatenate([jnp.zeros_like(w_uq[..., :MLA_NOPE]), _swap_pairs(w_uq[..., MLA_NOPE:]), zpad], axis=-1)
    p["w_uq"] = jnp.concatenate([main.reshape(L, MLA_Q_RANK, -1), swapped.reshape(L, MLA_Q_RANK, -1)],
                                axis=2).astype(BF16)
    scale_a = (MLA_NOPE + MLA_ROPE) ** -0.5 * LOG2E
    ones = jnp.ones((S, MLA_NOPE), F32)
    zeros_n = jnp.zeros((S, MLA_NOPE), F32)
    zeros_p = jnp.zeros((S, HEAD_SLOT - MLA_NOPE - MLA_ROPE), F32)
    shared["cqa"] = jnp.tile(jnp.concatenate([ones, ca, zeros_p], axis=1) * scale_a, (1, MLA_HEADS))
    shared["sqa"] = jnp.tile(jnp.concatenate([zeros_n, sa, zeros_p], axis=1) * scale_a, (1, MLA_HEADS))

    w_ukv = P["mla_w_ukv"].reshape(L, MLA_KV_RANK, MLA_HEADS, MLA_NOPE + MLA_V)
    k_lat = jnp.concatenate(
        [w_ukv[..., :MLA_NOPE], jnp.zeros((L, MLA_KV_RANK, MLA_HEADS, HEAD_SLOT - MLA_NOPE), F32)], axis=-1)
    place = jnp.concatenate([jnp.zeros((MLA_ROPE, MLA_NOPE), F32), jnp.eye(MLA_ROPE, dtype=F32),
                             jnp.zeros((MLA_ROPE, HEAD_SLOT - MLA_NOPE - MLA_ROPE), F32)], axis=1)
    place = jnp.broadcast_to(jnp.tile(place, (1, MLA_HEADS)), (L, MLA_ROPE, MLA_HEADS * HEAD_SLOT))
    zrows = jnp.zeros((L, Z_END - Z_KR - 2 * MLA_ROPE, MLA_HEADS * HEAD_SLOT), F32)
    p["w_k"] = jnp.concatenate([k_lat.reshape(L, MLA_KV_RANK, -1), place, place, zrows], axis=1).astype(BF16)
    p["w_v"] = w_ukv[..., MLA_NOPE:].reshape(L, MLA_KV_RANK, -1).astype(BF16)
    nope_w = w_ukv[..., :MLA_NOPE]
    lat8 = jnp.concatenate([nope_w, nope_w, nope_w, jnp.zeros((L, MLA_KV_RANK, MLA_HEADS, 2 * MLA_ROPE), F32)], axis=-1)
    place8 = jnp.concatenate([jnp.zeros((MLA_ROPE, 3 * MLA_NOPE), F32), jnp.eye(MLA_ROPE, dtype=F32),
                              jnp.eye(MLA_ROPE, dtype=F32)], axis=1)
    place8 = jnp.broadcast_to(jnp.tile(place8, (1, MLA_HEADS)), (L, MLA_ROPE, MLA_HEADS * Q8_SLOT))
    zrows8 = jnp.zeros((L, Z_END - Z_KR - 2 * MLA_ROPE, MLA_HEADS * Q8_SLOT), F32)
    p["w_k8"] = (jnp.concatenate([lat8.reshape(L, MLA_KV_RANK, -1), place8, place8, zrows8], axis=1)
                 * K8_SCALE).astype(BF16)
    shared["tk"] = jnp.concatenate([ca, sa, jnp.zeros((S, Z_END - Z_KR - 2 * MLA_ROPE), F32)], axis=1)

    gq = row(jnp.tile(P["gqa_q_norm"], (1, GQA_HEADS)) * (GQA_DIM ** -0.5 * LOG2E))
    gk = row(jnp.tile(P["gqa_k_norm"], (1, GQA_KV_HEADS)))
    p["tcq"] = jnp.tile(cb, (1, GQA_HEADS)) * gq
    p["tsq"] = jnp.tile(sb, (1, GQA_HEADS)) * _swap_pairs(gq)
    p["tck"] = jnp.tile(cb, (1, GQA_KV_HEADS)) * gk
    p["tsk"] = jnp.tile(sb, (1, GQA_KV_HEADS)) * _swap_pairs(gk)
    grp = jnp.arange(W_B) // GQA_DIM
    shared["bd"] = (grp[:, None] == grp[None, :]).astype(BF16)

    p["w_s"] = P["gmlp_w_s"].reshape(L, GMLP_GROUPS * GMLP_CHUNK, GMLP_CHUNK).astype(BF16)
    p["bias_s"] = jnp.repeat(jnp.swapaxes(P["gmlp_b_s"], 1, 2), GMLP_DIM, axis=2)
    p["out_norm_a"] = row(P["out_norm"][:, :W_A])
    p["out_norm_b"] = row(P["out_norm"][:, W_A:W_A + W_B])
    p["out_norm_c"] = row(P["out_norm"][:, W_A + W_B:])
    for name in ("w_out", "mem_w_q", "mem_w_kv", "mem_w_o", "ffn_w_up", "ffn_w_down"):
        p[name] = P[name].astype(BF16)
    p["ffn_conv_w"] = P["ffn_conv_w"]
    return p, shared


def kernel(x, mem, mix_norm, w_in, mla_q_norm, mla_w_uq, mla_kv_norm, mla_w_ukv, gqa_q_norm, gqa_k_norm, gmlp_v_norm, gmlp_w_s, gmlp_b_s, out_norm, w_out, mem_x_norm, mem_kv_norm, mem_w_q, mem_w_kv, mem_w_o, ffn_norm, ffn_w_up, ffn_conv_w, ffn_conv_b, ffn_w_down, final_norm):
    P = dict(mix_norm=mix_norm, w_in=w_in, mla_q_norm=mla_q_norm, mla_w_uq=mla_w_uq, mla_kv_norm=mla_kv_norm,
             mla_w_ukv=mla_w_ukv, gqa_q_norm=gqa_q_norm, gqa_k_norm=gqa_k_norm, gmlp_v_norm=gmlp_v_norm,
             gmlp_w_s=gmlp_w_s, gmlp_b_s=gmlp_b_s, out_norm=out_norm, w_out=w_out, mem_x_norm=mem_x_norm,
             mem_kv_norm=mem_kv_norm, mem_w_q=mem_w_q, mem_w_kv=mem_w_kv, mem_w_o=mem_w_o, ffn_norm=ffn_norm,
             ffn_w_up=ffn_w_up, ffn_conv_w=ffn_conv_w, ffn_conv_b=ffn_conv_b, ffn_w_down=ffn_w_down)
    B, S, D = x.shape
    assert D == D_MODEL and S % TOK_TILE == 0 and S % GRID_W == 0
    depth = w_in.shape[0]
    stacked, shared = _prep_params(S, P)
    final_gain = final_norm.reshape(1, -1)
    group = GQA_HEADS // GQA_KV_HEADS
    for l in range(depth):
        p = dict(shared, **{name: v[l] for name, v in stacked.items()})
        qaT, ka, vaT, kna, qa8T, ka8, qbT, kb, vbT, knb, qb8T, kb8, ync = _mix_in(x, p)
        yna = _attention(qaT, ka, vaT, kna, p["out_norm_a"], (qa8T, ka8), n_heads=MLA_HEADS,
                         k_lanes=tuple(h * HEAD_SLOT for h in range(MLA_HEADS)), bias_lane=MLA_BIAS_LANE,
                         v_rows=tuple(h * (MLA_V + V_PAD) for h in range(MLA_HEADS)), dv=MLA_V, name="attn_mla")
        ynb = _attention(qbT, kb, vbT, knb, p["out_norm_b"], (qb8T, kb8), n_heads=GQA_HEADS,
                         k_lanes=tuple((h // group) * HEAD_SLOT for h in range(GQA_HEADS)), bias_lane=GQA_BIAS_LANE,
                         v_rows=tuple((h // group) * (GQA_DIM + V_PAD) for h in range(GQA_HEADS)), dv=GQA_DIM, name="attn_gqa")
        mem_kT, mem_v = _mem_kv(mem, p["mem_kv_norm"], p["mem_w_kv"])
        x = _out_mem(yna, ynb, ync, x, p, mem_kT, mem_v)
        x = _ffn(x, p, final_gain, final=(l == depth - 1))
    return x
```

```python
import functools
from typing import NamedTuple

import jax
import jax.numpy as jnp
from jax import lax
from jax.experimental import pallas as pl
from jax.experimental.pallas import tpu as pltpu

F32 = jnp.float32
BF16 = jnp.bfloat16
F8 = jnp.float8_e4m3fn

D_MODEL = 1024
GRID_W = 64
ROPE_THETA = 10000.0
EPS = 1e-6
MLA_HEADS = 6
MLA_NOPE = 64
MLA_ROPE = 32
MLA_V = 64
MLA_Q_RANK = 256
MLA_KV_RANK = 128
GQA_HEADS = 6
GQA_KV_HEADS = 2
GQA_DIM = 64
GMLP_GROUPS = 4
GMLP_DIM = 64
GMLP_CHUNK = 128
W_A = MLA_HEADS * MLA_V
W_B = GQA_HEADS * GQA_DIM
W_C = GMLP_GROUPS * GMLP_DIM
MEM_HEADS = 4
MEM_DIM = 128
D_FF = 2816

LANES = 128
HEAD_SLOT = LANES
TOK_TILE = 512
FFN_TILE = 1024
Q_TILE = 512
KV_TILE = 512
F32_ROWS = 8
BF16_ROWS = 16
V_PAD = BF16_ROWS
LOG2E = 1.4426950408889634
MLA_BIAS_LANE = MLA_NOPE + MLA_ROPE
GQA_BIAS_LANE = GQA_DIM
STABILISER_MAX = 50.0
Q8_SLOT = 256
Q8_SCALE = 32.0
K8_SCALE = 8.0
FP8_MAX_SCALED = 400.0
FF_CHUNK = 256
HALO = BF16_ROWS
VMEM_LIMIT = 56 * 1024 * 1024

Z_CQ, Z_CKV, Z_GQ, Z_GQS, Z_GK, Z_GKS, Z_GV, Z_U, Z_VV, Z_KR, Z_END = (
    0, 256, 384, 768, 1152, 1280, 1408, 1536, 1792, 2048, 2176)


def _rms(x, g):
    return x * lax.rsqrt(jnp.mean(x * x, axis=-1, keepdims=True) + EPS) * g


def _dot(a, b):
    return jnp.dot(a, b, preferred_element_type=F32)


class _Layer(NamedTuple):
    stack: jax.Array
    layer: int


def _operand(a):
    return a.stack if isinstance(a, _Layer) else a


def _const_spec(a):
    if isinstance(a, _Layer):
        zeros = (0,) * (a.stack.ndim - 1)
        return pl.BlockSpec((None,) + a.stack.shape[1:], lambda *_: (a.layer,) + zeros, pipeline_mode=pl.Buffered(1))
    zeros = (0,) * a.ndim
    return pl.BlockSpec(a.shape, lambda *_: zeros, pipeline_mode=pl.Buffered(1))


def _params(n_axes):
    return pltpu.CompilerParams(dimension_semantics=("arbitrary",) * n_axes,
                                vmem_limit_bytes=VMEM_LIMIT)


def _group_ssq(v, bd):
    sq = v * v
    hi = sq.astype(BF16)
    lo = (sq - hi.astype(F32)).astype(BF16)
    return _dot(hi, bd) + _dot(lo, bd)


def _split_e4m3(x):
    hi = x.astype(F8)
    return hi, (x - hi.astype(F32)).astype(F8)


def _store_vT(ref, vT, n_heads, dv):
    ext = (lax.broadcasted_iota(jnp.int32, (V_PAD, KV_TILE), 0) == 0).astype(BF16)
    for n in range(vT.shape[1] // KV_TILE):
        for hd in range(n_heads):
            base = hd * (dv + V_PAD)
            ref[0, n, base:base + dv] = vT[hd * dv:(hd + 1) * dv, n * KV_TILE:(n + 1) * KV_TILE].astype(BF16)
            ref[0, n, base + dv:base + dv + V_PAD] = ext


def _store_key_norms(ref, k, n_slots, n_feat):
    lane = lax.broadcasted_iota(jnp.int32, (k.shape[0], HEAD_SLOT), 1)
    rows = []
    for n in range(n_slots):
        x = jnp.where(lane < n_feat, k[:, n * HEAD_SLOT:(n + 1) * HEAD_SLOT].astype(F32), 0.0)
        n2 = jnp.max(jnp.sum(x * x, axis=-1, keepdims=True), axis=0, keepdims=True)
        rows.append(jnp.broadcast_to(n2, (1, LANES)))
    rows.append(jnp.zeros((F32_ROWS - n_slots, LANES), F32))
    ref[0, 0] = jnp.concatenate(rows, axis=0)


def _mix_in_kernel(x_ref, g_ref, win_ref, gq_ref, wq_ref, gkv_ref, wk_ref, wk8_ref, wv_ref,
                   cqa_ref, sqa_ref, tk_ref, tcq_ref, tsq_ref, tck_ref, tsk_ref,
                   bd_ref, gv_ref, ws_ref, bias_ref, gc_ref,
                   qaT_ref, ka_ref, vaT_ref, kna_ref, qa8T_ref, ka8_ref,
                   qbT_ref, kb_ref, vbT_ref, knb_ref, qb8T_ref, kb8_ref, ync_ref):
    tok = x_ref.shape[1]
    h = _rms(x_ref[0], g_ref[...])
    z = _dot(h.astype(BF16), win_ref[...])

    cq = _rms(z[:, Z_CQ:Z_CKV], gq_ref[...]).astype(BF16)
    qa = _dot(cq, wq_ref[...])
    half = MLA_HEADS * HEAD_SLOT
    q_a = qa[:, :half] * cqa_ref[...] + qa[:, half:] * sqa_ref[...]
    q_aT = q_a.T
    qaT_ref[0] = q_aT.astype(BF16)
    q_hi, q_lo = _split_e4m3(q_aT * Q8_SCALE)
    for hd in range(MLA_HEADS):
        nope = slice(hd * HEAD_SLOT, hd * HEAD_SLOT + MLA_NOPE)
        rope = slice(hd * HEAD_SLOT + MLA_NOPE, hd * HEAD_SLOT + MLA_NOPE + MLA_ROPE)
        base = hd * Q8_SLOT
        qa8T_ref[0, base:base + MLA_NOPE] = q_hi[nope]
        qa8T_ref[0, base + MLA_NOPE:base + 2 * MLA_NOPE] = q_hi[nope]
        qa8T_ref[0, base + 2 * MLA_NOPE:base + 3 * MLA_NOPE] = q_lo[nope]
        qa8T_ref[0, base + 3 * MLA_NOPE:base + 3 * MLA_NOPE + MLA_ROPE] = q_hi[rope]
        qa8T_ref[0, base + 3 * MLA_NOPE + MLA_ROPE:base + Q8_SLOT] = q_hi[rope]

    ckv = _rms(z[:, Z_CKV:Z_GQ], gkv_ref[...])
    kr = z[:, Z_KR:Z_END] * tk_ref[...]
    lhs = jnp.concatenate([ckv, kr], axis=1).astype(BF16)
    slot_lane = lax.broadcasted_iota(jnp.int32, (1, MLA_HEADS * HEAD_SLOT), 1) % HEAD_SLOT
    k_a = (_dot(lhs, wk_ref[...]) + (slot_lane == MLA_BIAS_LANE).astype(F32)).astype(BF16)
    ka_ref[0] = k_a
    _store_key_norms(kna_ref, k_a, MLA_HEADS, MLA_BIAS_LANE)
    k8 = _dot(lhs, wk8_ref[...])
    depth = lax.broadcasted_iota(jnp.int32, (1, MLA_HEADS * Q8_SLOT), 1) % Q8_SLOT
    lo_block = jnp.logical_or(jnp.logical_and(depth >= MLA_NOPE, depth < 2 * MLA_NOPE),
                              depth >= 3 * MLA_NOPE + MLA_ROPE)
    ka8_ref[0] = jnp.where(lo_block, k8 - k8.astype(F8).astype(F32), k8).astype(F8)
    _store_vT(vaT_ref, _dot(lhs[:, :MLA_KV_RANK], wv_ref[...]).T, MLA_HEADS, MLA_V)

    bd = bd_ref[...]
    g_q = z[:, Z_GQ:Z_GQS]
    n_q = lax.rsqrt(_group_ssq(g_q, bd) * (1.0 / GQA_DIM) + EPS)
    q_b = (g_q * tcq_ref[...] + z[:, Z_GQS:Z_GK] * tsq_ref[...]) * n_q
    q_bT = q_b.T
    zero = jnp.zeros((HEAD_SLOT - GQA_DIM, tok), BF16)
    for hd in range(GQA_HEADS):
        qbT_ref[0, hd * HEAD_SLOT:hd * HEAD_SLOT + GQA_DIM] = q_bT[hd * GQA_DIM:(hd + 1) * GQA_DIM].astype(BF16)
        qbT_ref[0, hd * HEAD_SLOT + GQA_DIM:(hd + 1) * HEAD_SLOT] = zero
    q_hi, q_lo = _split_e4m3(q_bT * Q8_SCALE)
    zero8 = jnp.zeros((Q8_SLOT - 3 * GQA_DIM, tok), F8)
    for hd in range(GQA_HEADS):
        rows = slice(hd * GQA_DIM, (hd + 1) * GQA_DIM)
        base = hd * Q8_SLOT
        qb8T_ref[0, base:base + GQA_DIM] = q_hi[rows]
        qb8T_ref[0, base + GQA_DIM:base + 2 * GQA_DIM] = q_hi[rows]
        qb8T_ref[0, base + 2 * GQA_DIM:base + 3 * GQA_DIM] = q_lo[rows]
        qb8T_ref[0, base + 3 * GQA_DIM:base + Q8_SLOT] = zero8
    g_k = z[:, Z_GK:Z_GKS]
    n_k = lax.rsqrt(_group_ssq(g_k, bd[:LANES, :LANES]) * (1.0 / GQA_DIM) + EPS)
    k_b = (g_k * tck_ref[...] + z[:, Z_GKS:Z_GV] * tsk_ref[...]) * n_k
    lane = lax.broadcasted_iota(jnp.int32, (tok, LANES), 1)
    bias_lane = (lane == GQA_BIAS_LANE).astype(F32)
    low = lane < GQA_DIM
    k8 = k_b * K8_SCALE
    k_lo = k8 - k8.astype(F8).astype(F32)
    hi_r = pltpu.roll(k8, GQA_DIM, 1)
    lo_r = pltpu.roll(k_lo, GQA_DIM, 1)
    kb8_ref[0] = jnp.concatenate([jnp.where(low, k8, lo_r), jnp.where(low, k8, 0.0),
                                  jnp.where(low, hi_r, k_lo), jnp.where(low, hi_r, 0.0)], axis=1).astype(F8)
    k_slots = [jnp.where(low, k_b, bias_lane), jnp.where(low, pltpu.roll(k_b, GQA_DIM, 1), bias_lane)]
    k_b = jnp.concatenate(k_slots, axis=1).astype(BF16)
    kb_ref[0] = k_b
    _store_key_norms(knb_ref, k_b, GQA_KV_HEADS, GQA_BIAS_LANE)
    _store_vT(vbT_ref, z[:, Z_GV:Z_U].T, GQA_KV_HEADS, GQA_DIM)

    gm = jax.nn.gelu(z[:, Z_U:Z_KR])
    u = gm[:, :W_C]
    vv = _rms(gm[:, W_C:], gv_ref[...])
    lane_grp = lax.broadcasted_iota(jnp.int32, (GMLP_CHUNK, W_C), 1) // GMLP_DIM
    ws = ws_ref[...]
    bias = bias_ref[...]
    ycs = []
    for n in range(tok // GMLP_CHUNK):
        rows = slice(n * GMLP_CHUNK, (n + 1) * GMLP_CHUNK)
        r = _dot(ws, vv[rows].astype(BF16))
        mixed = r[3 * GMLP_CHUNK:]
        for grp in range(GMLP_GROUPS - 2, -1, -1):
            mixed = jnp.where(lane_grp == grp, r[grp * GMLP_CHUNK:(grp + 1) * GMLP_CHUNK], mixed)
        ycs.append(u[rows] * (mixed + bias))
    ync_ref[0] = _rms(jnp.concatenate(ycs, axis=0), gc_ref[...]).astype(BF16)


def _mix_in(x, p):
    B, S, D = x.shape
    T = TOK_TILE
    nkv = T // KV_TILE
    tile = lambda w: pl.BlockSpec((1, T, w), lambda s, b: (b, s, 0))
    def tab(t):
        if isinstance(t, _Layer):
            return pl.BlockSpec((None, T, t.stack.shape[2]), lambda s, b: (t.layer, s, 0))
        return pl.BlockSpec((T, t.shape[1]), lambda s, b: (s, 0))
    tposed = lambda r: pl.BlockSpec((1, r, T), lambda s, b: (b, 0, s))
    blocked = lambda r: pl.BlockSpec((1, nkv, r, KV_TILE), lambda s, b: (b, s, 0, 0))
    consts = [p["mix_norm"], p["w_in"], p["mla_q_norm"], p["w_uq"], p["mla_kv_norm"], p["w_k"], p["w_k8"], p["w_v"]]
    tabs = [p["cqa"], p["sqa"], p["tk"], p["tcq"], p["tsq"], p["tck"], p["tsk"]]
    consts2 = [p["bd"], p["gmlp_v_norm"], p["w_s"], p["bias_s"], p["out_norm_c"]]
    in_specs = ([tile(D)] + [_const_spec(c) for c in consts] + [tab(t) for t in tabs]
                + [_const_spec(c) for c in consts2])
    out_shape = [
        jax.ShapeDtypeStruct((B, MLA_HEADS * HEAD_SLOT, S), BF16),
        jax.ShapeDtypeStruct((B, S, MLA_HEADS * HEAD_SLOT), BF16),
        jax.ShapeDtypeStruct((B, S // KV_TILE, MLA_HEADS * (MLA_V + V_PAD), KV_TILE), BF16),
        jax.ShapeDtypeStruct((B, S // T, F32_ROWS, LANES), F32),
        jax.ShapeDtypeStruct((B, MLA_HEADS * Q8_SLOT, S), F8),
        jax.ShapeDtypeStruct((B, S, MLA_HEADS * Q8_SLOT), F8),
        jax.ShapeDtypeStruct((B, GQA_HEADS * HEAD_SLOT, S), BF16),
        jax.ShapeDtypeStruct((B, S, GQA_KV_HEADS * HEAD_SLOT), BF16),
        jax.ShapeDtypeStruct((B, S // KV_TILE, GQA_KV_HEADS * (GQA_DIM + V_PAD), KV_TILE), BF16),
        jax.ShapeDtypeStruct((B, S // T, F32_ROWS, LANES), F32),
        jax.ShapeDtypeStruct((B, GQA_HEADS * Q8_SLOT, S), F8),
        jax.ShapeDtypeStruct((B, S, GQA_KV_HEADS * Q8_SLOT), F8),
        jax.ShapeDtypeStruct((B, S, W_C), BF16),
    ]
    norms = pl.BlockSpec((1, 1, F32_ROWS, LANES), lambda s, b: (b, s, 0, 0))
    out_specs = [tposed(MLA_HEADS * HEAD_SLOT), tile(MLA_HEADS * HEAD_SLOT), blocked(MLA_HEADS * (MLA_V + V_PAD)),
                 norms, tposed(MLA_HEADS * Q8_SLOT), tile(MLA_HEADS * Q8_SLOT),
                 tposed(GQA_HEADS * HEAD_SLOT), tile(GQA_KV_HEADS * HEAD_SLOT),
                 blocked(GQA_KV_HEADS * (GQA_DIM + V_PAD)), norms, tposed(GQA_HEADS * Q8_SLOT),
                 tile(GQA_KV_HEADS * Q8_SLOT), tile(W_C)]
    return pl.pallas_call(
        _mix_in_kernel, grid=(S // T, B), in_specs=in_specs, out_specs=out_specs, out_shape=out_shape,
        compiler_params=_params(2), name="mix_in",
    )(x, *map(_operand, consts + tabs + consts2))


def _attn_kernel(q_ref, k_ref, v_ref, kn_ref, g_ref, *rest, n_heads, k_lanes, bias_lane, v_rows, dv, fp8):
    if fp8:
        q8_ref, k8_ref, o_ref, s_ref, p_ref, m_ref, acc_ref, oT_ref, qx_ref = rest
    else:
        o_ref, s_ref, p_ref, m_ref, acc_ref, oT_ref, qx_ref = rest
    nkv = v_ref.shape[1]
    mq = q_ref.shape[2]
    assert n_heads % 2 == 0

    kmax = jnp.max(kn_ref[0], axis=0)
    first_row = lax.broadcasted_iota(jnp.int32, (BF16_ROWS, mq), 0) == 0
    bound = jnp.zeros((1, 1), F32)
    qn2_max = jnp.zeros((1, 1), F32)
    kn2_max = jnp.zeros((1, 1), F32)
    for hd in range(n_heads):
        rows = slice(hd * HEAD_SLOT, (hd + 1) * HEAD_SLOT)
        q = q_ref[0, rows, :]
        qf = q.astype(F32)
        qn2 = jnp.sum(qf * qf, axis=0, keepdims=True)
        kn2 = kmax[k_lanes[hd] // HEAD_SLOT:k_lanes[hd] // HEAD_SLOT + 1, 0:1]
        u = jnp.sqrt(qn2 * kn2)
        bound = jnp.maximum(bound, jnp.max(u, axis=1, keepdims=True))
        if fp8:
            qn2_max = jnp.maximum(qn2_max, jnp.max(qn2, axis=1, keepdims=True))
            kn2_max = jnp.maximum(kn2_max, kn2)
            rows8 = slice(hd * Q8_SLOT, (hd + 1) * Q8_SLOT)
            qx_ref[rows8, :] = q8_ref[0, rows8, :]
            m_ref[hd] = u
        else:
            qx_ref[rows, :] = q
            bias_rows = slice(hd * HEAD_SLOT + bias_lane, hd * HEAD_SLOT + bias_lane + BF16_ROWS)
            qx_ref[bias_rows, :] = jnp.where(first_row, -u, 0.0).astype(BF16)
    stabilise = jnp.max(bound) <= STABILISER_MAX
    if fp8:
        stabilise = jnp.logical_and(stabilise, jnp.logical_and(
            jnp.max(qn2_max) <= (FP8_MAX_SCALED / Q8_SCALE) ** 2, jnp.max(kn2_max) <= (FP8_MAX_SCALED / K8_SCALE) ** 2))
    acc_ref[...] = jnp.zeros(acc_ref.shape, F32)

    def run(stabilised):
        def scores(hd, off):
            rows = slice(hd * HEAD_SLOT, (hd + 1) * HEAD_SLOT)
            if stabilised and fp8:
                slot = k_lanes[hd] // HEAD_SLOT
                k = k8_ref[0, pl.ds(off, KV_TILE), slot * Q8_SLOT:(slot + 1) * Q8_SLOT]
                qT = qx_ref[hd * Q8_SLOT:(hd + 1) * Q8_SLOT, :]
            else:
                k = k_ref[0, pl.ds(off, KV_TILE), k_lanes[hd]:k_lanes[hd] + HEAD_SLOT]
                qT = qx_ref[rows, :] if stabilised else q_ref[0, rows, :]
            s_ref[hd % 2] = _dot(k, qT)

        scores(0, 0)

        def step(j, carry):
            off = pl.multiple_of(j * KV_TILE, KV_TILE)
            off_next = pl.multiple_of(jnp.minimum(j + 1, nkv - 1) * KV_TILE, KV_TILE)
            for hd in range(n_heads):
                if hd + 1 < n_heads:
                    scores(hd + 1, off)
                else:
                    scores(0, off_next)
                vT = v_ref[0, j, v_rows[hd]:v_rows[hd] + dv + V_PAD, :]
                if stabilised:
                    s = s_ref[hd % 2]
                    if fp8:
                        s = s * (1.0 / (Q8_SCALE * K8_SCALE)) - m_ref[hd]
                    p_ref[hd] = jnp.exp2(s.astype(BF16))
                    acc_ref[hd] += _dot(vT, p_ref[hd])
                else:
                    m_old = m_ref[hd]
                    m_new = jnp.maximum(m_old, jnp.max(s_ref[hd % 2], axis=0, keepdims=True))
                    m_ref[hd] = m_new
                    p_ref[hd] = jnp.exp2(s_ref[hd % 2] - m_new).astype(BF16)
                    acc_ref[hd] = jnp.exp2(m_old - m_new) * acc_ref[hd] + _dot(vT, p_ref[hd])
            return carry

        lax.fori_loop(0, nkv, step, 0, unroll=8 if stabilised else 1)

    def running_max():
        m_ref[...] = jnp.full(m_ref.shape, -1e30, F32)
        run(False)

    lax.cond(stabilise, lambda: run(True), running_max)
    for hd in range(n_heads):
        oT_ref[hd * dv:(hd + 1) * dv, :] = acc_ref[hd, :dv] / acc_ref[hd, dv:dv + 1]
    o_ref[0] = _rms(oT_ref[...].T, g_ref[...]).astype(BF16)


def _attention(qT, k, vT, kn, gain, fp8_operands=(), *, n_heads, k_lanes, bias_lane, v_rows, dv, name):
    B, _, S = qT.shape
    fp8 = bool(fp8_operands)
    kernel = functools.partial(_attn_kernel, n_heads=n_heads, k_lanes=k_lanes, bias_lane=bias_lane,
                               v_rows=v_rows, dv=dv, fp8=fp8)
    q_slot, q_dtype = (Q8_SLOT, F8) if fp8 else (HEAD_SLOT, BF16)
    q_tile = lambda slot: pl.BlockSpec((1, n_heads * slot, Q_TILE), lambda b, i: (b, 0, i))
    whole = lambda shape: pl.BlockSpec((1,) + shape[1:], lambda b, i: (b,) + (0,) * (len(shape) - 1),
                                       pipeline_mode=pl.Buffered(1))
    in_specs = [q_tile(HEAD_SLOT), whole(k.shape), whole(vT.shape), whole(kn.shape), _const_spec(gain)]
    if fp8:
        in_specs += [q_tile(Q8_SLOT), whole(fp8_operands[1].shape)]
    return pl.pallas_call(
        kernel, grid=(B, S // Q_TILE), in_specs=in_specs,
        out_specs=pl.BlockSpec((1, Q_TILE, n_heads * dv), lambda b, i: (b, i, 0)),
        out_shape=jax.ShapeDtypeStruct((B, S, n_heads * dv), BF16),
        scratch_shapes=[pltpu.VMEM((2, KV_TILE, Q_TILE), F32), pltpu.VMEM((n_heads, KV_TILE, Q_TILE), BF16),
                        pltpu.VMEM((n_heads, 1, Q_TILE), F32), pltpu.VMEM((n_heads, dv + V_PAD, Q_TILE), F32),
                        pltpu.VMEM((n_heads * dv, Q_TILE), F32), pltpu.VMEM((n_heads * q_slot, Q_TILE), q_dtype)],
        compiler_params=_params(2), name=name,
    )(qT, k, vT, kn, _operand(gain), *fp8_operands)


def _mem_kv_kernel(mem_ref, g_ref, w_ref, kT_ref, v_ref):
    kv = _dot(_rms(mem_ref[0], g_ref[...]).astype(BF16), w_ref[...])
    width = MEM_HEADS * MEM_DIM
    kT_ref[0] = kv[:, :width].T.astype(BF16)
    ones = jnp.ones((kv.shape[0], MEM_DIM), BF16)
    for hd in range(MEM_HEADS):
        v_ref[0, hd, :, :MEM_DIM] = kv[:, width + hd * MEM_DIM:width + (hd + 1) * MEM_DIM].astype(BF16)
        v_ref[0, hd, :, MEM_DIM:] = ones


def _mem_kv(mem, gain, w_kv):
    B, Tm, D = mem.shape
    width = MEM_HEADS * MEM_DIM
    return pl.pallas_call(
        _mem_kv_kernel, grid=(B,),
        in_specs=[pl.BlockSpec((1, Tm, D), lambda b: (b, 0, 0)), _const_spec(gain), _const_spec(w_kv)],
        out_specs=[pl.BlockSpec((1, width, Tm), lambda b: (b, 0, 0)),
                   pl.BlockSpec((1, MEM_HEADS, Tm, 2 * MEM_DIM), lambda b: (b, 0, 0, 0))],
        out_shape=[jax.ShapeDtypeStruct((B, width, Tm), BF16),
                   jax.ShapeDtypeStruct((B, MEM_HEADS, Tm, 2 * MEM_DIM), BF16)],
        compiler_params=_params(1), name="mem_kv",
    )(mem, _operand(gain), _operand(w_kv))


def _out_mem_kernel(ya_ref, yb_ref, yc_ref, x_ref, wout_ref, g_ref, wq_ref, kT_ref, v_ref, wo_ref, o_ref):
    y = jnp.concatenate([ya_ref[0], yb_ref[0], yc_ref[0]], axis=1)
    x1 = x_ref[0] + _dot(y, wout_ref[...])
    h = _rms(x1, g_ref[...]).astype(BF16)
    q = (_dot(h, wq_ref[...]) * (MEM_DIM ** -0.5)).astype(BF16)
    heads = []
    for hd in range(MEM_HEADS):
        s = _dot(q[:, hd * MEM_DIM:(hd + 1) * MEM_DIM], kT_ref[0, hd * MEM_DIM:(hd + 1) * MEM_DIM, :])
        p = jnp.exp(s - jnp.max(s, axis=-1, keepdims=True)).astype(BF16)
        pv = _dot(p, v_ref[0, hd])
        heads.append(pv[:, :MEM_DIM] / pv[:, MEM_DIM:])
    o = jnp.concatenate(heads, axis=1).astype(BF16)
    o_ref[0] = x1 + _dot(o, wo_ref[...])


def _out_mem(ya, yb, yc, x, p, mem_kT, mem_v):
    B, S, D = x.shape
    T = TOK_TILE
    tile = lambda w: pl.BlockSpec((1, T, w), lambda b, s: (b, s, 0))
    per_b = lambda a: pl.BlockSpec((1,) + a.shape[1:], lambda b, s: (b,) + (0,) * (a.ndim - 1))
    consts = [p["w_out"], p["mem_x_norm"], p["mem_w_q"]]
    return pl.pallas_call(
        _out_mem_kernel, grid=(B, S // T),
        in_specs=[tile(W_A), tile(W_B), tile(W_C), tile(D)] + [_const_spec(c) for c in consts]
                 + [per_b(mem_kT), per_b(mem_v), _const_spec(p["mem_w_o"])],
        out_specs=tile(D), out_shape=jax.ShapeDtypeStruct((B, S, D), F32),
        compiler_params=_params(2), name="out_mem",
    )(ya, yb, yc, x, *map(_operand, consts), mem_kT, mem_v, _operand(p["mem_w_o"]))


def _ffn_kernel(x_ref, xp_ref, xn_ref, g_ref, wup_ref, cw_ref, cb_ref, wdn_ref, fg_ref, o_ref,
                h_ref, act_ref, *, final):
    tok = x_ref.shape[1]
    i = pl.program_id(1)
    g = g_ref[...]
    x = x_ref[0]
    keep_prev = (i > 0).astype(F32)
    keep_next = (i < pl.num_programs(1) - 1).astype(F32)
    h_ref[0:HALO] = (_rms(xp_ref[0], g) * keep_prev).astype(BF16)
    h_ref[HALO:HALO + tok] = _rms(x, g).astype(BF16)
    h_ref[HALO + tok:] = (_rms(xn_ref[0], g) * keep_next).astype(BF16)
    hext = h_ref[...]
    rows = tok + 2 * HALO

    def conv_up(cols):
        a = _dot(hext, wup_ref[:, cols])
        w = cw_ref[:, cols]
        return (pltpu.roll(a, 1, 0)[HALO:HALO + tok] * w[0:1] + a[HALO:HALO + tok] * w[1:2]
                + pltpu.roll(a, rows - 1, 0)[HALO:HALO + tok] * w[2:3] + cb_ref[:, cols])

    for c in range(D_FF // FF_CHUNK):
        gate = conv_up(slice(FF_CHUNK * c, FF_CHUNK * (c + 1)))
        val = conv_up(slice(D_FF + FF_CHUNK * c, D_FF + FF_CHUNK * (c + 1)))
        act_ref[:, FF_CHUNK * c:FF_CHUNK * (c + 1)] = (jax.nn.silu(gate) * val).astype(BF16)
    y = x + _dot(act_ref[...], wdn_ref[...])
    if final:
        y = _rms(y, fg_ref[...])
    o_ref[0] = y


def _ffn(x, p, final_gain, *, final):
    B, S, D = x.shape
    T = FFN_TILE
    per_tile = T // HALO
    n_halo = S // HALO
    tile = pl.BlockSpec((1, T, D), lambda b, s: (b, s, 0))
    prev = pl.BlockSpec((1, HALO, D), lambda b, s: (b, jnp.maximum(s * per_tile - 1, 0), 0))
    nxt = pl.BlockSpec((1, HALO, D), lambda b, s: (b, jnp.minimum((s + 1) * per_tile, n_halo - 1), 0))
    consts = [p["ffn_norm"], p["ffn_w_up"], p["ffn_conv_w"], p["ffn_conv_b"], p["ffn_w_down"], final_gain]
    return pl.pallas_call(
        functools.partial(_ffn_kernel, final=final), grid=(B, S // T),
        in_specs=[tile, prev, nxt] + [_const_spec(c) for c in consts],
        out_specs=tile, out_shape=jax.ShapeDtypeStruct((B, S, D), F32),
        scratch_shapes=[pltpu.VMEM((T + 2 * HALO, D), BF16), pltpu.VMEM((T, D_FF), BF16)],
        compiler_params=_params(2), name="ffn_final" if final else "ffn",
    )(x, x, x, *map(_operand, consts))


def _swap_pairs(w):
    n = w.shape[-1]
    return w.reshape(*w.shape[:-1], n // 2, 2)[..., ::-1].reshape(w.shape)


def _rope_tables(S, d_rot):
    rows = S // GRID_W
    row = jnp.repeat(jnp.arange(rows, dtype=F32), GRID_W)
    col = jnp.tile(jnp.arange(GRID_W, dtype=F32), rows)
    n = d_rot // 4
    inv = ROPE_THETA ** (-jnp.arange(n, dtype=F32) / n)
    ang = jnp.concatenate([row[:, None] * inv, col[:, None] * inv], axis=-1)
    cos, sin = jnp.cos(ang), jnp.sin(ang)
    c = jnp.repeat(cos, 2, axis=-1)
    s = jnp.stack([-sin, sin], axis=-1).reshape(S, d_rot)
    return c, s


def _prep_params(S, P):
    L = P["w_in"].shape[0]
    row = lambda v: v[:, None, :]
    ca, sa = _rope_tables(S, MLA_ROPE)
    cb, sb = _rope_tables(S, GQA_DIM)
    p, shared = {}, {}
    c_q, c_kv, k_rope, g_q, g_k, g_v, g_m = jnp.split(P["w_in"], [256, 384, 416, 800, 928, 1056], axis=2)
    pad = jnp.zeros((L, D_MODEL, Z_END - Z_KR - 2 * MLA_ROPE), F32)
    p["w_in"] = jnp.concatenate(
        [c_q, c_kv, g_q, _swap_pairs(g_q), g_k, _swap_pairs(g_k), g_v, g_m, k_rope, _swap_pairs(k_rope), pad],
        axis=2).astype(BF16)
    for name in ("mix_norm", "mla_q_norm", "mla_kv_norm", "gmlp_v_norm", "mem_x_norm", "mem_kv_norm", "ffn_norm",
                 "ffn_conv_b"):
        p[name] = row(P[name])

    w_uq = P["mla_w_uq"].reshape(L, MLA_Q_RANK, MLA_HEADS, MLA_NOPE + MLA_ROPE)
    zpad = jnp.zeros((L, MLA_Q_RANK, MLA_HEADS, HEAD_SLOT - MLA_NOPE - MLA_ROPE), F32)
    main = jnp.concatenate([w_uq, zpad], axis=-1)
    swapped = jnp.concatenate([jnp.zeros_like(w_uq[..., :MLA_NOPE]), _swap_pairs(w_uq[..., MLA_NOPE:]), zpad], axis=-1)
    p["w_uq"] = jnp.concatenate([main.reshape(L, MLA_Q_RANK, -1), swapped.reshape(L, MLA_Q_RANK, -1)],
                                axis=2).astype(BF16)
    scale_a = (MLA_NOPE + MLA_ROPE) ** -0.5 * LOG2E
    ones = jnp.ones((S, MLA_NOPE), F32)
    zeros_n = jnp.zeros((S, MLA_NOPE), F32)
    zeros_p = jnp.zeros((S, HEAD_SLOT - MLA_NOPE - MLA_ROPE), F32)
    shared["cqa"] = jnp.tile(jnp.concatenate([ones, ca, zeros_p], axis=1) * scale_a, (1, MLA_HEADS))
    shared["sqa"] = jnp.tile(jnp.concatenate([zeros_n, sa, zeros_p], axis=1) * scale_a, (1, MLA_HEADS))

    w_ukv = P["mla_w_ukv"].reshape(L, MLA_KV_RANK, MLA_HEADS, MLA_NOPE + MLA_V)
    k_lat = jnp.concatenate(
        [w_ukv[..., :MLA_NOPE], jnp.zeros((L, MLA_KV_RANK, MLA_HEADS, HEAD_SLOT - MLA_NOPE), F32)], axis=-1)
    place = jnp.concatenate([jnp.zeros((MLA_ROPE, MLA_NOPE), F32), jnp.eye(MLA_ROPE, dtype=F32),
                             jnp.zeros((MLA_ROPE, HEAD_SLOT - MLA_NOPE - MLA_ROPE), F32)], axis=1)
    place = jnp.broadcast_to(jnp.tile(place, (1, MLA_HEADS)), (L, MLA_ROPE, MLA_HEADS * HEAD_SLOT))
    zrows = jnp.zeros((L, Z_END - Z_KR - 2 * MLA_ROPE, MLA_HEADS * HEAD_SLOT), F32)
    p["w_k"] = jnp.concatenate([k_lat.reshape(L, MLA_KV_RANK, -1), place, place, zrows], axis=1).astype(BF16)
    p["w_v"] = w_ukv[..., MLA_NOPE:].reshape(L, MLA_KV_RANK, -1).astype(BF16)
    nope_w = w_ukv[..., :MLA_NOPE]
    lat8 = jnp.concatenate([nope_w, nope_w, nope_w, jnp.zeros((L, MLA_KV_RANK, MLA_HEADS, 2 * MLA_ROPE), F32)], axis=-1)
    place8 = jnp.concatenate([jnp.zeros((MLA_ROPE, 3 * MLA_NOPE), F32), jnp.eye(MLA_ROPE, dtype=F32),
                              jnp.eye(MLA_ROPE, dtype=F32)], axis=1)
    place8 = jnp.broadcast_to(jnp.tile(place8, (1, MLA_HEADS)), (L, MLA_ROPE, MLA_HEADS * Q8_SLOT))
    zrows8 = jnp.zeros((L, Z_END - Z_KR - 2 * MLA_ROPE, MLA_HEADS * Q8_SLOT), F32)
    p["w_k8"] = (jnp.concatenate([lat8.reshape(L, MLA_KV_RANK, -1), place8, place8, zrows8], axis=1)
                 * K8_SCALE).astype(BF16)
    shared["tk"] = jnp.concatenate([ca, sa, jnp.zeros((S, Z_END - Z_KR - 2 * MLA_ROPE), F32)], axis=1)

    gq = row(jnp.tile(P["gqa_q_norm"], (1, GQA_HEADS)) * (GQA_DIM ** -0.5 * LOG2E))
    gk = row(jnp.tile(P["gqa_k_norm"], (1, GQA_KV_HEADS)))
    p["tcq"] = jnp.tile(cb, (1, GQA_HEADS)) * gq
    p["tsq"] = jnp.tile(sb, (1, GQA_HEADS)) * _swap_pairs(gq)
    p["tck"] = jnp.tile(cb, (1, GQA_KV_HEADS)) * gk
    p["tsk"] = jnp.tile(sb, (1, GQA_KV_HEADS)) * _swap_pairs(gk)
    grp = jnp.arange(W_B) // GQA_DIM
    shared["bd"] = (grp[:, None] == grp[None, :]).astype(BF16)

    p["w_s"] = P["gmlp_w_s"].reshape(L, GMLP_GROUPS * GMLP_CHUNK, GMLP_CHUNK).astype(BF16)
    p["bias_s"] = jnp.repeat(jnp.swapaxes(P["gmlp_b_s"], 1, 2), GMLP_DIM, axis=2)
    p["out_norm_a"] = row(P["out_norm"][:, :W_A])
    p["out_norm_b"] = row(P["out_norm"][:, W_A:W_A + W_B])
    p["out_norm_c"] = row(P["out_norm"][:, W_A + W_B:])
    for name in ("w_out", "mem_w_q", "mem_w_kv", "mem_w_o", "ffn_w_up", "ffn_w_down"):
        p[name] = P[name].astype(BF16)
    p["ffn_conv_w"] = P["ffn_conv_w"]
    return p, shared


def kernel(x, mem, mix_norm, w_in, mla_q_norm, mla_w_uq, mla_kv_norm, mla_w_ukv, gqa_q_norm, gqa_k_norm, gmlp_v_norm, gmlp_w_s, gmlp_b_s, out_norm, w_out, mem_x_norm, mem_kv_norm, mem_w_q, mem_w_kv, mem_w_o, ffn_norm, ffn_w_up, ffn_conv_w, ffn_conv_b, ffn_w_down, final_norm):
    P = dict(mix_norm=mix_norm, w_in=w_in, mla_q_norm=mla_q_norm, mla_w_uq=mla_w_uq, mla_kv_norm=mla_kv_norm,
             mla_w_ukv=mla_w_ukv, gqa_q_norm=gqa_q_norm, gqa_k_norm=gqa_k_norm, gmlp_v_norm=gmlp_v_norm,
             gmlp_w_s=gmlp_w_s, gmlp_b_s=gmlp_b_s, out_norm=out_norm, w_out=w_out, mem_x_norm=mem_x_norm,
             mem_kv_norm=mem_kv_norm, mem_w_q=mem_w_q, mem_w_kv=mem_w_kv, mem_w_o=mem_w_o, ffn_norm=ffn_norm,
             ffn_w_up=ffn_w_up, ffn_conv_w=ffn_conv_w, ffn_conv_b=ffn_conv_b, ffn_w_down=ffn_w_down)
    B, S, D = x.shape
    assert D == D_MODEL and S % TOK_TILE == 0 and S % FFN_TILE == 0 and S % GRID_W == 0
    depth = w_in.shape[0]
    stacked, shared = _prep_params(S, P)
    final_gain = final_norm.reshape(1, -1)
    group = GQA_HEADS // GQA_KV_HEADS
    for l in range(depth):
        p = dict(shared, **{name: _Layer(v, l) for name, v in stacked.items()})
        qaT, ka, vaT, kna, qa8T, ka8, qbT, kb, vbT, knb, qb8T, kb8, ync = _mix_in(x, p)
        yna = _attention(qaT, ka, vaT, kna, p["out_norm_a"], (qa8T, ka8), n_heads=MLA_HEADS,
                         k_lanes=tuple(h * HEAD_SLOT for h in range(MLA_HEADS)), bias_lane=MLA_BIAS_LANE,
                         v_rows=tuple(h * (MLA_V + V_PAD) for h in range(MLA_HEADS)), dv=MLA_V, name="attn_mla")
        ynb = _attention(qbT, kb, vbT, knb, p["out_norm_b"], (qb8T, kb8), n_heads=GQA_HEADS,
                         k_lanes=tuple((h // group) * HEAD_SLOT for h in range(GQA_HEADS)), bias_lane=GQA_BIAS_LANE,
                         v_rows=tuple((h // group) * (GQA_DIM + V_PAD) for h in range(GQA_HEADS)), dv=GQA_DIM, name="attn_gqa")
        mem_kT, mem_v = _mem_kv(mem, p["mem_kv_norm"], p["mem_w_kv"])
        x = _out_mem(yna, ynb, ync, x, p, mem_kT, mem_v)
        x = _ffn(x, p, final_gain, final=(l == depth - 1))
    return x
```

```python
import functools
from typing import NamedTuple

import jax
import jax.numpy as jnp
from jax import lax
from jax.experimental import pallas as pl
from jax.experimental.pallas import tpu as pltpu

F32 = jnp.float32
BF16 = jnp.bfloat16
F8 = jnp.float8_e4m3fn

D_MODEL = 1024
GRID_W = 64
ROPE_THETA = 10000.0
EPS = 1e-6
MLA_HEADS = 6
MLA_NOPE = 64
MLA_ROPE = 32
MLA_V = 64
MLA_Q_RANK = 256
MLA_KV_RANK = 128
GQA_HEADS = 6
GQA_KV_HEADS = 2
GQA_DIM = 64
GMLP_GROUPS = 4
GMLP_DIM = 64
GMLP_CHUNK = 128
W_A = MLA_HEADS * MLA_V
W_B = GQA_HEADS * GQA_DIM
W_C = GMLP_GROUPS * GMLP_DIM
MEM_HEADS = 4
MEM_DIM = 128
D_FF = 2816

LANES = 128
HEAD_SLOT = LANES
TOK_TILE = 512
FFN_TILE = 1024
Q_TILE = 512
KV_TILE = 512
F32_ROWS = 8
BF16_ROWS = 16
V_PAD = BF16_ROWS
LOG2E = 1.4426950408889634
STABILISER_MAX = 50.0
Q8_SLOT = 256
Q8_SCALE = 32.0
K8_SCALE = 8.0
FP8_MAX_SCALED = 400.0
FF_CHUNK = 256
HALO = BF16_ROWS
VMEM_LIMIT = 56 * 1024 * 1024

Z_CQ, Z_CKV, Z_GQ, Z_GQS, Z_GK, Z_GKS, Z_GV, Z_U, Z_VV, Z_KR, Z_END = (
    0, 256, 384, 768, 1152, 1280, 1408, 1536, 1792, 2048, 2176)


def _rms(x, g):
    return x * lax.rsqrt(jnp.mean(x * x, axis=-1, keepdims=True) + EPS) * g


def _dot(a, b):
    return jnp.dot(a, b, preferred_element_type=F32)


class _Layer(NamedTuple):
    stack: jax.Array
    layer: int


def _operand(a):
    return a.stack if isinstance(a, _Layer) else a


def _const_spec(a):
    if isinstance(a, _Layer):
        zeros = (0,) * (a.stack.ndim - 1)
        return pl.BlockSpec((None,) + a.stack.shape[1:], lambda *_: (a.layer,) + zeros, pipeline_mode=pl.Buffered(1))
    zeros = (0,) * a.ndim
    return pl.BlockSpec(a.shape, lambda *_: zeros, pipeline_mode=pl.Buffered(1))


def _params(n_axes):
    return pltpu.CompilerParams(dimension_semantics=("arbitrary",) * n_axes,
                                vmem_limit_bytes=VMEM_LIMIT)


def _group_ssq(v, bd):
    sq = v * v
    hi = sq.astype(BF16)
    lo = (sq - hi.astype(F32)).astype(BF16)
    return _dot(hi, bd) + _dot(lo, bd)


def _split_e4m3(x):
    hi = x.astype(F8)
    return hi, (x - hi.astype(F32)).astype(F8)


def _store_vT(ref, vT, n_heads, dv):
    ext = (lax.broadcasted_iota(jnp.int32, (V_PAD, KV_TILE), 0) == 0).astype(BF16)
    for n in range(vT.shape[1] // KV_TILE):
        for hd in range(n_heads):
            base = hd * (dv + V_PAD)
            ref[0, n, base:base + dv] = vT[hd * dv:(hd + 1) * dv, n * KV_TILE:(n + 1) * KV_TILE].astype(BF16)
            ref[0, n, base + dv:base + dv + V_PAD] = ext


def _store_key_norms(ref, k, n_slots):
    rows = []
    for n in range(n_slots):
        x = k[:, n * HEAD_SLOT:(n + 1) * HEAD_SLOT].astype(F32)
        n2 = jnp.max(jnp.sum(x * x, axis=-1, keepdims=True), axis=0, keepdims=True)
        rows.append(jnp.broadcast_to(n2, (1, LANES)))
    rows.append(jnp.zeros((F32_ROWS - n_slots, LANES), F32))
    ref[0, 0] = jnp.concatenate(rows, axis=0)


def _mix_in_kernel(x_ref, g_ref, win_ref, gq_ref, wq_ref, gkv_ref, wk_ref, wk8_ref, wv_ref,
                   cqa_ref, sqa_ref, tk_ref, tcq_ref, tsq_ref, tck_ref, tsk_ref,
                   bd_ref, gv_ref, ws_ref, bias_ref, gc_ref,
                   qaT_ref, ka_ref, vaT_ref, kna_ref, qa8T_ref, ka8_ref,
                   qbT_ref, kb_ref, vbT_ref, knb_ref, qb8T_ref, kb8_ref, ync_ref):
    tok = x_ref.shape[1]
    h = _rms(x_ref[0], g_ref[...])
    z = _dot(h.astype(BF16), win_ref[...])

    cq = _rms(z[:, Z_CQ:Z_CKV], gq_ref[...]).astype(BF16)
    qa = _dot(cq, wq_ref[...])
    half = MLA_HEADS * HEAD_SLOT
    q_a = qa[:, :half] * cqa_ref[...] + qa[:, half:] * sqa_ref[...]
    q_aT = q_a.T
    qaT_ref[0] = q_aT.astype(BF16)
    q_hi, q_lo = _split_e4m3(q_aT * Q8_SCALE)
    for hd in range(MLA_HEADS):
        nope = slice(hd * HEAD_SLOT, hd * HEAD_SLOT + MLA_NOPE)
        rope = slice(hd * HEAD_SLOT + MLA_NOPE, hd * HEAD_SLOT + MLA_NOPE + MLA_ROPE)
        base = hd * Q8_SLOT
        qa8T_ref[0, base:base + MLA_NOPE] = q_hi[nope]
        qa8T_ref[0, base + MLA_NOPE:base + 2 * MLA_NOPE] = q_hi[nope]
        qa8T_ref[0, base + 2 * MLA_NOPE:base + 3 * MLA_NOPE] = q_lo[nope]
        qa8T_ref[0, base + 3 * MLA_NOPE:base + 3 * MLA_NOPE + MLA_ROPE] = q_hi[rope]
        qa8T_ref[0, base + 3 * MLA_NOPE + MLA_ROPE:base + Q8_SLOT] = q_hi[rope]

    ckv = _rms(z[:, Z_CKV:Z_GQ], gkv_ref[...])
    kr = z[:, Z_KR:Z_END] * tk_ref[...]
    lhs = jnp.concatenate([ckv, kr], axis=1).astype(BF16)
    k_a = _dot(lhs, wk_ref[...]).astype(BF16)
    ka_ref[0] = k_a
    _store_key_norms(kna_ref, k_a, MLA_HEADS)
    k8 = _dot(lhs, wk8_ref[...])
    depth = lax.broadcasted_iota(jnp.int32, (1, MLA_HEADS * Q8_SLOT), 1) % Q8_SLOT
    lo_block = jnp.logical_or(jnp.logical_and(depth >= MLA_NOPE, depth < 2 * MLA_NOPE),
                              depth >= 3 * MLA_NOPE + MLA_ROPE)
    ka8_ref[0] = jnp.where(lo_block, k8 - k8.astype(F8).astype(F32), k8).astype(F8)
    _store_vT(vaT_ref, _dot(lhs[:, :MLA_KV_RANK], wv_ref[...]).T, MLA_HEADS, MLA_V)

    bd = bd_ref[...]
    g_q = z[:, Z_GQ:Z_GQS]
    n_q = lax.rsqrt(_group_ssq(g_q, bd) * (1.0 / GQA_DIM) + EPS)
    q_b = (g_q * tcq_ref[...] + z[:, Z_GQS:Z_GK] * tsq_ref[...]) * n_q
    q_bT = q_b.T
    zero = jnp.zeros((HEAD_SLOT - GQA_DIM, tok), BF16)
    for hd in range(GQA_HEADS):
        qbT_ref[0, hd * HEAD_SLOT:hd * HEAD_SLOT + GQA_DIM] = q_bT[hd * GQA_DIM:(hd + 1) * GQA_DIM].astype(BF16)
        qbT_ref[0, hd * HEAD_SLOT + GQA_DIM:(hd + 1) * HEAD_SLOT] = zero
    q_hi, q_lo = _split_e4m3(q_bT * Q8_SCALE)
    zero8 = jnp.zeros((Q8_SLOT - 3 * GQA_DIM, tok), F8)
    for hd in range(GQA_HEADS):
        rows = slice(hd * GQA_DIM, (hd + 1) * GQA_DIM)
        base = hd * Q8_SLOT
        qb8T_ref[0, base:base + GQA_DIM] = q_hi[rows]
        qb8T_ref[0, base + GQA_DIM:base + 2 * GQA_DIM] = q_hi[rows]
        qb8T_ref[0, base + 2 * GQA_DIM:base + 3 * GQA_DIM] = q_lo[rows]
        qb8T_ref[0, base + 3 * GQA_DIM:base + Q8_SLOT] = zero8
    g_k = z[:, Z_GK:Z_GKS]
    n_k = lax.rsqrt(_group_ssq(g_k, bd[:LANES, :LANES]) * (1.0 / GQA_DIM) + EPS)
    k_b = (g_k * tck_ref[...] + z[:, Z_GKS:Z_GV] * tsk_ref[...]) * n_k
    low = lax.broadcasted_iota(jnp.int32, (tok, LANES), 1) < GQA_DIM
    slots = [jnp.where(low, k_b, 0.0), jnp.where(low, pltpu.roll(k_b, GQA_DIM, 1), 0.0)]
    blocks = []
    for k_slot in slots:
        k8 = k_slot * K8_SCALE
        k_lo = k8 - k8.astype(F8).astype(F32)
        blocks += [jnp.where(low, k8, pltpu.roll(k_lo, GQA_DIM, 1)), k8]
    kb8_ref[0] = jnp.concatenate(blocks, axis=1).astype(F8)
    k_b = jnp.concatenate(slots, axis=1).astype(BF16)
    kb_ref[0] = k_b
    _store_key_norms(knb_ref, k_b, GQA_KV_HEADS)
    _store_vT(vbT_ref, z[:, Z_GV:Z_U].T, GQA_KV_HEADS, GQA_DIM)

    gm = jax.nn.gelu(z[:, Z_U:Z_KR])
    u = gm[:, :W_C]
    vv = _rms(gm[:, W_C:], gv_ref[...])
    lane_grp = lax.broadcasted_iota(jnp.int32, (GMLP_CHUNK, W_C), 1) // GMLP_DIM
    ws = ws_ref[...]
    bias = bias_ref[...]
    ycs = []
    for n in range(tok // GMLP_CHUNK):
        rows = slice(n * GMLP_CHUNK, (n + 1) * GMLP_CHUNK)
        r = _dot(ws, vv[rows].astype(BF16))
        mixed = r[3 * GMLP_CHUNK:]
        for grp in range(GMLP_GROUPS - 2, -1, -1):
            mixed = jnp.where(lane_grp == grp, r[grp * GMLP_CHUNK:(grp + 1) * GMLP_CHUNK], mixed)
        ycs.append(u[rows] * (mixed + bias))
    ync_ref[0] = _rms(jnp.concatenate(ycs, axis=0), gc_ref[...]).astype(BF16)


def _mix_in(x, p):
    B, S, D = x.shape
    T = TOK_TILE
    nkv = T // KV_TILE
    tile = lambda w: pl.BlockSpec((1, T, w), lambda s, b: (b, s, 0))
    def tab(t):
        if isinstance(t, _Layer):
            return pl.BlockSpec((None, T, t.stack.shape[2]), lambda s, b: (t.layer, s, 0))
        return pl.BlockSpec((T, t.shape[1]), lambda s, b: (s, 0))
    tposed = lambda r: pl.BlockSpec((1, r, T), lambda s, b: (b, 0, s))
    blocked = lambda r: pl.BlockSpec((1, nkv, r, KV_TILE), lambda s, b: (b, s, 0, 0))
    consts = [p["mix_norm"], p["w_in"], p["mla_q_norm"], p["w_uq"], p["mla_kv_norm"], p["w_k"], p["w_k8"], p["w_v"]]
    tabs = [p["cqa"], p["sqa"], p["tk"], p["tcq"], p["tsq"], p["tck"], p["tsk"]]
    consts2 = [p["bd"], p["gmlp_v_norm"], p["w_s"], p["bias_s"], p["out_norm_c"]]
    in_specs = ([tile(D)] + [_const_spec(c) for c in consts] + [tab(t) for t in tabs]
                + [_const_spec(c) for c in consts2])
    out_shape = [
        jax.ShapeDtypeStruct((B, MLA_HEADS * HEAD_SLOT, S), BF16),
        jax.ShapeDtypeStruct((B, S, MLA_HEADS * HEAD_SLOT), BF16),
        jax.ShapeDtypeStruct((B, S // KV_TILE, MLA_HEADS * (MLA_V + V_PAD), KV_TILE), BF16),
        jax.ShapeDtypeStruct((B, S // T, F32_ROWS, LANES), F32),
        jax.ShapeDtypeStruct((B, MLA_HEADS * Q8_SLOT, S), F8),
        jax.ShapeDtypeStruct((B, S, MLA_HEADS * Q8_SLOT), F8),
        jax.ShapeDtypeStruct((B, GQA_HEADS * HEAD_SLOT, S), BF16),
        jax.ShapeDtypeStruct((B, S, GQA_KV_HEADS * HEAD_SLOT), BF16),
        jax.ShapeDtypeStruct((B, S // KV_TILE, GQA_KV_HEADS * (GQA_DIM + V_PAD), KV_TILE), BF16),
        jax.ShapeDtypeStruct((B, S // T, F32_ROWS, LANES), F32),
        jax.ShapeDtypeStruct((B, GQA_HEADS * Q8_SLOT, S), F8),
        jax.ShapeDtypeStruct((B, S, GQA_KV_HEADS * Q8_SLOT), F8),
        jax.ShapeDtypeStruct((B, S, W_C), BF16),
    ]
    norms = pl.BlockSpec((1, 1, F32_ROWS, LANES), lambda s, b: (b, s, 0, 0))
    out_specs = [tposed(MLA_HEADS * HEAD_SLOT), tile(MLA_HEADS * HEAD_SLOT), blocked(MLA_HEADS * (MLA_V + V_PAD)),
                 norms, tposed(MLA_HEADS * Q8_SLOT), tile(MLA_HEADS * Q8_SLOT),
                 tposed(GQA_HEADS * HEAD_SLOT), tile(GQA_KV_HEADS * HEAD_SLOT),
                 blocked(GQA_KV_HEADS * (GQA_DIM + V_PAD)), norms, tposed(GQA_HEADS * Q8_SLOT),
                 tile(GQA_KV_HEADS * Q8_SLOT), tile(W_C)]
    return pl.pallas_call(
        _mix_in_kernel, grid=(S // T, B), in_specs=in_specs, out_specs=out_specs, out_shape=out_shape,
        compiler_params=_params(2), name="mix_in",
    )(x, *map(_operand, consts + tabs + consts2))


def _attn_kernel(q_ref, k_ref, v_ref, kn_ref, g_ref, q8_ref, k8_ref, o_ref,
                 s_ref, p_ref, m_ref, acc_ref, oT_ref, *, n_heads, k_slots, v_rows, dv):
    nkv = v_ref.shape[1]
    assert n_heads % 2 == 0

    kmax = jnp.max(kn_ref[0], axis=0)
    bound = jnp.zeros((1, 1), F32)
    qn2_max = jnp.zeros((1, 1), F32)
    kn2_max = jnp.zeros((1, 1), F32)
    for hd in range(n_heads):
        qf = q_ref[0, hd * HEAD_SLOT:(hd + 1) * HEAD_SLOT, :].astype(F32)
        qn2 = jnp.sum(qf * qf, axis=0, keepdims=True)
        kn2 = kmax[k_slots[hd]:k_slots[hd] + 1, 0:1]
        u = jnp.sqrt(qn2 * kn2)
        m_ref[hd] = u
        bound = jnp.maximum(bound, jnp.max(u, axis=1, keepdims=True))
        qn2_max = jnp.maximum(qn2_max, jnp.max(qn2, axis=1, keepdims=True))
        kn2_max = jnp.maximum(kn2_max, kn2)
    stabilise = jnp.logical_and(jnp.max(bound) <= STABILISER_MAX, jnp.logical_and(
        jnp.max(qn2_max) <= (FP8_MAX_SCALED / Q8_SCALE) ** 2, jnp.max(kn2_max) <= (FP8_MAX_SCALED / K8_SCALE) ** 2))
    acc_ref[...] = jnp.zeros(acc_ref.shape, F32)

    def run(stabilised):
        def scores(hd, off):
            ks, qs, depth = (k8_ref, q8_ref, Q8_SLOT) if stabilised else (k_ref, q_ref, HEAD_SLOT)
            k = ks[0, pl.ds(off, KV_TILE), k_slots[hd] * depth:(k_slots[hd] + 1) * depth]
            qT = qs[0, hd * depth:(hd + 1) * depth, :]
            s_ref[hd % 2] = _dot(k, qT)

        scores(0, 0)
        last = n_heads - 1

        def pv(hd, j):
            vT = v_ref[0, j, v_rows[hd]:v_rows[hd] + dv + V_PAD, :]
            acc_ref[hd] += _dot(vT, p_ref[hd])

        if stabilised:
            p_ref[last] = jnp.zeros(p_ref.shape[1:], BF16)

        def step(j, carry):
            off = pl.multiple_of(j * KV_TILE, KV_TILE)
            off_next = pl.multiple_of(jnp.minimum(j + 1, nkv - 1) * KV_TILE, KV_TILE)
            for hd in range(n_heads):
                if hd + 1 < n_heads:
                    scores(hd + 1, off)
                else:
                    scores(0, off_next)
                if stabilised:
                    if hd > 0:
                        pv(hd - 1, j)
                    else:
                        pv(last, jnp.maximum(j - 1, 0))
                    s = s_ref[hd % 2] * (1.0 / (Q8_SCALE * K8_SCALE)) - m_ref[hd]
                    p_ref[hd] = jnp.exp2(s.astype(BF16))
                else:
                    vT = v_ref[0, j, v_rows[hd]:v_rows[hd] + dv + V_PAD, :]
                    m_old = m_ref[hd]
                    m_new = jnp.maximum(m_old, jnp.max(s_ref[hd % 2], axis=0, keepdims=True))
                    m_ref[hd] = m_new
                    p_ref[hd] = jnp.exp2(s_ref[hd % 2] - m_new).astype(BF16)
                    acc_ref[hd] = jnp.exp2(m_old - m_new) * acc_ref[hd] + _dot(vT, p_ref[hd])
            return carry

        lax.fori_loop(0, nkv, step, 0, unroll=8 if stabilised else 1)
        if stabilised:
            pv(last, nkv - 1)

    def running_max():
        m_ref[...] = jnp.full(m_ref.shape, -1e30, F32)
        run(False)

    lax.cond(stabilise, lambda: run(True), running_max)
    for hd in range(n_heads):
        oT_ref[hd * dv:(hd + 1) * dv, :] = acc_ref[hd, :dv] / acc_ref[hd, dv:dv + 1]
    o_ref[0] = _rms(oT_ref[...].T, g_ref[...]).astype(BF16)


def _attention(qT, k, vT, kn, gain, q8T, k8, *, n_heads, k_slots, v_rows, dv, name):
    B, _, S = qT.shape
    kernel = functools.partial(_attn_kernel, n_heads=n_heads, k_slots=k_slots, v_rows=v_rows, dv=dv)
    q_tile = lambda slot: pl.BlockSpec((1, n_heads * slot, Q_TILE), lambda b, i: (b, 0, i))
    whole = lambda shape: pl.BlockSpec((1,) + shape[1:], lambda b, i: (b,) + (0,) * (len(shape) - 1),
                                       pipeline_mode=pl.Buffered(1))
    return pl.pallas_call(
        kernel, grid=(B, S // Q_TILE),
        in_specs=[q_tile(HEAD_SLOT), whole(k.shape), whole(vT.shape), whole(kn.shape), _const_spec(gain),
                  q_tile(Q8_SLOT), whole(k8.shape)],
        out_specs=pl.BlockSpec((1, Q_TILE, n_heads * dv), lambda b, i: (b, i, 0)),
        out_shape=jax.ShapeDtypeStruct((B, S, n_heads * dv), BF16),
        scratch_shapes=[pltpu.VMEM((2, KV_TILE, Q_TILE), F32), pltpu.VMEM((n_heads, KV_TILE, Q_TILE), BF16),
                        pltpu.VMEM((n_heads, 1, Q_TILE), F32), pltpu.VMEM((n_heads, dv + V_PAD, Q_TILE), F32),
                        pltpu.VMEM((n_heads * dv, Q_TILE), F32)],
        compiler_params=_params(2), name=name,
    )(qT, k, vT, kn, _operand(gain), q8T, k8)


def _mem_kv_kernel(mem_ref, g_ref, w_ref, kT_ref, v_ref):
    kv = _dot(_rms(mem_ref[0], g_ref[...]).astype(BF16), w_ref[...])
    width = MEM_HEADS * MEM_DIM
    kT_ref[0] = kv[:, :width].T.astype(BF16)
    ones = jnp.ones((kv.shape[0], MEM_DIM), BF16)
    for hd in range(MEM_HEADS):
        v_ref[0, hd, :, :MEM_DIM] = kv[:, width + hd * MEM_DIM:width + (hd + 1) * MEM_DIM].astype(BF16)
        v_ref[0, hd, :, MEM_DIM:] = ones


def _mem_kv(mem, gain, w_kv):
    B, Tm, D = mem.shape
    width = MEM_HEADS * MEM_DIM
    return pl.pallas_call(
        _mem_kv_kernel, grid=(B,),
        in_specs=[pl.BlockSpec((1, Tm, D), lambda b: (b, 0, 0)), _const_spec(gain), _const_spec(w_kv)],
        out_specs=[pl.BlockSpec((1, width, Tm), lambda b: (b, 0, 0)),
                   pl.BlockSpec((1, MEM_HEADS, Tm, 2 * MEM_DIM), lambda b: (b, 0, 0, 0))],
        out_shape=[jax.ShapeDtypeStruct((B, width, Tm), BF16),
                   jax.ShapeDtypeStruct((B, MEM_HEADS, Tm, 2 * MEM_DIM), BF16)],
        compiler_params=_params(1), name="mem_kv",
    )(mem, _operand(gain), _operand(w_kv))


def _out_mem_kernel(ya_ref, yb_ref, yc_ref, x_ref, wout_ref, g_ref, wq_ref, kT_ref, v_ref, wo_ref, o_ref):
    y = jnp.concatenate([ya_ref[0], yb_ref[0], yc_ref[0]], axis=1)
    x1 = x_ref[0] + _dot(y, wout_ref[...])
    h = _rms(x1, g_ref[...]).astype(BF16)
    q = (_dot(h, wq_ref[...]) * (MEM_DIM ** -0.5)).astype(BF16)
    heads = []
    for hd in range(MEM_HEADS):
        s = _dot(q[:, hd * MEM_DIM:(hd + 1) * MEM_DIM], kT_ref[0, hd * MEM_DIM:(hd + 1) * MEM_DIM, :])
        p = jnp.exp(s - jnp.max(s, axis=-1, keepdims=True)).astype(BF16)
        pv = _dot(p, v_ref[0, hd])
        heads.append(pv[:, :MEM_DIM] / pv[:, MEM_DIM:])
    o = jnp.concatenate(heads, axis=1).astype(BF16)
    o_ref[0] = x1 + _dot(o, wo_ref[...])


def _out_mem(ya, yb, yc, x, p, mem_kT, mem_v):
    B, S, D = x.shape
    T = TOK_TILE
    tile = lambda w: pl.BlockSpec((1, T, w), lambda b, s: (b, s, 0))
    per_b = lambda a: pl.BlockSpec((1,) + a.shape[1:], lambda b, s: (b,) + (0,) * (a.ndim - 1))
    consts = [p["w_out"], p["mem_x_norm"], p["mem_w_q"]]
    return pl.pallas_call(
        _out_mem_kernel, grid=(B, S // T),
        in_specs=[tile(W_A), tile(W_B), tile(W_C), tile(D)] + [_const_spec(c) for c in consts]
                 + [per_b(mem_kT), per_b(mem_v), _const_spec(p["mem_w_o"])],
        out_specs=tile(D), out_shape=jax.ShapeDtypeStruct((B, S, D), F32),
        compiler_params=_params(2), name="out_mem",
    )(ya, yb, yc, x, *map(_operand, consts), mem_kT, mem_v, _operand(p["mem_w_o"]))


def _ffn_kernel(x_ref, xp_ref, xn_ref, g_ref, wup_ref, cw_ref, cb_ref, wdn_ref, fg_ref, o_ref,
                h_ref, act_ref, *, final):
    tok = x_ref.shape[1]
    i = pl.program_id(1)
    g = g_ref[...]
    x = x_ref[0]
    keep_prev = (i > 0).astype(F32)
    keep_next = (i < pl.num_programs(1) - 1).astype(F32)
    h_ref[0:HALO] = (_rms(xp_ref[0], g) * keep_prev).astype(BF16)
    h_ref[HALO:HALO + tok] = _rms(x, g).astype(BF16)
    h_ref[HALO + tok:] = (_rms(xn_ref[0], g) * keep_next).astype(BF16)
    hext = h_ref[...]
    rows = tok + 2 * HALO

    def conv_up(cols):
        a = _dot(hext, wup_ref[:, cols])
        w = cw_ref[:, cols]
        return (pltpu.roll(a, 1, 0)[HALO:HALO + tok] * w[0:1] + a[HALO:HALO + tok] * w[1:2]
                + pltpu.roll(a, rows - 1, 0)[HALO:HALO + tok] * w[2:3] + cb_ref[:, cols])

    for c in range(D_FF // FF_CHUNK):
        gate = conv_up(slice(FF_CHUNK * c, FF_CHUNK * (c + 1)))
        val = conv_up(slice(D_FF + FF_CHUNK * c, D_FF + FF_CHUNK * (c + 1)))
        act_ref[:, FF_CHUNK * c:FF_CHUNK * (c + 1)] = (jax.nn.silu(gate) * val).astype(BF16)
    y = x + _dot(act_ref[...], wdn_ref[...])
    if final:
        y = _rms(y, fg_ref[...])
    o_ref[0] = y


def _ffn(x, p, final_gain, *, final):
    B, S, D = x.shape
    T = FFN_TILE
    per_tile = T // HALO
    n_halo = S // HALO
    tile = pl.BlockSpec((1, T, D), lambda b, s: (b, s, 0))
    prev = pl.BlockSpec((1, HALO, D), lambda b, s: (b, jnp.maximum(s * per_tile - 1, 0), 0))
    nxt = pl.BlockSpec((1, HALO, D), lambda b, s: (b, jnp.minimum((s + 1) * per_tile, n_halo - 1), 0))
    consts = [p["ffn_norm"], p["ffn_w_up"], p["ffn_conv_w"], p["ffn_conv_b"], p["ffn_w_down"], final_gain]
    return pl.pallas_call(
        functools.partial(_ffn_kernel, final=final), grid=(B, S // T),
        in_specs=[tile, prev, nxt] + [_const_spec(c) for c in consts],
        out_specs=tile, out_shape=jax.ShapeDtypeStruct((B, S, D), F32),
        scratch_shapes=[pltpu.VMEM((T + 2 * HALO, D), BF16), pltpu.VMEM((T, D_FF), BF16)],
        compiler_params=_params(2), name="ffn_final" if final else "ffn",
    )(x, x, x, *map(_operand, consts))


def _swap_pairs(w):
    n = w.shape[-1]
    return w.reshape(*w.shape[:-1], n // 2, 2)[..., ::-1].reshape(w.shape)


def _rope_tables(S, d_rot):
    rows = S // GRID_W
    row = jnp.repeat(jnp.arange(rows, dtype=F32), GRID_W)
    col = jnp.tile(jnp.arange(GRID_W, dtype=F32), rows)
    n = d_rot // 4
    inv = ROPE_THETA ** (-jnp.arange(n, dtype=F32) / n)
    ang = jnp.concatenate([row[:, None] * inv, col[:, None] * inv], axis=-1)
    cos, sin = jnp.cos(ang), jnp.sin(ang)
    c = jnp.repeat(cos, 2, axis=-1)
    s = jnp.stack([-sin, sin], axis=-1).reshape(S, d_rot)
    return c, s


def _prep_params(S, P):
    L = P["w_in"].shape[0]
    row = lambda v: v[:, None, :]
    ca, sa = _rope_tables(S, MLA_ROPE)
    cb, sb = _rope_tables(S, GQA_DIM)
    p, shared = {}, {}
    c_q, c_kv, k_rope, g_q, g_k, g_v, g_m = jnp.split(P["w_in"], [256, 384, 416, 800, 928, 1056], axis=2)
    pad = jnp.zeros((L, D_MODEL, Z_END - Z_KR - 2 * MLA_ROPE), F32)
    p["w_in"] = jnp.concatenate(
        [c_q, c_kv, g_q, _swap_pairs(g_q), g_k, _swap_pairs(g_k), g_v, g_m, k_rope, _swap_pairs(k_rope), pad],
        axis=2).astype(BF16)
    for name in ("mix_norm", "mla_q_norm", "mla_kv_norm", "gmlp_v_norm", "mem_x_norm", "mem_kv_norm", "ffn_norm",
                 "ffn_conv_b"):
        p[name] = row(P[name])

    w_uq = P["mla_w_uq"].reshape(L, MLA_Q_RANK, MLA_HEADS, MLA_NOPE + MLA_ROPE)
    zpad = jnp.zeros((L, MLA_Q_RANK, MLA_HEADS, HEAD_SLOT - MLA_NOPE - MLA_ROPE), F32)
    main = jnp.concatenate([w_uq, zpad], axis=-1)
    swapped = jnp.concatenate([jnp.zeros_like(w_uq[..., :MLA_NOPE]), _swap_pairs(w_uq[..., MLA_NOPE:]), zpad], axis=-1)
    p["w_uq"] = jnp.concatenate([main.reshape(L, MLA_Q_RANK, -1), swapped.reshape(L, MLA_Q_RANK, -1)],
                                axis=2).astype(BF16)
    scale_a = (MLA_NOPE + MLA_ROPE) ** -0.5 * LOG2E
    ones = jnp.ones((S, MLA_NOPE), F32)
    zeros_n = jnp.zeros((S, MLA_NOPE), F32)
    zeros_p = jnp.zeros((S, HEAD_SLOT - MLA_NOPE - MLA_ROPE), F32)
    shared["cqa"] = jnp.tile(jnp.concatenate([ones, ca, zeros_p], axis=1) * scale_a, (1, MLA_HEADS))
    shared["sqa"] = jnp.tile(jnp.concatenate([zeros_n, sa, zeros_p], axis=1) * scale_a, (1, MLA_HEADS))

    w_ukv = P["mla_w_ukv"].reshape(L, MLA_KV_RANK, MLA_HEADS, MLA_NOPE + MLA_V)
    k_lat = jnp.concatenate(
        [w_ukv[..., :MLA_NOPE], jnp.zeros((L, MLA_KV_RANK, MLA_HEADS, HEAD_SLOT - MLA_NOPE), F32)], axis=-1)
    place = jnp.concatenate([jnp.zeros((MLA_ROPE, MLA_NOPE), F32), jnp.eye(MLA_ROPE, dtype=F32),
                             jnp.zeros((MLA_ROPE, HEAD_SLOT - MLA_NOPE - MLA_ROPE), F32)], axis=1)
    place = jnp.broadcast_to(jnp.tile(place, (1, MLA_HEADS)), (L, MLA_ROPE, MLA_HEADS * HEAD_SLOT))
    zrows = jnp.zeros((L, Z_END - Z_KR - 2 * MLA_ROPE, MLA_HEADS * HEAD_SLOT), F32)
    p["w_k"] = jnp.concatenate([k_lat.reshape(L, MLA_KV_RANK, -1), place, place, zrows], axis=1).astype(BF16)
    p["w_v"] = w_ukv[..., MLA_NOPE:].reshape(L, MLA_KV_RANK, -1).astype(BF16)
    nope_w = w_ukv[..., :MLA_NOPE]
    lat8 = jnp.concatenate([nope_w, nope_w, nope_w, jnp.zeros((L, MLA_KV_RANK, MLA_HEADS, 2 * MLA_ROPE), F32)], axis=-1)
    place8 = jnp.concatenate([jnp.zeros((MLA_ROPE, 3 * MLA_NOPE), F32), jnp.eye(MLA_ROPE, dtype=F32),
                              jnp.eye(MLA_ROPE, dtype=F32)], axis=1)
    place8 = jnp.broadcast_to(jnp.tile(place8, (1, MLA_HEADS)), (L, MLA_ROPE, MLA_HEADS * Q8_SLOT))
    zrows8 = jnp.zeros((L, Z_END - Z_KR - 2 * MLA_ROPE, MLA_HEADS * Q8_SLOT), F32)
    p["w_k8"] = (jnp.concatenate([lat8.reshape(L, MLA_KV_RANK, -1), place8, place8, zrows8], axis=1)
                 * K8_SCALE).astype(BF16)
    shared["tk"] = jnp.concatenate([ca, sa, jnp.zeros((S, Z_END - Z_KR - 2 * MLA_ROPE), F32)], axis=1)

    gq = row(jnp.tile(P["gqa_q_norm"], (1, GQA_HEADS)) * (GQA_DIM ** -0.5 * LOG2E))
    gk = row(jnp.tile(P["gqa_k_norm"], (1, GQA_KV_HEADS)))
    p["tcq"] = jnp.tile(cb, (1, GQA_HEADS)) * gq
    p["tsq"] = jnp.tile(sb, (1, GQA_HEADS)) * _swap_pairs(gq)
    p["tck"] = jnp.tile(cb, (1, GQA_KV_HEADS)) * gk
    p["tsk"] = jnp.tile(sb, (1, GQA_KV_HEADS)) * _swap_pairs(gk)
    grp = jnp.arange(W_B) // GQA_DIM
    shared["bd"] = (grp[:, None] == grp[None, :]).astype(BF16)

    p["w_s"] = P["gmlp_w_s"].reshape(L, GMLP_GROUPS * GMLP_CHUNK, GMLP_CHUNK).astype(BF16)
    p["bias_s"] = jnp.repeat(jnp.swapaxes(P["gmlp_b_s"], 1, 2), GMLP_DIM, axis=2)
    p["out_norm_a"] = row(P["out_norm"][:, :W_A])
    p["out_norm_b"] = row(P["out_norm"][:, W_A:W_A + W_B])
    p["out_norm_c"] = row(P["out_norm"][:, W_A + W_B:])
    for name in ("w_out", "mem_w_q", "mem_w_kv", "mem_w_o", "ffn_w_up", "ffn_w_down"):
        p[name] = P[name].astype(BF16)
    p["ffn_conv_w"] = P["ffn_conv_w"]
    return p, shared


def kernel(x, mem, mix_norm, w_in, mla_q_norm, mla_w_uq, mla_kv_norm, mla_w_ukv, gqa_q_norm, gqa_k_norm, gmlp_v_norm, gmlp_w_s, gmlp_b_s, out_norm, w_out, mem_x_norm, mem_kv_norm, mem_w_q, mem_w_kv, mem_w_o, ffn_norm, ffn_w_up, ffn_conv_w, ffn_conv_b, ffn_w_down, final_norm):
    P = dict(mix_norm=mix_norm, w_in=w_in, mla_q_norm=mla_q_norm, mla_w_uq=mla_w_uq, mla_kv_norm=mla_kv_norm,
             mla_w_ukv=mla_w_ukv, gqa_q_norm=gqa_q_norm, gqa_k_norm=gqa_k_norm, gmlp_v_norm=gmlp_v_norm,
             gmlp_w_s=gmlp_w_s, gmlp_b_s=gmlp_b_s, out_norm=out_norm, w_out=w_out, mem_x_norm=mem_x_norm,
             mem_kv_norm=mem_kv_norm, mem_w_q=mem_w_q, mem_w_kv=mem_w_kv, mem_w_o=mem_w_o, ffn_norm=ffn_norm,
             ffn_w_up=ffn_w_up, ffn_conv_w=ffn_conv_w, ffn_conv_b=ffn_conv_b, ffn_w_down=ffn_w_down)
    B, S, D = x.shape
    assert D == D_MODEL and S % TOK_TILE == 0 and S % FFN_TILE == 0 and S % GRID_W == 0
    depth = w_in.shape[0]
    stacked, shared = _prep_params(S, P)
    final_gain = final_norm.reshape(1, -1)
    group = GQA_HEADS // GQA_KV_HEADS
    for l in range(depth):
        p = dict(shared, **{name: _Layer(v, l) for name, v in stacked.items()})
        qaT, ka, vaT, kna, qa8T, ka8, qbT, kb, vbT, knb, qb8T, kb8, ync = _mix_in(x, p)
        yna = _attention(qaT, ka, vaT, kna, p["out_norm_a"], qa8T, ka8, n_heads=MLA_HEADS,
                         k_slots=tuple(range(MLA_HEADS)),
                         v_rows=tuple(h * (MLA_V + V_PAD) for h in range(MLA_HEADS)), dv=MLA_V, name="attn_mla")
        ynb = _attention(qbT, kb, vbT, knb, p["out_norm_b"], qb8T, kb8, n_heads=GQA_HEADS,
                         k_slots=tuple(h // group for h in range(GQA_HEADS)),
                         v_rows=tuple((h // group) * (GQA_DIM + V_PAD) for h in range(GQA_HEADS)), dv=GQA_DIM, name="attn_gqa")
        mem_kT, mem_v = _mem_kv(mem, p["mem_kv_norm"], p["mem_w_kv"])
        x = _out_mem(yna, ynb, ync, x, p, mem_kT, mem_v)
        x = _ffn(x, p, final_gain, final=(l == depth - 1))
    return x
```

```python
import functools
from typing import NamedTuple

import jax
import jax.numpy as jnp
from jax import lax
from jax.experimental import pallas as pl
from jax.experimental.pallas import tpu as pltpu

F32 = jnp.float32
BF16 = jnp.bfloat16
F8 = jnp.float8_e4m3fn

D_MODEL = 1024
GRID_W = 64
ROPE_THETA = 10000.0
EPS = 1e-6
MLA_HEADS = 6
MLA_NOPE = 64
MLA_ROPE = 32
MLA_V = 64
MLA_Q_RANK = 256
MLA_KV_RANK = 128
GQA_HEADS = 6
GQA_KV_HEADS = 2
GQA_DIM = 64
GMLP_GROUPS = 4
GMLP_DIM = 64
GMLP_CHUNK = 128
W_A = MLA_HEADS * MLA_V
W_B = GQA_HEADS * GQA_DIM
W_C = GMLP_GROUPS * GMLP_DIM
MEM_HEADS = 4
MEM_DIM = 128
D_FF = 2816

LANES = 128
HEAD_SLOT = LANES
TOK_TILE = 512
MIX_ROWS = 256
FFN_TILE = 1024
Q_TILE = 512
KV_TILE = 512
F32_ROWS = 8
BF16_ROWS = 16
V_PAD = BF16_ROWS
LOG2E = 1.4426950408889634
STABILISER_MAX = 50.0
Q8_SLOT = 256
Q8_SCALE = 32.0
K8_SCALE = 8.0
FP8_MAX_SCALED = 400.0
FF_CHUNK = 256
HALO = BF16_ROWS
VMEM_LIMIT = 56 * 1024 * 1024

Z_CQ, Z_CKV, Z_GQ, Z_GQS, Z_GK, Z_GKS, Z_GV, Z_U, Z_VV, Z_KR, Z_END = (
    0, 256, 384, 768, 1152, 1280, 1408, 1536, 1792, 2048, 2176)


def _rms(x, g):
    return x * lax.rsqrt(jnp.mean(x * x, axis=-1, keepdims=True) + EPS) * g


def _dot(a, b):
    return jnp.dot(a, b, preferred_element_type=F32)


class _Layer(NamedTuple):
    stack: jax.Array
    layer: int


def _operand(a):
    return a.stack if isinstance(a, _Layer) else a


def _const_spec(a):
    if isinstance(a, _Layer):
        zeros = (0,) * (a.stack.ndim - 1)
        return pl.BlockSpec((None,) + a.stack.shape[1:], lambda *_: (a.layer,) + zeros, pipeline_mode=pl.Buffered(1))
    zeros = (0,) * a.ndim
    return pl.BlockSpec(a.shape, lambda *_: zeros, pipeline_mode=pl.Buffered(1))


def _params(n_axes):
    return pltpu.CompilerParams(dimension_semantics=("arbitrary",) * n_axes,
                                vmem_limit_bytes=VMEM_LIMIT)


def _group_ssq(v, bd):
    sq = v * v
    hi = sq.astype(BF16)
    lo = (sq - hi.astype(F32)).astype(BF16)
    return _dot(hi, bd) + _dot(lo, bd)


def _split_e4m3(x):
    hi = x.astype(F8)
    return hi, (x - hi.astype(F32)).astype(F8)


def _store_vT(ref, vT, n_heads, dv, tok0):
    width = vT.shape[1]
    assert KV_TILE % width == 0 and tok0 % width == 0
    n, cols = tok0 // KV_TILE, slice(tok0 % KV_TILE, tok0 % KV_TILE + width)
    ext = (lax.broadcasted_iota(jnp.int32, (V_PAD, width), 0) == 0).astype(BF16)
    for hd in range(n_heads):
        base = hd * (dv + V_PAD)
        ref[0, n, base:base + dv, cols] = vT[hd * dv:(hd + 1) * dv].astype(BF16)
        ref[0, n, base + dv:base + dv + V_PAD, cols] = ext


def _key_norms(k, n_slots):
    rows = []
    for n in range(n_slots):
        x = k[:, n * HEAD_SLOT:(n + 1) * HEAD_SLOT].astype(F32)
        n2 = jnp.max(jnp.sum(x * x, axis=-1, keepdims=True), axis=0, keepdims=True)
        rows.append(jnp.broadcast_to(n2, (1, LANES)))
    rows.append(jnp.zeros((F32_ROWS - n_slots, LANES), F32))
    return jnp.concatenate(rows, axis=0)


def _mix_in_kernel(x_ref, g_ref, win_ref, gq_ref, wq_ref, gkv_ref, wk_ref, wk8_ref, wv_ref,
                   cqa_ref, sqa_ref, tk_ref, tcq_ref, tsq_ref, tck_ref, tsk_ref,
                   bd_ref, gv_ref, ws_ref, bias_ref, gc_ref,
                   qaT_ref, ka_ref, vaT_ref, kna_ref, qa8T_ref, ka8_ref,
                   qbT_ref, kb_ref, vbT_ref, knb_ref, qb8T_ref, kb8_ref, ync_ref):
    norms_a, norms_b = [], []
    for r0 in range(0, x_ref.shape[1], MIX_ROWS):
        rows_ = slice(r0, r0 + MIX_ROWS)
        _mix_in_rows(r0, x_ref[0, rows_], g_ref, win_ref, gq_ref, wq_ref, gkv_ref, wk_ref, wk8_ref, wv_ref,
                     [t[rows_] for t in (cqa_ref, sqa_ref, tk_ref, tcq_ref, tsq_ref, tck_ref, tsk_ref)],
                     bd_ref, gv_ref, ws_ref, bias_ref, gc_ref,
                     qaT_ref, ka_ref, vaT_ref, norms_a, qa8T_ref, ka8_ref,
                     qbT_ref, kb_ref, vbT_ref, norms_b, qb8T_ref, kb8_ref, ync_ref)
    kna_ref[0, 0] = functools.reduce(jnp.maximum, norms_a)
    knb_ref[0, 0] = functools.reduce(jnp.maximum, norms_b)


def _mix_in_rows(r0, x, g_ref, win_ref, gq_ref, wq_ref, gkv_ref, wk_ref, wk8_ref, wv_ref, tables,
                 bd_ref, gv_ref, ws_ref, bias_ref, gc_ref,
                 qaT_ref, ka_ref, vaT_ref, norms_a, qa8T_ref, ka8_ref,
                 qbT_ref, kb_ref, vbT_ref, norms_b, qb8T_ref, kb8_ref, ync_ref):
    cqa, sqa, tk, tcq, tsq, tck, tsk = tables
    tok = x.shape[0]
    rows_ = slice(r0, r0 + tok)
    h = _rms(x, g_ref[...])
    z = _dot(h.astype(BF16), win_ref[...])

    cq = _rms(z[:, Z_CQ:Z_CKV], gq_ref[...]).astype(BF16)
    qa = _dot(cq, wq_ref[...])
    half = MLA_HEADS * HEAD_SLOT
    q_a = qa[:, :half] * cqa + qa[:, half:] * sqa
    q_aT = q_a.T
    qaT_ref[0, :, rows_] = q_aT.astype(BF16)
    q_hi, q_lo = _split_e4m3(q_aT * Q8_SCALE)
    for hd in range(MLA_HEADS):
        nope = slice(hd * HEAD_SLOT, hd * HEAD_SLOT + MLA_NOPE)
        rope = slice(hd * HEAD_SLOT + MLA_NOPE, hd * HEAD_SLOT + MLA_NOPE + MLA_ROPE)
        base = hd * Q8_SLOT
        qa8T_ref[0, base:base + MLA_NOPE, rows_] = q_hi[nope]
        qa8T_ref[0, base + MLA_NOPE:base + 2 * MLA_NOPE, rows_] = q_hi[nope]
        qa8T_ref[0, base + 2 * MLA_NOPE:base + 3 * MLA_NOPE, rows_] = q_lo[nope]
        qa8T_ref[0, base + 3 * MLA_NOPE:base + 3 * MLA_NOPE + MLA_ROPE, rows_] = q_hi[rope]
        qa8T_ref[0, base + 3 * MLA_NOPE + MLA_ROPE:base + Q8_SLOT, rows_] = q_hi[rope]

    ckv = _rms(z[:, Z_CKV:Z_GQ], gkv_ref[...])
    kr = z[:, Z_KR:Z_END] * tk
    lhs = jnp.concatenate([ckv, kr], axis=1).astype(BF16)
    k_a = _dot(lhs, wk_ref[...]).astype(BF16)
    ka_ref[0, rows_] = k_a
    norms_a.append(_key_norms(k_a, MLA_HEADS))
    k8 = _dot(lhs, wk8_ref[...])
    depth = lax.broadcasted_iota(jnp.int32, (1, MLA_HEADS * Q8_SLOT), 1) % Q8_SLOT
    lo_block = jnp.logical_or(jnp.logical_and(depth >= MLA_NOPE, depth < 2 * MLA_NOPE),
                              depth >= 3 * MLA_NOPE + MLA_ROPE)
    ka8_ref[0, rows_] = jnp.where(lo_block, k8 - k8.astype(F8).astype(F32), k8).astype(F8)
    _store_vT(vaT_ref, _dot(lhs[:, :MLA_KV_RANK], wv_ref[...]).T, MLA_HEADS, MLA_V, r0)

    bd = bd_ref[...]
    g_q = z[:, Z_GQ:Z_GQS]
    n_q = lax.rsqrt(_group_ssq(g_q, bd) * (1.0 / GQA_DIM) + EPS)
    q_b = (g_q * tcq + z[:, Z_GQS:Z_GK] * tsq) * n_q
    q_bT = q_b.T
    zero = jnp.zeros((HEAD_SLOT - GQA_DIM, tok), BF16)
    for hd in range(GQA_HEADS):
        qbT_ref[0, hd * HEAD_SLOT:hd * HEAD_SLOT + GQA_DIM, rows_] = q_bT[hd * GQA_DIM:(hd + 1) * GQA_DIM].astype(BF16)
        qbT_ref[0, hd * HEAD_SLOT + GQA_DIM:(hd + 1) * HEAD_SLOT, rows_] = zero
    q_hi, q_lo = _split_e4m3(q_bT * Q8_SCALE)
    zero8 = jnp.zeros((Q8_SLOT - 3 * GQA_DIM, tok), F8)
    for hd in range(GQA_HEADS):
        rows = slice(hd * GQA_DIM, (hd + 1) * GQA_DIM)
        base = hd * Q8_SLOT
        qb8T_ref[0, base:base + GQA_DIM, rows_] = q_hi[rows]
        qb8T_ref[0, base + GQA_DIM:base + 2 * GQA_DIM, rows_] = q_hi[rows]
        qb8T_ref[0, base + 2 * GQA_DIM:base + 3 * GQA_DIM, rows_] = q_lo[rows]
        qb8T_ref[0, base + 3 * GQA_DIM:base + Q8_SLOT, rows_] = zero8
    g_k = z[:, Z_GK:Z_GKS]
    n_k = lax.rsqrt(_group_ssq(g_k, bd[:LANES, :LANES]) * (1.0 / GQA_DIM) + EPS)
    k_b = (g_k * tck + z[:, Z_GKS:Z_GV] * tsk) * n_k
    low = lax.broadcasted_iota(jnp.int32, (tok, LANES), 1) < GQA_DIM
    slots = [jnp.where(low, k_b, 0.0), jnp.where(low, pltpu.roll(k_b, GQA_DIM, 1), 0.0)]
    blocks = []
    for k_slot in slots:
        k8 = k_slot * K8_SCALE
        k_lo = k8 - k8.astype(F8).astype(F32)
        blocks += [jnp.where(low, k8, pltpu.roll(k_lo, GQA_DIM, 1)), k8]
    kb8_ref[0, rows_] = jnp.concatenate(blocks, axis=1).astype(F8)
    k_b = jnp.concatenate(slots, axis=1).astype(BF16)
    kb_ref[0, rows_] = k_b
    norms_b.append(_key_norms(k_b, GQA_KV_HEADS))
    _store_vT(vbT_ref, z[:, Z_GV:Z_U].T, GQA_KV_HEADS, GQA_DIM, r0)

    gm = jax.nn.gelu(z[:, Z_U:Z_KR])
    u = gm[:, :W_C]
    vv = _rms(gm[:, W_C:], gv_ref[...])
    lane_grp = lax.broadcasted_iota(jnp.int32, (GMLP_CHUNK, W_C), 1) // GMLP_DIM
    ws = ws_ref[...]
    bias = bias_ref[...]
    ycs = []
    for n in range(tok // GMLP_CHUNK):
        rows = slice(n * GMLP_CHUNK, (n + 1) * GMLP_CHUNK)
        r = _dot(ws, vv[rows].astype(BF16))
        mixed = r[3 * GMLP_CHUNK:]
        for grp in range(GMLP_GROUPS - 2, -1, -1):
            mixed = jnp.where(lane_grp == grp, r[grp * GMLP_CHUNK:(grp + 1) * GMLP_CHUNK], mixed)
        ycs.append(u[rows] * (mixed + bias))
    ync_ref[0, rows_] = _rms(jnp.concatenate(ycs, axis=0), gc_ref[...]).astype(BF16)


def _mix_in(x, p):
    B, S, D = x.shape
    T = TOK_TILE
    nkv = T // KV_TILE
    tile = lambda w: pl.BlockSpec((1, T, w), lambda s, b: (b, s, 0))
    def tab(t):
        if isinstance(t, _Layer):
            return pl.BlockSpec((None, T, t.stack.shape[2]), lambda s, b: (t.layer, s, 0))
        return pl.BlockSpec((T, t.shape[1]), lambda s, b: (s, 0))
    tposed = lambda r: pl.BlockSpec((1, r, T), lambda s, b: (b, 0, s))
    blocked = lambda r: pl.BlockSpec((1, nkv, r, KV_TILE), lambda s, b: (b, s, 0, 0))
    consts = [p["mix_norm"], p["w_in"], p["mla_q_norm"], p["w_uq"], p["mla_kv_norm"], p["w_k"], p["w_k8"], p["w_v"]]
    tabs = [p["cqa"], p["sqa"], p["tk"], p["tcq"], p["tsq"], p["tck"], p["tsk"]]
    consts2 = [p["bd"], p["gmlp_v_norm"], p["w_s"], p["bias_s"], p["out_norm_c"]]
    in_specs = ([tile(D)] + [_const_spec(c) for c in consts] + [tab(t) for t in tabs]
                + [_const_spec(c) for c in consts2])
    out_shape = [
        jax.ShapeDtypeStruct((B, MLA_HEADS * HEAD_SLOT, S), BF16),
        jax.ShapeDtypeStruct((B, S, MLA_HEADS * HEAD_SLOT), BF16),
        jax.ShapeDtypeStruct((B, S // KV_TILE, MLA_HEADS * (MLA_V + V_PAD), KV_TILE), BF16),
        jax.ShapeDtypeStruct((B, S // T, F32_ROWS, LANES), F32),
        jax.ShapeDtypeStruct((B, MLA_HEADS * Q8_SLOT, S), F8),
        jax.ShapeDtypeStruct((B, S, MLA_HEADS * Q8_SLOT), F8),
        jax.ShapeDtypeStruct((B, GQA_HEADS * HEAD_SLOT, S), BF16),
        jax.ShapeDtypeStruct((B, S, GQA_KV_HEADS * HEAD_SLOT), BF16),
        jax.ShapeDtypeStruct((B, S // KV_TILE, GQA_KV_HEADS * (GQA_DIM + V_PAD), KV_TILE), BF16),
        jax.ShapeDtypeStruct((B, S // T, F32_ROWS, LANES), F32),
        jax.ShapeDtypeStruct((B, GQA_HEADS * Q8_SLOT, S), F8),
        jax.ShapeDtypeStruct((B, S, GQA_KV_HEADS * Q8_SLOT), F8),
        jax.ShapeDtypeStruct((B, S, W_C), BF16),
    ]
    norms = pl.BlockSpec((1, 1, F32_ROWS, LANES), lambda s, b: (b, s, 0, 0))
    out_specs = [tposed(MLA_HEADS * HEAD_SLOT), tile(MLA_HEADS * HEAD_SLOT), blocked(MLA_HEADS * (MLA_V + V_PAD)),
                 norms, tposed(MLA_HEADS * Q8_SLOT), tile(MLA_HEADS * Q8_SLOT),
                 tposed(GQA_HEADS * HEAD_SLOT), tile(GQA_KV_HEADS * HEAD_SLOT),
                 blocked(GQA_KV_HEADS * (GQA_DIM + V_PAD)), norms, tposed(GQA_HEADS * Q8_SLOT),
                 tile(GQA_KV_HEADS * Q8_SLOT), tile(W_C)]
    return pl.pallas_call(
        _mix_in_kernel, grid=(S // T, B), in_specs=in_specs, out_specs=out_specs, out_shape=out_shape,
        compiler_params=_params(2), name="mix_in",
    )(x, *map(_operand, consts + tabs + consts2))


def _attn_kernel(q_ref, k_ref, v_ref, kn_ref, g_ref, q8_ref, k8_ref, o_ref,
                 s_ref, p_ref, m_ref, acc_ref, oT_ref, *, n_heads, k_slots, v_rows, dv):
    nkv = v_ref.shape[1]
    assert n_heads % 2 == 0

    kmax = jnp.max(kn_ref[0], axis=0)
    bound = jnp.zeros((1, 1), F32)
    qn2_max = jnp.zeros((1, 1), F32)
    kn2_max = jnp.zeros((1, 1), F32)
    for hd in range(n_heads):
        qf = q_ref[0, hd * HEAD_SLOT:(hd + 1) * HEAD_SLOT, :].astype(F32)
        qn2 = jnp.sum(qf * qf, axis=0, keepdims=True)
        kn2 = kmax[k_slots[hd]:k_slots[hd] + 1, 0:1]
        u = jnp.sqrt(qn2 * kn2)
        m_ref[hd] = u
        bound = jnp.maximum(bound, jnp.max(u, axis=1, keepdims=True))
        qn2_max = jnp.maximum(qn2_max, jnp.max(qn2, axis=1, keepdims=True))
        kn2_max = jnp.maximum(kn2_max, kn2)
    stabilise = jnp.logical_and(jnp.max(bound) <= STABILISER_MAX, jnp.logical_and(
        jnp.max(qn2_max) <= (FP8_MAX_SCALED / Q8_SCALE) ** 2, jnp.max(kn2_max) <= (FP8_MAX_SCALED / K8_SCALE) ** 2))
    acc_ref[...] = jnp.zeros(acc_ref.shape, F32)

    def run(stabilised):
        def scores(hd, off):
            ks, qs, depth = (k8_ref, q8_ref, Q8_SLOT) if stabilised else (k_ref, q_ref, HEAD_SLOT)
            k = ks[0, pl.ds(off, KV_TILE), k_slots[hd] * depth:(k_slots[hd] + 1) * depth]
            qT = qs[0, hd * depth:(hd + 1) * depth, :]
            s_ref[hd % 2] = _dot(k, qT)

        scores(0, 0)
        last = n_heads - 1

        def pv(hd, j):
            vT = v_ref[0, j, v_rows[hd]:v_rows[hd] + dv + V_PAD, :]
            acc_ref[hd] += _dot(vT, p_ref[hd])

        if stabilised:
            p_ref[last] = jnp.zeros(p_ref.shape[1:], BF16)

        def step(j, carry):
            off = pl.multiple_of(j * KV_TILE, KV_TILE)
            off_next = pl.multiple_of(jnp.minimum(j + 1, nkv - 1) * KV_TILE, KV_TILE)
            for hd in range(n_heads):
                if hd + 1 < n_heads:
                    scores(hd + 1, off)
                else:
                    scores(0, off_next)
                if stabilised:
                    if hd > 0:
                        pv(hd - 1, j)
                    else:
                        pv(last, jnp.maximum(j - 1, 0))
                    s = s_ref[hd % 2] * (1.0 / (Q8_SCALE * K8_SCALE)) - m_ref[hd]
                    p_ref[hd] = jnp.exp2(s.astype(BF16))
                else:
                    vT = v_ref[0, j, v_rows[hd]:v_rows[hd] + dv + V_PAD, :]
                    m_old = m_ref[hd]
                    m_new = jnp.maximum(m_old, jnp.max(s_ref[hd % 2], axis=0, keepdims=True))
                    m_ref[hd] = m_new
                    p_ref[hd] = jnp.exp2(s_ref[hd % 2] - m_new).astype(BF16)
                    acc_ref[hd] = jnp.exp2(m_old - m_new) * acc_ref[hd] + _dot(vT, p_ref[hd])
            return carry

        lax.fori_loop(0, nkv, step, 0, unroll=8 if stabilised else 1)
        if stabilised:
            pv(last, nkv - 1)

    def running_max():
        m_ref[...] = jnp.full(m_ref.shape, -1e30, F32)
        run(False)

    lax.cond(stabilise, lambda: run(True), running_max)
    for hd in range(n_heads):
        oT_ref[hd * dv:(hd + 1) * dv, :] = acc_ref[hd, :dv] / acc_ref[hd, dv:dv + 1]
    o_ref[0] = _rms(oT_ref[...].T, g_ref[...]).astype(BF16)


def _attention(qT, k, vT, kn, gain, q8T, k8, *, n_heads, k_slots, v_rows, dv, name):
    B, _, S = qT.shape
    kernel = functools.partial(_attn_kernel, n_heads=n_heads, k_slots=k_slots, v_rows=v_rows, dv=dv)
    q_tile = lambda slot: pl.BlockSpec((1, n_heads * slot, Q_TILE), lambda b, i: (b, 0, i))
    whole = lambda shape: pl.BlockSpec((1,) + shape[1:], lambda b, i: (b,) + (0,) * (len(shape) - 1),
                                       pipeline_mode=pl.Buffered(1))
    return pl.pallas_call(
        kernel, grid=(B, S // Q_TILE),
        in_specs=[q_tile(HEAD_SLOT), whole(k.shape), whole(vT.shape), whole(kn.shape), _const_spec(gain),
                  q_tile(Q8_SLOT), whole(k8.shape)],
        out_specs=pl.BlockSpec((1, Q_TILE, n_heads * dv), lambda b, i: (b, i, 0)),
        out_shape=jax.ShapeDtypeStruct((B, S, n_heads * dv), BF16),
        scratch_shapes=[pltpu.VMEM((2, KV_TILE, Q_TILE), F32), pltpu.VMEM((n_heads, KV_TILE, Q_TILE), BF16),
                        pltpu.VMEM((n_heads, 1, Q_TILE), F32), pltpu.VMEM((n_heads, dv + V_PAD, Q_TILE), F32),
                        pltpu.VMEM((n_heads * dv, Q_TILE), F32)],
        compiler_params=_params(2), name=name,
    )(qT, k, vT, kn, _operand(gain), q8T, k8)


def _mem_kv_kernel(mem_ref, g_ref, w_ref, kT_ref, v_ref):
    kv = _dot(_rms(mem_ref[0], g_ref[...]).astype(BF16), w_ref[...])
    width = MEM_HEADS * MEM_DIM
    kT_ref[0] = kv[:, :width].T.astype(BF16)
    ones = jnp.ones((kv.shape[0], MEM_DIM), BF16)
    for hd in range(MEM_HEADS):
        v_ref[0, hd, :, :MEM_DIM] = kv[:, width + hd * MEM_DIM:width + (hd + 1) * MEM_DIM].astype(BF16)
        v_ref[0, hd, :, MEM_DIM:] = ones


def _mem_kv(mem, gain, w_kv):
    B, Tm, D = mem.shape
    width = MEM_HEADS * MEM_DIM
    return pl.pallas_call(
        _mem_kv_kernel, grid=(B,),
        in_specs=[pl.BlockSpec((1, Tm, D), lambda b: (b, 0, 0)), _const_spec(gain), _const_spec(w_kv)],
        out_specs=[pl.BlockSpec((1, width, Tm), lambda b: (b, 0, 0)),
                   pl.BlockSpec((1, MEM_HEADS, Tm, 2 * MEM_DIM), lambda b: (b, 0, 0, 0))],
        out_shape=[jax.ShapeDtypeStruct((B, width, Tm), BF16),
                   jax.ShapeDtypeStruct((B, MEM_HEADS, Tm, 2 * MEM_DIM), BF16)],
        compiler_params=_params(1), name="mem_kv",
    )(mem, _operand(gain), _operand(w_kv))


def _out_mem_kernel(ya_ref, yb_ref, yc_ref, x_ref, wout_ref, g_ref, wq_ref, kT_ref, v_ref, wo_ref, o_ref):
    y = jnp.concatenate([ya_ref[0], yb_ref[0], yc_ref[0]], axis=1)
    x1 = x_ref[0] + _dot(y, wout_ref[...])
    h = _rms(x1, g_ref[...]).astype(BF16)
    q = (_dot(h, wq_ref[...]) * (MEM_DIM ** -0.5)).astype(BF16)
    heads = []
    for hd in range(MEM_HEADS):
        s = _dot(q[:, hd * MEM_DIM:(hd + 1) * MEM_DIM], kT_ref[0, hd * MEM_DIM:(hd + 1) * MEM_DIM, :])
        p = jnp.exp(s - jnp.max(s, axis=-1, keepdims=True)).astype(BF16)
        pv = _dot(p, v_ref[0, hd])
        heads.append(pv[:, :MEM_DIM] / pv[:, MEM_DIM:])
    o = jnp.concatenate(heads, axis=1).astype(BF16)
    o_ref[0] = x1 + _dot(o, wo_ref[...])


def _out_mem(ya, yb, yc, x, p, mem_kT, mem_v):
    B, S, D = x.shape
    T = TOK_TILE
    tile = lambda w: pl.BlockSpec((1, T, w), lambda b, s: (b, s, 0))
    per_b = lambda a: pl.BlockSpec((1,) + a.shape[1:], lambda b, s: (b,) + (0,) * (a.ndim - 1))
    consts = [p["w_out"], p["mem_x_norm"], p["mem_w_q"]]
    return pl.pallas_call(
        _out_mem_kernel, grid=(B, S // T),
        in_specs=[tile(W_A), tile(W_B), tile(W_C), tile(D)] + [_const_spec(c) for c in consts]
                 + [per_b(mem_kT), per_b(mem_v), _const_spec(p["mem_w_o"])],
        out_specs=tile(D), out_shape=jax.ShapeDtypeStruct((B, S, D), F32),
        compiler_params=_params(2), name="out_mem",
    )(ya, yb, yc, x, *map(_operand, consts), mem_kT, mem_v, _operand(p["mem_w_o"]))


def _ffn_kernel(x_ref, xp_ref, xn_ref, g_ref, wup_ref, cw_ref, cb_ref, wdn_ref, fg_ref, o_ref,
                h_ref, act_ref, *, final):
    tok = x_ref.shape[1]
    i = pl.program_id(1)
    g = g_ref[...]
    x = x_ref[0]
    keep_prev = (i > 0).astype(F32)
    keep_next = (i < pl.num_programs(1) - 1).astype(F32)
    h_ref[0:HALO] = (_rms(xp_ref[0], g) * keep_prev).astype(BF16)
    h_ref[HALO:HALO + tok] = _rms(x, g).astype(BF16)
    h_ref[HALO + tok:] = (_rms(xn_ref[0], g) * keep_next).astype(BF16)
    hext = h_ref[...]
    rows = tok + 2 * HALO

    def conv_up(cols):
        a = _dot(hext, wup_ref[:, cols])
        w = cw_ref[:, cols]
        return (pltpu.roll(a, 1, 0)[HALO:HALO + tok] * w[0:1] + a[HALO:HALO + tok] * w[1:2]
                + pltpu.roll(a, rows - 1, 0)[HALO:HALO + tok] * w[2:3] + cb_ref[:, cols])

    for c in range(D_FF // FF_CHUNK):
        gate = conv_up(slice(FF_CHUNK * c, FF_CHUNK * (c + 1)))
        val = conv_up(slice(D_FF + FF_CHUNK * c, D_FF + FF_CHUNK * (c + 1)))
        act_ref[:, FF_CHUNK * c:FF_CHUNK * (c + 1)] = (jax.nn.silu(gate) * val).astype(BF16)
    y = x + _dot(act_ref[...], wdn_ref[...])
    if final:
        y = _rms(y, fg_ref[...])
    o_ref[0] = y


def _ffn(x, p, final_gain, *, final):
    B, S, D = x.shape
    T = FFN_TILE
    per_tile = T // HALO
    n_halo = S // HALO
    tile = pl.BlockSpec((1, T, D), lambda b, s: (b, s, 0))
    prev = pl.BlockSpec((1, HALO, D), lambda b, s: (b, jnp.maximum(s * per_tile - 1, 0), 0))
    nxt = pl.BlockSpec((1, HALO, D), lambda b, s: (b, jnp.minimum((s + 1) * per_tile, n_halo - 1), 0))
    consts = [p["ffn_norm"], p["ffn_w_up"], p["ffn_conv_w"], p["ffn_conv_b"], p["ffn_w_down"], final_gain]
    return pl.pallas_call(
        functools.partial(_ffn_kernel, final=final), grid=(B, S // T),
        in_specs=[tile, prev, nxt] + [_const_spec(c) for c in consts],
        out_specs=tile, out_shape=jax.ShapeDtypeStruct((B, S, D), F32),
        scratch_shapes=[pltpu.VMEM((T + 2 * HALO, D), BF16), pltpu.VMEM((T, D_FF), BF16)],
        compiler_params=_params(2), name="ffn_final" if final else "ffn",
    )(x, x, x, *map(_operand, consts))


def _swap_pairs(w):
    n = w.shape[-1]
    return w.reshape(*w.shape[:-1], n // 2, 2)[..., ::-1].reshape(w.shape)


def _rope_tables(S, d_rot):
    rows = S // GRID_W
    row = jnp.repeat(jnp.arange(rows, dtype=F32), GRID_W)
    col = jnp.tile(jnp.arange(GRID_W, dtype=F32), rows)
    n = d_rot // 4
    inv = ROPE_THETA ** (-jnp.arange(n, dtype=F32) / n)
    ang = jnp.concatenate([row[:, None] * inv, col[:, None] * inv], axis=-1)
    cos, sin = jnp.cos(ang), jnp.sin(ang)
    c = jnp.repeat(cos, 2, axis=-1)
    s = jnp.stack([-sin, sin], axis=-1).reshape(S, d_rot)
    return c, s


def _prep_params(S, P):
    L = P["w_in"].shape[0]
    row = lambda v: v[:, None, :]
    ca, sa = _rope_tables(S, MLA_ROPE)
    cb, sb = _rope_tables(S, GQA_DIM)
    p, shared = {}, {}
    c_q, c_kv, k_rope, g_q, g_k, g_v, g_m = jnp.split(P["w_in"], [256, 384, 416, 800, 928, 1056], axis=2)
    pad = jnp.zeros((L, D_MODEL, Z_END - Z_KR - 2 * MLA_ROPE), F32)
    p["w_in"] = jnp.concatenate(
        [c_q, c_kv, g_q, _swap_pairs(g_q), g_k, _swap_pairs(g_k), g_v, g_m, k_rope, _swap_pairs(k_rope), pad],
        axis=2).astype(BF16)
    for name in ("mix_norm", "mla_q_norm", "mla_kv_norm", "gmlp_v_norm", "mem_x_norm", "mem_kv_norm", "ffn_norm",
                 "ffn_conv_b"):
        p[name] = row(P[name])

    w_uq = P["mla_w_uq"].reshape(L, MLA_Q_RANK, MLA_HEADS, MLA_NOPE + MLA_ROPE)
    zpad = jnp.zeros((L, MLA_Q_RANK, MLA_HEADS, HEAD_SLOT - MLA_NOPE - MLA_ROPE), F32)
    main = jnp.concatenate([w_uq, zpad], axis=-1)
    swapped = jnp.concatenate([jnp.zeros_like(w_uq[..., :MLA_NOPE]), _swap_pairs(w_uq[..., MLA_NOPE:]), zpad], axis=-1)
    p["w_uq"] = jnp.concatenate([main.reshape(L, MLA_Q_RANK, -1), swapped.reshape(L, MLA_Q_RANK, -1)],
                                axis=2).astype(BF16)
    scale_a = (MLA_NOPE + MLA_ROPE) ** -0.5 * LOG2E
    ones = jnp.ones((S, MLA_NOPE), F32)
    zeros_n = jnp.zeros((S, MLA_NOPE), F32)
    zeros_p = jnp.zeros((S, HEAD_SLOT - MLA_NOPE - MLA_ROPE), F32)
    shared["cqa"] = jnp.tile(jnp.concatenate([ones, ca, zeros_p], axis=1) * scale_a, (1, MLA_HEADS))
    shared["sqa"] = jnp.tile(jnp.concatenate([zeros_n, sa, zeros_p], axis=1) * scale_a, (1, MLA_HEADS))

    w_ukv = P["mla_w_ukv"].reshape(L, MLA_KV_RANK, MLA_HEADS, MLA_NOPE + MLA_V)
    k_lat = jnp.concatenate(
        [w_ukv[..., :MLA_NOPE], jnp.zeros((L, MLA_KV_RANK, MLA_HEADS, HEAD_SLOT - MLA_NOPE), F32)], axis=-1)
    place = jnp.concatenate([jnp.zeros((MLA_ROPE, MLA_NOPE), F32), jnp.eye(MLA_ROPE, dtype=F32),
                             jnp.zeros((MLA_ROPE, HEAD_SLOT - MLA_NOPE - MLA_ROPE), F32)], axis=1)
    place = jnp.broadcast_to(jnp.tile(place, (1, MLA_HEADS)), (L, MLA_ROPE, MLA_HEADS * HEAD_SLOT))
    zrows = jnp.zeros((L, Z_END - Z_KR - 2 * MLA_ROPE, MLA_HEADS * HEAD_SLOT), F32)
    p["w_k"] = jnp.concatenate([k_lat.reshape(L, MLA_KV_RANK, -1), place, place, zrows], axis=1).astype(BF16)
    p["w_v"] = w_ukv[..., MLA_NOPE:].reshape(L, MLA_KV_RANK, -1).astype(BF16)
    nope_w = w_ukv[..., :MLA_NOPE]
    lat8 = jnp.concatenate([nope_w, nope_w, nope_w, jnp.zeros((L, MLA_KV_RANK, MLA_HEADS, 2 * MLA_ROPE), F32)], axis=-1)
    place8 = jnp.concatenate([jnp.zeros((MLA_ROPE, 3 * MLA_NOPE), F32), jnp.eye(MLA_ROPE, dtype=F32),
                              jnp.eye(MLA_ROPE, dtype=F32)], axis=1)
    place8 = jnp.broadcast_to(jnp.tile(place8, (1, MLA_HEADS)), (L, MLA_ROPE, MLA_HEADS * Q8_SLOT))
    zrows8 = jnp.zeros((L, Z_END - Z_KR - 2 * MLA_ROPE, MLA_HEADS * Q8_SLOT), F32)
    p["w_k8"] = (jnp.concatenate([lat8.reshape(L, MLA_KV_RANK, -1), place8, place8, zrows8], axis=1)
                 * K8_SCALE).astype(BF16)
    shared["tk"] = jnp.concatenate([ca, sa, jnp.zeros((S, Z_END - Z_KR - 2 * MLA_ROPE), F32)], axis=1)

    gq = row(jnp.tile(P["gqa_q_norm"], (1, GQA_HEADS)) * (GQA_DIM ** -0.5 * LOG2E))
    gk = row(jnp.tile(P["gqa_k_norm"], (1, GQA_KV_HEADS)))
    p["tcq"] = jnp.tile(cb, (1, GQA_HEADS)) * gq
    p["tsq"] = jnp.tile(sb, (1, GQA_HEADS)) * _swap_pairs(gq)
    p["tck"] = jnp.tile(cb, (1, GQA_KV_HEADS)) * gk
    p["tsk"] = jnp.tile(sb, (1, GQA_KV_HEADS)) * _swap_pairs(gk)
    grp = jnp.arange(W_B) // GQA_DIM
    shared["bd"] = (grp[:, None] == grp[None, :]).astype(BF16)

    p["w_s"] = P["gmlp_w_s"].reshape(L, GMLP_GROUPS * GMLP_CHUNK, GMLP_CHUNK).astype(BF16)
    p["bias_s"] = jnp.repeat(jnp.swapaxes(P["gmlp_b_s"], 1, 2), GMLP_DIM, axis=2)
    p["out_norm_a"] = row(P["out_norm"][:, :W_A])
    p["out_norm_b"] = row(P["out_norm"][:, W_A:W_A + W_B])
    p["out_norm_c"] = row(P["out_norm"][:, W_A + W_B:])
    for name in ("w_out", "mem_w_q", "mem_w_kv", "mem_w_o", "ffn_w_up", "ffn_w_down"):
        p[name] = P[name].astype(BF16)
    p["ffn_conv_w"] = P["ffn_conv_w"]
    return p, shared


def kernel(x, mem, mix_norm, w_in, mla_q_norm, mla_w_uq, mla_kv_norm, mla_w_ukv, gqa_q_norm, gqa_k_norm, gmlp_v_norm, gmlp_w_s, gmlp_b_s, out_norm, w_out, mem_x_norm, mem_kv_norm, mem_w_q, mem_w_kv, mem_w_o, ffn_norm, ffn_w_up, ffn_conv_w, ffn_conv_b, ffn_w_down, final_norm):
    P = dict(mix_norm=mix_norm, w_in=w_in, mla_q_norm=mla_q_norm, mla_w_uq=mla_w_uq, mla_kv_norm=mla_kv_norm,
             mla_w_ukv=mla_w_ukv, gqa_q_norm=gqa_q_norm, gqa_k_norm=gqa_k_norm, gmlp_v_norm=gmlp_v_norm,
             gmlp_w_s=gmlp_w_s, gmlp_b_s=gmlp_b_s, out_norm=out_norm, w_out=w_out, mem_x_norm=mem_x_norm,
             mem_kv_norm=mem_kv_norm, mem_w_q=mem_w_q, mem_w_kv=mem_w_kv, mem_w_o=mem_w_o, ffn_norm=ffn_norm,
             ffn_w_up=ffn_w_up, ffn_conv_w=ffn_conv_w, ffn_conv_b=ffn_conv_b, ffn_w_down=ffn_w_down)
    B, S, D = x.shape
    assert D == D_MODEL and S % TOK_TILE == 0 and S % FFN_TILE == 0 and S % GRID_W == 0
    depth = w_in.shape[0]
    stacked, shared = _prep_params(S, P)
    final_gain = final_norm.reshape(1, -1)
    group = GQA_HEADS // GQA_KV_HEADS
    for l in range(depth):
        p = dict(shared, **{name: _Layer(v, l) for name, v in stacked.items()})
        qaT, ka, vaT, kna, qa8T, ka8, qbT, kb, vbT, knb, qb8T, kb8, ync = _mix_in(x, p)
        yna = _attention(qaT, ka, vaT, kna, p["out_norm_a"], qa8T, ka8, n_heads=MLA_HEADS,
                         k_slots=tuple(range(MLA_HEADS)),
                         v_rows=tuple(h * (MLA_V + V_PAD) for h in range(MLA_HEADS)), dv=MLA_V, name="attn_mla")
        ynb = _attention(qbT, kb, vbT, knb, p["out_norm_b"], qb8T, kb8, n_heads=GQA_HEADS,
                         k_slots=tuple(h // group for h in range(GQA_HEADS)),
                         v_rows=tuple((h // group) * (GQA_DIM + V_PAD) for h in range(GQA_HEADS)), dv=GQA_DIM, name="attn_gqa")
        mem_kT, mem_v = _mem_kv(mem, p["mem_kv_norm"], p["mem_w_kv"])
        x = _out_mem(yna, ynb, ync, x, p, mem_kT, mem_v)
        x = _ffn(x, p, final_gain, final=(l == depth - 1))
    return x
```

```python
import functools
from typing import NamedTuple

import jax
import jax.numpy as jnp
from jax import lax
from jax.experimental import pallas as pl
from jax.experimental.pallas import tpu as pltpu

F32 = jnp.float32
BF16 = jnp.bfloat16
F8 = jnp.float8_e4m3fn

D_MODEL = 1024
GRID_W = 64
ROPE_THETA = 10000.0
EPS = 1e-6
MLA_HEADS = 6
MLA_NOPE = 64
MLA_ROPE = 32
MLA_V = 64
MLA_Q_RANK = 256
MLA_KV_RANK = 128
GQA_HEADS = 6
GQA_KV_HEADS = 2
GQA_DIM = 64
GMLP_GROUPS = 4
GMLP_DIM = 64
GMLP_CHUNK = 128
W_A = MLA_HEADS * MLA_V
W_B = GQA_HEADS * GQA_DIM
W_C = GMLP_GROUPS * GMLP_DIM
MEM_HEADS = 4
MEM_DIM = 128
D_FF = 2816

LANES = 128
HEAD_SLOT = LANES
TOK_TILE = 512
MIX_ROWS = 256
WIDE_TILE = 1024
Q_TILE = 512
KV_TILE = 512
F32_ROWS = 8
BF16_ROWS = 16
V_PAD = BF16_ROWS
LOG2E = 1.4426950408889634
STABILISER_MAX = 50.0
Q8_SLOT = 256
Q8_SCALE = 32.0
K8_SCALE = 8.0
FP8_MAX_SCALED = 400.0
FF_CHUNK = 256
HALO = BF16_ROWS
VMEM_LIMIT = 56 * 1024 * 1024

Z_CQ, Z_CKV, Z_GQ, Z_GQS, Z_GK, Z_GKS, Z_GV, Z_U, Z_VV, Z_KR, Z_END = (
    0, 256, 384, 768, 1152, 1280, 1408, 1536, 1792, 2048, 2176)


def _rms(x, g):
    return x * lax.rsqrt(jnp.mean(x * x, axis=-1, keepdims=True) + EPS) * g


def _dot(a, b):
    return jnp.dot(a, b, preferred_element_type=F32)


class _Layer(NamedTuple):
    stack: jax.Array
    layer: int


def _operand(a):
    return a.stack if isinstance(a, _Layer) else a


def _const_spec(a):
    if isinstance(a, _Layer):
        zeros = (0,) * (a.stack.ndim - 1)
        return pl.BlockSpec((None,) + a.stack.shape[1:], lambda *_: (a.layer,) + zeros, pipeline_mode=pl.Buffered(1))
    zeros = (0,) * a.ndim
    return pl.BlockSpec(a.shape, lambda *_: zeros, pipeline_mode=pl.Buffered(1))


def _params(n_axes):
    return pltpu.CompilerParams(dimension_semantics=("arbitrary",) * n_axes,
                                vmem_limit_bytes=VMEM_LIMIT)


def _group_ssq(v, bd):
    sq = v * v
    hi = sq.astype(BF16)
    lo = (sq - hi.astype(F32)).astype(BF16)
    return _dot(hi, bd) + _dot(lo, bd)


def _split_e4m3(x):
    hi = x.astype(F8)
    return hi, (x - hi.astype(F32)).astype(F8)


def _store_vT(ref, vT, n_heads, dv, tok0):
    width = vT.shape[1]
    assert KV_TILE % width == 0 and tok0 % width == 0
    n, cols = tok0 // KV_TILE, slice(tok0 % KV_TILE, tok0 % KV_TILE + width)
    ext = (lax.broadcasted_iota(jnp.int32, (V_PAD, width), 0) == 0).astype(BF16)
    for hd in range(n_heads):
        base = hd * (dv + V_PAD)
        ref[0, n, base:base + dv, cols] = vT[hd * dv:(hd + 1) * dv].astype(BF16)
        ref[0, n, base + dv:base + dv + V_PAD, cols] = ext


def _key_norms(k, n_slots):
    rows = []
    for n in range(n_slots):
        x = k[:, n * HEAD_SLOT:(n + 1) * HEAD_SLOT].astype(F32)
        n2 = jnp.max(jnp.sum(x * x, axis=-1, keepdims=True), axis=0, keepdims=True)
        rows.append(jnp.broadcast_to(n2, (1, LANES)))
    rows.append(jnp.zeros((F32_ROWS - n_slots, LANES), F32))
    return jnp.concatenate(rows, axis=0)


def _mix_in_kernel(x_ref, g_ref, win_ref, gq_ref, wq_ref, gkv_ref, wk_ref, wk8_ref, wv_ref,
                   cqa_ref, sqa_ref, tk_ref, tcq_ref, tsq_ref, tck_ref, tsk_ref,
                   bd_ref, gv_ref, ws_ref, bias_ref, gc_ref,
                   qaT_ref, ka_ref, vaT_ref, kna_ref, qa8T_ref, ka8_ref,
                   qbT_ref, kb_ref, vbT_ref, knb_ref, qb8T_ref, kb8_ref, ync_ref):
    norms_a, norms_b = [], []
    for r0 in range(0, x_ref.shape[1], MIX_ROWS):
        rows_ = slice(r0, r0 + MIX_ROWS)
        _mix_in_rows(r0, x_ref[0, rows_], g_ref, win_ref, gq_ref, wq_ref, gkv_ref, wk_ref, wk8_ref, wv_ref,
                     [t[rows_] for t in (cqa_ref, sqa_ref, tk_ref, tcq_ref, tsq_ref, tck_ref, tsk_ref)],
                     bd_ref, gv_ref, ws_ref, bias_ref, gc_ref,
                     qaT_ref, ka_ref, vaT_ref, norms_a, qa8T_ref, ka8_ref,
                     qbT_ref, kb_ref, vbT_ref, norms_b, qb8T_ref, kb8_ref, ync_ref)
    kna_ref[0, 0] = functools.reduce(jnp.maximum, norms_a)
    knb_ref[0, 0] = functools.reduce(jnp.maximum, norms_b)


def _mix_in_rows(r0, x, g_ref, win_ref, gq_ref, wq_ref, gkv_ref, wk_ref, wk8_ref, wv_ref, tables,
                 bd_ref, gv_ref, ws_ref, bias_ref, gc_ref,
                 qaT_ref, ka_ref, vaT_ref, norms_a, qa8T_ref, ka8_ref,
                 qbT_ref, kb_ref, vbT_ref, norms_b, qb8T_ref, kb8_ref, ync_ref):
    cqa, sqa, tk, tcq, tsq, tck, tsk = tables
    tok = x.shape[0]
    rows_ = slice(r0, r0 + tok)
    h = _rms(x, g_ref[...])
    z = _dot(h.astype(BF16), win_ref[...])

    cq = _rms(z[:, Z_CQ:Z_CKV], gq_ref[...]).astype(BF16)
    qa = _dot(cq, wq_ref[...])
    half = MLA_HEADS * HEAD_SLOT
    q_a = qa[:, :half] * cqa + qa[:, half:] * sqa
    q_aT = q_a.T
    qaT_ref[0, :, rows_] = q_aT.astype(BF16)
    q_hi, q_lo = _split_e4m3(q_aT * Q8_SCALE)
    for hd in range(MLA_HEADS):
        nope = slice(hd * HEAD_SLOT, hd * HEAD_SLOT + MLA_NOPE)
        rope = slice(hd * HEAD_SLOT + MLA_NOPE, hd * HEAD_SLOT + MLA_NOPE + MLA_ROPE)
        base = hd * Q8_SLOT
        qa8T_ref[0, base:base + MLA_NOPE, rows_] = q_hi[nope]
        qa8T_ref[0, base + MLA_NOPE:base + 2 * MLA_NOPE, rows_] = q_hi[nope]
        qa8T_ref[0, base + 2 * MLA_NOPE:base + 3 * MLA_NOPE, rows_] = q_lo[nope]
        qa8T_ref[0, base + 3 * MLA_NOPE:base + 3 * MLA_NOPE + MLA_ROPE, rows_] = q_hi[rope]
        qa8T_ref[0, base + 3 * MLA_NOPE + MLA_ROPE:base + Q8_SLOT, rows_] = q_hi[rope]

    ckv = _rms(z[:, Z_CKV:Z_GQ], gkv_ref[...])
    kr = z[:, Z_KR:Z_END] * tk
    lhs = jnp.concatenate([ckv, kr], axis=1).astype(BF16)
    k_a = _dot(lhs, wk_ref[...]).astype(BF16)
    ka_ref[0, rows_] = k_a
    norms_a.append(_key_norms(k_a, MLA_HEADS))
    k8 = _dot(lhs, wk8_ref[...])
    depth = lax.broadcasted_iota(jnp.int32, (1, MLA_HEADS * Q8_SLOT), 1) % Q8_SLOT
    lo_block = jnp.logical_or(jnp.logical_and(depth >= MLA_NOPE, depth < 2 * MLA_NOPE),
                              depth >= 3 * MLA_NOPE + MLA_ROPE)
    ka8_ref[0, rows_] = jnp.where(lo_block, k8 - k8.astype(F8).astype(F32), k8).astype(F8)
    _store_vT(vaT_ref, _dot(lhs[:, :MLA_KV_RANK], wv_ref[...]).T, MLA_HEADS, MLA_V, r0)

    bd = bd_ref[...]
    g_q = z[:, Z_GQ:Z_GQS]
    n_q = lax.rsqrt(_group_ssq(g_q, bd) * (1.0 / GQA_DIM) + EPS)
    q_b = (g_q * tcq + z[:, Z_GQS:Z_GK] * tsq) * n_q
    q_bT = q_b.T
    zero = jnp.zeros((HEAD_SLOT - GQA_DIM, tok), BF16)
    for hd in range(GQA_HEADS):
        qbT_ref[0, hd * HEAD_SLOT:hd * HEAD_SLOT + GQA_DIM, rows_] = q_bT[hd * GQA_DIM:(hd + 1) * GQA_DIM].astype(BF16)
        qbT_ref[0, hd * HEAD_SLOT + GQA_DIM:(hd + 1) * HEAD_SLOT, rows_] = zero
    q_hi, q_lo = _split_e4m3(q_bT * Q8_SCALE)
    zero8 = jnp.zeros((Q8_SLOT - 3 * GQA_DIM, tok), F8)
    for hd in range(GQA_HEADS):
        rows = slice(hd * GQA_DIM, (hd + 1) * GQA_DIM)
        base = hd * Q8_SLOT
        qb8T_ref[0, base:base + GQA_DIM, rows_] = q_hi[rows]
        qb8T_ref[0, base + GQA_DIM:base + 2 * GQA_DIM, rows_] = q_hi[rows]
        qb8T_ref[0, base + 2 * GQA_DIM:base + 3 * GQA_DIM, rows_] = q_lo[rows]
        qb8T_ref[0, base + 3 * GQA_DIM:base + Q8_SLOT, rows_] = zero8
    g_k = z[:, Z_GK:Z_GKS]
    n_k = lax.rsqrt(_group_ssq(g_k, bd[:LANES, :LANES]) * (1.0 / GQA_DIM) + EPS)
    k_b = (g_k * tck + z[:, Z_GKS:Z_GV] * tsk) * n_k
    low = lax.broadcasted_iota(jnp.int32, (tok, LANES), 1) < GQA_DIM
    slots = [jnp.where(low, k_b, 0.0), jnp.where(low, pltpu.roll(k_b, GQA_DIM, 1), 0.0)]
    blocks = []
    for k_slot in slots:
        k8 = k_slot * K8_SCALE
        k_lo = k8 - k8.astype(F8).astype(F32)
        blocks += [jnp.where(low, k8, pltpu.roll(k_lo, GQA_DIM, 1)), k8]
    kb8_ref[0, rows_] = jnp.concatenate(blocks, axis=1).astype(F8)
    k_b = jnp.concatenate(slots, axis=1).astype(BF16)
    kb_ref[0, rows_] = k_b
    norms_b.append(_key_norms(k_b, GQA_KV_HEADS))
    _store_vT(vbT_ref, z[:, Z_GV:Z_U].T, GQA_KV_HEADS, GQA_DIM, r0)

    gm = jax.nn.gelu(z[:, Z_U:Z_KR])
    u = gm[:, :W_C]
    vv = _rms(gm[:, W_C:], gv_ref[...])
    lane_grp = lax.broadcasted_iota(jnp.int32, (GMLP_CHUNK, W_C), 1) // GMLP_DIM
    ws = ws_ref[...]
    bias = bias_ref[...]
    ycs = []
    for n in range(tok // GMLP_CHUNK):
        rows = slice(n * GMLP_CHUNK, (n + 1) * GMLP_CHUNK)
        r = _dot(ws, vv[rows].astype(BF16))
        mixed = r[3 * GMLP_CHUNK:]
        for grp in range(GMLP_GROUPS - 2, -1, -1):
            mixed = jnp.where(lane_grp == grp, r[grp * GMLP_CHUNK:(grp + 1) * GMLP_CHUNK], mixed)
        ycs.append(u[rows] * (mixed + bias))
    ync_ref[0, rows_] = _rms(jnp.concatenate(ycs, axis=0), gc_ref[...]).astype(BF16)


def _mix_in(x, p):
    B, S, D = x.shape
    T = TOK_TILE
    nkv = T // KV_TILE
    tile = lambda w: pl.BlockSpec((1, T, w), lambda s, b: (b, s, 0))
    def tab(t):
        if isinstance(t, _Layer):
            return pl.BlockSpec((None, T, t.stack.shape[2]), lambda s, b: (t.layer, s, 0))
        return pl.BlockSpec((T, t.shape[1]), lambda s, b: (s, 0))
    tposed = lambda r: pl.BlockSpec((1, r, T), lambda s, b: (b, 0, s))
    blocked = lambda r: pl.BlockSpec((1, nkv, r, KV_TILE), lambda s, b: (b, s, 0, 0))
    consts = [p["mix_norm"], p["w_in"], p["mla_q_norm"], p["w_uq"], p["mla_kv_norm"], p["w_k"], p["w_k8"], p["w_v"]]
    tabs = [p["cqa"], p["sqa"], p["tk"], p["tcq"], p["tsq"], p["tck"], p["tsk"]]
    consts2 = [p["bd"], p["gmlp_v_norm"], p["w_s"], p["bias_s"], p["out_norm_c"]]
    in_specs = ([tile(D)] + [_const_spec(c) for c in consts] + [tab(t) for t in tabs]
                + [_const_spec(c) for c in consts2])
    out_shape = [
        jax.ShapeDtypeStruct((B, MLA_HEADS * HEAD_SLOT, S), BF16),
        jax.ShapeDtypeStruct((B, S, MLA_HEADS * HEAD_SLOT), BF16),
        jax.ShapeDtypeStruct((B, S // KV_TILE, MLA_HEADS * (MLA_V + V_PAD), KV_TILE), BF16),
        jax.ShapeDtypeStruct((B, S // T, F32_ROWS, LANES), F32),
        jax.ShapeDtypeStruct((B, MLA_HEADS * Q8_SLOT, S), F8),
        jax.ShapeDtypeStruct((B, S, MLA_HEADS * Q8_SLOT), F8),
        jax.ShapeDtypeStruct((B, GQA_HEADS * HEAD_SLOT, S), BF16),
        jax.ShapeDtypeStruct((B, S, GQA_KV_HEADS * HEAD_SLOT), BF16),
        jax.ShapeDtypeStruct((B, S // KV_TILE, GQA_KV_HEADS * (GQA_DIM + V_PAD), KV_TILE), BF16),
        jax.ShapeDtypeStruct((B, S // T, F32_ROWS, LANES), F32),
        jax.ShapeDtypeStruct((B, GQA_HEADS * Q8_SLOT, S), F8),
        jax.ShapeDtypeStruct((B, S, GQA_KV_HEADS * Q8_SLOT), F8),
        jax.ShapeDtypeStruct((B, S, W_C), BF16),
    ]
    norms = pl.BlockSpec((1, 1, F32_ROWS, LANES), lambda s, b: (b, s, 0, 0))
    out_specs = [tposed(MLA_HEADS * HEAD_SLOT), tile(MLA_HEADS * HEAD_SLOT), blocked(MLA_HEADS * (MLA_V + V_PAD)),
                 norms, tposed(MLA_HEADS * Q8_SLOT), tile(MLA_HEADS * Q8_SLOT),
                 tposed(GQA_HEADS * HEAD_SLOT), tile(GQA_KV_HEADS * HEAD_SLOT),
                 blocked(GQA_KV_HEADS * (GQA_DIM + V_PAD)), norms, tposed(GQA_HEADS * Q8_SLOT),
                 tile(GQA_KV_HEADS * Q8_SLOT), tile(W_C)]
    return pl.pallas_call(
        _mix_in_kernel, grid=(S // T, B), in_specs=in_specs, out_specs=out_specs, out_shape=out_shape,
        compiler_params=_params(2), name="mix_in",
    )(x, *map(_operand, consts + tabs + consts2))


def _attn_kernel(q_ref, k_ref, v_ref, kn_ref, g_ref, q8_ref, k8_ref, o_ref,
                 s_ref, p_ref, m_ref, acc_ref, oT_ref, *, n_heads, k_slots, v_rows, dv):
    nkv = v_ref.shape[1]
    assert n_heads % 2 == 0

    kmax = jnp.max(kn_ref[0], axis=0)
    bound = jnp.zeros((1, 1), F32)
    qn2_max = jnp.zeros((1, 1), F32)
    kn2_max = jnp.zeros((1, 1), F32)
    for hd in range(n_heads):
        qf = q_ref[0, hd * HEAD_SLOT:(hd + 1) * HEAD_SLOT, :].astype(F32)
        qn2 = jnp.sum(qf * qf, axis=0, keepdims=True)
        kn2 = kmax[k_slots[hd]:k_slots[hd] + 1, 0:1]
        u = jnp.sqrt(qn2 * kn2)
        m_ref[hd] = u
        bound = jnp.maximum(bound, jnp.max(u, axis=1, keepdims=True))
        qn2_max = jnp.maximum(qn2_max, jnp.max(qn2, axis=1, keepdims=True))
        kn2_max = jnp.maximum(kn2_max, kn2)
    stabilise = jnp.logical_and(jnp.max(bound) <= STABILISER_MAX, jnp.logical_and(
        jnp.max(qn2_max) <= (FP8_MAX_SCALED / Q8_SCALE) ** 2, jnp.max(kn2_max) <= (FP8_MAX_SCALED / K8_SCALE) ** 2))
    acc_ref[...] = jnp.zeros(acc_ref.shape, F32)

    def run(stabilised):
        def scores(hd, off):
            ks, qs, depth = (k8_ref, q8_ref, Q8_SLOT) if stabilised else (k_ref, q_ref, HEAD_SLOT)
            k = ks[0, pl.ds(off, KV_TILE), k_slots[hd] * depth:(k_slots[hd] + 1) * depth]
            qT = qs[0, hd * depth:(hd + 1) * depth, :]
            s_ref[hd % 2] = _dot(k, qT)

        scores(0, 0)
        last = n_heads - 1

        def pv(hd, j):
            vT = v_ref[0, j, v_rows[hd]:v_rows[hd] + dv + V_PAD, :]
            acc_ref[hd] += _dot(vT, p_ref[hd])

        if stabilised:
            p_ref[last] = jnp.zeros(p_ref.shape[1:], BF16)

        def step(j, carry):
            off = pl.multiple_of(j * KV_TILE, KV_TILE)
            off_next = pl.multiple_of(jnp.minimum(j + 1, nkv - 1) * KV_TILE, KV_TILE)
            for hd in range(n_heads):
                if hd + 1 < n_heads:
                    scores(hd + 1, off)
                else:
                    scores(0, off_next)
                if stabilised:
                    if hd > 0:
                        pv(hd - 1, j)
                    else:
                        pv(last, jnp.maximum(j - 1, 0))
                    s = s_ref[hd % 2] * (1.0 / (Q8_SCALE * K8_SCALE)) - m_ref[hd]
                    p_ref[hd] = jnp.exp2(s.astype(BF16))
                else:
                    vT = v_ref[0, j, v_rows[hd]:v_rows[hd] + dv + V_PAD, :]
                    m_old = m_ref[hd]
                    m_new = jnp.maximum(m_old, jnp.max(s_ref[hd % 2], axis=0, keepdims=True))
                    m_ref[hd] = m_new
                    p_ref[hd] = jnp.exp2(s_ref[hd % 2] - m_new).astype(BF16)
                    acc_ref[hd] = jnp.exp2(m_old - m_new) * acc_ref[hd] + _dot(vT, p_ref[hd])
            return carry

        lax.fori_loop(0, nkv, step, 0, unroll=8 if stabilised else 1)
        if stabilised:
            pv(last, nkv - 1)

    def running_max():
        m_ref[...] = jnp.full(m_ref.shape, -1e30, F32)
        run(False)

    lax.cond(stabilise, lambda: run(True), running_max)
    for hd in range(n_heads):
        oT_ref[hd * dv:(hd + 1) * dv, :] = acc_ref[hd, :dv] / acc_ref[hd, dv:dv + 1]
    o_ref[0] = _rms(oT_ref[...].T, g_ref[...]).astype(BF16)


def _attention(qT, k, vT, kn, gain, q8T, k8, *, n_heads, k_slots, v_rows, dv, name):
    B, _, S = qT.shape
    kernel = functools.partial(_attn_kernel, n_heads=n_heads, k_slots=k_slots, v_rows=v_rows, dv=dv)
    q_tile = lambda slot: pl.BlockSpec((1, n_heads * slot, Q_TILE), lambda b, i: (b, 0, i))
    whole = lambda shape: pl.BlockSpec((1,) + shape[1:], lambda b, i: (b,) + (0,) * (len(shape) - 1),
                                       pipeline_mode=pl.Buffered(1))
    return pl.pallas_call(
        kernel, grid=(B, S // Q_TILE),
        in_specs=[q_tile(HEAD_SLOT), whole(k.shape), whole(vT.shape), whole(kn.shape), _const_spec(gain),
                  q_tile(Q8_SLOT), whole(k8.shape)],
        out_specs=pl.BlockSpec((1, Q_TILE, n_heads * dv), lambda b, i: (b, i, 0)),
        out_shape=jax.ShapeDtypeStruct((B, S, n_heads * dv), BF16),
        scratch_shapes=[pltpu.VMEM((2, KV_TILE, Q_TILE), F32), pltpu.VMEM((n_heads, KV_TILE, Q_TILE), BF16),
                        pltpu.VMEM((n_heads, 1, Q_TILE), F32), pltpu.VMEM((n_heads, dv + V_PAD, Q_TILE), F32),
                        pltpu.VMEM((n_heads * dv, Q_TILE), F32)],
        compiler_params=_params(2), name=name,
    )(qT, k, vT, kn, _operand(gain), q8T, k8)


def _mem_kv_kernel(mem_ref, g_ref, w_ref, kT_ref, v_ref):
    kv = _dot(_rms(mem_ref[0], g_ref[...]).astype(BF16), w_ref[...])
    width = MEM_HEADS * MEM_DIM
    kT_ref[0] = kv[:, :width].T.astype(BF16)
    ones = jnp.ones((kv.shape[0], MEM_DIM), BF16)
    for hd in range(MEM_HEADS):
        v_ref[0, hd, :, :MEM_DIM] = kv[:, width + hd * MEM_DIM:width + (hd + 1) * MEM_DIM].astype(BF16)
        v_ref[0, hd, :, MEM_DIM:] = ones


def _mem_kv(mem, gain, w_kv):
    B, Tm, D = mem.shape
    width = MEM_HEADS * MEM_DIM
    return pl.pallas_call(
        _mem_kv_kernel, grid=(B,),
        in_specs=[pl.BlockSpec((1, Tm, D), lambda b: (b, 0, 0)), _const_spec(gain), _const_spec(w_kv)],
        out_specs=[pl.BlockSpec((1, width, Tm), lambda b: (b, 0, 0)),
                   pl.BlockSpec((1, MEM_HEADS, Tm, 2 * MEM_DIM), lambda b: (b, 0, 0, 0))],
        out_shape=[jax.ShapeDtypeStruct((B, width, Tm), BF16),
                   jax.ShapeDtypeStruct((B, MEM_HEADS, Tm, 2 * MEM_DIM), BF16)],
        compiler_params=_params(1), name="mem_kv",
    )(mem, _operand(gain), _operand(w_kv))


def _out_mem_kernel(ya_ref, yb_ref, yc_ref, x_ref, wout_ref, g_ref, wq_ref, kT_ref, v_ref, wo_ref, o_ref):
    y = jnp.concatenate([ya_ref[0], yb_ref[0], yc_ref[0]], axis=1)
    x1 = x_ref[0] + _dot(y, wout_ref[...])
    h = _rms(x1, g_ref[...]).astype(BF16)
    q = (_dot(h, wq_ref[...]) * (MEM_DIM ** -0.5)).astype(BF16)
    heads = []
    for hd in range(MEM_HEADS):
        s = _dot(q[:, hd * MEM_DIM:(hd + 1) * MEM_DIM], kT_ref[0, hd * MEM_DIM:(hd + 1) * MEM_DIM, :])
        p = jnp.exp(s - jnp.max(s, axis=-1, keepdims=True)).astype(BF16)
        pv = _dot(p, v_ref[0, hd])
        heads.append(pv[:, :MEM_DIM] / pv[:, MEM_DIM:])
    o = jnp.concatenate(heads, axis=1).astype(BF16)
    o_ref[0] = x1 + _dot(o, wo_ref[...])


def _out_mem(ya, yb, yc, x, p, mem_kT, mem_v):
    B, S, D = x.shape
    T = WIDE_TILE
    tile = lambda w: pl.BlockSpec((1, T, w), lambda b, s: (b, s, 0))
    per_b = lambda a: pl.BlockSpec((1,) + a.shape[1:], lambda b, s: (b,) + (0,) * (a.ndim - 1))
    consts = [p["w_out"], p["mem_x_norm"], p["mem_w_q"]]
    return pl.pallas_call(
        _out_mem_kernel, grid=(B, S // T),
        in_specs=[tile(W_A), tile(W_B), tile(W_C), tile(D)] + [_const_spec(c) for c in consts]
                 + [per_b(mem_kT), per_b(mem_v), _const_spec(p["mem_w_o"])],
        out_specs=tile(D), out_shape=jax.ShapeDtypeStruct((B, S, D), F32),
        compiler_params=_params(2), name="out_mem",
    )(ya, yb, yc, x, *map(_operand, consts), mem_kT, mem_v, _operand(p["mem_w_o"]))


def _ffn_kernel(x_ref, xp_ref, xn_ref, g_ref, wup_ref, cw_ref, cb_ref, wdn_ref, fg_ref, o_ref,
                h_ref, act_ref, *, final):
    tok = x_ref.shape[1]
    i = pl.program_id(1)
    g = g_ref[...]
    x = x_ref[0]
    keep_prev = (i > 0).astype(F32)
    keep_next = (i < pl.num_programs(1) - 1).astype(F32)
    h_ref[0:HALO] = (_rms(xp_ref[0], g) * keep_prev).astype(BF16)
    h_ref[HALO:HALO + tok] = _rms(x, g).astype(BF16)
    h_ref[HALO + tok:] = (_rms(xn_ref[0], g) * keep_next).astype(BF16)
    hext = h_ref[...]
    rows = tok + 2 * HALO

    def conv_up(cols):
        a = _dot(hext, wup_ref[:, cols])
        w = cw_ref[:, cols]
        return (pltpu.roll(a, 1, 0)[HALO:HALO + tok] * w[0:1] + a[HALO:HALO + tok] * w[1:2]
                + pltpu.roll(a, rows - 1, 0)[HALO:HALO + tok] * w[2:3] + cb_ref[:, cols])

    for c in range(D_FF // FF_CHUNK):
        gate = conv_up(slice(FF_CHUNK * c, FF_CHUNK * (c + 1)))
        val = conv_up(slice(D_FF + FF_CHUNK * c, D_FF + FF_CHUNK * (c + 1)))
        act_ref[:, FF_CHUNK * c:FF_CHUNK * (c + 1)] = (jax.nn.silu(gate) * val).astype(BF16)
    y = x + _dot(act_ref[...], wdn_ref[...])
    if final:
        y = _rms(y, fg_ref[...])
    o_ref[0] = y


def _ffn(x, p, final_gain, *, final):
    B, S, D = x.shape
    T = WIDE_TILE
    per_tile = T // HALO
    n_halo = S // HALO
    tile = pl.BlockSpec((1, T, D), lambda b, s: (b, s, 0))
    prev = pl.BlockSpec((1, HALO, D), lambda b, s: (b, jnp.maximum(s * per_tile - 1, 0), 0))
    nxt = pl.BlockSpec((1, HALO, D), lambda b, s: (b, jnp.minimum((s + 1) * per_tile, n_halo - 1), 0))
    consts = [p["ffn_norm"], p["ffn_w_up"], p["ffn_conv_w"], p["ffn_conv_b"], p["ffn_w_down"], final_gain]
    return pl.pallas_call(
        functools.partial(_ffn_kernel, final=final), grid=(B, S // T),
        in_specs=[tile, prev, nxt] + [_const_spec(c) for c in consts],
        out_specs=tile, out_shape=jax.ShapeDtypeStruct((B, S, D), F32),
        scratch_shapes=[pltpu.VMEM((T + 2 * HALO, D), BF16), pltpu.VMEM((T, D_FF), BF16)],
        compiler_params=_params(2), name="ffn_final" if final else "ffn",
    )(x, x, x, *map(_operand, consts))


def _swap_pairs(w):
    n = w.shape[-1]
    return w.reshape(*w.shape[:-1], n // 2, 2)[..., ::-1].reshape(w.shape)


def _rope_tables(S, d_rot):
    rows = S // GRID_W
    row = jnp.repeat(jnp.arange(rows, dtype=F32), GRID_W)
    col = jnp.tile(jnp.arange(GRID_W, dtype=F32), rows)
    n = d_rot // 4
    inv = ROPE_THETA ** (-jnp.arange(n, dtype=F32) / n)
    ang = jnp.concatenate([row[:, None] * inv, col[:, None] * inv], axis=-1)
    cos, sin = jnp.cos(ang), jnp.sin(ang)
    c = jnp.repeat(cos, 2, axis=-1)
    s = jnp.stack([-sin, sin], axis=-1).reshape(S, d_rot)
    return c, s


def _prep_params(S, P):
    L = P["w_in"].shape[0]
    row = lambda v: v[:, None, :]
    ca, sa = _rope_tables(S, MLA_ROPE)
    cb, sb = _rope_tables(S, GQA_DIM)
    p, shared = {}, {}
    c_q, c_kv, k_rope, g_q, g_k, g_v, g_m = jnp.split(P["w_in"], [256, 384, 416, 800, 928, 1056], axis=2)
    pad = jnp.zeros((L, D_MODEL, Z_END - Z_KR - 2 * MLA_ROPE), F32)
    p["w_in"] = jnp.concatenate(
        [c_q, c_kv, g_q, _swap_pairs(g_q), g_k, _swap_pairs(g_k), g_v, g_m, k_rope, _swap_pairs(k_rope), pad],
        axis=2).astype(BF16)
    for name in ("mix_norm", "mla_q_norm", "mla_kv_norm", "gmlp_v_norm", "mem_x_norm", "mem_kv_norm", "ffn_norm",
                 "ffn_conv_b"):
        p[name] = row(P[name])

    w_uq = P["mla_w_uq"].reshape(L, MLA_Q_RANK, MLA_HEADS, MLA_NOPE + MLA_ROPE)
    zpad = jnp.zeros((L, MLA_Q_RANK, MLA_HEADS, HEAD_SLOT - MLA_NOPE - MLA_ROPE), F32)
    main = jnp.concatenate([w_uq, zpad], axis=-1)
    swapped = jnp.concatenate([jnp.zeros_like(w_uq[..., :MLA_NOPE]), _swap_pairs(w_uq[..., MLA_NOPE:]), zpad], axis=-1)
    p["w_uq"] = jnp.concatenate([main.reshape(L, MLA_Q_RANK, -1), swapped.reshape(L, MLA_Q_RANK, -1)],
                                axis=2).astype(BF16)
    scale_a = (MLA_NOPE + MLA_ROPE) ** -0.5 * LOG2E
    ones = jnp.ones((S, MLA_NOPE), F32)
    zeros_n = jnp.zeros((S, MLA_NOPE), F32)
    zeros_p = jnp.zeros((S, HEAD_SLOT - MLA_NOPE - MLA_ROPE), F32)
    shared["cqa"] = jnp.tile(jnp.concatenate([ones, ca, zeros_p], axis=1) * scale_a, (1, MLA_HEADS))
    shared["sqa"] = jnp.tile(jnp.concatenate([zeros_n, sa, zeros_p], axis=1) * scale_a, (1, MLA_HEADS))

    w_ukv = P["mla_w_ukv"].reshape(L, MLA_KV_RANK, MLA_HEADS, MLA_NOPE + MLA_V)
    k_lat = jnp.concatenate(
        [w_ukv[..., :MLA_NOPE], jnp.zeros((L, MLA_KV_RANK, MLA_HEADS, HEAD_SLOT - MLA_NOPE), F32)], axis=-1)
    place = jnp.concatenate([jnp.zeros((MLA_ROPE, MLA_NOPE), F32), jnp.eye(MLA_ROPE, dtype=F32),
                             jnp.zeros((MLA_ROPE, HEAD_SLOT - MLA_NOPE - MLA_ROPE), F32)], axis=1)
    place = jnp.broadcast_to(jnp.tile(place, (1, MLA_HEADS)), (L, MLA_ROPE, MLA_HEADS * HEAD_SLOT))
    zrows = jnp.zeros((L, Z_END - Z_KR - 2 * MLA_ROPE, MLA_HEADS * HEAD_SLOT), F32)
    p["w_k"] = jnp.concatenate([k_lat.reshape(L, MLA_KV_RANK, -1), place, place, zrows], axis=1).astype(BF16)
    p["w_v"] = w_ukv[..., MLA_NOPE:].reshape(L, MLA_KV_RANK, -1).astype(BF16)
    nope_w = w_ukv[..., :MLA_NOPE]
    lat8 = jnp.concatenate([nope_w, nope_w, nope_w, jnp.zeros((L, MLA_KV_RANK, MLA_HEADS, 2 * MLA_ROPE), F32)], axis=-1)
    place8 = jnp.concatenate([jnp.zeros((MLA_ROPE, 3 * MLA_NOPE), F32), jnp.eye(MLA_ROPE, dtype=F32),
                              jnp.eye(MLA_ROPE, dtype=F32)], axis=1)
    place8 = jnp.broadcast_to(jnp.tile(place8, (1, MLA_HEADS)), (L, MLA_ROPE, MLA_HEADS * Q8_SLOT))
    zrows8 = jnp.zeros((L, Z_END - Z_KR - 2 * MLA_ROPE, MLA_HEADS * Q8_SLOT), F32)
    p["w_k8"] = (jnp.concatenate([lat8.reshape(L, MLA_KV_RANK, -1), place8, place8, zrows8], axis=1)
                 * K8_SCALE).astype(BF16)
    shared["tk"] = jnp.concatenate([ca, sa, jnp.zeros((S, Z_END - Z_KR - 2 * MLA_ROPE), F32)], axis=1)

    gq = row(jnp.tile(P["gqa_q_norm"], (1, GQA_HEADS)) * (GQA_DIM ** -0.5 * LOG2E))
    gk = row(jnp.tile(P["gqa_k_norm"], (1, GQA_KV_HEADS)))
    p["tcq"] = jnp.tile(cb, (1, GQA_HEADS)) * gq
    p["tsq"] = jnp.tile(sb, (1, GQA_HEADS)) * _swap_pairs(gq)
    p["tck"] = jnp.tile(cb, (1, GQA_KV_HEADS)) * gk
    p["tsk"] = jnp.tile(sb, (1, GQA_KV_HEADS)) * _swap_pairs(gk)
    grp = jnp.arange(W_B) // GQA_DIM
    shared["bd"] = (grp[:, None] == grp[None, :]).astype(BF16)

    p["w_s"] = P["gmlp_w_s"].reshape(L, GMLP_GROUPS * GMLP_CHUNK, GMLP_CHUNK).astype(BF16)
    p["bias_s"] = jnp.repeat(jnp.swapaxes(P["gmlp_b_s"], 1, 2), GMLP_DIM, axis=2)
    p["out_norm_a"] = row(P["out_norm"][:, :W_A])
    p["out_norm_b"] = row(P["out_norm"][:, W_A:W_A + W_B])
    p["out_norm_c"] = row(P["out_norm"][:, W_A + W_B:])
    for name in ("w_out", "mem_w_q", "mem_w_kv", "mem_w_o", "ffn_w_up", "ffn_w_down"):
        p[name] = P[name].astype(BF16)
    p["ffn_conv_w"] = P["ffn_conv_w"]
    return p, shared


def kernel(x, mem, mix_norm, w_in, mla_q_norm, mla_w_uq, mla_kv_norm, mla_w_ukv, gqa_q_norm, gqa_k_norm, gmlp_v_norm, gmlp_w_s, gmlp_b_s, out_norm, w_out, mem_x_norm, mem_kv_norm, mem_w_q, mem_w_kv, mem_w_o, ffn_norm, ffn_w_up, ffn_conv_w, ffn_conv_b, ffn_w_down, final_norm):
    P = dict(mix_norm=mix_norm, w_in=w_in, mla_q_norm=mla_q_norm, mla_w_uq=mla_w_uq, mla_kv_norm=mla_kv_norm,
             mla_w_ukv=mla_w_ukv, gqa_q_norm=gqa_q_norm, gqa_k_norm=gqa_k_norm, gmlp_v_norm=gmlp_v_norm,
             gmlp_w_s=gmlp_w_s, gmlp_b_s=gmlp_b_s, out_norm=out_norm, w_out=w_out, mem_x_norm=mem_x_norm,
             mem_kv_norm=mem_kv_norm, mem_w_q=mem_w_q, mem_w_kv=mem_w_kv, mem_w_o=mem_w_o, ffn_norm=ffn_norm,
             ffn_w_up=ffn_w_up, ffn_conv_w=ffn_conv_w, ffn_conv_b=ffn_conv_b, ffn_w_down=ffn_w_down)
    B, S, D = x.shape
    assert D == D_MODEL and S % TOK_TILE == 0 and S % WIDE_TILE == 0 and S % GRID_W == 0
    depth = w_in.shape[0]
    stacked, shared = _prep_params(S, P)
    final_gain = final_norm.reshape(1, -1)
    group = GQA_HEADS // GQA_KV_HEADS
    for l in range(depth):
        p = dict(shared, **{name: _Layer(v, l) for name, v in stacked.items()})
        qaT, ka, vaT, kna, qa8T, ka8, qbT, kb, vbT, knb, qb8T, kb8, ync = _mix_in(x, p)
        yna = _attention(qaT, ka, vaT, kna, p["out_norm_a"], qa8T, ka8, n_heads=MLA_HEADS,
                         k_slots=tuple(range(MLA_HEADS)),
                         v_rows=tuple(h * (MLA_V + V_PAD) for h in range(MLA_HEADS)), dv=MLA_V, name="attn_mla")
        ynb = _attention(qbT, kb, vbT, knb, p["out_norm_b"], qb8T, kb8, n_heads=GQA_HEADS,
                         k_slots=tuple(h // group for h in range(GQA_HEADS)),
                         v_rows=tuple((h // group) * (GQA_DIM + V_PAD) for h in range(GQA_HEADS)), dv=GQA_DIM, name="attn_gqa")
        mem_kT, mem_v = _mem_kv(mem, p["mem_kv_norm"], p["mem_w_kv"])
        x = _out_mem(yna, ynb, ync, x, p, mem_kT, mem_v)
        x = _ffn(x, p, final_gain, final=(l == depth - 1))
    return x
```

```python
import functools
from typing import NamedTuple

import jax
import jax.numpy as jnp
from jax import lax
from jax.experimental import pallas as pl
from jax.experimental.pallas import tpu as pltpu

F32 = jnp.float32
BF16 = jnp.bfloat16
F8 = jnp.float8_e4m3fn

D_MODEL = 1024
GRID_W = 64
ROPE_THETA = 10000.0
EPS = 1e-6
MLA_HEADS = 6
MLA_NOPE = 64
MLA_ROPE = 32
MLA_V = 64
MLA_Q_RANK = 256
MLA_KV_RANK = 128
GQA_HEADS = 6
GQA_KV_HEADS = 2
GQA_DIM = 64
GMLP_GROUPS = 4
GMLP_DIM = 64
GMLP_CHUNK = 128
W_A = MLA_HEADS * MLA_V
W_B = GQA_HEADS * GQA_DIM
W_C = GMLP_GROUPS * GMLP_DIM
MEM_HEADS = 4
MEM_DIM = 128
D_FF = 2816

LANES = 128
HEAD_SLOT = LANES
TOK_TILE = 512
MIX_ROWS = 256
WIDE_TILE = 1024
Q_TILE = 256
KV_TILE = 512
F32_ROWS = 8
BF16_ROWS = 16
V_PAD = BF16_ROWS
LOG2E = 1.4426950408889634
STABILISER_MAX = 50.0
Q8_SLOT = 256
Q8_SCALE = 32.0
K8_SCALE = 8.0
FP8_MAX_SCALED = 400.0
FF_CHUNK = 256
HALO = BF16_ROWS
VMEM_LIMIT = 56 * 1024 * 1024

Z_CQ, Z_CKV, Z_GQ, Z_GQS, Z_GK, Z_GKS, Z_GV, Z_U, Z_VV, Z_KR, Z_END = (
    0, 256, 384, 768, 1152, 1280, 1408, 1536, 1792, 2048, 2176)


def _rms(x, g):
    return x * lax.rsqrt(jnp.mean(x * x, axis=-1, keepdims=True) + EPS) * g


def _dot(a, b):
    return jnp.dot(a, b, preferred_element_type=F32)


class _Layer(NamedTuple):
    stack: jax.Array
    layer: int


def _operand(a):
    return a.stack if isinstance(a, _Layer) else a


def _const_spec(a):
    if isinstance(a, _Layer):
        zeros = (0,) * (a.stack.ndim - 1)
        return pl.BlockSpec((None,) + a.stack.shape[1:], lambda *_: (a.layer,) + zeros, pipeline_mode=pl.Buffered(1))
    zeros = (0,) * a.ndim
    return pl.BlockSpec(a.shape, lambda *_: zeros, pipeline_mode=pl.Buffered(1))


def _params(n_axes):
    return pltpu.CompilerParams(dimension_semantics=("arbitrary",) * n_axes,
                                vmem_limit_bytes=VMEM_LIMIT)


def _group_ssq(v, bd):
    sq = v * v
    hi = sq.astype(BF16)
    lo = (sq - hi.astype(F32)).astype(BF16)
    return _dot(hi, bd) + _dot(lo, bd)


def _split_e4m3(x):
    hi = x.astype(F8)
    return hi, (x - hi.astype(F32)).astype(F8)


def _store_vT(ref, vT, n_heads, dv, tok0):
    width = vT.shape[1]
    assert KV_TILE % width == 0 and tok0 % width == 0
    n, cols = tok0 // KV_TILE, slice(tok0 % KV_TILE, tok0 % KV_TILE + width)
    ext = (lax.broadcasted_iota(jnp.int32, (V_PAD, width), 0) == 0).astype(BF16)
    for hd in range(n_heads):
        base = hd * (dv + V_PAD)
        ref[0, n, base:base + dv, cols] = vT[hd * dv:(hd + 1) * dv].astype(BF16)
        ref[0, n, base + dv:base + dv + V_PAD, cols] = ext


def _key_norms(k, n_slots):
    rows = []
    for n in range(n_slots):
        x = k[:, n * HEAD_SLOT:(n + 1) * HEAD_SLOT].astype(F32)
        n2 = jnp.max(jnp.sum(x * x, axis=-1, keepdims=True), axis=0, keepdims=True)
        rows.append(jnp.broadcast_to(n2, (1, LANES)))
    rows.append(jnp.zeros((F32_ROWS - n_slots, LANES), F32))
    return jnp.concatenate(rows, axis=0)


def _mix_in_kernel(x_ref, g_ref, win_ref, gq_ref, wq_ref, gkv_ref, wk_ref, wk8_ref, wv_ref,
                   cqa_ref, sqa_ref, tk_ref, tcq_ref, tsq_ref, tck_ref, tsk_ref,
                   bd_ref, gv_ref, ws_ref, bias_ref, gc_ref,
                   qaT_ref, ka_ref, vaT_ref, kna_ref, qa8T_ref, ka8_ref,
                   qbT_ref, kb_ref, vbT_ref, knb_ref, qb8T_ref, kb8_ref, ync_ref):
    norms_a, norms_b = [], []
    for r0 in range(0, x_ref.shape[1], MIX_ROWS):
        rows_ = slice(r0, r0 + MIX_ROWS)
        _mix_in_rows(r0, x_ref[0, rows_], g_ref, win_ref, gq_ref, wq_ref, gkv_ref, wk_ref, wk8_ref, wv_ref,
                     [t[rows_] for t in (cqa_ref, sqa_ref, tk_ref, tcq_ref, tsq_ref, tck_ref, tsk_ref)],
                     bd_ref, gv_ref, ws_ref, bias_ref, gc_ref,
                     qaT_ref, ka_ref, vaT_ref, norms_a, qa8T_ref, ka8_ref,
                     qbT_ref, kb_ref, vbT_ref, norms_b, qb8T_ref, kb8_ref, ync_ref)
    kna_ref[0, 0] = functools.reduce(jnp.maximum, norms_a)
    knb_ref[0, 0] = functools.reduce(jnp.maximum, norms_b)


def _mix_in_rows(r0, x, g_ref, win_ref, gq_ref, wq_ref, gkv_ref, wk_ref, wk8_ref, wv_ref, tables,
                 bd_ref, gv_ref, ws_ref, bias_ref, gc_ref,
                 qaT_ref, ka_ref, vaT_ref, norms_a, qa8T_ref, ka8_ref,
                 qbT_ref, kb_ref, vbT_ref, norms_b, qb8T_ref, kb8_ref, ync_ref):
    cqa, sqa, tk, tcq, tsq, tck, tsk = tables
    tok = x.shape[0]
    rows_ = slice(r0, r0 + tok)
    h = _rms(x, g_ref[...])
    z = _dot(h.astype(BF16), win_ref[...])

    cq = _rms(z[:, Z_CQ:Z_CKV], gq_ref[...]).astype(BF16)
    qa = _dot(cq, wq_ref[...])
    half = MLA_HEADS * HEAD_SLOT
    q_a = qa[:, :half] * cqa + qa[:, half:] * sqa
    q_aT = q_a.T
    qaT_ref[0, :, rows_] = q_aT.astype(BF16)
    q_hi, q_lo = _split_e4m3(q_aT * Q8_SCALE)
    for hd in range(MLA_HEADS):
        nope = slice(hd * HEAD_SLOT, hd * HEAD_SLOT + MLA_NOPE)
        rope = slice(hd * HEAD_SLOT + MLA_NOPE, hd * HEAD_SLOT + MLA_NOPE + MLA_ROPE)
        base = hd * Q8_SLOT
        qa8T_ref[0, base:base + MLA_NOPE, rows_] = q_hi[nope]
        qa8T_ref[0, base + MLA_NOPE:base + 2 * MLA_NOPE, rows_] = q_hi[nope]
        qa8T_ref[0, base + 2 * MLA_NOPE:base + 3 * MLA_NOPE, rows_] = q_lo[nope]
        qa8T_ref[0, base + 3 * MLA_NOPE:base + 3 * MLA_NOPE + MLA_ROPE, rows_] = q_hi[rope]
        qa8T_ref[0, base + 3 * MLA_NOPE + MLA_ROPE:base + Q8_SLOT, rows_] = q_hi[rope]

    ckv = _rms(z[:, Z_CKV:Z_GQ], gkv_ref[...])
    kr = z[:, Z_KR:Z_END] * tk
    lhs = jnp.concatenate([ckv, kr], axis=1).astype(BF16)
    k_a = _dot(lhs, wk_ref[...]).astype(BF16)
    ka_ref[0, rows_] = k_a
    norms_a.append(_key_norms(k_a, MLA_HEADS))
    k8 = _dot(lhs, wk8_ref[...])
    depth = lax.broadcasted_iota(jnp.int32, (1, MLA_HEADS * Q8_SLOT), 1) % Q8_SLOT
    lo_block = jnp.logical_or(jnp.logical_and(depth >= MLA_NOPE, depth < 2 * MLA_NOPE),
                              depth >= 3 * MLA_NOPE + MLA_ROPE)
    ka8_ref[0, rows_] = jnp.where(lo_block, k8 - k8.astype(F8).astype(F32), k8).astype(F8)
    _store_vT(vaT_ref, _dot(lhs[:, :MLA_KV_RANK], wv_ref[...]).T, MLA_HEADS, MLA_V, r0)

    bd = bd_ref[...]
    g_q = z[:, Z_GQ:Z_GQS]
    n_q = lax.rsqrt(_group_ssq(g_q, bd) * (1.0 / GQA_DIM) + EPS)
    q_b = (g_q * tcq + z[:, Z_GQS:Z_GK] * tsq) * n_q
    q_bT = q_b.T
    zero = jnp.zeros((HEAD_SLOT - GQA_DIM, tok), BF16)
    for hd in range(GQA_HEADS):
        qbT_ref[0, hd * HEAD_SLOT:hd * HEAD_SLOT + GQA_DIM, rows_] = q_bT[hd * GQA_DIM:(hd + 1) * GQA_DIM].astype(BF16)
        qbT_ref[0, hd * HEAD_SLOT + GQA_DIM:(hd + 1) * HEAD_SLOT, rows_] = zero
    q_hi, q_lo = _split_e4m3(q_bT * Q8_SCALE)
    zero8 = jnp.zeros((Q8_SLOT - 3 * GQA_DIM, tok), F8)
    for hd in range(GQA_HEADS):
        rows = slice(hd * GQA_DIM, (hd + 1) * GQA_DIM)
        base = hd * Q8_SLOT
        qb8T_ref[0, base:base + GQA_DIM, rows_] = q_hi[rows]
        qb8T_ref[0, base + GQA_DIM:base + 2 * GQA_DIM, rows_] = q_hi[rows]
        qb8T_ref[0, base + 2 * GQA_DIM:base + 3 * GQA_DIM, rows_] = q_lo[rows]
        qb8T_ref[0, base + 3 * GQA_DIM:base + Q8_SLOT, rows_] = zero8
    g_k = z[:, Z_GK:Z_GKS]
    n_k = lax.rsqrt(_group_ssq(g_k, bd[:LANES, :LANES]) * (1.0 / GQA_DIM) + EPS)
    k_b = (g_k * tck + z[:, Z_GKS:Z_GV] * tsk) * n_k
    low = lax.broadcasted_iota(jnp.int32, (tok, LANES), 1) < GQA_DIM
    slots = [jnp.where(low, k_b, 0.0), jnp.where(low, pltpu.roll(k_b, GQA_DIM, 1), 0.0)]
    blocks = []
    for k_slot in slots:
        k8 = k_slot * K8_SCALE
        k_lo = k8 - k8.astype(F8).astype(F32)
        blocks += [jnp.where(low, k8, pltpu.roll(k_lo, GQA_DIM, 1)), k8]
    kb8_ref[0, rows_] = jnp.concatenate(blocks, axis=1).astype(F8)
    k_b = jnp.concatenate(slots, axis=1).astype(BF16)
    kb_ref[0, rows_] = k_b
    norms_b.append(_key_norms(k_b, GQA_KV_HEADS))
    _store_vT(vbT_ref, z[:, Z_GV:Z_U].T, GQA_KV_HEADS, GQA_DIM, r0)

    gm = jax.nn.gelu(z[:, Z_U:Z_KR])
    u = gm[:, :W_C]
    vv = _rms(gm[:, W_C:], gv_ref[...])
    lane_grp = lax.broadcasted_iota(jnp.int32, (GMLP_CHUNK, W_C), 1) // GMLP_DIM
    ws = ws_ref[...]
    bias = bias_ref[...]
    ycs = []
    for n in range(tok // GMLP_CHUNK):
        rows = slice(n * GMLP_CHUNK, (n + 1) * GMLP_CHUNK)
        r = _dot(ws, vv[rows].astype(BF16))
        mixed = r[3 * GMLP_CHUNK:]
        for grp in range(GMLP_GROUPS - 2, -1, -1):
            mixed = jnp.where(lane_grp == grp, r[grp * GMLP_CHUNK:(grp + 1) * GMLP_CHUNK], mixed)
        ycs.append(u[rows] * (mixed + bias))
    ync_ref[0, rows_] = _rms(jnp.concatenate(ycs, axis=0), gc_ref[...]).astype(BF16)


def _mix_in(x, p):
    B, S, D = x.shape
    T = TOK_TILE
    nkv = T // KV_TILE
    tile = lambda w: pl.BlockSpec((1, T, w), lambda s, b: (b, s, 0))
    def tab(t):
        if isinstance(t, _Layer):
            return pl.BlockSpec((None, T, t.stack.shape[2]), lambda s, b: (t.layer, s, 0))
        return pl.BlockSpec((T, t.shape[1]), lambda s, b: (s, 0))
    tposed = lambda r: pl.BlockSpec((1, r, T), lambda s, b: (b, 0, s))
    blocked = lambda r: pl.BlockSpec((1, nkv, r, KV_TILE), lambda s, b: (b, s, 0, 0))
    consts = [p["mix_norm"], p["w_in"], p["mla_q_norm"], p["w_uq"], p["mla_kv_norm"], p["w_k"], p["w_k8"], p["w_v"]]
    tabs = [p["cqa"], p["sqa"], p["tk"], p["tcq"], p["tsq"], p["tck"], p["tsk"]]
    consts2 = [p["bd"], p["gmlp_v_norm"], p["w_s"], p["bias_s"], p["out_norm_c"]]
    in_specs = ([tile(D)] + [_const_spec(c) for c in consts] + [tab(t) for t in tabs]
                + [_const_spec(c) for c in consts2])
    out_shape = [
        jax.ShapeDtypeStruct((B, MLA_HEADS * HEAD_SLOT, S), BF16),
        jax.ShapeDtypeStruct((B, S, MLA_HEADS * HEAD_SLOT), BF16),
        jax.ShapeDtypeStruct((B, S // KV_TILE, MLA_HEADS * (MLA_V + V_PAD), KV_TILE), BF16),
        jax.ShapeDtypeStruct((B, S // T, F32_ROWS, LANES), F32),
        jax.ShapeDtypeStruct((B, MLA_HEADS * Q8_SLOT, S), F8),
        jax.ShapeDtypeStruct((B, S, MLA_HEADS * Q8_SLOT), F8),
        jax.ShapeDtypeStruct((B, GQA_HEADS * HEAD_SLOT, S), BF16),
        jax.ShapeDtypeStruct((B, S, GQA_KV_HEADS * HEAD_SLOT), BF16),
        jax.ShapeDtypeStruct((B, S // KV_TILE, GQA_KV_HEADS * (GQA_DIM + V_PAD), KV_TILE), BF16),
        jax.ShapeDtypeStruct((B, S // T, F32_ROWS, LANES), F32),
        jax.ShapeDtypeStruct((B, GQA_HEADS * Q8_SLOT, S), F8),
        jax.ShapeDtypeStruct((B, S, GQA_KV_HEADS * Q8_SLOT), F8),
        jax.ShapeDtypeStruct((B, S, W_C), BF16),
    ]
    norms = pl.BlockSpec((1, 1, F32_ROWS, LANES), lambda s, b: (b, s, 0, 0))
    out_specs = [tposed(MLA_HEADS * HEAD_SLOT), tile(MLA_HEADS * HEAD_SLOT), blocked(MLA_HEADS * (MLA_V + V_PAD)),
                 norms, tposed(MLA_HEADS * Q8_SLOT), tile(MLA_HEADS * Q8_SLOT),
                 tposed(GQA_HEADS * HEAD_SLOT), tile(GQA_KV_HEADS * HEAD_SLOT),
                 blocked(GQA_KV_HEADS * (GQA_DIM + V_PAD)), norms, tposed(GQA_HEADS * Q8_SLOT),
                 tile(GQA_KV_HEADS * Q8_SLOT), tile(W_C)]
    return pl.pallas_call(
        _mix_in_kernel, grid=(S // T, B), in_specs=in_specs, out_specs=out_specs, out_shape=out_shape,
        compiler_params=_params(2), name="mix_in",
    )(x, *map(_operand, consts + tabs + consts2))


def _attn_kernel(q_ref, k_ref, v_ref, kn_ref, g_ref, q8_ref, k8_ref, o_ref,
                 s_ref, p_ref, m_ref, acc_ref, oT_ref, *, n_heads, k_slots, v_rows, dv):
    nkv = v_ref.shape[1]
    assert n_heads % 2 == 0

    kmax = jnp.max(kn_ref[0], axis=0)
    bound = jnp.zeros((1, 1), F32)
    qn2_max = jnp.zeros((1, 1), F32)
    kn2_max = jnp.zeros((1, 1), F32)
    for hd in range(n_heads):
        qf = q_ref[0, hd * HEAD_SLOT:(hd + 1) * HEAD_SLOT, :].astype(F32)
        qn2 = jnp.sum(qf * qf, axis=0, keepdims=True)
        kn2 = kmax[k_slots[hd]:k_slots[hd] + 1, 0:1]
        u = jnp.sqrt(qn2 * kn2)
        m_ref[hd] = u
        bound = jnp.maximum(bound, jnp.max(u, axis=1, keepdims=True))
        qn2_max = jnp.maximum(qn2_max, jnp.max(qn2, axis=1, keepdims=True))
        kn2_max = jnp.maximum(kn2_max, kn2)
    stabilise = jnp.logical_and(jnp.max(bound) <= STABILISER_MAX, jnp.logical_and(
        jnp.max(qn2_max) <= (FP8_MAX_SCALED / Q8_SCALE) ** 2, jnp.max(kn2_max) <= (FP8_MAX_SCALED / K8_SCALE) ** 2))
    acc_ref[...] = jnp.zeros(acc_ref.shape, F32)

    def run(stabilised):
        def scores(hd, off):
            ks, qs, depth = (k8_ref, q8_ref, Q8_SLOT) if stabilised else (k_ref, q_ref, HEAD_SLOT)
            k = ks[0, pl.ds(off, KV_TILE), k_slots[hd] * depth:(k_slots[hd] + 1) * depth]
            qT = qs[0, hd * depth:(hd + 1) * depth, :]
            s_ref[hd % 2] = _dot(k, qT)

        scores(0, 0)
        last = n_heads - 1

        def pv(hd, j):
            vT = v_ref[0, j, v_rows[hd]:v_rows[hd] + dv + V_PAD, :]
            acc_ref[hd] += _dot(vT, p_ref[hd])

        if stabilised:
            p_ref[last] = jnp.zeros(p_ref.shape[1:], BF16)

        def step(j, carry):
            off = pl.multiple_of(j * KV_TILE, KV_TILE)
            off_next = pl.multiple_of(jnp.minimum(j + 1, nkv - 1) * KV_TILE, KV_TILE)
            for hd in range(n_heads):
                if hd + 1 < n_heads:
                    scores(hd + 1, off)
                else:
                    scores(0, off_next)
                if stabilised:
                    if hd > 0:
                        pv(hd - 1, j)
                    else:
                        pv(last, jnp.maximum(j - 1, 0))
                    s = s_ref[hd % 2] * (1.0 / (Q8_SCALE * K8_SCALE)) - m_ref[hd]
                    p_ref[hd] = jnp.exp2(s.astype(BF16))
                else:
                    vT = v_ref[0, j, v_rows[hd]:v_rows[hd] + dv + V_PAD, :]
                    m_old = m_ref[hd]
                    m_new = jnp.maximum(m_old, jnp.max(s_ref[hd % 2], axis=0, keepdims=True))
                    m_ref[hd] = m_new
                    p_ref[hd] = jnp.exp2(s_ref[hd % 2] - m_new).astype(BF16)
                    acc_ref[hd] = jnp.exp2(m_old - m_new) * acc_ref[hd] + _dot(vT, p_ref[hd])
            return carry

        lax.fori_loop(0, nkv, step, 0, unroll=16 if stabilised else 1)
        if stabilised:
            pv(last, nkv - 1)

    def running_max():
        m_ref[...] = jnp.full(m_ref.shape, -1e30, F32)
        run(False)

    lax.cond(stabilise, lambda: run(True), running_max)
    for hd in range(n_heads):
        oT_ref[hd * dv:(hd + 1) * dv, :] = acc_ref[hd, :dv] / acc_ref[hd, dv:dv + 1]
    o_ref[0] = _rms(oT_ref[...].T, g_ref[...]).astype(BF16)


def _attention(qT, k, vT, kn, gain, q8T, k8, *, n_heads, k_slots, v_rows, dv, name):
    B, _, S = qT.shape
    kernel = functools.partial(_attn_kernel, n_heads=n_heads, k_slots=k_slots, v_rows=v_rows, dv=dv)
    q_tile = lambda slot: pl.BlockSpec((1, n_heads * slot, Q_TILE), lambda b, i: (b, 0, i))
    whole = lambda shape: pl.BlockSpec((1,) + shape[1:], lambda b, i: (b,) + (0,) * (len(shape) - 1),
                                       pipeline_mode=pl.Buffered(1))
    return pl.pallas_call(
        kernel, grid=(B, S // Q_TILE),
        in_specs=[q_tile(HEAD_SLOT), whole(k.shape), whole(vT.shape), whole(kn.shape), _const_spec(gain),
                  q_tile(Q8_SLOT), whole(k8.shape)],
        out_specs=pl.BlockSpec((1, Q_TILE, n_heads * dv), lambda b, i: (b, i, 0)),
        out_shape=jax.ShapeDtypeStruct((B, S, n_heads * dv), BF16),
        scratch_shapes=[pltpu.VMEM((2, KV_TILE, Q_TILE), F32), pltpu.VMEM((n_heads, KV_TILE, Q_TILE), BF16),
                        pltpu.VMEM((n_heads, 1, Q_TILE), F32), pltpu.VMEM((n_heads, dv + V_PAD, Q_TILE), F32),
                        pltpu.VMEM((n_heads * dv, Q_TILE), F32)],
        compiler_params=_params(2), name=name,
    )(qT, k, vT, kn, _operand(gain), q8T, k8)


def _mem_kv_kernel(mem_ref, g_ref, w_ref, kT_ref, v_ref):
    kv = _dot(_rms(mem_ref[0], g_ref[...]).astype(BF16), w_ref[...])
    width = MEM_HEADS * MEM_DIM
    kT_ref[0] = kv[:, :width].T.astype(BF16)
    ones = jnp.ones((kv.shape[0], MEM_DIM), BF16)
    for hd in range(MEM_HEADS):
        v_ref[0, hd, :, :MEM_DIM] = kv[:, width + hd * MEM_DIM:width + (hd + 1) * MEM_DIM].astype(BF16)
        v_ref[0, hd, :, MEM_DIM:] = ones


def _mem_kv(mem, gain, w_kv):
    B, Tm, D = mem.shape
    width = MEM_HEADS * MEM_DIM
    return pl.pallas_call(
        _mem_kv_kernel, grid=(B,),
        in_specs=[pl.BlockSpec((1, Tm, D), lambda b: (b, 0, 0)), _const_spec(gain), _const_spec(w_kv)],
        out_specs=[pl.BlockSpec((1, width, Tm), lambda b: (b, 0, 0)),
                   pl.BlockSpec((1, MEM_HEADS, Tm, 2 * MEM_DIM), lambda b: (b, 0, 0, 0))],
        out_shape=[jax.ShapeDtypeStruct((B, width, Tm), BF16),
                   jax.ShapeDtypeStruct((B, MEM_HEADS, Tm, 2 * MEM_DIM), BF16)],
        compiler_params=_params(1), name="mem_kv",
    )(mem, _operand(gain), _operand(w_kv))


def _out_mem_kernel(ya_ref, yb_ref, yc_ref, x_ref, wout_ref, g_ref, wq_ref, kT_ref, v_ref, wo_ref, o_ref):
    y = jnp.concatenate([ya_ref[0], yb_ref[0], yc_ref[0]], axis=1)
    x1 = x_ref[0] + _dot(y, wout_ref[...])
    h = _rms(x1, g_ref[...]).astype(BF16)
    q = (_dot(h, wq_ref[...]) * (MEM_DIM ** -0.5)).astype(BF16)
    heads = []
    for hd in range(MEM_HEADS):
        s = _dot(q[:, hd * MEM_DIM:(hd + 1) * MEM_DIM], kT_ref[0, hd * MEM_DIM:(hd + 1) * MEM_DIM, :])
        p = jnp.exp(s - jnp.max(s, axis=-1, keepdims=True)).astype(BF16)
        pv = _dot(p, v_ref[0, hd])
        heads.append(pv[:, :MEM_DIM] / pv[:, MEM_DIM:])
    o = jnp.concatenate(heads, axis=1).astype(BF16)
    o_ref[0] = x1 + _dot(o, wo_ref[...])


def _out_mem(ya, yb, yc, x, p, mem_kT, mem_v):
    B, S, D = x.shape
    T = WIDE_TILE
    tile = lambda w: pl.BlockSpec((1, T, w), lambda b, s: (b, s, 0))
    per_b = lambda a: pl.BlockSpec((1,) + a.shape[1:], lambda b, s: (b,) + (0,) * (a.ndim - 1))
    consts = [p["w_out"], p["mem_x_norm"], p["mem_w_q"]]
    return pl.pallas_call(
        _out_mem_kernel, grid=(B, S // T),
        in_specs=[tile(W_A), tile(W_B), tile(W_C), tile(D)] + [_const_spec(c) for c in consts]
                 + [per_b(mem_kT), per_b(mem_v), _const_spec(p["mem_w_o"])],
        out_specs=tile(D), out_shape=jax.ShapeDtypeStruct((B, S, D), F32),
        compiler_params=_params(2), name="out_mem",
    )(ya, yb, yc, x, *map(_operand, consts), mem_kT, mem_v, _operand(p["mem_w_o"]))


def _ffn_kernel(x_ref, xp_ref, xn_ref, g_ref, wup_ref, cw_ref, cb_ref, wdn_ref, fg_ref, o_ref,
                h_ref, act_ref, *, final):
    tok = x_ref.shape[1]
    i = pl.program_id(1)
    g = g_ref[...]
    x = x_ref[0]
    keep_prev = (i > 0).astype(F32)
    keep_next = (i < pl.num_programs(1) - 1).astype(F32)
    h_ref[0:HALO] = (_rms(xp_ref[0], g) * keep_prev).astype(BF16)
    h_ref[HALO:HALO + tok] = _rms(x, g).astype(BF16)
    h_ref[HALO + tok:] = (_rms(xn_ref[0], g) * keep_next).astype(BF16)
    hext = h_ref[...]
    rows = tok + 2 * HALO

    def conv_up(cols):
        a = _dot(hext, wup_ref[:, cols])
        w = cw_ref[:, cols]
        return (pltpu.roll(a, 1, 0)[HALO:HALO + tok] * w[0:1] + a[HALO:HALO + tok] * w[1:2]
                + pltpu.roll(a, rows - 1, 0)[HALO:HALO + tok] * w[2:3] + cb_ref[:, cols])

    for c in range(D_FF // FF_CHUNK):
        gate = conv_up(slice(FF_CHUNK * c, FF_CHUNK * (c + 1)))
        val = conv_up(slice(D_FF + FF_CHUNK * c, D_FF + FF_CHUNK * (c + 1)))
        act_ref[:, FF_CHUNK * c:FF_CHUNK * (c + 1)] = (jax.nn.silu(gate) * val).astype(BF16)
    y = x + _dot(act_ref[...], wdn_ref[...])
    if final:
        y = _rms(y, fg_ref[...])
    o_ref[0] = y


def _ffn(x, p, final_gain, *, final):
    B, S, D = x.shape
    T = WIDE_TILE
    per_tile = T // HALO
    n_halo = S // HALO
    tile = pl.BlockSpec((1, T, D), lambda b, s: (b, s, 0))
    prev = pl.BlockSpec((1, HALO, D), lambda b, s: (b, jnp.maximum(s * per_tile - 1, 0), 0))
    nxt = pl.BlockSpec((1, HALO, D), lambda b, s: (b, jnp.minimum((s + 1) * per_tile, n_halo - 1), 0))
    consts = [p["ffn_norm"], p["ffn_w_up"], p["ffn_conv_w"], p["ffn_conv_b"], p["ffn_w_down"], final_gain]
    return pl.pallas_call(
        functools.partial(_ffn_kernel, final=final), grid=(B, S // T),
        in_specs=[tile, prev, nxt] + [_const_spec(c) for c in consts],
        out_specs=tile, out_shape=jax.ShapeDtypeStruct((B, S, D), F32),
        scratch_shapes=[pltpu.VMEM((T + 2 * HALO, D), BF16), pltpu.VMEM((T, D_FF), BF16)],
        compiler_params=_params(2), name="ffn_final" if final else "ffn",
    )(x, x, x, *map(_operand, consts))


def _swap_pairs(w):
    n = w.shape[-1]
    return w.reshape(*w.shape[:-1], n // 2, 2)[..., ::-1].reshape(w.shape)


def _rope_tables(S, d_rot):
    rows = S // GRID_W
    row = jnp.repeat(jnp.arange(rows, dtype=F32), GRID_W)
    col = jnp.tile(jnp.arange(GRID_W, dtype=F32), rows)
    n = d_rot // 4
    inv = ROPE_THETA ** (-jnp.arange(n, dtype=F32) / n)
    ang = jnp.concatenate([row[:, None] * inv, col[:, None] * inv], axis=-1)
    cos, sin = jnp.cos(ang), jnp.sin(ang)
    c = jnp.repeat(cos, 2, axis=-1)
    s = jnp.stack([-sin, sin], axis=-1).reshape(S, d_rot)
    return c, s


def _prep_params(S, P):
    L = P["w_in"].shape[0]
    row = lambda v: v[:, None, :]
    ca, sa = _rope_tables(S, MLA_ROPE)
    cb, sb = _rope_tables(S, GQA_DIM)
    p, shared = {}, {}
    c_q, c_kv, k_rope, g_q, g_k, g_v, g_m = jnp.split(P["w_in"], [256, 384, 416, 800, 928, 1056], axis=2)
    pad = jnp.zeros((L, D_MODEL, Z_END - Z_KR - 2 * MLA_ROPE), F32)
    p["w_in"] = jnp.concatenate(
        [c_q, c_kv, g_q, _swap_pairs(g_q), g_k, _swap_pairs(g_k), g_v, g_m, k_rope, _swap_pairs(k_rope), pad],
        axis=2).astype(BF16)
    for name in ("mix_norm", "mla_q_norm", "mla_kv_norm", "gmlp_v_norm", "mem_x_norm", "mem_kv_norm", "ffn_norm",
                 "ffn_conv_b"):
        p[name] = row(P[name])

    w_uq = P["mla_w_uq"].reshape(L, MLA_Q_RANK, MLA_HEADS, MLA_NOPE + MLA_ROPE)
    zpad = jnp.zeros((L, MLA_Q_RANK, MLA_HEADS, HEAD_SLOT - MLA_NOPE - MLA_ROPE), F32)
    main = jnp.concatenate([w_uq, zpad], axis=-1)
    swapped = jnp.concatenate([jnp.zeros_like(w_uq[..., :MLA_NOPE]), _swap_pairs(w_uq[..., MLA_NOPE:]), zpad], axis=-1)
    p["w_uq"] = jnp.concatenate([main.reshape(L, MLA_Q_RANK, -1), swapped.reshape(L, MLA_Q_RANK, -1)],
                                axis=2).astype(BF16)
    scale_a = (MLA_NOPE + MLA_ROPE) ** -0.5 * LOG2E
    ones = jnp.ones((S, MLA_NOPE), F32)
    zeros_n = jnp.zeros((S, MLA_NOPE), F32)
    zeros_p = jnp.zeros((S, HEAD_SLOT - MLA_NOPE - MLA_ROPE), F32)
    shared["cqa"] = jnp.tile(jnp.concatenate([ones, ca, zeros_p], axis=1) * scale_a, (1, MLA_HEADS))
    shared["sqa"] = jnp.tile(jnp.concatenate([zeros_n, sa, zeros_p], axis=1) * scale_a, (1, MLA_HEADS))

    w_ukv = P["mla_w_ukv"].reshape(L, MLA_KV_RANK, MLA_HEADS, MLA_NOPE + MLA_V)
    k_lat = jnp.concatenate(
        [w_ukv[..., :MLA_NOPE], jnp.zeros((L, MLA_KV_RANK, MLA_HEADS, HEAD_SLOT - MLA_NOPE), F32)], axis=-1)
    place = jnp.concatenate([jnp.zeros((MLA_ROPE, MLA_NOPE), F32), jnp.eye(MLA_ROPE, dtype=F32),
                             jnp.zeros((MLA_ROPE, HEAD_SLOT - MLA_NOPE - MLA_ROPE), F32)], axis=1)
    place = jnp.broadcast_to(jnp.tile(place, (1, MLA_HEADS)), (L, MLA_ROPE, MLA_HEADS * HEAD_SLOT))
    zrows = jnp.zeros((L, Z_END - Z_KR - 2 * MLA_ROPE, MLA_HEADS * HEAD_SLOT), F32)
    p["w_k"] = jnp.concatenate([k_lat.reshape(L, MLA_KV_RANK, -1), place, place, zrows], axis=1).astype(BF16)
    p["w_v"] = w_ukv[..., MLA_NOPE:].reshape(L, MLA_KV_RANK, -1).astype(BF16)
    nope_w = w_ukv[..., :MLA_NOPE]
    lat8 = jnp.concatenate([nope_w, nope_w, nope_w, jnp.zeros((L, MLA_KV_RANK, MLA_HEADS, 2 * MLA_ROPE), F32)], axis=-1)
    place8 = jnp.concatenate([jnp.zeros((MLA_ROPE, 3 * MLA_NOPE), F32), jnp.eye(MLA_ROPE, dtype=F32),
                              jnp.eye(MLA_ROPE, dtype=F32)], axis=1)
    place8 = jnp.broadcast_to(jnp.tile(place8, (1, MLA_HEADS)), (L, MLA_ROPE, MLA_HEADS * Q8_SLOT))
    zrows8 = jnp.zeros((L, Z_END - Z_KR - 2 * MLA_ROPE, MLA_HEADS * Q8_SLOT), F32)
    p["w_k8"] = (jnp.concatenate([lat8.reshape(L, MLA_KV_RANK, -1), place8, place8, zrows8], axis=1)
                 * K8_SCALE).astype(BF16)
    shared["tk"] = jnp.concatenate([ca, sa, jnp.zeros((S, Z_END - Z_KR - 2 * MLA_ROPE), F32)], axis=1)

    gq = row(jnp.tile(P["gqa_q_norm"], (1, GQA_HEADS)) * (GQA_DIM ** -0.5 * LOG2E))
    gk = row(jnp.tile(P["gqa_k_norm"], (1, GQA_KV_HEADS)))
    p["tcq"] = jnp.tile(cb, (1, GQA_HEADS)) * gq
    p["tsq"] = jnp.tile(sb, (1, GQA_HEADS)) * _swap_pairs(gq)
    p["tck"] = jnp.tile(cb, (1, GQA_KV_HEADS)) * gk
    p["tsk"] = jnp.tile(sb, (1, GQA_KV_HEADS)) * _swap_pairs(gk)
    grp = jnp.arange(W_B) // GQA_DIM
    shared["bd"] = (grp[:, None] == grp[None, :]).astype(BF16)

    p["w_s"] = P["gmlp_w_s"].reshape(L, GMLP_GROUPS * GMLP_CHUNK, GMLP_CHUNK).astype(BF16)
    p["bias_s"] = jnp.repeat(jnp.swapaxes(P["gmlp_b_s"], 1, 2), GMLP_DIM, axis=2)
    p["out_norm_a"] = row(P["out_norm"][:, :W_A])
    p["out_norm_b"] = row(P["out_norm"][:, W_A:W_A + W_B])
    p["out_norm_c"] = row(P["out_norm"][:, W_A + W_B:])
    for name in ("w_out", "mem_w_q", "mem_w_kv", "mem_w_o", "ffn_w_up", "ffn_w_down"):
        p[name] = P[name].astype(BF16)
    p["ffn_conv_w"] = P["ffn_conv_w"]
    return p, shared


def kernel(x, mem, mix_norm, w_in, mla_q_norm, mla_w_uq, mla_kv_norm, mla_w_ukv, gqa_q_norm, gqa_k_norm, gmlp_v_norm, gmlp_w_s, gmlp_b_s, out_norm, w_out, mem_x_norm, mem_kv_norm, mem_w_q, mem_w_kv, mem_w_o, ffn_norm, ffn_w_up, ffn_conv_w, ffn_conv_b, ffn_w_down, final_norm):
    P = dict(mix_norm=mix_norm, w_in=w_in, mla_q_norm=mla_q_norm, mla_w_uq=mla_w_uq, mla_kv_norm=mla_kv_norm,
             mla_w_ukv=mla_w_ukv, gqa_q_norm=gqa_q_norm, gqa_k_norm=gqa_k_norm, gmlp_v_norm=gmlp_v_norm,
             gmlp_w_s=gmlp_w_s, gmlp_b_s=gmlp_b_s, out_norm=out_norm, w_out=w_out, mem_x_norm=mem_x_norm,
             mem_kv_norm=mem_kv_norm, mem_w_q=mem_w_q, mem_w_kv=mem_w_kv, mem_w_o=mem_w_o, ffn_norm=ffn_norm,
             ffn_w_up=ffn_w_up, ffn_conv_w=ffn_conv_w, ffn_conv_b=ffn_conv_b, ffn_w_down=ffn_w_down)
    B, S, D = x.shape
    assert D == D_MODEL and S % TOK_TILE == 0 and S % WIDE_TILE == 0 and S % GRID_W == 0
    depth = w_in.shape[0]
    stacked, shared = _prep_params(S, P)
    final_gain = final_norm.reshape(1, -1)
    group = GQA_HEADS // GQA_KV_HEADS
    for l in range(depth):
        p = dict(shared, **{name: _Layer(v, l) for name, v in stacked.items()})
        qaT, ka, vaT, kna, qa8T, ka8, qbT, kb, vbT, knb, qb8T, kb8, ync = _mix_in(x, p)
        yna = _attention(qaT, ka, vaT, kna, p["out_norm_a"], qa8T, ka8, n_heads=MLA_HEADS,
                         k_slots=tuple(range(MLA_HEADS)),
                         v_rows=tuple(h * (MLA_V + V_PAD) for h in range(MLA_HEADS)), dv=MLA_V, name="attn_mla")
        ynb = _attention(qbT, kb, vbT, knb, p["out_norm_b"], qb8T, kb8, n_heads=GQA_HEADS,
                         k_slots=tuple(h // group for h in range(GQA_HEADS)),
                         v_rows=tuple((h // group) * (GQA_DIM + V_PAD) for h in range(GQA_HEADS)), dv=GQA_DIM, name="attn_gqa")
        mem_kT, mem_v = _mem_kv(mem, p["mem_kv_norm"], p["mem_w_kv"])
        x = _out_mem(yna, ynb, ync, x, p, mem_kT, mem_v)
        x = _ffn(x, p, final_gain, final=(l == depth - 1))
    return x
```

```python
import functools
from typing import NamedTuple

import jax
import jax.numpy as jnp
from jax import lax
from jax.experimental import pallas as pl
from jax.experimental.pallas import tpu as pltpu

F32 = jnp.float32
BF16 = jnp.bfloat16
F8 = jnp.float8_e4m3fn

D_MODEL = 1024
GRID_W = 64
ROPE_THETA = 10000.0
EPS = 1e-6
MLA_HEADS = 6
MLA_NOPE = 64
MLA_ROPE = 32
MLA_V = 64
MLA_Q_RANK = 256
MLA_KV_RANK = 128
GQA_HEADS = 6
GQA_KV_HEADS = 2
GQA_DIM = 64
GMLP_GROUPS = 4
GMLP_DIM = 64
GMLP_CHUNK = 128
W_A = MLA_HEADS * MLA_V
W_B = GQA_HEADS * GQA_DIM
W_C = GMLP_GROUPS * GMLP_DIM
MEM_HEADS = 4
MEM_DIM = 128
D_FF = 2816

LANES = 128
HEAD_SLOT = LANES
TOK_TILE = 512
MIX_ROWS = 256
WIDE_TILE = 1024
Q_TILE = 512
Q_BLOCK = 256
KV_TILE = 512
F32_ROWS = 8
BF16_ROWS = 16
V_PAD = BF16_ROWS
LOG2E = 1.4426950408889634
STABILISER_MAX = 50.0
Q8_SLOT = 256
Q8_SCALE = 32.0
K8_SCALE = 8.0
FP8_MAX_SCALED = 400.0
FF_CHUNK = 256
HALO = BF16_ROWS
VMEM_LIMIT = 56 * 1024 * 1024

Z_CQ, Z_CKV, Z_GQ, Z_GQS, Z_GK, Z_GKS, Z_GV, Z_U, Z_VV, Z_KR, Z_END = (
    0, 256, 384, 768, 1152, 1280, 1408, 1536, 1792, 2048, 2176)


def _rms(x, g):
    return x * lax.rsqrt(jnp.mean(x * x, axis=-1, keepdims=True) + EPS) * g


def _dot(a, b):
    return jnp.dot(a, b, preferred_element_type=F32)


class _Layer(NamedTuple):
    stack: jax.Array
    layer: int


def _operand(a):
    return a.stack if isinstance(a, _Layer) else a


def _const_spec(a):
    if isinstance(a, _Layer):
        zeros = (0,) * (a.stack.ndim - 1)
        return pl.BlockSpec((None,) + a.stack.shape[1:], lambda *_: (a.layer,) + zeros, pipeline_mode=pl.Buffered(1))
    zeros = (0,) * a.ndim
    return pl.BlockSpec(a.shape, lambda *_: zeros, pipeline_mode=pl.Buffered(1))


def _params(n_axes):
    return pltpu.CompilerParams(dimension_semantics=("arbitrary",) * n_axes,
                                vmem_limit_bytes=VMEM_LIMIT)


def _group_ssq(v, bd):
    sq = v * v
    hi = sq.astype(BF16)
    lo = (sq - hi.astype(F32)).astype(BF16)
    return _dot(hi, bd) + _dot(lo, bd)


def _split_e4m3(x):
    hi = x.astype(F8)
    return hi, (x - hi.astype(F32)).astype(F8)


def _store_vT(ref, vT, n_heads, dv, tok0):
    width = vT.shape[1]
    assert KV_TILE % width == 0 and tok0 % width == 0
    n, cols = tok0 // KV_TILE, slice(tok0 % KV_TILE, tok0 % KV_TILE + width)
    ext = (lax.broadcasted_iota(jnp.int32, (V_PAD, width), 0) == 0).astype(BF16)
    for hd in range(n_heads):
        base = hd * (dv + V_PAD)
        ref[0, n, base:base + dv, cols] = vT[hd * dv:(hd + 1) * dv].astype(BF16)
        ref[0, n, base + dv:base + dv + V_PAD, cols] = ext


def _key_norms(k, n_slots):
    rows = []
    for n in range(n_slots):
        x = k[:, n * HEAD_SLOT:(n + 1) * HEAD_SLOT].astype(F32)
        n2 = jnp.max(jnp.sum(x * x, axis=-1, keepdims=True), axis=0, keepdims=True)
        rows.append(jnp.broadcast_to(n2, (1, LANES)))
    rows.append(jnp.zeros((F32_ROWS - n_slots, LANES), F32))
    return jnp.concatenate(rows, axis=0)


def _mix_in_kernel(x_ref, g_ref, win_ref, gq_ref, wq_ref, gkv_ref, wk_ref, wk8_ref, wv_ref,
                   cqa_ref, sqa_ref, tk_ref, tcq_ref, tsq_ref, tck_ref, tsk_ref,
                   bd_ref, gv_ref, ws_ref, bias_ref, gc_ref,
                   qaT_ref, ka_ref, vaT_ref, kna_ref, qa8T_ref, ka8_ref,
                   qbT_ref, kb_ref, vbT_ref, knb_ref, qb8T_ref, kb8_ref, ync_ref):
    norms_a, norms_b = [], []
    for r0 in range(0, x_ref.shape[1], MIX_ROWS):
        rows_ = slice(r0, r0 + MIX_ROWS)
        _mix_in_rows(r0, x_ref[0, rows_], g_ref, win_ref, gq_ref, wq_ref, gkv_ref, wk_ref, wk8_ref, wv_ref,
                     [t[rows_] for t in (cqa_ref, sqa_ref, tk_ref, tcq_ref, tsq_ref, tck_ref, tsk_ref)],
                     bd_ref, gv_ref, ws_ref, bias_ref, gc_ref,
                     qaT_ref, ka_ref, vaT_ref, norms_a, qa8T_ref, ka8_ref,
                     qbT_ref, kb_ref, vbT_ref, norms_b, qb8T_ref, kb8_ref, ync_ref)
    kna_ref[0, 0] = functools.reduce(jnp.maximum, norms_a)
    knb_ref[0, 0] = functools.reduce(jnp.maximum, norms_b)


def _mix_in_rows(r0, x, g_ref, win_ref, gq_ref, wq_ref, gkv_ref, wk_ref, wk8_ref, wv_ref, tables,
                 bd_ref, gv_ref, ws_ref, bias_ref, gc_ref,
                 qaT_ref, ka_ref, vaT_ref, norms_a, qa8T_ref, ka8_ref,
                 qbT_ref, kb_ref, vbT_ref, norms_b, qb8T_ref, kb8_ref, ync_ref):
    cqa, sqa, tk, tcq, tsq, tck, tsk = tables
    tok = x.shape[0]
    rows_ = slice(r0, r0 + tok)
    h = _rms(x, g_ref[...])
    z = _dot(h.astype(BF16), win_ref[...])

    cq = _rms(z[:, Z_CQ:Z_CKV], gq_ref[...]).astype(BF16)
    qa = _dot(cq, wq_ref[...])
    half = MLA_HEADS * HEAD_SLOT
    q_a = qa[:, :half] * cqa + qa[:, half:] * sqa
    q_aT = q_a.T
    qaT_ref[0, :, rows_] = q_aT.astype(BF16)
    q_hi, q_lo = _split_e4m3(q_aT * Q8_SCALE)
    for hd in range(MLA_HEADS):
        nope = slice(hd * HEAD_SLOT, hd * HEAD_SLOT + MLA_NOPE)
        rope = slice(hd * HEAD_SLOT + MLA_NOPE, hd * HEAD_SLOT + MLA_NOPE + MLA_ROPE)
        base = hd * Q8_SLOT
        qa8T_ref[0, base:base + MLA_NOPE, rows_] = q_hi[nope]
        qa8T_ref[0, base + MLA_NOPE:base + 2 * MLA_NOPE, rows_] = q_hi[nope]
        qa8T_ref[0, base + 2 * MLA_NOPE:base + 3 * MLA_NOPE, rows_] = q_lo[nope]
        qa8T_ref[0, base + 3 * MLA_NOPE:base + 3 * MLA_NOPE + MLA_ROPE, rows_] = q_hi[rope]
        qa8T_ref[0, base + 3 * MLA_NOPE + MLA_ROPE:base + Q8_SLOT, rows_] = q_hi[rope]

    ckv = _rms(z[:, Z_CKV:Z_GQ], gkv_ref[...])
    kr = z[:, Z_KR:Z_END] * tk
    lhs = jnp.concatenate([ckv, kr], axis=1).astype(BF16)
    k_a = _dot(lhs, wk_ref[...]).astype(BF16)
    ka_ref[0, rows_] = k_a
    norms_a.append(_key_norms(k_a, MLA_HEADS))
    k8 = _dot(lhs, wk8_ref[...])
    depth = lax.broadcasted_iota(jnp.int32, (1, MLA_HEADS * Q8_SLOT), 1) % Q8_SLOT
    lo_block = jnp.logical_or(jnp.logical_and(depth >= MLA_NOPE, depth < 2 * MLA_NOPE),
                              depth >= 3 * MLA_NOPE + MLA_ROPE)
    ka8_ref[0, rows_] = jnp.where(lo_block, k8 - k8.astype(F8).astype(F32), k8).astype(F8)
    _store_vT(vaT_ref, _dot(lhs[:, :MLA_KV_RANK], wv_ref[...]).T, MLA_HEADS, MLA_V, r0)

    bd = bd_ref[...]
    g_q = z[:, Z_GQ:Z_GQS]
    n_q = lax.rsqrt(_group_ssq(g_q, bd) * (1.0 / GQA_DIM) + EPS)
    q_b = (g_q * tcq + z[:, Z_GQS:Z_GK] * tsq) * n_q
    q_bT = q_b.T
    zero = jnp.zeros((HEAD_SLOT - GQA_DIM, tok), BF16)
    for hd in range(GQA_HEADS):
        qbT_ref[0, hd * HEAD_SLOT:hd * HEAD_SLOT + GQA_DIM, rows_] = q_bT[hd * GQA_DIM:(hd + 1) * GQA_DIM].astype(BF16)
        qbT_ref[0, hd * HEAD_SLOT + GQA_DIM:(hd + 1) * HEAD_SLOT, rows_] = zero
    q_hi, q_lo = _split_e4m3(q_bT * Q8_SCALE)
    zero8 = jnp.zeros((Q8_SLOT - 3 * GQA_DIM, tok), F8)
    for hd in range(GQA_HEADS):
        rows = slice(hd * GQA_DIM, (hd + 1) * GQA_DIM)
        base = hd * Q8_SLOT
        qb8T_ref[0, base:base + GQA_DIM, rows_] = q_hi[rows]
        qb8T_ref[0, base + GQA_DIM:base + 2 * GQA_DIM, rows_] = q_hi[rows]
        qb8T_ref[0, base + 2 * GQA_DIM:base + 3 * GQA_DIM, rows_] = q_lo[rows]
        qb8T_ref[0, base + 3 * GQA_DIM:base + Q8_SLOT, rows_] = zero8
    g_k = z[:, Z_GK:Z_GKS]
    n_k = lax.rsqrt(_group_ssq(g_k, bd[:LANES, :LANES]) * (1.0 / GQA_DIM) + EPS)
    k_b = (g_k * tck + z[:, Z_GKS:Z_GV] * tsk) * n_k
    low = lax.broadcasted_iota(jnp.int32, (tok, LANES), 1) < GQA_DIM
    slots = [jnp.where(low, k_b, 0.0), jnp.where(low, pltpu.roll(k_b, GQA_DIM, 1), 0.0)]
    blocks = []
    for k_slot in slots:
        k8 = k_slot * K8_SCALE
        k_lo = k8 - k8.astype(F8).astype(F32)
        blocks += [jnp.where(low, k8, pltpu.roll(k_lo, GQA_DIM, 1)), k8]
    kb8_ref[0, rows_] = jnp.concatenate(blocks, axis=1).astype(F8)
    k_b = jnp.concatenate(slots, axis=1).astype(BF16)
    kb_ref[0, rows_] = k_b
    norms_b.append(_key_norms(k_b, GQA_KV_HEADS))
    _store_vT(vbT_ref, z[:, Z_GV:Z_U].T, GQA_KV_HEADS, GQA_DIM, r0)

    gm = jax.nn.gelu(z[:, Z_U:Z_KR])
    u = gm[:, :W_C]
    vv = _rms(gm[:, W_C:], gv_ref[...])
    lane_grp = lax.broadcasted_iota(jnp.int32, (GMLP_CHUNK, W_C), 1) // GMLP_DIM
    ws = ws_ref[...]
    bias = bias_ref[...]
    ycs = []
    for n in range(tok // GMLP_CHUNK):
        rows = slice(n * GMLP_CHUNK, (n + 1) * GMLP_CHUNK)
        r = _dot(ws, vv[rows].astype(BF16))
        mixed = r[3 * GMLP_CHUNK:]
        for grp in range(GMLP_GROUPS - 2, -1, -1):
            mixed = jnp.where(lane_grp == grp, r[grp * GMLP_CHUNK:(grp + 1) * GMLP_CHUNK], mixed)
        ycs.append(u[rows] * (mixed + bias))
    ync_ref[0, rows_] = _rms(jnp.concatenate(ycs, axis=0), gc_ref[...]).astype(BF16)


def _mix_in(x, p):
    B, S, D = x.shape
    T = TOK_TILE
    nkv = T // KV_TILE
    tile = lambda w: pl.BlockSpec((1, T, w), lambda s, b: (b, s, 0))
    def tab(t):
        if isinstance(t, _Layer):
            return pl.BlockSpec((None, T, t.stack.shape[2]), lambda s, b: (t.layer, s, 0))
        return pl.BlockSpec((T, t.shape[1]), lambda s, b: (s, 0))
    tposed = lambda r: pl.BlockSpec((1, r, T), lambda s, b: (b, 0, s))
    blocked = lambda r: pl.BlockSpec((1, nkv, r, KV_TILE), lambda s, b: (b, s, 0, 0))
    consts = [p["mix_norm"], p["w_in"], p["mla_q_norm"], p["w_uq"], p["mla_kv_norm"], p["w_k"], p["w_k8"], p["w_v"]]
    tabs = [p["cqa"], p["sqa"], p["tk"], p["tcq"], p["tsq"], p["tck"], p["tsk"]]
    consts2 = [p["bd"], p["gmlp_v_norm"], p["w_s"], p["bias_s"], p["out_norm_c"]]
    in_specs = ([tile(D)] + [_const_spec(c) for c in consts] + [tab(t) for t in tabs]
                + [_const_spec(c) for c in consts2])
    out_shape = [
        jax.ShapeDtypeStruct((B, MLA_HEADS * HEAD_SLOT, S), BF16),
        jax.ShapeDtypeStruct((B, S, MLA_HEADS * HEAD_SLOT), BF16),
        jax.ShapeDtypeStruct((B, S // KV_TILE, MLA_HEADS * (MLA_V + V_PAD), KV_TILE), BF16),
        jax.ShapeDtypeStruct((B, S // T, F32_ROWS, LANES), F32),
        jax.ShapeDtypeStruct((B, MLA_HEADS * Q8_SLOT, S), F8),
        jax.ShapeDtypeStruct((B, S, MLA_HEADS * Q8_SLOT), F8),
        jax.ShapeDtypeStruct((B, GQA_HEADS * HEAD_SLOT, S), BF16),
        jax.ShapeDtypeStruct((B, S, GQA_KV_HEADS * HEAD_SLOT), BF16),
        jax.ShapeDtypeStruct((B, S // KV_TILE, GQA_KV_HEADS * (GQA_DIM + V_PAD), KV_TILE), BF16),
        jax.ShapeDtypeStruct((B, S // T, F32_ROWS, LANES), F32),
        jax.ShapeDtypeStruct((B, GQA_HEADS * Q8_SLOT, S), F8),
        jax.ShapeDtypeStruct((B, S, GQA_KV_HEADS * Q8_SLOT), F8),
        jax.ShapeDtypeStruct((B, S, W_C), BF16),
    ]
    norms = pl.BlockSpec((1, 1, F32_ROWS, LANES), lambda s, b: (b, s, 0, 0))
    out_specs = [tposed(MLA_HEADS * HEAD_SLOT), tile(MLA_HEADS * HEAD_SLOT), blocked(MLA_HEADS * (MLA_V + V_PAD)),
                 norms, tposed(MLA_HEADS * Q8_SLOT), tile(MLA_HEADS * Q8_SLOT),
                 tposed(GQA_HEADS * HEAD_SLOT), tile(GQA_KV_HEADS * HEAD_SLOT),
                 blocked(GQA_KV_HEADS * (GQA_DIM + V_PAD)), norms, tposed(GQA_HEADS * Q8_SLOT),
                 tile(GQA_KV_HEADS * Q8_SLOT), tile(W_C)]
    return pl.pallas_call(
        _mix_in_kernel, grid=(S // T, B), in_specs=in_specs, out_specs=out_specs, out_shape=out_shape,
        compiler_params=_params(2), name="mix_in",
    )(x, *map(_operand, consts + tabs + consts2))


def _attn_kernel(q_ref, k_ref, v_ref, kn_ref, g_ref, q8_ref, k8_ref, o_ref,
                 s_ref, p_ref, m_ref, acc_ref, oT_ref, *, n_heads, k_slots, v_rows, dv):
    nkv = v_ref.shape[1]
    n_blk = q_ref.shape[2] // Q_BLOCK
    n_work = n_heads * n_blk
    assert n_work % 2 == 0

    def cols(w):
        return slice((w % n_blk) * Q_BLOCK, (w % n_blk + 1) * Q_BLOCK)

    kmax = jnp.max(kn_ref[0], axis=0)
    bound = jnp.zeros((1, 1), F32)
    qn2_max = jnp.zeros((1, 1), F32)
    kn2_max = jnp.zeros((1, 1), F32)
    for hd in range(n_heads):
        qf = q_ref[0, hd * HEAD_SLOT:(hd + 1) * HEAD_SLOT, :].astype(F32)
        qn2 = jnp.sum(qf * qf, axis=0, keepdims=True)
        kn2 = kmax[k_slots[hd]:k_slots[hd] + 1, 0:1]
        u = jnp.sqrt(qn2 * kn2)
        for c in range(n_blk):
            m_ref[hd * n_blk + c] = u[:, c * Q_BLOCK:(c + 1) * Q_BLOCK]
        bound = jnp.maximum(bound, jnp.max(u, axis=1, keepdims=True))
        qn2_max = jnp.maximum(qn2_max, jnp.max(qn2, axis=1, keepdims=True))
        kn2_max = jnp.maximum(kn2_max, kn2)
    stabilise = jnp.logical_and(jnp.max(bound) <= STABILISER_MAX, jnp.logical_and(
        jnp.max(qn2_max) <= (FP8_MAX_SCALED / Q8_SCALE) ** 2, jnp.max(kn2_max) <= (FP8_MAX_SCALED / K8_SCALE) ** 2))
    acc_ref[...] = jnp.zeros(acc_ref.shape, F32)

    def run(stabilised):
        def scores(w, off):
            hd = w // n_blk
            ks, qs, depth = (k8_ref, q8_ref, Q8_SLOT) if stabilised else (k_ref, q_ref, HEAD_SLOT)
            k = ks[0, pl.ds(off, KV_TILE), k_slots[hd] * depth:(k_slots[hd] + 1) * depth]
            qT = qs[0, hd * depth:(hd + 1) * depth, cols(w)]
            s_ref[w % 2] = _dot(k, qT)

        scores(0, 0)
        last = n_work - 1

        def v_tile(w, j):
            hd = w // n_blk
            return v_ref[0, j, v_rows[hd]:v_rows[hd] + dv + V_PAD, :]

        def pv(w, j):
            acc_ref[w] += _dot(v_tile(w, j), p_ref[w])

        if stabilised:
            p_ref[last] = jnp.zeros(p_ref.shape[1:], BF16)

        def step(j, carry):
            off = pl.multiple_of(j * KV_TILE, KV_TILE)
            off_next = pl.multiple_of(jnp.minimum(j + 1, nkv - 1) * KV_TILE, KV_TILE)
            for w in range(n_work):
                if w + 1 < n_work:
                    scores(w + 1, off)
                else:
                    scores(0, off_next)
                if stabilised:
                    if w > 0:
                        pv(w - 1, j)
                    else:
                        pv(last, jnp.maximum(j - 1, 0))
                    s = s_ref[w % 2] * (1.0 / (Q8_SCALE * K8_SCALE)) - m_ref[w]
                    p_ref[w] = jnp.exp2(s.astype(BF16))
                else:
                    m_old = m_ref[w]
                    m_new = jnp.maximum(m_old, jnp.max(s_ref[w % 2], axis=0, keepdims=True))
                    m_ref[w] = m_new
                    p_ref[w] = jnp.exp2(s_ref[w % 2] - m_new).astype(BF16)
                    acc_ref[w] = jnp.exp2(m_old - m_new) * acc_ref[w] + _dot(v_tile(w, j), p_ref[w])
            return carry

        lax.fori_loop(0, nkv, step, 0, unroll=16 if stabilised else 1)
        if stabilised:
            pv(last, nkv - 1)

    def running_max():
        m_ref[...] = jnp.full(m_ref.shape, -1e30, F32)
        run(False)

    lax.cond(stabilise, lambda: run(True), running_max)
    for w in range(n_work):
        hd = w // n_blk
        oT_ref[hd * dv:(hd + 1) * dv, cols(w)] = acc_ref[w, :dv] / acc_ref[w, dv:dv + 1]
    o_ref[0] = _rms(oT_ref[...].T, g_ref[...]).astype(BF16)


def _attention(qT, k, vT, kn, gain, q8T, k8, *, n_heads, k_slots, v_rows, dv, name):
    B, _, S = qT.shape
    kernel = functools.partial(_attn_kernel, n_heads=n_heads, k_slots=k_slots, v_rows=v_rows, dv=dv)
    n_work = n_heads * (Q_TILE // Q_BLOCK)
    q_tile = lambda slot: pl.BlockSpec((1, n_heads * slot, Q_TILE), lambda b, i: (b, 0, i))
    whole = lambda shape: pl.BlockSpec((1,) + shape[1:], lambda b, i: (b,) + (0,) * (len(shape) - 1),
                                       pipeline_mode=pl.Buffered(1))
    return pl.pallas_call(
        kernel, grid=(B, S // Q_TILE),
        in_specs=[q_tile(HEAD_SLOT), whole(k.shape), whole(vT.shape), whole(kn.shape), _const_spec(gain),
                  q_tile(Q8_SLOT), whole(k8.shape)],
        out_specs=pl.BlockSpec((1, Q_TILE, n_heads * dv), lambda b, i: (b, i, 0)),
        out_shape=jax.ShapeDtypeStruct((B, S, n_heads * dv), BF16),
        scratch_shapes=[pltpu.VMEM((2, KV_TILE, Q_BLOCK), F32), pltpu.VMEM((n_work, KV_TILE, Q_BLOCK), BF16),
                        pltpu.VMEM((n_work, 1, Q_BLOCK), F32), pltpu.VMEM((n_work, dv + V_PAD, Q_BLOCK), F32),
                        pltpu.VMEM((n_heads * dv, Q_TILE), F32)],
        compiler_params=_params(2), name=name,
    )(qT, k, vT, kn, _operand(gain), q8T, k8)


def _mem_kv_kernel(mem_ref, g_ref, w_ref, kT_ref, v_ref):
    kv = _dot(_rms(mem_ref[0], g_ref[...]).astype(BF16), w_ref[...])
    width = MEM_HEADS * MEM_DIM
    kT_ref[0] = kv[:, :width].T.astype(BF16)
    ones = jnp.ones((kv.shape[0], MEM_DIM), BF16)
    for hd in range(MEM_HEADS):
        v_ref[0, hd, :, :MEM_DIM] = kv[:, width + hd * MEM_DIM:width + (hd + 1) * MEM_DIM].astype(BF16)
        v_ref[0, hd, :, MEM_DIM:] = ones


def _mem_kv(mem, gain, w_kv):
    B, Tm, D = mem.shape
    width = MEM_HEADS * MEM_DIM
    return pl.pallas_call(
        _mem_kv_kernel, grid=(B,),
        in_specs=[pl.BlockSpec((1, Tm, D), lambda b: (b, 0, 0)), _const_spec(gain), _const_spec(w_kv)],
        out_specs=[pl.BlockSpec((1, width, Tm), lambda b: (b, 0, 0)),
                   pl.BlockSpec((1, MEM_HEADS, Tm, 2 * MEM_DIM), lambda b: (b, 0, 0, 0))],
        out_shape=[jax.ShapeDtypeStruct((B, width, Tm), BF16),
                   jax.ShapeDtypeStruct((B, MEM_HEADS, Tm, 2 * MEM_DIM), BF16)],
        compiler_params=_params(1), name="mem_kv",
    )(mem, _operand(gain), _operand(w_kv))


def _out_mem_kernel(ya_ref, yb_ref, yc_ref, x_ref, wout_ref, g_ref, wq_ref, kT_ref, v_ref, wo_ref, o_ref):
    y = jnp.concatenate([ya_ref[0], yb_ref[0], yc_ref[0]], axis=1)
    x1 = x_ref[0] + _dot(y, wout_ref[...])
    h = _rms(x1, g_ref[...]).astype(BF16)
    q = (_dot(h, wq_ref[...]) * (MEM_DIM ** -0.5)).astype(BF16)
    heads = []
    for hd in range(MEM_HEADS):
        s = _dot(q[:, hd * MEM_DIM:(hd + 1) * MEM_DIM], kT_ref[0, hd * MEM_DIM:(hd + 1) * MEM_DIM, :])
        p = jnp.exp(s - jnp.max(s, axis=-1, keepdims=True)).astype(BF16)
        pv = _dot(p, v_ref[0, hd])
        heads.append(pv[:, :MEM_DIM] / pv[:, MEM_DIM:])
    o = jnp.concatenate(heads, axis=1).astype(BF16)
    o_ref[0] = x1 + _dot(o, wo_ref[...])


def _out_mem(ya, yb, yc, x, p, mem_kT, mem_v):
    B, S, D = x.shape
    T = WIDE_TILE
    tile = lambda w: pl.BlockSpec((1, T, w), lambda b, s: (b, s, 0))
    per_b = lambda a: pl.BlockSpec((1,) + a.shape[1:], lambda b, s: (b,) + (0,) * (a.ndim - 1))
    consts = [p["w_out"], p["mem_x_norm"], p["mem_w_q"]]
    return pl.pallas_call(
        _out_mem_kernel, grid=(B, S // T),
        in_specs=[tile(W_A), tile(W_B), tile(W_C), tile(D)] + [_const_spec(c) for c in consts]
                 + [per_b(mem_kT), per_b(mem_v), _const_spec(p["mem_w_o"])],
        out_specs=tile(D), out_shape=jax.ShapeDtypeStruct((B, S, D), F32),
        compiler_params=_params(2), name="out_mem",
    )(ya, yb, yc, x, *map(_operand, consts), mem_kT, mem_v, _operand(p["mem_w_o"]))


def _ffn_kernel(x_ref, xp_ref, xn_ref, g_ref, wup_ref, cw_ref, cb_ref, wdn_ref, fg_ref, o_ref,
                h_ref, act_ref, *, final):
    tok = x_ref.shape[1]
    i = pl.program_id(1)
    g = g_ref[...]
    x = x_ref[0]
    keep_prev = (i > 0).astype(F32)
    keep_next = (i < pl.num_programs(1) - 1).astype(F32)
    h_ref[0:HALO] = (_rms(xp_ref[0], g) * keep_prev).astype(BF16)
    h_ref[HALO:HALO + tok] = _rms(x, g).astype(BF16)
    h_ref[HALO + tok:] = (_rms(xn_ref[0], g) * keep_next).astype(BF16)
    hext = h_ref[...]
    rows = tok + 2 * HALO

    def conv_up(cols):
        a = _dot(hext, wup_ref[:, cols])
        w = cw_ref[:, cols]
        return (pltpu.roll(a, 1, 0)[HALO:HALO + tok] * w[0:1] + a[HALO:HALO + tok] * w[1:2]
                + pltpu.roll(a, rows - 1, 0)[HALO:HALO + tok] * w[2:3] + cb_ref[:, cols])

    for c in range(D_FF // FF_CHUNK):
        gate = conv_up(slice(FF_CHUNK * c, FF_CHUNK * (c + 1)))
        val = conv_up(slice(D_FF + FF_CHUNK * c, D_FF + FF_CHUNK * (c + 1)))
        act_ref[:, FF_CHUNK * c:FF_CHUNK * (c + 1)] = (jax.nn.silu(gate) * val).astype(BF16)
    y = x + _dot(act_ref[...], wdn_ref[...])
    if final:
        y = _rms(y, fg_ref[...])
    o_ref[0] = y


def _ffn(x, p, final_gain, *, final):
    B, S, D = x.shape
    T = WIDE_TILE
    per_tile = T // HALO
    n_halo = S // HALO
    tile = pl.BlockSpec((1, T, D), lambda b, s: (b, s, 0))
    prev = pl.BlockSpec((1, HALO, D), lambda b, s: (b, jnp.maximum(s * per_tile - 1, 0), 0))
    nxt = pl.BlockSpec((1, HALO, D), lambda b, s: (b, jnp.minimum((s + 1) * per_tile, n_halo - 1), 0))
    consts = [p["ffn_norm"], p["ffn_w_up"], p["ffn_conv_w"], p["ffn_conv_b"], p["ffn_w_down"], final_gain]
    return pl.pallas_call(
        functools.partial(_ffn_kernel, final=final), grid=(B, S // T),
        in_specs=[tile, prev, nxt] + [_const_spec(c) for c in consts],
        out_specs=tile, out_shape=jax.ShapeDtypeStruct((B, S, D), F32),
        scratch_shapes=[pltpu.VMEM((T + 2 * HALO, D), BF16), pltpu.VMEM((T, D_FF), BF16)],
        compiler_params=_params(2), name="ffn_final" if final else "ffn",
    )(x, x, x, *map(_operand, consts))


def _swap_pairs(w):
    n = w.shape[-1]
    return w.reshape(*w.shape[:-1], n // 2, 2)[..., ::-1].reshape(w.shape)


def _rope_tables(S, d_rot):
    rows = S // GRID_W
    row = jnp.repeat(jnp.arange(rows, dtype=F32), GRID_W)
    col = jnp.tile(jnp.arange(GRID_W, dtype=F32), rows)
    n = d_rot // 4
    inv = ROPE_THETA ** (-jnp.arange(n, dtype=F32) / n)
    ang = jnp.concatenate([row[:, None] * inv, col[:, None] * inv], axis=-1)
    cos, sin = jnp.cos(ang), jnp.sin(ang)
    c = jnp.repeat(cos, 2, axis=-1)
    s = jnp.stack([-sin, sin], axis=-1).reshape(S, d_rot)
    return c, s


def _prep_params(S, P):
    L = P["w_in"].shape[0]
    row = lambda v: v[:, None, :]
    ca, sa = _rope_tables(S, MLA_ROPE)
    cb, sb = _rope_tables(S, GQA_DIM)
    p, shared = {}, {}
    c_q, c_kv, k_rope, g_q, g_k, g_v, g_m = jnp.split(P["w_in"], [256, 384, 416, 800, 928, 1056], axis=2)
    pad = jnp.zeros((L, D_MODEL, Z_END - Z_KR - 2 * MLA_ROPE), F32)
    p["w_in"] = jnp.concatenate(
        [c_q, c_kv, g_q, _swap_pairs(g_q), g_k, _swap_pairs(g_k), g_v, g_m, k_rope, _swap_pairs(k_rope), pad],
        axis=2).astype(BF16)
    for name in ("mix_norm", "mla_q_norm", "mla_kv_norm", "gmlp_v_norm", "mem_x_norm", "mem_kv_norm", "ffn_norm",
                 "ffn_conv_b"):
        p[name] = row(P[name])

    w_uq = P["mla_w_uq"].reshape(L, MLA_Q_RANK, MLA_HEADS, MLA_NOPE + MLA_ROPE)
    zpad = jnp.zeros((L, MLA_Q_RANK, MLA_HEADS, HEAD_SLOT - MLA_NOPE - MLA_ROPE), F32)
    main = jnp.concatenate([w_uq, zpad], axis=-1)
    swapped = jnp.concatenate([jnp.zeros_like(w_uq[..., :MLA_NOPE]), _swap_pairs(w_uq[..., MLA_NOPE:]), zpad], axis=-1)
    p["w_uq"] = jnp.concatenate([main.reshape(L, MLA_Q_RANK, -1), swapped.reshape(L, MLA_Q_RANK, -1)],
                                axis=2).astype(BF16)
    scale_a = (MLA_NOPE + MLA_ROPE) ** -0.5 * LOG2E
    ones = jnp.ones((S, MLA_NOPE), F32)
    zeros_n = jnp.zeros((S, MLA_NOPE), F32)
    zeros_p = jnp.zeros((S, HEAD_SLOT - MLA_NOPE - MLA_ROPE), F32)
    shared["cqa"] = jnp.tile(jnp.concatenate([ones, ca, zeros_p], axis=1) * scale_a, (1, MLA_HEADS))
    shared["sqa"] = jnp.tile(jnp.concatenate([zeros_n, sa, zeros_p], axis=1) * scale_a, (1, MLA_HEADS))

    w_ukv = P["mla_w_ukv"].reshape(L, MLA_KV_RANK, MLA_HEADS, MLA_NOPE + MLA_V)
    k_lat = jnp.concatenate(
        [w_ukv[..., :MLA_NOPE], jnp.zeros((L, MLA_KV_RANK, MLA_HEADS, HEAD_SLOT - MLA_NOPE), F32)], axis=-1)
    place = jnp.concatenate([jnp.zeros((MLA_ROPE, MLA_NOPE), F32), jnp.eye(MLA_ROPE, dtype=F32),
                             jnp.zeros((MLA_ROPE, HEAD_SLOT - MLA_NOPE - MLA_ROPE), F32)], axis=1)
    place = jnp.broadcast_to(jnp.tile(place, (1, MLA_HEADS)), (L, MLA_ROPE, MLA_HEADS * HEAD_SLOT))
    zrows = jnp.zeros((L, Z_END - Z_KR - 2 * MLA_ROPE, MLA_HEADS * HEAD_SLOT), F32)
    p["w_k"] = jnp.concatenate([k_lat.reshape(L, MLA_KV_RANK, -1), place, place, zrows], axis=1).astype(BF16)
    p["w_v"] = w_ukv[..., MLA_NOPE:].reshape(L, MLA_KV_RANK, -1).astype(BF16)
    nope_w = w_ukv[..., :MLA_NOPE]
    lat8 = jnp.concatenate([nope_w, nope_w, nope_w, jnp.zeros((L, MLA_KV_RANK, MLA_HEADS, 2 * MLA_ROPE), F32)], axis=-1)
    place8 = jnp.concatenate([jnp.zeros((MLA_ROPE, 3 * MLA_NOPE), F32), jnp.eye(MLA_ROPE, dtype=F32),
                              jnp.eye(MLA_ROPE, dtype=F32)], axis=1)
    place8 = jnp.broadcast_to(jnp.tile(place8, (1, MLA_HEADS)), (L, MLA_ROPE, MLA_HEADS * Q8_SLOT))
    zrows8 = jnp.zeros((L, Z_END - Z_KR - 2 * MLA_ROPE, MLA_HEADS * Q8_SLOT), F32)
    p["w_k8"] = (jnp.concatenate([lat8.reshape(L, MLA_KV_RANK, -1), place8, place8, zrows8], axis=1)
                 * K8_SCALE).astype(BF16)
    shared["tk"] = jnp.concatenate([ca, sa, jnp.zeros((S, Z_END - Z_KR - 2 * MLA_ROPE), F32)], axis=1)

    gq = row(jnp.tile(P["gqa_q_norm"], (1, GQA_HEADS)) * (GQA_DIM ** -0.5 * LOG2E))
    gk = row(jnp.tile(P["gqa_k_norm"], (1, GQA_KV_HEADS)))
    p["tcq"] = jnp.tile(cb, (1, GQA_HEADS)) * gq
    p["tsq"] = jnp.tile(sb, (1, GQA_HEADS)) * _swap_pairs(gq)
    p["tck"] = jnp.tile(cb, (1, GQA_KV_HEADS)) * gk
    p["tsk"] = jnp.tile(sb, (1, GQA_KV_HEADS)) * _swap_pairs(gk)
    grp = jnp.arange(W_B) // GQA_DIM
    shared["bd"] = (grp[:, None] == grp[None, :]).astype(BF16)

    p["w_s"] = P["gmlp_w_s"].reshape(L, GMLP_GROUPS * GMLP_CHUNK, GMLP_CHUNK).astype(BF16)
    p["bias_s"] = jnp.repeat(jnp.swapaxes(P["gmlp_b_s"], 1, 2), GMLP_DIM, axis=2)
    p["out_norm_a"] = row(P["out_norm"][:, :W_A])
    p["out_norm_b"] = row(P["out_norm"][:, W_A:W_A + W_B])
    p["out_norm_c"] = row(P["out_norm"][:, W_A + W_B:])
    for name in ("w_out", "mem_w_q", "mem_w_kv", "mem_w_o", "ffn_w_up", "ffn_w_down"):
        p[name] = P[name].astype(BF16)
    p["ffn_conv_w"] = P["ffn_conv_w"]
    return p, shared


def kernel(x, mem, mix_norm, w_in, mla_q_norm, mla_w_uq, mla_kv_norm, mla_w_ukv, gqa_q_norm, gqa_k_norm, gmlp_v_norm, gmlp_w_s, gmlp_b_s, out_norm, w_out, mem_x_norm, mem_kv_norm, mem_w_q, mem_w_kv, mem_w_o, ffn_norm, ffn_w_up, ffn_conv_w, ffn_conv_b, ffn_w_down, final_norm):
    P = dict(mix_norm=mix_norm, w_in=w_in, mla_q_norm=mla_q_norm, mla_w_uq=mla_w_uq, mla_kv_norm=mla_kv_norm,
             mla_w_ukv=mla_w_ukv, gqa_q_norm=gqa_q_norm, gqa_k_norm=gqa_k_norm, gmlp_v_norm=gmlp_v_norm,
             gmlp_w_s=gmlp_w_s, gmlp_b_s=gmlp_b_s, out_norm=out_norm, w_out=w_out, mem_x_norm=mem_x_norm,
             mem_kv_norm=mem_kv_norm, mem_w_q=mem_w_q, mem_w_kv=mem_w_kv, mem_w_o=mem_w_o, ffn_norm=ffn_norm,
             ffn_w_up=ffn_w_up, ffn_conv_w=ffn_conv_w, ffn_conv_b=ffn_conv_b, ffn_w_down=ffn_w_down)
    B, S, D = x.shape
    assert D == D_MODEL and S % TOK_TILE == 0 and S % WIDE_TILE == 0 and S % GRID_W == 0
    depth = w_in.shape[0]
    stacked, shared = _prep_params(S, P)
    final_gain = final_norm.reshape(1, -1)
    group = GQA_HEADS // GQA_KV_HEADS
    for l in range(depth):
        p = dict(shared, **{name: _Layer(v, l) for name, v in stacked.items()})
        qaT, ka, vaT, kna, qa8T, ka8, qbT, kb, vbT, knb, qb8T, kb8, ync = _mix_in(x, p)
        yna = _attention(qaT, ka, vaT, kna, p["out_norm_a"], qa8T, ka8, n_heads=MLA_HEADS,
                         k_slots=tuple(range(MLA_HEADS)),
                         v_rows=tuple(h * (MLA_V + V_PAD) for h in range(MLA_HEADS)), dv=MLA_V, name="attn_mla")
        ynb = _attention(qbT, kb, vbT, knb, p["out_norm_b"], qb8T, kb8, n_heads=GQA_HEADS,
                         k_slots=tuple(h // group for h in range(GQA_HEADS)),
                         v_rows=tuple((h // group) * (GQA_DIM + V_PAD) for h in range(GQA_HEADS)), dv=GQA_DIM, name="attn_gqa")
        mem_kT, mem_v = _mem_kv(mem, p["mem_kv_norm"], p["mem_w_kv"])
        x = _out_mem(yna, ynb, ync, x, p, mem_kT, mem_v)
        x = _ffn(x, p, final_gain, final=(l == depth - 1))
    return x
```

```python
import functools
from typing import NamedTuple

import jax
import jax.numpy as jnp
from jax import lax
from jax.experimental import pallas as pl
from jax.experimental.pallas import tpu as pltpu

F32 = jnp.float32
BF16 = jnp.bfloat16
F8 = jnp.float8_e4m3fn

D_MODEL = 1024
GRID_W = 64
ROPE_THETA = 10000.0
EPS = 1e-6
MLA_HEADS = 6
MLA_NOPE = 64
MLA_ROPE = 32
MLA_V = 64
MLA_Q_RANK = 256
MLA_KV_RANK = 128
GQA_HEADS = 6
GQA_KV_HEADS = 2
GQA_DIM = 64
GMLP_GROUPS = 4
GMLP_DIM = 64
GMLP_CHUNK = 128
W_A = MLA_HEADS * MLA_V
W_B = GQA_HEADS * GQA_DIM
W_C = GMLP_GROUPS * GMLP_DIM
MEM_HEADS = 4
MEM_DIM = 128
D_FF = 2816

LANES = 128
HEAD_SLOT = LANES
TOK_TILE = 512
MIX_ROWS = 256
WIDE_TILE = 1024
Q_TILE = 512
Q_BLOCK = 256
KV_TILE = 512
F32_ROWS = 8
BF16_ROWS = 16
V_PAD = BF16_ROWS
LOG2E = 1.4426950408889634
STABILISER_MAX = 50.0
Q8_SLOT = 256
Q8_SCALE = 32.0
K8_SCALE = 8.0
FP8_MAX_SCALED = 400.0
FF_CHUNK = 256
HALO = BF16_ROWS
VMEM_LIMIT = 56 * 1024 * 1024

Z_CQ, Z_CKV, Z_GQ, Z_GQS, Z_GK, Z_GKS, Z_GV, Z_U, Z_VV, Z_KR, Z_END = (
    0, 256, 384, 768, 1152, 1280, 1408, 1536, 1792, 2048, 2176)


def _rms(x, g):
    return x * lax.rsqrt(jnp.mean(x * x, axis=-1, keepdims=True) + EPS) * g


def _dot(a, b):
    return jnp.dot(a, b, preferred_element_type=F32)


class _Layer(NamedTuple):
    stack: jax.Array
    layer: int


def _operand(a):
    return a.stack if isinstance(a, _Layer) else a


def _const_spec(a):
    if isinstance(a, _Layer):
        zeros = (0,) * (a.stack.ndim - 1)
        return pl.BlockSpec((None,) + a.stack.shape[1:], lambda *_: (a.layer,) + zeros, pipeline_mode=pl.Buffered(1))
    zeros = (0,) * a.ndim
    return pl.BlockSpec(a.shape, lambda *_: zeros, pipeline_mode=pl.Buffered(1))


def _params(n_axes):
    return pltpu.CompilerParams(dimension_semantics=("arbitrary",) * n_axes,
                                vmem_limit_bytes=VMEM_LIMIT)


def _group_ssq(v, bd):
    sq = v * v
    hi = sq.astype(BF16)
    lo = (sq - hi.astype(F32)).astype(BF16)
    return _dot(hi, bd) + _dot(lo, bd)


def _split_e4m3(x):
    hi = x.astype(F8)
    return hi, (x - hi.astype(F32)).astype(F8)


def _store_vT(ref, vT, n_heads, dv, tok0):
    width = vT.shape[1]
    assert KV_TILE % width == 0 and tok0 % width == 0
    n, cols = tok0 // KV_TILE, slice(tok0 % KV_TILE, tok0 % KV_TILE + width)
    ext = (lax.broadcasted_iota(jnp.int32, (V_PAD, width), 0) == 0).astype(BF16)
    for hd in range(n_heads):
        base = hd * (dv + V_PAD)
        ref[0, n, base:base + dv, cols] = vT[hd * dv:(hd + 1) * dv].astype(BF16)
        ref[0, n, base + dv:base + dv + V_PAD, cols] = ext


def _key_norms(k, n_slots):
    rows = []
    for n in range(n_slots):
        x = k[:, n * HEAD_SLOT:(n + 1) * HEAD_SLOT].astype(F32)
        n2 = jnp.max(jnp.sum(x * x, axis=-1, keepdims=True), axis=0, keepdims=True)
        rows.append(jnp.broadcast_to(n2, (1, LANES)))
    rows.append(jnp.zeros((F32_ROWS - n_slots, LANES), F32))
    return jnp.concatenate(rows, axis=0)


def _mix_in_kernel(x_ref, g_ref, win_ref, gq_ref, wq_ref, gkv_ref, wk_ref, wk8_ref, wv_ref,
                   cqa_ref, sqa_ref, tk_ref, cb_ref, sb_ref, gqc_ref, gqs_ref, gkc_ref, gks_ref,
                   bd_ref, gv_ref, ws_ref, bias_ref, gc_ref,
                   qaT_ref, ka_ref, vaT_ref, kna_ref, qa8T_ref, ka8_ref,
                   qbT_ref, kb_ref, vbT_ref, knb_ref, qb8T_ref, kb8_ref, ync_ref):
    norms_a, norms_b = [], []
    for r0 in range(0, x_ref.shape[1], MIX_ROWS):
        rows_ = slice(r0, r0 + MIX_ROWS)
        _mix_in_rows(r0, x_ref[0, rows_], g_ref, win_ref, gq_ref, wq_ref, gkv_ref, wk_ref, wk8_ref, wv_ref,
                     _rope_tiles(cqa_ref[rows_], sqa_ref[rows_], tk_ref[rows_], cb_ref[rows_], sb_ref[rows_],
                                 gqc_ref[...], gqs_ref[...], gkc_ref[...], gks_ref[...]),
                     bd_ref, gv_ref, ws_ref, bias_ref, gc_ref,
                     qaT_ref, ka_ref, vaT_ref, norms_a, qa8T_ref, ka8_ref,
                     qbT_ref, kb_ref, vbT_ref, norms_b, qb8T_ref, kb8_ref, ync_ref)
    kna_ref[0, 0] = functools.reduce(jnp.maximum, norms_a)
    knb_ref[0, 0] = functools.reduce(jnp.maximum, norms_b)


def _rope_tiles(cqa, sqa, tk, cb, sb, gqc, gqs, gkc, gks):
    cb_q, sb_q = jnp.tile(cb, (1, GQA_HEADS // 2)), jnp.tile(sb, (1, GQA_HEADS // 2))
    return (jnp.tile(cqa, (1, MLA_HEADS)), jnp.tile(sqa, (1, MLA_HEADS)), tk,
            cb_q * gqc, sb_q * gqs, cb * gkc, sb * gks)


def _mix_in_rows(r0, x, g_ref, win_ref, gq_ref, wq_ref, gkv_ref, wk_ref, wk8_ref, wv_ref, tables,
                 bd_ref, gv_ref, ws_ref, bias_ref, gc_ref,
                 qaT_ref, ka_ref, vaT_ref, norms_a, qa8T_ref, ka8_ref,
                 qbT_ref, kb_ref, vbT_ref, norms_b, qb8T_ref, kb8_ref, ync_ref):
    cqa, sqa, tk, tcq, tsq, tck, tsk = tables
    tok = x.shape[0]
    rows_ = slice(r0, r0 + tok)
    h = _rms(x, g_ref[...])
    z = _dot(h.astype(BF16), win_ref[...])

    cq = _rms(z[:, Z_CQ:Z_CKV], gq_ref[...]).astype(BF16)
    qa = _dot(cq, wq_ref[...])
    half = MLA_HEADS * HEAD_SLOT
    q_a = qa[:, :half] * cqa + qa[:, half:] * sqa
    q_aT = q_a.T
    qaT_ref[0, :, rows_] = q_aT.astype(BF16)
    q_hi, q_lo = _split_e4m3(q_aT * Q8_SCALE)
    for hd in range(MLA_HEADS):
        nope = slice(hd * HEAD_SLOT, hd * HEAD_SLOT + MLA_NOPE)
        rope = slice(hd * HEAD_SLOT + MLA_NOPE, hd * HEAD_SLOT + MLA_NOPE + MLA_ROPE)
        base = hd * Q8_SLOT
        qa8T_ref[0, base:base + MLA_NOPE, rows_] = q_hi[nope]
        qa8T_ref[0, base + MLA_NOPE:base + 2 * MLA_NOPE, rows_] = q_hi[nope]
        qa8T_ref[0, base + 2 * MLA_NOPE:base + 3 * MLA_NOPE, rows_] = q_lo[nope]
        qa8T_ref[0, base + 3 * MLA_NOPE:base + 3 * MLA_NOPE + MLA_ROPE, rows_] = q_hi[rope]
        qa8T_ref[0, base + 3 * MLA_NOPE + MLA_ROPE:base + Q8_SLOT, rows_] = q_hi[rope]

    ckv = _rms(z[:, Z_CKV:Z_GQ], gkv_ref[...])
    kr = z[:, Z_KR:Z_END] * tk
    lhs = jnp.concatenate([ckv, kr], axis=1).astype(BF16)
    k_a = _dot(lhs, wk_ref[...]).astype(BF16)
    ka_ref[0, rows_] = k_a
    norms_a.append(_key_norms(k_a, MLA_HEADS))
    k8 = _dot(lhs, wk8_ref[...])
    depth = lax.broadcasted_iota(jnp.int32, (1, MLA_HEADS * Q8_SLOT), 1) % Q8_SLOT
    lo_block = jnp.logical_or(jnp.logical_and(depth >= MLA_NOPE, depth < 2 * MLA_NOPE),
                              depth >= 3 * MLA_NOPE + MLA_ROPE)
    ka8_ref[0, rows_] = jnp.where(lo_block, k8 - k8.astype(F8).astype(F32), k8).astype(F8)
    _store_vT(vaT_ref, _dot(lhs[:, :MLA_KV_RANK], wv_ref[...]).T, MLA_HEADS, MLA_V, r0)

    bd = bd_ref[...]
    g_q = z[:, Z_GQ:Z_GQS]
    n_q = lax.rsqrt(_group_ssq(g_q, bd) * (1.0 / GQA_DIM) + EPS)
    q_b = (g_q * tcq + z[:, Z_GQS:Z_GK] * tsq) * n_q
    q_bT = q_b.T
    zero = jnp.zeros((HEAD_SLOT - GQA_DIM, tok), BF16)
    for hd in range(GQA_HEADS):
        qbT_ref[0, hd * HEAD_SLOT:hd * HEAD_SLOT + GQA_DIM, rows_] = q_bT[hd * GQA_DIM:(hd + 1) * GQA_DIM].astype(BF16)
        qbT_ref[0, hd * HEAD_SLOT + GQA_DIM:(hd + 1) * HEAD_SLOT, rows_] = zero
    q_hi, q_lo = _split_e4m3(q_bT * Q8_SCALE)
    zero8 = jnp.zeros((Q8_SLOT - 3 * GQA_DIM, tok), F8)
    for hd in range(GQA_HEADS):
        rows = slice(hd * GQA_DIM, (hd + 1) * GQA_DIM)
        base = hd * Q8_SLOT
        qb8T_ref[0, base:base + GQA_DIM, rows_] = q_hi[rows]
        qb8T_ref[0, base + GQA_DIM:base + 2 * GQA_DIM, rows_] = q_hi[rows]
        qb8T_ref[0, base + 2 * GQA_DIM:base + 3 * GQA_DIM, rows_] = q_lo[rows]
        qb8T_ref[0, base + 3 * GQA_DIM:base + Q8_SLOT, rows_] = zero8
    g_k = z[:, Z_GK:Z_GKS]
    n_k = lax.rsqrt(_group_ssq(g_k, bd[:LANES, :LANES]) * (1.0 / GQA_DIM) + EPS)
    k_b = (g_k * tck + z[:, Z_GKS:Z_GV] * tsk) * n_k
    low = lax.broadcasted_iota(jnp.int32, (tok, LANES), 1) < GQA_DIM
    slots = [jnp.where(low, k_b, 0.0), jnp.where(low, pltpu.roll(k_b, GQA_DIM, 1), 0.0)]
    blocks = []
    for k_slot in slots:
        k8 = k_slot * K8_SCALE
        k_lo = k8 - k8.astype(F8).astype(F32)
        blocks += [jnp.where(low, k8, pltpu.roll(k_lo, GQA_DIM, 1)), k8]
    kb8_ref[0, rows_] = jnp.concatenate(blocks, axis=1).astype(F8)
    k_b = jnp.concatenate(slots, axis=1).astype(BF16)
    kb_ref[0, rows_] = k_b
    norms_b.append(_key_norms(k_b, GQA_KV_HEADS))
    _store_vT(vbT_ref, z[:, Z_GV:Z_U].T, GQA_KV_HEADS, GQA_DIM, r0)

    gm = jax.nn.gelu(z[:, Z_U:Z_KR])
    u = gm[:, :W_C]
    vv = _rms(gm[:, W_C:], gv_ref[...])
    lane_grp = lax.broadcasted_iota(jnp.int32, (GMLP_CHUNK, W_C), 1) // GMLP_DIM
    ws = ws_ref[...]
    bias = bias_ref[...]
    ycs = []
    for n in range(tok // GMLP_CHUNK):
        rows = slice(n * GMLP_CHUNK, (n + 1) * GMLP_CHUNK)
        r = _dot(ws, vv[rows].astype(BF16))
        mixed = r[3 * GMLP_CHUNK:]
        for grp in range(GMLP_GROUPS - 2, -1, -1):
            mixed = jnp.where(lane_grp == grp, r[grp * GMLP_CHUNK:(grp + 1) * GMLP_CHUNK], mixed)
        ycs.append(u[rows] * (mixed + bias))
    ync_ref[0, rows_] = _rms(jnp.concatenate(ycs, axis=0), gc_ref[...]).astype(BF16)


def _mix_in(x, p):
    B, S, D = x.shape
    T = TOK_TILE
    nkv = T // KV_TILE
    tile = lambda w: pl.BlockSpec((1, T, w), lambda s, b: (b, s, 0))
    def tab(t):
        if isinstance(t, _Layer):
            return pl.BlockSpec((None, T, t.stack.shape[2]), lambda s, b: (t.layer, s, 0))
        return pl.BlockSpec((T, t.shape[1]), lambda s, b: (s, 0))
    tposed = lambda r: pl.BlockSpec((1, r, T), lambda s, b: (b, 0, s))
    blocked = lambda r: pl.BlockSpec((1, nkv, r, KV_TILE), lambda s, b: (b, s, 0, 0))
    consts = [p["mix_norm"], p["w_in"], p["mla_q_norm"], p["w_uq"], p["mla_kv_norm"], p["w_k"], p["w_k8"], p["w_v"]]
    tabs = [p["cqa"], p["sqa"], p["tk"], p["cb"], p["sb"]]
    consts2 = [p["gq_c"], p["gq_s"], p["gk_c"], p["gk_s"], p["bd"], p["gmlp_v_norm"], p["w_s"], p["bias_s"], p["out_norm_c"]]
    in_specs = ([tile(D)] + [_const_spec(c) for c in consts] + [tab(t) for t in tabs]
                + [_const_spec(c) for c in consts2])
    out_shape = [
        jax.ShapeDtypeStruct((B, MLA_HEADS * HEAD_SLOT, S), BF16),
        jax.ShapeDtypeStruct((B, S, MLA_HEADS * HEAD_SLOT), BF16),
        jax.ShapeDtypeStruct((B, S // KV_TILE, MLA_HEADS * (MLA_V + V_PAD), KV_TILE), BF16),
        jax.ShapeDtypeStruct((B, S // T, F32_ROWS, LANES), F32),
        jax.ShapeDtypeStruct((B, MLA_HEADS * Q8_SLOT, S), F8),
        jax.ShapeDtypeStruct((B, S, MLA_HEADS * Q8_SLOT), F8),
        jax.ShapeDtypeStruct((B, GQA_HEADS * HEAD_SLOT, S), BF16),
        jax.ShapeDtypeStruct((B, S, GQA_KV_HEADS * HEAD_SLOT), BF16),
        jax.ShapeDtypeStruct((B, S // KV_TILE, GQA_KV_HEADS * (GQA_DIM + V_PAD), KV_TILE), BF16),
        jax.ShapeDtypeStruct((B, S // T, F32_ROWS, LANES), F32),
        jax.ShapeDtypeStruct((B, GQA_HEADS * Q8_SLOT, S), F8),
        jax.ShapeDtypeStruct((B, S, GQA_KV_HEADS * Q8_SLOT), F8),
        jax.ShapeDtypeStruct((B, S, W_C), BF16),
    ]
    norms = pl.BlockSpec((1, 1, F32_ROWS, LANES), lambda s, b: (b, s, 0, 0))
    out_specs = [tposed(MLA_HEADS * HEAD_SLOT), tile(MLA_HEADS * HEAD_SLOT), blocked(MLA_HEADS * (MLA_V + V_PAD)),
                 norms, tposed(MLA_HEADS * Q8_SLOT), tile(MLA_HEADS * Q8_SLOT),
                 tposed(GQA_HEADS * HEAD_SLOT), tile(GQA_KV_HEADS * HEAD_SLOT),
                 blocked(GQA_KV_HEADS * (GQA_DIM + V_PAD)), norms, tposed(GQA_HEADS * Q8_SLOT),
                 tile(GQA_KV_HEADS * Q8_SLOT), tile(W_C)]
    return pl.pallas_call(
        _mix_in_kernel, grid=(S // T, B), in_specs=in_specs, out_specs=out_specs, out_shape=out_shape,
        compiler_params=_params(2), name="mix_in",
    )(x, *map(_operand, consts + tabs + consts2))


def _attn_kernel(q_ref, k_ref, v_ref, kn_ref, g_ref, q8_ref, k8_ref, o_ref,
                 s_ref, p_ref, m_ref, acc_ref, oT_ref, *, n_heads, k_slots, v_rows, dv):
    nkv = v_ref.shape[1]
    n_blk = q_ref.shape[2] // Q_BLOCK
    n_work = n_heads * n_blk
    assert n_work % 2 == 0

    def cols(w):
        return slice((w % n_blk) * Q_BLOCK, (w % n_blk + 1) * Q_BLOCK)

    kmax = jnp.max(kn_ref[0], axis=0)
    bound = jnp.zeros((1, 1), F32)
    qn2_max = jnp.zeros((1, 1), F32)
    kn2_max = jnp.zeros((1, 1), F32)
    for hd in range(n_heads):
        qf = q_ref[0, hd * HEAD_SLOT:(hd + 1) * HEAD_SLOT, :].astype(F32)
        qn2 = jnp.sum(qf * qf, axis=0, keepdims=True)
        kn2 = kmax[k_slots[hd]:k_slots[hd] + 1, 0:1]
        u = jnp.sqrt(qn2 * kn2)
        for c in range(n_blk):
            m_ref[hd * n_blk + c] = u[:, c * Q_BLOCK:(c + 1) * Q_BLOCK]
        bound = jnp.maximum(bound, jnp.max(u, axis=1, keepdims=True))
        qn2_max = jnp.maximum(qn2_max, jnp.max(qn2, axis=1, keepdims=True))
        kn2_max = jnp.maximum(kn2_max, kn2)
    stabilise = jnp.logical_and(jnp.max(bound) <= STABILISER_MAX, jnp.logical_and(
        jnp.max(qn2_max) <= (FP8_MAX_SCALED / Q8_SCALE) ** 2, jnp.max(kn2_max) <= (FP8_MAX_SCALED / K8_SCALE) ** 2))
    acc_ref[...] = jnp.zeros(acc_ref.shape, F32)

    def run(stabilised):
        def scores(w, off):
            hd = w // n_blk
            ks, qs, depth = (k8_ref, q8_ref, Q8_SLOT) if stabilised else (k_ref, q_ref, HEAD_SLOT)
            k = ks[0, pl.ds(off, KV_TILE), k_slots[hd] * depth:(k_slots[hd] + 1) * depth]
            qT = qs[0, hd * depth:(hd + 1) * depth, cols(w)]
            s_ref[w % 2] = _dot(k, qT)

        scores(0, 0)
        last = n_work - 1

        def v_tile(w, j):
            hd = w // n_blk
            return v_ref[0, j, v_rows[hd]:v_rows[hd] + dv + V_PAD, :]

        def pv(w, j):
            acc_ref[w] += _dot(v_tile(w, j), p_ref[w])

        if stabilised:
            p_ref[last] = jnp.zeros(p_ref.shape[1:], BF16)

        def step(j, carry):
            off = pl.multiple_of(j * KV_TILE, KV_TILE)
            off_next = pl.multiple_of(jnp.minimum(j + 1, nkv - 1) * KV_TILE, KV_TILE)
            for w in range(n_work):
                if w + 1 < n_work:
                    scores(w + 1, off)
                else:
                    scores(0, off_next)
                if stabilised:
                    if w > 0:
                        pv(w - 1, j)
                    else:
                        pv(last, jnp.maximum(j - 1, 0))
                    s = s_ref[w % 2] * (1.0 / (Q8_SCALE * K8_SCALE)) - m_ref[w]
                    p_ref[w] = jnp.exp2(s.astype(BF16))
                else:
                    m_old = m_ref[w]
                    m_new = jnp.maximum(m_old, jnp.max(s_ref[w % 2], axis=0, keepdims=True))
                    m_ref[w] = m_new
                    p_ref[w] = jnp.exp2(s_ref[w % 2] - m_new).astype(BF16)
                    acc_ref[w] = jnp.exp2(m_old - m_new) * acc_ref[w] + _dot(v_tile(w, j), p_ref[w])
            return carry

        lax.fori_loop(0, nkv, step, 0, unroll=16 if stabilised else 1)
        if stabilised:
            pv(last, nkv - 1)

    def running_max():
        m_ref[...] = jnp.full(m_ref.shape, -1e30, F32)
        run(False)

    lax.cond(stabilise, lambda: run(True), running_max)
    for w in range(n_work):
        hd = w // n_blk
        oT_ref[hd * dv:(hd + 1) * dv, cols(w)] = acc_ref[w, :dv] / acc_ref[w, dv:dv + 1]
    o_ref[0] = _rms(oT_ref[...].T, g_ref[...]).astype(BF16)


def _attention(qT, k, vT, kn, gain, q8T, k8, *, n_heads, k_slots, v_rows, dv, name):
    B, _, S = qT.shape
    kernel = functools.partial(_attn_kernel, n_heads=n_heads, k_slots=k_slots, v_rows=v_rows, dv=dv)
    n_work = n_heads * (Q_TILE // Q_BLOCK)
    q_tile = lambda slot: pl.BlockSpec((1, n_heads * slot, Q_TILE), lambda b, i: (b, 0, i))
    whole = lambda shape: pl.BlockSpec((1,) + shape[1:], lambda b, i: (b,) + (0,) * (len(shape) - 1),
                                       pipeline_mode=pl.Buffered(1))
    return pl.pallas_call(
        kernel, grid=(B, S // Q_TILE),
        in_specs=[q_tile(HEAD_SLOT), whole(k.shape), whole(vT.shape), whole(kn.shape), _const_spec(gain),
                  q_tile(Q8_SLOT), whole(k8.shape)],
        out_specs=pl.BlockSpec((1, Q_TILE, n_heads * dv), lambda b, i: (b, i, 0)),
        out_shape=jax.ShapeDtypeStruct((B, S, n_heads * dv), BF16),
        scratch_shapes=[pltpu.VMEM((2, KV_TILE, Q_BLOCK), F32), pltpu.VMEM((n_work, KV_TILE, Q_BLOCK), BF16),
                        pltpu.VMEM((n_work, 1, Q_BLOCK), F32), pltpu.VMEM((n_work, dv + V_PAD, Q_BLOCK), F32),
                        pltpu.VMEM((n_heads * dv, Q_TILE), F32)],
        compiler_params=_params(2), name=name,
    )(qT, k, vT, kn, _operand(gain), q8T, k8)


def _mem_kv_kernel(mem_ref, g_ref, w_ref, kT_ref, v_ref):
    kv = _dot(_rms(mem_ref[0], g_ref[...]).astype(BF16), w_ref[...])
    width = MEM_HEADS * MEM_DIM
    kT_ref[0] = kv[:, :width].T.astype(BF16)
    ones = jnp.ones((kv.shape[0], MEM_DIM), BF16)
    for hd in range(MEM_HEADS):
        v_ref[0, hd, :, :MEM_DIM] = kv[:, width + hd * MEM_DIM:width + (hd + 1) * MEM_DIM].astype(BF16)
        v_ref[0, hd, :, MEM_DIM:] = ones


def _mem_kv(mem, gain, w_kv):
    B, Tm, D = mem.shape
    width = MEM_HEADS * MEM_DIM
    return pl.pallas_call(
        _mem_kv_kernel, grid=(B,),
        in_specs=[pl.BlockSpec((1, Tm, D), lambda b: (b, 0, 0)), _const_spec(gain), _const_spec(w_kv)],
        out_specs=[pl.BlockSpec((1, width, Tm), lambda b: (b, 0, 0)),
                   pl.BlockSpec((1, MEM_HEADS, Tm, 2 * MEM_DIM), lambda b: (b, 0, 0, 0))],
        out_shape=[jax.ShapeDtypeStruct((B, width, Tm), BF16),
                   jax.ShapeDtypeStruct((B, MEM_HEADS, Tm, 2 * MEM_DIM), BF16)],
        compiler_params=_params(1), name="mem_kv",
    )(mem, _operand(gain), _operand(w_kv))


def _out_mem_kernel(ya_ref, yb_ref, yc_ref, x_ref, wout_ref, g_ref, wq_ref, kT_ref, v_ref, wo_ref, o_ref):
    y = jnp.concatenate([ya_ref[0], yb_ref[0], yc_ref[0]], axis=1)
    x1 = x_ref[0] + _dot(y, wout_ref[...])
    h = _rms(x1, g_ref[...]).astype(BF16)
    q = (_dot(h, wq_ref[...]) * (MEM_DIM ** -0.5)).astype(BF16)
    heads = []
    for hd in range(MEM_HEADS):
        s = _dot(q[:, hd * MEM_DIM:(hd + 1) * MEM_DIM], kT_ref[0, hd * MEM_DIM:(hd + 1) * MEM_DIM, :])
        p = jnp.exp(s - jnp.max(s, axis=-1, keepdims=True)).astype(BF16)
        pv = _dot(p, v_ref[0, hd])
        heads.append(pv[:, :MEM_DIM] / pv[:, MEM_DIM:])
    o = jnp.concatenate(heads, axis=1).astype(BF16)
    o_ref[0] = x1 + _dot(o, wo_ref[...])


def _out_mem(ya, yb, yc, x, p, mem_kT, mem_v):
    B, S, D = x.shape
    T = WIDE_TILE
    tile = lambda w: pl.BlockSpec((1, T, w), lambda b, s: (b, s, 0))
    per_b = lambda a: pl.BlockSpec((1,) + a.shape[1:], lambda b, s: (b,) + (0,) * (a.ndim - 1))
    consts = [p["w_out"], p["mem_x_norm"], p["mem_w_q"]]
    return pl.pallas_call(
        _out_mem_kernel, grid=(B, S // T),
        in_specs=[tile(W_A), tile(W_B), tile(W_C), tile(D)] + [_const_spec(c) for c in consts]
                 + [per_b(mem_kT), per_b(mem_v), _const_spec(p["mem_w_o"])],
        out_specs=tile(D), out_shape=jax.ShapeDtypeStruct((B, S, D), F32),
        compiler_params=_params(2), name="out_mem",
    )(ya, yb, yc, x, *map(_operand, consts), mem_kT, mem_v, _operand(p["mem_w_o"]))


def _ffn_kernel(x_ref, xp_ref, xn_ref, g_ref, wup_ref, cw_ref, cb_ref, wdn_ref, fg_ref, o_ref,
                h_ref, act_ref, *, final):
    tok = x_ref.shape[1]
    i = pl.program_id(1)
    g = g_ref[...]
    x = x_ref[0]
    keep_prev = (i > 0).astype(F32)
    keep_next = (i < pl.num_programs(1) - 1).astype(F32)
    h_ref[0:HALO] = (_rms(xp_ref[0], g) * keep_prev).astype(BF16)
    h_ref[HALO:HALO + tok] = _rms(x, g).astype(BF16)
    h_ref[HALO + tok:] = (_rms(xn_ref[0], g) * keep_next).astype(BF16)
    hext = h_ref[...]
    rows = tok + 2 * HALO

    def conv_up(cols):
        a = _dot(hext, wup_ref[:, cols])
        w = cw_ref[:, cols]
        return (pltpu.roll(a, 1, 0)[HALO:HALO + tok] * w[0:1] + a[HALO:HALO + tok] * w[1:2]
                + pltpu.roll(a, rows - 1, 0)[HALO:HALO + tok] * w[2:3] + cb_ref[:, cols])

    for c in range(D_FF // FF_CHUNK):
        gate = conv_up(slice(FF_CHUNK * c, FF_CHUNK * (c + 1)))
        val = conv_up(slice(D_FF + FF_CHUNK * c, D_FF + FF_CHUNK * (c + 1)))
        act_ref[:, FF_CHUNK * c:FF_CHUNK * (c + 1)] = (jax.nn.silu(gate) * val).astype(BF16)
    y = x + _dot(act_ref[...], wdn_ref[...])
    if final:
        y = _rms(y, fg_ref[...])
    o_ref[0] = y


def _ffn(x, p, final_gain, *, final):
    B, S, D = x.shape
    T = WIDE_TILE
    per_tile = T // HALO
    n_halo = S // HALO
    tile = pl.BlockSpec((1, T, D), lambda b, s: (b, s, 0))
    prev = pl.BlockSpec((1, HALO, D), lambda b, s: (b, jnp.maximum(s * per_tile - 1, 0), 0))
    nxt = pl.BlockSpec((1, HALO, D), lambda b, s: (b, jnp.minimum((s + 1) * per_tile, n_halo - 1), 0))
    consts = [p["ffn_norm"], p["ffn_w_up"], p["ffn_conv_w"], p["ffn_conv_b"], p["ffn_w_down"], final_gain]
    return pl.pallas_call(
        functools.partial(_ffn_kernel, final=final), grid=(B, S // T),
        in_specs=[tile, prev, nxt] + [_const_spec(c) for c in consts],
        out_specs=tile, out_shape=jax.ShapeDtypeStruct((B, S, D), F32),
        scratch_shapes=[pltpu.VMEM((T + 2 * HALO, D), BF16), pltpu.VMEM((T, D_FF), BF16)],
        compiler_params=_params(2), name="ffn_final" if final else "ffn",
    )(x, x, x, *map(_operand, consts))


def _swap_pairs(w):
    n = w.shape[-1]
    return w.reshape(*w.shape[:-1], n // 2, 2)[..., ::-1].reshape(w.shape)


def _rope_tables(S, d_rot):
    rows = S // GRID_W
    row = jnp.repeat(jnp.arange(rows, dtype=F32), GRID_W)
    col = jnp.tile(jnp.arange(GRID_W, dtype=F32), rows)
    n = d_rot // 4
    inv = ROPE_THETA ** (-jnp.arange(n, dtype=F32) / n)
    ang = jnp.concatenate([row[:, None] * inv, col[:, None] * inv], axis=-1)
    cos, sin = jnp.cos(ang), jnp.sin(ang)
    c = jnp.repeat(cos, 2, axis=-1)
    s = jnp.stack([-sin, sin], axis=-1).reshape(S, d_rot)
    return c, s


def _prep_params(S, P):
    L = P["w_in"].shape[0]
    row = lambda v: v[:, None, :]
    ca, sa = _rope_tables(S, MLA_ROPE)
    cb, sb = _rope_tables(S, GQA_DIM)
    p, shared = {}, {}
    c_q, c_kv, k_rope, g_q, g_k, g_v, g_m = jnp.split(P["w_in"], [256, 384, 416, 800, 928, 1056], axis=2)
    pad = jnp.zeros((L, D_MODEL, Z_END - Z_KR - 2 * MLA_ROPE), F32)
    p["w_in"] = jnp.concatenate(
        [c_q, c_kv, g_q, _swap_pairs(g_q), g_k, _swap_pairs(g_k), g_v, g_m, k_rope, _swap_pairs(k_rope), pad],
        axis=2).astype(BF16)
    for name in ("mix_norm", "mla_q_norm", "mla_kv_norm", "gmlp_v_norm", "mem_x_norm", "mem_kv_norm", "ffn_norm",
                 "ffn_conv_b"):
        p[name] = row(P[name])

    w_uq = P["mla_w_uq"].reshape(L, MLA_Q_RANK, MLA_HEADS, MLA_NOPE + MLA_ROPE)
    zpad = jnp.zeros((L, MLA_Q_RANK, MLA_HEADS, HEAD_SLOT - MLA_NOPE - MLA_ROPE), F32)
    main = jnp.concatenate([w_uq, zpad], axis=-1)
    swapped = jnp.concatenate([jnp.zeros_like(w_uq[..., :MLA_NOPE]), _swap_pairs(w_uq[..., MLA_NOPE:]), zpad], axis=-1)
    p["w_uq"] = jnp.concatenate([main.reshape(L, MLA_Q_RANK, -1), swapped.reshape(L, MLA_Q_RANK, -1)],
                                axis=2).astype(BF16)
    scale_a = (MLA_NOPE + MLA_ROPE) ** -0.5 * LOG2E
    ones = jnp.ones((S, MLA_NOPE), F32)
    zeros_n = jnp.zeros((S, MLA_NOPE), F32)
    zeros_p = jnp.zeros((S, HEAD_SLOT - MLA_NOPE - MLA_ROPE), F32)
    shared["cqa"] = jnp.concatenate([ones, ca, zeros_p], axis=1) * scale_a
    shared["sqa"] = jnp.concatenate([zeros_n, sa, zeros_p], axis=1) * scale_a

    w_ukv = P["mla_w_ukv"].reshape(L, MLA_KV_RANK, MLA_HEADS, MLA_NOPE + MLA_V)
    k_lat = jnp.concatenate(
        [w_ukv[..., :MLA_NOPE], jnp.zeros((L, MLA_KV_RANK, MLA_HEADS, HEAD_SLOT - MLA_NOPE), F32)], axis=-1)
    place = jnp.concatenate([jnp.zeros((MLA_ROPE, MLA_NOPE), F32), jnp.eye(MLA_ROPE, dtype=F32),
                             jnp.zeros((MLA_ROPE, HEAD_SLOT - MLA_NOPE - MLA_ROPE), F32)], axis=1)
    place = jnp.broadcast_to(jnp.tile(place, (1, MLA_HEADS)), (L, MLA_ROPE, MLA_HEADS * HEAD_SLOT))
    zrows = jnp.zeros((L, Z_END - Z_KR - 2 * MLA_ROPE, MLA_HEADS * HEAD_SLOT), F32)
    p["w_k"] = jnp.concatenate([k_lat.reshape(L, MLA_KV_RANK, -1), place, place, zrows], axis=1).astype(BF16)
    p["w_v"] = w_ukv[..., MLA_NOPE:].reshape(L, MLA_KV_RANK, -1).astype(BF16)
    nope_w = w_ukv[..., :MLA_NOPE]
    lat8 = jnp.concatenate([nope_w, nope_w, nope_w, jnp.zeros((L, MLA_KV_RANK, MLA_HEADS, 2 * MLA_ROPE), F32)], axis=-1)
    place8 = jnp.concatenate([jnp.zeros((MLA_ROPE, 3 * MLA_NOPE), F32), jnp.eye(MLA_ROPE, dtype=F32),
                              jnp.eye(MLA_ROPE, dtype=F32)], axis=1)
    place8 = jnp.broadcast_to(jnp.tile(place8, (1, MLA_HEADS)), (L, MLA_ROPE, MLA_HEADS * Q8_SLOT))
    zrows8 = jnp.zeros((L, Z_END - Z_KR - 2 * MLA_ROPE, MLA_HEADS * Q8_SLOT), F32)
    p["w_k8"] = (jnp.concatenate([lat8.reshape(L, MLA_KV_RANK, -1), place8, place8, zrows8], axis=1)
                 * K8_SCALE).astype(BF16)
    shared["tk"] = jnp.concatenate([ca, sa, jnp.zeros((S, Z_END - Z_KR - 2 * MLA_ROPE), F32)], axis=1)

    gq = row(jnp.tile(P["gqa_q_norm"], (1, GQA_HEADS)) * (GQA_DIM ** -0.5 * LOG2E))
    gk = row(jnp.tile(P["gqa_k_norm"], (1, GQA_KV_HEADS)))
    p["gq_c"], p["gq_s"], p["gk_c"], p["gk_s"] = gq, _swap_pairs(gq), gk, _swap_pairs(gk)
    shared["cb"] = jnp.tile(cb, (1, LANES // GQA_DIM))
    shared["sb"] = jnp.tile(sb, (1, LANES // GQA_DIM))
    grp = jnp.arange(W_B) // GQA_DIM
    shared["bd"] = (grp[:, None] == grp[None, :]).astype(BF16)

    p["w_s"] = P["gmlp_w_s"].reshape(L, GMLP_GROUPS * GMLP_CHUNK, GMLP_CHUNK).astype(BF16)
    p["bias_s"] = jnp.repeat(jnp.swapaxes(P["gmlp_b_s"], 1, 2), GMLP_DIM, axis=2)
    p["out_norm_a"] = row(P["out_norm"][:, :W_A])
    p["out_norm_b"] = row(P["out_norm"][:, W_A:W_A + W_B])
    p["out_norm_c"] = row(P["out_norm"][:, W_A + W_B:])
    for name in ("w_out", "mem_w_q", "mem_w_kv", "mem_w_o", "ffn_w_up", "ffn_w_down"):
        p[name] = P[name].astype(BF16)
    p["ffn_conv_w"] = P["ffn_conv_w"]
    return p, shared


def kernel(x, mem, mix_norm, w_in, mla_q_norm, mla_w_uq, mla_kv_norm, mla_w_ukv, gqa_q_norm, gqa_k_norm, gmlp_v_norm, gmlp_w_s, gmlp_b_s, out_norm, w_out, mem_x_norm, mem_kv_norm, mem_w_q, mem_w_kv, mem_w_o, ffn_norm, ffn_w_up, ffn_conv_w, ffn_conv_b, ffn_w_down, final_norm):
    P = dict(mix_norm=mix_norm, w_in=w_in, mla_q_norm=mla_q_norm, mla_w_uq=mla_w_uq, mla_kv_norm=mla_kv_norm,
             mla_w_ukv=mla_w_ukv, gqa_q_norm=gqa_q_norm, gqa_k_norm=gqa_k_norm, gmlp_v_norm=gmlp_v_norm,
             gmlp_w_s=gmlp_w_s, gmlp_b_s=gmlp_b_s, out_norm=out_norm, w_out=w_out, mem_x_norm=mem_x_norm,
             mem_kv_norm=mem_kv_norm, mem_w_q=mem_w_q, mem_w_kv=mem_w_kv, mem_w_o=mem_w_o, ffn_norm=ffn_norm,
             ffn_w_up=ffn_w_up, ffn_conv_w=ffn_conv_w, ffn_conv_b=ffn_conv_b, ffn_w_down=ffn_w_down)
    B, S, D = x.shape
    assert D == D_MODEL and S % TOK_TILE == 0 and S % WIDE_TILE == 0 and S % GRID_W == 0
    depth = w_in.shape[0]
    stacked, shared = _prep_params(S, P)
    final_gain = final_norm.reshape(1, -1)
    group = GQA_HEADS // GQA_KV_HEADS
    for l in range(depth):
        p = dict(shared, **{name: _Layer(v, l) for name, v in stacked.items()})
        qaT, ka, vaT, kna, qa8T, ka8, qbT, kb, vbT, knb, qb8T, kb8, ync = _mix_in(x, p)
        yna = _attention(qaT, ka, vaT, kna, p["out_norm_a"], qa8T, ka8, n_heads=MLA_HEADS,
                         k_slots=tuple(range(MLA_HEADS)),
                         v_rows=tuple(h * (MLA_V + V_PAD) for h in range(MLA_HEADS)), dv=MLA_V, name="attn_mla")
        ynb = _attention(qbT, kb, vbT, knb, p["out_norm_b"], qb8T, kb8, n_heads=GQA_HEADS,
                         k_slots=tuple(h // group for h in range(GQA_HEADS)),
                         v_rows=tuple((h // group) * (GQA_DIM + V_PAD) for h in range(GQA_HEADS)), dv=GQA_DIM, name="attn_gqa")
        mem_kT, mem_v = _mem_kv(mem, p["mem_kv_norm"], p["mem_w_kv"])
        x = _out_mem(yna, ynb, ync, x, p, mem_kT, mem_v)
        x = _ffn(x, p, final_gain, final=(l == depth - 1))
    return x
```

```python
import functools
from typing import NamedTuple

import jax
import jax.numpy as jnp
from jax import lax
from jax.experimental import pallas as pl
from jax.experimental.pallas import tpu as pltpu

F32 = jnp.float32
BF16 = jnp.bfloat16
F8 = jnp.float8_e4m3fn

D_MODEL = 1024
GRID_W = 64
ROPE_THETA = 10000.0
EPS = 1e-6
MLA_HEADS = 6
MLA_NOPE = 64
MLA_ROPE = 32
MLA_V = 64
MLA_Q_RANK = 256
MLA_KV_RANK = 128
GQA_HEADS = 6
GQA_KV_HEADS = 2
GQA_DIM = 64
GMLP_GROUPS = 4
GMLP_DIM = 64
GMLP_CHUNK = 128
W_A = MLA_HEADS * MLA_V
W_B = GQA_HEADS * GQA_DIM
W_C = GMLP_GROUPS * GMLP_DIM
MEM_HEADS = 4
MEM_DIM = 128
D_FF = 2816

LANES = 128
HEAD_SLOT = LANES
TOK_TILE = 1024
MIX_ROWS = 256
WIDE_TILE = 1024
Q_TILE = 512
Q_BLOCK = 256
KV_TILE = 512
F32_ROWS = 8
BF16_ROWS = 16
V_PAD = BF16_ROWS
LOG2E = 1.4426950408889634
STABILISER_MAX = 50.0
Q8_SLOT = 256
Q8_SCALE = 32.0
K8_SCALE = 8.0
FP8_MAX_SCALED = 400.0
FF_CHUNK = 256
HALO = BF16_ROWS
VMEM_LIMIT = 56 * 1024 * 1024

Z_CQ, Z_CKV, Z_GQ, Z_GQS, Z_GK, Z_GKS, Z_GV, Z_U, Z_VV, Z_KR, Z_END = (
    0, 256, 384, 768, 1152, 1280, 1408, 1536, 1792, 2048, 2176)


def _rms(x, g):
    return x * lax.rsqrt(jnp.mean(x * x, axis=-1, keepdims=True) + EPS) * g


def _dot(a, b):
    return jnp.dot(a, b, preferred_element_type=F32)


class _Layer(NamedTuple):
    stack: jax.Array
    layer: int


def _operand(a):
    return a.stack if isinstance(a, _Layer) else a


def _const_spec(a):
    if isinstance(a, _Layer):
        zeros = (0,) * (a.stack.ndim - 1)
        return pl.BlockSpec((None,) + a.stack.shape[1:], lambda *_: (a.layer,) + zeros, pipeline_mode=pl.Buffered(1))
    zeros = (0,) * a.ndim
    return pl.BlockSpec(a.shape, lambda *_: zeros, pipeline_mode=pl.Buffered(1))


def _params(n_axes):
    return pltpu.CompilerParams(dimension_semantics=("arbitrary",) * n_axes,
                                vmem_limit_bytes=VMEM_LIMIT)


def _group_ssq(v, bd):
    sq = v * v
    hi = sq.astype(BF16)
    lo = (sq - hi.astype(F32)).astype(BF16)
    return _dot(hi, bd) + _dot(lo, bd)


def _split_e4m3(x):
    hi = x.astype(F8)
    return hi, (x - hi.astype(F32)).astype(F8)


def _store_vT(ref, vT, n_heads, dv, tok0):
    width = vT.shape[1]
    assert KV_TILE % width == 0 and tok0 % width == 0
    n, cols = tok0 // KV_TILE, slice(tok0 % KV_TILE, tok0 % KV_TILE + width)
    ext = (lax.broadcasted_iota(jnp.int32, (V_PAD, width), 0) == 0).astype(BF16)
    for hd in range(n_heads):
        base = hd * (dv + V_PAD)
        ref[0, n, base:base + dv, cols] = vT[hd * dv:(hd + 1) * dv].astype(BF16)
        ref[0, n, base + dv:base + dv + V_PAD, cols] = ext


def _key_norms(k, n_slots):
    rows = []
    for n in range(n_slots):
        x = k[:, n * HEAD_SLOT:(n + 1) * HEAD_SLOT].astype(F32)
        n2 = jnp.max(jnp.sum(x * x, axis=-1, keepdims=True), axis=0, keepdims=True)
        rows.append(jnp.broadcast_to(n2, (1, LANES)))
    rows.append(jnp.zeros((F32_ROWS - n_slots, LANES), F32))
    return jnp.concatenate(rows, axis=0)


def _mix_in_kernel(x_ref, g_ref, win_ref, gq_ref, wq_ref, gkv_ref, wk_ref, wk8_ref, wv_ref,
                   cqa_ref, sqa_ref, tk_ref, cb_ref, sb_ref, gqc_ref, gqs_ref, gkc_ref, gks_ref,
                   bd_ref, gv_ref, ws_ref, bias_ref, gc_ref,
                   qaT_ref, ka_ref, vaT_ref, kna_ref, qa8T_ref, ka8_ref,
                   qbT_ref, kb_ref, vbT_ref, knb_ref, qb8T_ref, kb8_ref, ync_ref):
    norms_a, norms_b = [], []
    for r0 in range(0, x_ref.shape[1], MIX_ROWS):
        rows_ = slice(r0, r0 + MIX_ROWS)
        _mix_in_rows(r0, x_ref[0, rows_], g_ref, win_ref, gq_ref, wq_ref, gkv_ref, wk_ref, wk8_ref, wv_ref,
                     _rope_tiles(cqa_ref[rows_], sqa_ref[rows_], tk_ref[rows_], cb_ref[rows_], sb_ref[rows_],
                                 gqc_ref[...], gqs_ref[...], gkc_ref[...], gks_ref[...]),
                     bd_ref, gv_ref, ws_ref, bias_ref, gc_ref,
                     qaT_ref, ka_ref, vaT_ref, norms_a, qa8T_ref, ka8_ref,
                     qbT_ref, kb_ref, vbT_ref, norms_b, qb8T_ref, kb8_ref, ync_ref)
    kna_ref[0, 0] = functools.reduce(jnp.maximum, norms_a)
    knb_ref[0, 0] = functools.reduce(jnp.maximum, norms_b)


def _rope_tiles(cqa, sqa, tk, cb, sb, gqc, gqs, gkc, gks):
    cb_q, sb_q = jnp.tile(cb, (1, GQA_HEADS // 2)), jnp.tile(sb, (1, GQA_HEADS // 2))
    return (jnp.tile(cqa, (1, MLA_HEADS)), jnp.tile(sqa, (1, MLA_HEADS)), tk,
            cb_q * gqc, sb_q * gqs, cb * gkc, sb * gks)


def _mix_in_rows(r0, x, g_ref, win_ref, gq_ref, wq_ref, gkv_ref, wk_ref, wk8_ref, wv_ref, tables,
                 bd_ref, gv_ref, ws_ref, bias_ref, gc_ref,
                 qaT_ref, ka_ref, vaT_ref, norms_a, qa8T_ref, ka8_ref,
                 qbT_ref, kb_ref, vbT_ref, norms_b, qb8T_ref, kb8_ref, ync_ref):
    cqa, sqa, tk, tcq, tsq, tck, tsk = tables
    tok = x.shape[0]
    rows_ = slice(r0, r0 + tok)
    h = _rms(x, g_ref[...])
    z = _dot(h.astype(BF16), win_ref[...])

    cq = _rms(z[:, Z_CQ:Z_CKV], gq_ref[...]).astype(BF16)
    qa = _dot(cq, wq_ref[...])
    half = MLA_HEADS * HEAD_SLOT
    q_a = qa[:, :half] * cqa + qa[:, half:] * sqa
    q_aT = q_a.T
    qaT_ref[0, :, rows_] = q_aT.astype(BF16)
    q_hi, q_lo = _split_e4m3(q_aT * Q8_SCALE)
    for hd in range(MLA_HEADS):
        nope = slice(hd * HEAD_SLOT, hd * HEAD_SLOT + MLA_NOPE)
        rope = slice(hd * HEAD_SLOT + MLA_NOPE, hd * HEAD_SLOT + MLA_NOPE + MLA_ROPE)
        base = hd * Q8_SLOT
        qa8T_ref[0, base:base + MLA_NOPE, rows_] = q_hi[nope]
        qa8T_ref[0, base + MLA_NOPE:base + 2 * MLA_NOPE, rows_] = q_hi[nope]
        qa8T_ref[0, base + 2 * MLA_NOPE:base + 3 * MLA_NOPE, rows_] = q_lo[nope]
        qa8T_ref[0, base + 3 * MLA_NOPE:base + 3 * MLA_NOPE + MLA_ROPE, rows_] = q_hi[rope]
        qa8T_ref[0, base + 3 * MLA_NOPE + MLA_ROPE:base + Q8_SLOT, rows_] = q_hi[rope]

    ckv = _rms(z[:, Z_CKV:Z_GQ], gkv_ref[...])
    kr = z[:, Z_KR:Z_END] * tk
    lhs = jnp.concatenate([ckv, kr], axis=1).astype(BF16)
    k_a = _dot(lhs, wk_ref[...]).astype(BF16)
    ka_ref[0, rows_] = k_a
    norms_a.append(_key_norms(k_a, MLA_HEADS))
    k8 = _dot(lhs, wk8_ref[...])
    depth = lax.broadcasted_iota(jnp.int32, (1, MLA_HEADS * Q8_SLOT), 1) % Q8_SLOT
    lo_block = jnp.logical_or(jnp.logical_and(depth >= MLA_NOPE, depth < 2 * MLA_NOPE),
                              depth >= 3 * MLA_NOPE + MLA_ROPE)
    ka8_ref[0, rows_] = jnp.where(lo_block, k8 - k8.astype(F8).astype(F32), k8).astype(F8)
    _store_vT(vaT_ref, _dot(lhs[:, :MLA_KV_RANK], wv_ref[...]).T, MLA_HEADS, MLA_V, r0)

    bd = bd_ref[...]
    g_q = z[:, Z_GQ:Z_GQS]
    n_q = lax.rsqrt(_group_ssq(g_q, bd) * (1.0 / GQA_DIM) + EPS)
    q_b = (g_q * tcq + z[:, Z_GQS:Z_GK] * tsq) * n_q
    q_bT = q_b.T
    zero = jnp.zeros((HEAD_SLOT - GQA_DIM, tok), BF16)
    for hd in range(GQA_HEADS):
        qbT_ref[0, hd * HEAD_SLOT:hd * HEAD_SLOT + GQA_DIM, rows_] = q_bT[hd * GQA_DIM:(hd + 1) * GQA_DIM].astype(BF16)
        qbT_ref[0, hd * HEAD_SLOT + GQA_DIM:(hd + 1) * HEAD_SLOT, rows_] = zero
    q_hi, q_lo = _split_e4m3(q_bT * Q8_SCALE)
    zero8 = jnp.zeros((Q8_SLOT - 3 * GQA_DIM, tok), F8)
    for hd in range(GQA_HEADS):
        rows = slice(hd * GQA_DIM, (hd + 1) * GQA_DIM)
        base = hd * Q8_SLOT
        qb8T_ref[0, base:base + GQA_DIM, rows_] = q_hi[rows]
        qb8T_ref[0, base + GQA_DIM:base + 2 * GQA_DIM, rows_] = q_hi[rows]
        qb8T_ref[0, base + 2 * GQA_DIM:base + 3 * GQA_DIM, rows_] = q_lo[rows]
        qb8T_ref[0, base + 3 * GQA_DIM:base + Q8_SLOT, rows_] = zero8
    g_k = z[:, Z_GK:Z_GKS]
    n_k = lax.rsqrt(_group_ssq(g_k, bd[:LANES, :LANES]) * (1.0 / GQA_DIM) + EPS)
    k_b = (g_k * tck + z[:, Z_GKS:Z_GV] * tsk) * n_k
    low = lax.broadcasted_iota(jnp.int32, (tok, LANES), 1) < GQA_DIM
    slots = [jnp.where(low, k_b, 0.0), jnp.where(low, pltpu.roll(k_b, GQA_DIM, 1), 0.0)]
    blocks = []
    for k_slot in slots:
        k8 = k_slot * K8_SCALE
        k_lo = k8 - k8.astype(F8).astype(F32)
        blocks += [jnp.where(low, k8, pltpu.roll(k_lo, GQA_DIM, 1)), k8]
    kb8_ref[0, rows_] = jnp.concatenate(blocks, axis=1).astype(F8)
    k_b = jnp.concatenate(slots, axis=1).astype(BF16)
    kb_ref[0, rows_] = k_b
    norms_b.append(_key_norms(k_b, GQA_KV_HEADS))
    _store_vT(vbT_ref, z[:, Z_GV:Z_U].T, GQA_KV_HEADS, GQA_DIM, r0)

    gm = jax.nn.gelu(z[:, Z_U:Z_KR])
    u = gm[:, :W_C]
    vv = _rms(gm[:, W_C:], gv_ref[...])
    lane_grp = lax.broadcasted_iota(jnp.int32, (GMLP_CHUNK, W_C), 1) // GMLP_DIM
    ws = ws_ref[...]
    bias = bias_ref[...]
    ycs = []
    for n in range(tok // GMLP_CHUNK):
        rows = slice(n * GMLP_CHUNK, (n + 1) * GMLP_CHUNK)
        r = _dot(ws, vv[rows].astype(BF16))
        mixed = r[3 * GMLP_CHUNK:]
        for grp in range(GMLP_GROUPS - 2, -1, -1):
            mixed = jnp.where(lane_grp == grp, r[grp * GMLP_CHUNK:(grp + 1) * GMLP_CHUNK], mixed)
        ycs.append(u[rows] * (mixed + bias))
    ync_ref[0, rows_] = _rms(jnp.concatenate(ycs, axis=0), gc_ref[...]).astype(BF16)


def _mix_in(x, p):
    B, S, D = x.shape
    T = TOK_TILE
    nkv = T // KV_TILE
    tile = lambda w: pl.BlockSpec((1, T, w), lambda s, b: (b, s, 0))
    def tab(t):
        if isinstance(t, _Layer):
            return pl.BlockSpec((None, T, t.stack.shape[2]), lambda s, b: (t.layer, s, 0))
        return pl.BlockSpec((T, t.shape[1]), lambda s, b: (s, 0))
    tposed = lambda r: pl.BlockSpec((1, r, T), lambda s, b: (b, 0, s))
    blocked = lambda r: pl.BlockSpec((1, nkv, r, KV_TILE), lambda s, b: (b, s, 0, 0))
    consts = [p["mix_norm"], p["w_in"], p["mla_q_norm"], p["w_uq"], p["mla_kv_norm"], p["w_k"], p["w_k8"], p["w_v"]]
    tabs = [p["cqa"], p["sqa"], p["tk"], p["cb"], p["sb"]]
    consts2 = [p["gq_c"], p["gq_s"], p["gk_c"], p["gk_s"], p["bd"], p["gmlp_v_norm"], p["w_s"], p["bias_s"], p["out_norm_c"]]
    in_specs = ([tile(D)] + [_const_spec(c) for c in consts] + [tab(t) for t in tabs]
                + [_const_spec(c) for c in consts2])
    out_shape = [
        jax.ShapeDtypeStruct((B, MLA_HEADS * HEAD_SLOT, S), BF16),
        jax.ShapeDtypeStruct((B, S, MLA_HEADS * HEAD_SLOT), BF16),
        jax.ShapeDtypeStruct((B, S // KV_TILE, MLA_HEADS * (MLA_V + V_PAD), KV_TILE), BF16),
        jax.ShapeDtypeStruct((B, S // T, F32_ROWS, LANES), F32),
        jax.ShapeDtypeStruct((B, MLA_HEADS * Q8_SLOT, S), F8),
        jax.ShapeDtypeStruct((B, S, MLA_HEADS * Q8_SLOT), F8),
        jax.ShapeDtypeStruct((B, GQA_HEADS * HEAD_SLOT, S), BF16),
        jax.ShapeDtypeStruct((B, S, GQA_KV_HEADS * HEAD_SLOT), BF16),
        jax.ShapeDtypeStruct((B, S // KV_TILE, GQA_KV_HEADS * (GQA_DIM + V_PAD), KV_TILE), BF16),
        jax.ShapeDtypeStruct((B, S // T, F32_ROWS, LANES), F32),
        jax.ShapeDtypeStruct((B, GQA_HEADS * Q8_SLOT, S), F8),
        jax.ShapeDtypeStruct((B, S, GQA_KV_HEADS * Q8_SLOT), F8),
        jax.ShapeDtypeStruct((B, S, W_C), BF16),
    ]
    norms = pl.BlockSpec((1, 1, F32_ROWS, LANES), lambda s, b: (b, s, 0, 0))
    out_specs = [tposed(MLA_HEADS * HEAD_SLOT), tile(MLA_HEADS * HEAD_SLOT), blocked(MLA_HEADS * (MLA_V + V_PAD)),
                 norms, tposed(MLA_HEADS * Q8_SLOT), tile(MLA_HEADS * Q8_SLOT),
                 tposed(GQA_HEADS * HEAD_SLOT), tile(GQA_KV_HEADS * HEAD_SLOT),
                 blocked(GQA_KV_HEADS * (GQA_DIM + V_PAD)), norms, tposed(GQA_HEADS * Q8_SLOT),
                 tile(GQA_KV_HEADS * Q8_SLOT), tile(W_C)]
    return pl.pallas_call(
        _mix_in_kernel, grid=(S // T, B), in_specs=in_specs, out_specs=out_specs, out_shape=out_shape,
        compiler_params=_params(2), name="mix_in",
    )(x, *map(_operand, consts + tabs + consts2))


def _attn_kernel(q_ref, k_ref, v_ref, kn_ref, g_ref, q8_ref, k8_ref, o_ref,
                 s_ref, p_ref, m_ref, acc_ref, oT_ref, *, n_heads, k_slots, v_rows, dv):
    nkv = v_ref.shape[1]
    n_blk = q_ref.shape[2] // Q_BLOCK
    n_work = n_heads * n_blk
    assert n_work % 2 == 0

    def cols(w):
        return slice((w % n_blk) * Q_BLOCK, (w % n_blk + 1) * Q_BLOCK)

    kmax = jnp.max(kn_ref[0], axis=0)
    bound = jnp.zeros((1, 1), F32)
    qn2_max = jnp.zeros((1, 1), F32)
    kn2_max = jnp.zeros((1, 1), F32)
    for hd in range(n_heads):
        qf = q_ref[0, hd * HEAD_SLOT:(hd + 1) * HEAD_SLOT, :].astype(F32)
        qn2 = jnp.sum(qf * qf, axis=0, keepdims=True)
        kn2 = kmax[k_slots[hd]:k_slots[hd] + 1, 0:1]
        u = jnp.sqrt(qn2 * kn2)
        for c in range(n_blk):
            m_ref[hd * n_blk + c] = u[:, c * Q_BLOCK:(c + 1) * Q_BLOCK]
        bound = jnp.maximum(bound, jnp.max(u, axis=1, keepdims=True))
        qn2_max = jnp.maximum(qn2_max, jnp.max(qn2, axis=1, keepdims=True))
        kn2_max = jnp.maximum(kn2_max, kn2)
    stabilise = jnp.logical_and(jnp.max(bound) <= STABILISER_MAX, jnp.logical_and(
        jnp.max(qn2_max) <= (FP8_MAX_SCALED / Q8_SCALE) ** 2, jnp.max(kn2_max) <= (FP8_MAX_SCALED / K8_SCALE) ** 2))
    acc_ref[...] = jnp.zeros(acc_ref.shape, F32)

    def run(stabilised):
        def scores(w, off):
            hd = w // n_blk
            ks, qs, depth = (k8_ref, q8_ref, Q8_SLOT) if stabilised else (k_ref, q_ref, HEAD_SLOT)
            k = ks[0, pl.ds(off, KV_TILE), k_slots[hd] * depth:(k_slots[hd] + 1) * depth]
            qT = qs[0, hd * depth:(hd + 1) * depth, cols(w)]
            s_ref[w % 2] = _dot(k, qT)

        scores(0, 0)
        last = n_work - 1

        def v_tile(w, j):
            hd = w // n_blk
            return v_ref[0, j, v_rows[hd]:v_rows[hd] + dv + V_PAD, :]

        def pv(w, j):
            acc_ref[w] += _dot(v_tile(w, j), p_ref[w])

        if stabilised:
            p_ref[last] = jnp.zeros(p_ref.shape[1:], BF16)

        def step(j, carry):
            off = pl.multiple_of(j * KV_TILE, KV_TILE)
            off_next = pl.multiple_of(jnp.minimum(j + 1, nkv - 1) * KV_TILE, KV_TILE)
            for w in range(n_work):
                if w + 1 < n_work:
                    scores(w + 1, off)
                else:
                    scores(0, off_next)
                if stabilised:
                    if w > 0:
                        pv(w - 1, j)
                    else:
                        pv(last, jnp.maximum(j - 1, 0))
                    s = s_ref[w % 2] * (1.0 / (Q8_SCALE * K8_SCALE)) - m_ref[w]
                    p_ref[w] = jnp.exp2(s.astype(BF16))
                else:
                    m_old = m_ref[w]
                    m_new = jnp.maximum(m_old, jnp.max(s_ref[w % 2], axis=0, keepdims=True))
                    m_ref[w] = m_new
                    p_ref[w] = jnp.exp2(s_ref[w % 2] - m_new).astype(BF16)
                    acc_ref[w] = jnp.exp2(m_old - m_new) * acc_ref[w] + _dot(v_tile(w, j), p_ref[w])
            return carry

        lax.fori_loop(0, nkv, step, 0, unroll=16 if stabilised else 1)
        if stabilised:
            pv(last, nkv - 1)

    def running_max():
        m_ref[...] = jnp.full(m_ref.shape, -1e30, F32)
        run(False)

    lax.cond(stabilise, lambda: run(True), running_max)
    for w in range(n_work):
        hd = w // n_blk
        oT_ref[hd * dv:(hd + 1) * dv, cols(w)] = acc_ref[w, :dv] / acc_ref[w, dv:dv + 1]
    o_ref[0] = _rms(oT_ref[...].T, g_ref[...]).astype(BF16)


def _attention(qT, k, vT, kn, gain, q8T, k8, *, n_heads, k_slots, v_rows, dv, name):
    B, _, S = qT.shape
    kernel = functools.partial(_attn_kernel, n_heads=n_heads, k_slots=k_slots, v_rows=v_rows, dv=dv)
    n_work = n_heads * (Q_TILE // Q_BLOCK)
    q_tile = lambda slot: pl.BlockSpec((1, n_heads * slot, Q_TILE), lambda b, i: (b, 0, i))
    whole = lambda shape: pl.BlockSpec((1,) + shape[1:], lambda b, i: (b,) + (0,) * (len(shape) - 1),
                                       pipeline_mode=pl.Buffered(1))
    return pl.pallas_call(
        kernel, grid=(B, S // Q_TILE),
        in_specs=[q_tile(HEAD_SLOT), whole(k.shape), whole(vT.shape), whole(kn.shape), _const_spec(gain),
                  q_tile(Q8_SLOT), whole(k8.shape)],
        out_specs=pl.BlockSpec((1, Q_TILE, n_heads * dv), lambda b, i: (b, i, 0)),
        out_shape=jax.ShapeDtypeStruct((B, S, n_heads * dv), BF16),
        scratch_shapes=[pltpu.VMEM((2, KV_TILE, Q_BLOCK), F32), pltpu.VMEM((n_work, KV_TILE, Q_BLOCK), BF16),
                        pltpu.VMEM((n_work, 1, Q_BLOCK), F32), pltpu.VMEM((n_work, dv + V_PAD, Q_BLOCK), F32),
                        pltpu.VMEM((n_heads * dv, Q_TILE), F32)],
        compiler_params=_params(2), name=name,
    )(qT, k, vT, kn, _operand(gain), q8T, k8)


def _mem_kv_kernel(mem_ref, g_ref, w_ref, kT_ref, v_ref):
    kv = _dot(_rms(mem_ref[0], g_ref[...]).astype(BF16), w_ref[...])
    width = MEM_HEADS * MEM_DIM
    kT_ref[0] = kv[:, :width].T.astype(BF16)
    ones = jnp.ones((kv.shape[0], MEM_DIM), BF16)
    for hd in range(MEM_HEADS):
        v_ref[0, hd, :, :MEM_DIM] = kv[:, width + hd * MEM_DIM:width + (hd + 1) * MEM_DIM].astype(BF16)
        v_ref[0, hd, :, MEM_DIM:] = ones


def _mem_kv(mem, gain, w_kv):
    B, Tm, D = mem.shape
    width = MEM_HEADS * MEM_DIM
    return pl.pallas_call(
        _mem_kv_kernel, grid=(B,),
        in_specs=[pl.BlockSpec((1, Tm, D), lambda b: (b, 0, 0)), _const_spec(gain), _const_spec(w_kv)],
        out_specs=[pl.BlockSpec((1, width, Tm), lambda b: (b, 0, 0)),
                   pl.BlockSpec((1, MEM_HEADS, Tm, 2 * MEM_DIM), lambda b: (b, 0, 0, 0))],
        out_shape=[jax.ShapeDtypeStruct((B, width, Tm), BF16),
                   jax.ShapeDtypeStruct((B, MEM_HEADS, Tm, 2 * MEM_DIM), BF16)],
        compiler_params=_params(1), name="mem_kv",
    )(mem, _operand(gain), _operand(w_kv))


def _out_mem_kernel(ya_ref, yb_ref, yc_ref, x_ref, wout_ref, g_ref, wq_ref, kT_ref, v_ref, wo_ref, o_ref):
    y = jnp.concatenate([ya_ref[0], yb_ref[0], yc_ref[0]], axis=1)
    x1 = x_ref[0] + _dot(y, wout_ref[...])
    h = _rms(x1, g_ref[...]).astype(BF16)
    q = (_dot(h, wq_ref[...]) * (MEM_DIM ** -0.5)).astype(BF16)
    heads = []
    for hd in range(MEM_HEADS):
        s = _dot(q[:, hd * MEM_DIM:(hd + 1) * MEM_DIM], kT_ref[0, hd * MEM_DIM:(hd + 1) * MEM_DIM, :])
        p = jnp.exp(s - jnp.max(s, axis=-1, keepdims=True)).astype(BF16)
        pv = _dot(p, v_ref[0, hd])
        heads.append(pv[:, :MEM_DIM] / pv[:, MEM_DIM:])
    o = jnp.concatenate(heads, axis=1).astype(BF16)
    o_ref[0] = x1 + _dot(o, wo_ref[...])


def _out_mem(ya, yb, yc, x, p, mem_kT, mem_v):
    B, S, D = x.shape
    T = WIDE_TILE
    tile = lambda w: pl.BlockSpec((1, T, w), lambda b, s: (b, s, 0))
    per_b = lambda a: pl.BlockSpec((1,) + a.shape[1:], lambda b, s: (b,) + (0,) * (a.ndim - 1))
    consts = [p["w_out"], p["mem_x_norm"], p["mem_w_q"]]
    return pl.pallas_call(
        _out_mem_kernel, grid=(B, S // T),
        in_specs=[tile(W_A), tile(W_B), tile(W_C), tile(D)] + [_const_spec(c) for c in consts]
                 + [per_b(mem_kT), per_b(mem_v), _const_spec(p["mem_w_o"])],
        out_specs=tile(D), out_shape=jax.ShapeDtypeStruct((B, S, D), F32),
        compiler_params=_params(2), name="out_mem",
    )(ya, yb, yc, x, *map(_operand, consts), mem_kT, mem_v, _operand(p["mem_w_o"]))


def _ffn_kernel(x_ref, xp_ref, xn_ref, g_ref, wup_ref, cw_ref, cb_ref, wdn_ref, fg_ref, o_ref,
                h_ref, act_ref, *, final):
    tok = x_ref.shape[1]
    i = pl.program_id(1)
    g = g_ref[...]
    x = x_ref[0]
    keep_prev = (i > 0).astype(F32)
    keep_next = (i < pl.num_programs(1) - 1).astype(F32)
    h_ref[0:HALO] = (_rms(xp_ref[0], g) * keep_prev).astype(BF16)
    h_ref[HALO:HALO + tok] = _rms(x, g).astype(BF16)
    h_ref[HALO + tok:] = (_rms(xn_ref[0], g) * keep_next).astype(BF16)
    hext = h_ref[...]
    rows = tok + 2 * HALO

    def conv_up(cols):
        a = _dot(hext, wup_ref[:, cols])
        w = cw_ref[:, cols]
        return (pltpu.roll(a, 1, 0)[HALO:HALO + tok] * w[0:1] + a[HALO:HALO + tok] * w[1:2]
                + pltpu.roll(a, rows - 1, 0)[HALO:HALO + tok] * w[2:3] + cb_ref[:, cols])

    for c in range(D_FF // FF_CHUNK):
        gate = conv_up(slice(FF_CHUNK * c, FF_CHUNK * (c + 1)))
        val = conv_up(slice(D_FF + FF_CHUNK * c, D_FF + FF_CHUNK * (c + 1)))
        act_ref[:, FF_CHUNK * c:FF_CHUNK * (c + 1)] = (jax.nn.silu(gate) * val).astype(BF16)
    y = x + _dot(act_ref[...], wdn_ref[...])
    if final:
        y = _rms(y, fg_ref[...])
    o_ref[0] = y


def _ffn(x, p, final_gain, *, final):
    B, S, D = x.shape
    T = WIDE_TILE
    per_tile = T // HALO
    n_halo = S // HALO
    tile = pl.BlockSpec((1, T, D), lambda b, s: (b, s, 0))
    prev = pl.BlockSpec((1, HALO, D), lambda b, s: (b, jnp.maximum(s * per_tile - 1, 0), 0))
    nxt = pl.BlockSpec((1, HALO, D), lambda b, s: (b, jnp.minimum((s + 1) * per_tile, n_halo - 1), 0))
    consts = [p["ffn_norm"], p["ffn_w_up"], p["ffn_conv_w"], p["ffn_conv_b"], p["ffn_w_down"], final_gain]
    return pl.pallas_call(
        functools.partial(_ffn_kernel, final=final), grid=(B, S // T),
        in_specs=[tile, prev, nxt] + [_const_spec(c) for c in consts],
        out_specs=tile, out_shape=jax.ShapeDtypeStruct((B, S, D), F32),
        scratch_shapes=[pltpu.VMEM((T + 2 * HALO, D), BF16), pltpu.VMEM((T, D_FF), BF16)],
        compiler_params=_params(2), name="ffn_final" if final else "ffn",
    )(x, x, x, *map(_operand, consts))


def _swap_pairs(w):
    n = w.shape[-1]
    return w.reshape(*w.shape[:-1], n // 2, 2)[..., ::-1].reshape(w.shape)


def _rope_tables(S, d_rot):
    rows = S // GRID_W
    row = jnp.repeat(jnp.arange(rows, dtype=F32), GRID_W)
    col = jnp.tile(jnp.arange(GRID_W, dtype=F32), rows)
    n = d_rot // 4
    inv = ROPE_THETA ** (-jnp.arange(n, dtype=F32) / n)
    ang = jnp.concatenate([row[:, None] * inv, col[:, None] * inv], axis=-1)
    cos, sin = jnp.cos(ang), jnp.sin(ang)
    c = jnp.repeat(cos, 2, axis=-1)
    s = jnp.stack([-sin, sin], axis=-1).reshape(S, d_rot)
    return c, s


def _prep_params(S, P):
    L = P["w_in"].shape[0]
    row = lambda v: v[:, None, :]
    ca, sa = _rope_tables(S, MLA_ROPE)
    cb, sb = _rope_tables(S, GQA_DIM)
    p, shared = {}, {}
    c_q, c_kv, k_rope, g_q, g_k, g_v, g_m = jnp.split(P["w_in"], [256, 384, 416, 800, 928, 1056], axis=2)
    pad = jnp.zeros((L, D_MODEL, Z_END - Z_KR - 2 * MLA_ROPE), F32)
    p["w_in"] = jnp.concatenate(
        [c_q, c_kv, g_q, _swap_pairs(g_q), g_k, _swap_pairs(g_k), g_v, g_m, k_rope, _swap_pairs(k_rope), pad],
        axis=2).astype(BF16)
    for name in ("mix_norm", "mla_q_norm", "mla_kv_norm", "gmlp_v_norm", "mem_x_norm", "mem_kv_norm", "ffn_norm",
                 "ffn_conv_b"):
        p[name] = row(P[name])

    w_uq = P["mla_w_uq"].reshape(L, MLA_Q_RANK, MLA_HEADS, MLA_NOPE + MLA_ROPE)
    zpad = jnp.zeros((L, MLA_Q_RANK, MLA_HEADS, HEAD_SLOT - MLA_NOPE - MLA_ROPE), F32)
    main = jnp.concatenate([w_uq, zpad], axis=-1)
    swapped = jnp.concatenate([jnp.zeros_like(w_uq[..., :MLA_NOPE]), _swap_pairs(w_uq[..., MLA_NOPE:]), zpad], axis=-1)
    p["w_uq"] = jnp.concatenate([main.reshape(L, MLA_Q_RANK, -1), swapped.reshape(L, MLA_Q_RANK, -1)],
                                axis=2).astype(BF16)
    scale_a = (MLA_NOPE + MLA_ROPE) ** -0.5 * LOG2E
    ones = jnp.ones((S, MLA_NOPE), F32)
    zeros_n = jnp.zeros((S, MLA_NOPE), F32)
    zeros_p = jnp.zeros((S, HEAD_SLOT - MLA_NOPE - MLA_ROPE), F32)
    shared["cqa"] = jnp.concatenate([ones, ca, zeros_p], axis=1) * scale_a
    shared["sqa"] = jnp.concatenate([zeros_n, sa, zeros_p], axis=1) * scale_a

    w_ukv = P["mla_w_ukv"].reshape(L, MLA_KV_RANK, MLA_HEADS, MLA_NOPE + MLA_V)
    k_lat = jnp.concatenate(
        [w_ukv[..., :MLA_NOPE], jnp.zeros((L, MLA_KV_RANK, MLA_HEADS, HEAD_SLOT - MLA_NOPE), F32)], axis=-1)
    place = jnp.concatenate([jnp.zeros((MLA_ROPE, MLA_NOPE), F32), jnp.eye(MLA_ROPE, dtype=F32),
                             jnp.zeros((MLA_ROPE, HEAD_SLOT - MLA_NOPE - MLA_ROPE), F32)], axis=1)
    place = jnp.broadcast_to(jnp.tile(place, (1, MLA_HEADS)), (L, MLA_ROPE, MLA_HEADS * HEAD_SLOT))
    zrows = jnp.zeros((L, Z_END - Z_KR - 2 * MLA_ROPE, MLA_HEADS * HEAD_SLOT), F32)
    p["w_k"] = jnp.concatenate([k_lat.reshape(L, MLA_KV_RANK, -1), place, place, zrows], axis=1).astype(BF16)
    p["w_v"] = w_ukv[..., MLA_NOPE:].reshape(L, MLA_KV_RANK, -1).astype(BF16)
    nope_w = w_ukv[..., :MLA_NOPE]
    lat8 = jnp.concatenate([nope_w, nope_w, nope_w, jnp.zeros((L, MLA_KV_RANK, MLA_HEADS, 2 * MLA_ROPE), F32)], axis=-1)
    place8 = jnp.concatenate([jnp.zeros((MLA_ROPE, 3 * MLA_NOPE), F32), jnp.eye(MLA_ROPE, dtype=F32),
                              jnp.eye(MLA_ROPE, dtype=F32)], axis=1)
    place8 = jnp.broadcast_to(jnp.tile(place8, (1, MLA_HEADS)), (L, MLA_ROPE, MLA_HEADS * Q8_SLOT))
    zrows8 = jnp.zeros((L, Z_END - Z_KR - 2 * MLA_ROPE, MLA_HEADS * Q8_SLOT), F32)
    p["w_k8"] = (jnp.concatenate([lat8.reshape(L, MLA_KV_RANK, -1), place8, place8, zrows8], axis=1)
                 * K8_SCALE).astype(BF16)
    shared["tk"] = jnp.concatenate([ca, sa, jnp.zeros((S, Z_END - Z_KR - 2 * MLA_ROPE), F32)], axis=1)

    gq = row(jnp.tile(P["gqa_q_norm"], (1, GQA_HEADS)) * (GQA_DIM ** -0.5 * LOG2E))
    gk = row(jnp.tile(P["gqa_k_norm"], (1, GQA_KV_HEADS)))
    p["gq_c"], p["gq_s"], p["gk_c"], p["gk_s"] = gq, _swap_pairs(gq), gk, _swap_pairs(gk)
    shared["cb"] = jnp.tile(cb, (1, LANES // GQA_DIM))
    shared["sb"] = jnp.tile(sb, (1, LANES // GQA_DIM))
    grp = jnp.arange(W_B) // GQA_DIM
    shared["bd"] = (grp[:, None] == grp[None, :]).astype(BF16)

    p["w_s"] = P["gmlp_w_s"].reshape(L, GMLP_GROUPS * GMLP_CHUNK, GMLP_CHUNK).astype(BF16)
    p["bias_s"] = jnp.repeat(jnp.swapaxes(P["gmlp_b_s"], 1, 2), GMLP_DIM, axis=2)
    p["out_norm_a"] = row(P["out_norm"][:, :W_A])
    p["out_norm_b"] = row(P["out_norm"][:, W_A:W_A + W_B])
    p["out_norm_c"] = row(P["out_norm"][:, W_A + W_B:])
    for name in ("w_out", "mem_w_q", "mem_w_kv", "mem_w_o", "ffn_w_up", "ffn_w_down"):
        p[name] = P[name].astype(BF16)
    p["ffn_conv_w"] = P["ffn_conv_w"]
    return p, shared


def kernel(x, mem, mix_norm, w_in, mla_q_norm, mla_w_uq, mla_kv_norm, mla_w_ukv, gqa_q_norm, gqa_k_norm, gmlp_v_norm, gmlp_w_s, gmlp_b_s, out_norm, w_out, mem_x_norm, mem_kv_norm, mem_w_q, mem_w_kv, mem_w_o, ffn_norm, ffn_w_up, ffn_conv_w, ffn_conv_b, ffn_w_down, final_norm):
    P = dict(mix_norm=mix_norm, w_in=w_in, mla_q_norm=mla_q_norm, mla_w_uq=mla_w_uq, mla_kv_norm=mla_kv_norm,
             mla_w_ukv=mla_w_ukv, gqa_q_norm=gqa_q_norm, gqa_k_norm=gqa_k_norm, gmlp_v_norm=gmlp_v_norm,
             gmlp_w_s=gmlp_w_s, gmlp_b_s=gmlp_b_s, out_norm=out_norm, w_out=w_out, mem_x_norm=mem_x_norm,
             mem_kv_norm=mem_kv_norm, mem_w_q=mem_w_q, mem_w_kv=mem_w_kv, mem_w_o=mem_w_o, ffn_norm=ffn_norm,
             ffn_w_up=ffn_w_up, ffn_conv_w=ffn_conv_w, ffn_conv_b=ffn_conv_b, ffn_w_down=ffn_w_down)
    B, S, D = x.shape
    assert D == D_MODEL and S % TOK_TILE == 0 and S % WIDE_TILE == 0 and S % GRID_W == 0
    depth = w_in.shape[0]
    stacked, shared = _prep_params(S, P)
    final_gain = final_norm.reshape(1, -1)
    group = GQA_HEADS // GQA_KV_HEADS
    for l in range(depth):
        p = dict(shared, **{name: _Layer(v, l) for name, v in stacked.items()})
        qaT, ka, vaT, kna, qa8T, ka8, qbT, kb, vbT, knb, qb8T, kb8, ync = _mix_in(x, p)
        yna = _attention(qaT, ka, vaT, kna, p["out_norm_a"], qa8T, ka8, n_heads=MLA_HEADS,
                         k_slots=tuple(range(MLA_HEADS)),
                         v_rows=tuple(h * (MLA_V + V_PAD) for h in range(MLA_HEADS)), dv=MLA_V, name="attn_mla")
        ynb = _attention(qbT, kb, vbT, knb, p["out_norm_b"], qb8T, kb8, n_heads=GQA_HEADS,
                         k_slots=tuple(h // group for h in range(GQA_HEADS)),
                         v_rows=tuple((h // group) * (GQA_DIM + V_PAD) for h in range(GQA_HEADS)), dv=GQA_DIM, name="attn_gqa")
        mem_kT, mem_v = _mem_kv(mem, p["mem_kv_norm"], p["mem_w_kv"])
        x = _out_mem(yna, ynb, ync, x, p, mem_kT, mem_v)
        x = _ffn(x, p, final_gain, final=(l == depth - 1))
    return x
```
